```python
import math
import jax
import jax.numpy as jnp
from jax import lax
import numpy as np

D_MODEL = 2048
BATCH = 4
SEQ = 2048
DEPTH = 2

GRID_W = 64
CTX_LEN = 256
EPS = 1e-6
SHORT_K = 3
HY_D = 1024
HY_EMB = 33
HY_HID = 64
HY_FAST_DECAY = 0.3
HY_SLOW_DECAY = 1.5
HY_DECAY_TARGET = 1e-2
ML_H = 8
ML_DH = 128
ML_D = ML_H * ML_DH
ML_CHUNK = 128
AB_IN = 3 * HY_D + 4 * ML_D + 4 * ML_H
SSD_H = 16
SSD_P = 64
SSD_D = SSD_H * SSD_P
SSD_G = 2
SSD_N = 128
SSD_CHUNK = 128
SSD_XBC = SSD_D + 2 * SSD_G * SSD_N
S5_W = 512
S5_GS = 16
S5_G = S5_W // S5_GS
S5_P = 64
CD_IN = SSD_D + SSD_XBC + 2 * SSD_H + S5_W
FFN_F = 5632
N_EXPERTS = 8
TOP_K = 2
EXPERT_F = 5632

kernel_name = 'hybrid_hyena_mlstm_ssd_s5_moe_prefix_dit'


def _cuts(*widths):
    out, acc = [], 0
    for w in widths:
        acc += w
        out.append(acc)
    return out


def rmsnorm(x, w=None):
    xf = x.astype(jnp.float32)
    y = xf * lax.rsqrt(jnp.mean(xf * xf, axis=-1, keepdims=True) + EPS)
    if w is not None:
        y = y * w.astype(jnp.float32)
    return y.astype(x.dtype)


def modulate(x, shift, scale):
    return rmsnorm(x) * (1 + scale) + shift


def short_conv(u, w, b, rows):
    bsz, L, ch = u.shape
    pad = SHORT_K // 2
    seg = u[:, None] if rows is None else u.reshape(bsz, rows, L // rows, ch)
    n = seg.shape[2]
    sp = jnp.pad(seg, ((0, 0), (0, 0), (pad, pad), (0, 0)))
    y = b + sp[:, :, 0:n] * w[0]
    for j in range(1, SHORT_K):
        y = y + sp[:, :, j:j + n] * w[j]
    return y.reshape(bsz, L, ch)


def swiglu(h, w_gu, w_down):
    g, u = jnp.split(h @ w_gu, 2, axis=-1)
    return (jax.nn.silu(g) * u) @ w_down


def moe_swiglu(h, w_router, w_gu, w_down):
    bsz, L, d = h.shape
    t = h.reshape(bsz * L, d)
    logits = (t @ w_router).astype(jnp.float32)
    top_val, top_idx = lax.top_k(logits, TOP_K)
    probs = jax.nn.softmax(top_val, axis=-1)
    comb = jnp.sum(jax.nn.one_hot(top_idx, N_EXPERTS, dtype=jnp.float32) * probs[..., None], axis=1)
    out = jnp.zeros_like(t)
    for e in range(N_EXPERTS):
        out = out + comb[:, e:e + 1].astype(t.dtype) * swiglu(t, w_gu[e], w_down[e])
    return out.reshape(bsz, L, d)


def flip_seq(tree):
    return jax.tree_util.tree_map(lambda a: jnp.flip(a, axis=1), tree)


def prefix_bidir(scan_fn, ctx_f, lat_f, par_f, ctx_b, lat_b, par_b, state0):
    yc_f, s_f = scan_fn(ctx_f, par_f, state0)
    yl_f, _ = scan_fn(lat_f, par_f, s_f)
    yc_b, s_b = scan_fn(flip_seq(ctx_b), par_b, state0)
    yl_b, _ = scan_fn(flip_seq(lat_b), par_b, s_b)
    merge = lambda a, b: a + jnp.flip(b, axis=1)
    return (jax.tree_util.tree_map(merge, yc_f, yc_b), jax.tree_util.tree_map(merge, yl_f, yl_b))


def hyena_filter(L, w1, b1, w2, b2, w3, freq):
    w1, b1, w2, b2, w3, freq = (a.astype(jnp.float32) for a in (w1, b1, w2, b2, w3, freq))
    pos = jnp.arange(L, dtype=jnp.float32)
    t = pos / max(L - 1, 1)
    n_bands = (HY_EMB - 1) // 2
    bands = jnp.linspace(1e-4, n_bands - 1, n_bands, dtype=jnp.float32)
    ang = (2 * math.pi / L) * pos[:, None] * bands[None, :]
    feats = jnp.concatenate([t[:, None], jnp.cos(ang), -jnp.sin(ang)], axis=-1)
    h = jnp.sin(freq[0] * (feats @ w1 + b1))
    h = jnp.sin(freq[1] * (h @ w2 + b2))
    h = (h @ w3).reshape(L, 2, HY_D)
    deltas = jnp.abs(jnp.linspace(math.log(HY_DECAY_TARGET) / HY_SLOW_DECAY, math.log(HY_DECAY_TARGET) / HY_FAST_DECAY, HY_D, dtype=jnp.float32))
    h = h * jnp.exp(-t[:, None, None] * deltas)
    h_f, h_b = h[:, 0], h[:, 1]
    l1 = jnp.sum(jnp.abs(h_f), axis=0) + jnp.sum(jnp.abs(h_b[1:]), axis=0)
    return h_f / l1, h_b / l1


def bidir_long_conv(u, h_f, h_b, bias):
    L, ch = h_f.shape
    k = jnp.concatenate([h_f, jnp.zeros((1, ch), h_f.dtype), h_b[:0:-1]], axis=0)
    y = jnp.fft.irfft(jnp.fft.rfft(u, n=2 * L, axis=1) * jnp.fft.rfft(k, axis=0)[None], n=2 * L, axis=1)[:, :L]
    return y + bias * u


def hyena_mixer(proj, conv_w, conv_b, filt, bias, rows):
    u = short_conv(proj, conv_w, conv_b, rows).astype(jnp.float32)
    x0, x1, v = jnp.split(u, 3, axis=-1)
    h_f, h_b = hyena_filter(proj.shape[1], *filt)
    y = x0 * bidir_long_conv(x1 * v, h_f, h_b, bias.astype(jnp.float32))
    return y.astype(proj.dtype)


def mlstm_chunk_scan(seq, gate_bias, state0):
    q, k, v, i_pre, f_pre = seq
    bsz, L, H, dh = q.shape
    nc = L // ML_CHUNK
    gate_bias = gate_bias.astype(jnp.float32)

    def chunks(a):
        return jnp.moveaxis(a.reshape(bsz, nc, ML_CHUNK, H, *a.shape[3:]), 3, 1)

    ig = chunks(i_pre + gate_bias[0])
    lf = chunks(jax.nn.log_sigmoid(f_pre + gate_bias[1]))
    q, k, v = chunks(q) * ML_DH ** -0.5, chunks(k), chunks(v)
    bcum = jnp.cumsum(lf, axis=-1)
    g = bcum[..., -1]
    a = g[..., None] - bcum + ig
    m_loc = jnp.max(a, axis=-1)
    w_loc = jnp.exp(a - m_loc[..., None])
    c_loc = jnp.einsum('bhcs,bhcsk,bhcsv->bhckv', w_loc, k, v)
    n_loc = jnp.einsum('bhcs,bhcsk->bhck', w_loc, k)

    def step(carry, inp):
        c_prev, n_prev, m_prev = carry
        cl, nl, ml, gc = inp
        m_new = jnp.maximum(gc + m_prev, ml)
        s_prev = jnp.exp(gc + m_prev - m_new)
        s_loc = jnp.exp(ml - m_new)
        new = (s_prev[..., None, None] * c_prev + s_loc[..., None, None] * cl,
               s_prev[..., None] * n_prev + s_loc[..., None] * nl, m_new)
        return new, carry

    xs = (jnp.moveaxis(c_loc, 2, 0), jnp.moveaxis(n_loc, 2, 0), jnp.moveaxis(m_loc, 2, 0), jnp.moveaxis(g, 2, 0))
    state_T, (c_in, n_in, m_in) = lax.scan(step, state0, xs)
    c_in, n_in, m_in = jnp.moveaxis(c_in, 0, 2), jnp.moveaxis(n_in, 0, 2), jnp.moveaxis(m_in, 0, 2)
    lower = jnp.tril(jnp.ones((ML_CHUNK, ML_CHUNK), dtype=bool))
    dmat = jnp.where(lower, bcum[..., :, None] - bcum[..., None, :] + ig[..., None, :], -jnp.inf)
    m_inter = bcum + m_in[..., None]
    m_t = jnp.maximum(jnp.max(dmat, axis=-1), m_inter)
    s = jnp.einsum('bhctk,bhcsk->bhcts', q, k) * jnp.exp(dmat - m_t[..., None])
    w_inter = jnp.exp(m_inter - m_t)
    num = jnp.einsum('bhcts,bhcsv->bhctv', s, v) + w_inter[..., None] * jnp.einsum('bhctk,bhckv->bhctv', q, c_in)
    den = jnp.sum(s, axis=-1) + w_inter * jnp.einsum('bhctk,bhck->bhct', q, n_in)
    h = num / jnp.maximum(jnp.abs(den), jnp.exp(-m_t))[..., None]
    return jnp.moveaxis(h, 1, 3).reshape(bsz, L, H, dh), state_T


def head_norm(h, w):
    mu = jnp.mean(h, axis=-1, keepdims=True)
    var = jnp.mean(jnp.square(h - mu), axis=-1, keepdims=True)
    y = ((h - mu) * lax.rsqrt(var + EPS)).reshape(h.shape[0], h.shape[1], -1)
    return y * w.astype(jnp.float32)


def mlstm_inputs(qk_cols, v_cols, gate_cols, conv_w, conv_b, rows):
    bsz, L, _ = v_cols.shape
    qk = jax.nn.silu(short_conv(qk_cols, conv_w, conv_b, rows).astype(jnp.float32)).reshape(bsz, L, 2, ML_H, ML_DH)
    v = v_cols.astype(jnp.float32).reshape(bsz, L, ML_H, ML_DH)
    g = gate_cols.astype(jnp.float32).reshape(bsz, L, 4, ML_H)
    q, k = qk[:, :, 0], qk[:, :, 1]
    return (q, k, v, g[:, :, 0], g[:, :, 1]), (q, k, v, g[:, :, 2], g[:, :, 3])


def mixer_ab(hc, hl, rows, need_ctx, w_in, hy_conv_w, hy_conv_b, hy_filt, hy_bias,
             ml_conv_w, ml_conv_b, ml_gate_b, ml_norm_w, w_out):
    cuts = _cuts(3 * HY_D, 2 * ML_D, ML_D, ML_D)
    pc = jnp.split(hc @ w_in, cuts, axis=-1)
    pl = jnp.split(hl @ w_in, cuts, axis=-1)
    seq_cf, seq_cb = mlstm_inputs(pc[1], pc[2], pc[4], ml_conv_w, ml_conv_b, None)
    seq_lf, seq_lb = mlstm_inputs(pl[1], pl[2], pl[4], ml_conv_w, ml_conv_b, rows)
    bsz = hl.shape[0]
    state0 = (jnp.zeros((bsz, ML_H, ML_DH, ML_DH), jnp.float32), jnp.zeros((bsz, ML_H, ML_DH), jnp.float32),
              jnp.zeros((bsz, ML_H), jnp.float32))
    mc, ml = prefix_bidir(mlstm_chunk_scan, seq_cf, seq_lf, ml_gate_b[:2], seq_cb, seq_lb, ml_gate_b[2:], state0)

    def finish(p, m, r):
        yh = hyena_mixer(p[0], hy_conv_w, hy_conv_b, hy_filt, hy_bias, r)
        ym = head_norm(m, ml_norm_w) * jax.nn.sigmoid(p[3].astype(jnp.float32))
        return jnp.concatenate([yh, ym.astype(yh.dtype)], axis=-1) @ w_out

    yl = finish(pl, ml, rows)
    yc = finish(pc, mc, None) if need_ctx else None
    return yc, yl


def ssd_chunk_scan(seq, par, h0):
    xs, bm, cm, dt_raw = seq
    a_log, dt_bias = par
    bsz, L, H, P = xs.shape
    q_len = SSD_CHUNK
    nc = L // q_len
    r = H // SSD_G
    dt = jax.nn.softplus(dt_raw + dt_bias.astype(jnp.float32))
    la = dt * (-jnp.exp(a_log.astype(jnp.float32)))
    xc = xs.reshape(bsz, nc, q_len, SSD_G, r, P)
    dtc = dt.reshape(bsz, nc, q_len, SSD_G, r)
    acum = jnp.cumsum(la.reshape(bsz, nc, q_len, SSD_G, r), axis=2)
    bc = bm.reshape(bsz, nc, q_len, SSD_G, SSD_N)
    cc = cm.reshape(bsz, nc, q_len, SSD_G, SSD_N)
    lower = jnp.tril(jnp.ones((q_len, q_len), dtype=bool))[:, :, None, None]
    seg = acum[:, :, :, None] - acum[:, :, None, :]
    decay = jnp.exp(jnp.where(lower, seg, -jnp.inf))
    cb = jnp.einsum('bctgn,bcsgn->bctsg', cc, bc)
    y_diag = jnp.einsum('bctsgr,bcsgrp->bctgrp', cb[..., None] * decay * dtc[:, :, None], xc)
    w_end = jnp.exp(acum[:, :, -1:] - acum) * dtc
    states = jnp.einsum('bcsgn,bcsgr,bcsgrp->bcgrpn', bc, w_end, xc)
    chunk_decay = jnp.exp(acum[:, :, -1])

    def step(h, inp):
        st, dc = inp
        return dc[..., None, None] * h + st, h

    h_T, h_in = lax.scan(step, h0.reshape(bsz, SSD_G, r, P, SSD_N),
                         (jnp.moveaxis(states, 1, 0), jnp.moveaxis(chunk_decay, 1, 0)))
    h_in = jnp.moveaxis(h_in, 0, 1)
    y_off = jnp.einsum('bctgn,bcgrpn->bctgrp', cc, h_in) * jnp.exp(acum)[..., None]
    y = (y_diag + y_off).reshape(bsz, L, H, P)
    return y, h_T.reshape(bsz, H, P, SSD_N)


def cplx_affine_combine(e1, e2):
    a1r, a1i, b1r, b1i = e1
    a2r, a2i, b2r, b2i = e2
    return (a2r * a1r - a2i * a1i, a2r * a1i + a2i * a1r,
            a2r * b1r - a2i * b1i + b2r, a2r * b1i + a2i * b1r + b2i)


def s5_scan(seq, par, s0):
    (u,) = seq
    lam_re, lam_im, log_dt, b_re, b_im = (p.astype(jnp.float32) for p in par)
    dt = jnp.exp(log_dt)[:, None]
    mag = jnp.exp(lam_re * dt)
    lb_re, lb_im = mag * jnp.cos(lam_im * dt), mag * jnp.sin(lam_im * dt)
    den = lam_re * lam_re + lam_im * lam_im
    f_re = ((lb_re - 1) * lam_re + lb_im * lam_im) / den
    f_im = (lb_im * lam_re - (lb_re - 1) * lam_im) / den
    bb_re = f_re[..., None] * b_re - f_im[..., None] * b_im
    bb_im = f_re[..., None] * b_im + f_im[..., None] * b_re
    bu_re = jnp.einsum('gpi,blgi->blgp', bb_re, u)
    bu_im = jnp.einsum('gpi,blgi->blgp', bb_im, u)
    s_re, s_im = s0
    bu_re = bu_re.at[:, 0].add(lb_re * s_re - lb_im * s_im)
    bu_im = bu_im.at[:, 0].add(lb_re * s_im + lb_im * s_re)
    a_re = jnp.broadcast_to(lb_re, bu_re.shape)
    a_im = jnp.broadcast_to(lb_im, bu_im.shape)
    _, _, x_re, x_im = lax.associative_scan(cplx_affine_combine, (a_re, a_im, bu_re, bu_im), axis=1)
    return (x_re, x_im), (x_re[:, -1], x_im[:, -1])


def mixer_cd(hc, hl, rows, need_ctx, w_in, ssd_conv_w, ssd_conv_b, ssd_A_log, ssd_dt_bias, ssd_D, ssd_norm_w,
             s5_lam_re, s5_lam_im, s5_log_dt, s5_B_re, s5_B_im, s5_C_re, s5_C_im, s5_D, s5_glu_w, w_out):
    cuts = _cuts(SSD_D, SSD_XBC, 2 * SSD_H)
    pc = jnp.split(hc @ w_in, cuts, axis=-1)
    pl = jnp.split(hl @ w_in, cuts, axis=-1)

    def ssd_inputs(p, r):
        xbc = jax.nn.silu(short_conv(p[1], ssd_conv_w, ssd_conv_b, r).astype(jnp.float32))
        bsz, L, _ = xbc.shape
        xs, bm, cm = jnp.split(xbc, _cuts(SSD_D, SSD_G * SSD_N), axis=-1)
        xs = xs.reshape(bsz, L, SSD_H, SSD_P)
        bm = bm.reshape(bsz, L, SSD_G, SSD_N)
        cm = cm.reshape(bsz, L, SSD_G, SSD_N)
        dt = p[2].astype(jnp.float32).reshape(bsz, L, 2, SSD_H)
        return xs, (xs, bm, cm, dt[:, :, 0]), (xs, bm, cm, dt[:, :, 1])

    xs_c, sc_f, sc_b = ssd_inputs(pc, None)
    xs_l, sl_f, sl_b = ssd_inputs(pl, rows)
    bsz = hl.shape[0]
    h0 = jnp.zeros((bsz, SSD_H, SSD_P, SSD_N), jnp.float32)
    yc_ssd, yl_ssd = prefix_bidir(ssd_chunk_scan, sc_f, sl_f, (ssd_A_log[0], ssd_dt_bias[0]),
                                  sc_b, sl_b, (ssd_A_log[1], ssd_dt_bias[1]), h0)

    def s5_in(p):
        b_, L, _ = p[3].shape
        return (p[3].astype(jnp.float32).reshape(b_, L, S5_G, S5_GS),)

    uc, ul = s5_in(pc), s5_in(pl)
    s5_par = lambda d: (s5_lam_re[d], s5_lam_im[d], s5_log_dt[d], s5_B_re, s5_B_im)
    s0 = (jnp.zeros((bsz, S5_G, S5_P), jnp.float32), jnp.zeros((bsz, S5_G, S5_P), jnp.float32))
    stc, stl = prefix_bidir(s5_scan, uc, ul, s5_par(0), uc, ul, s5_par(1), s0)

    def finish(p, y_ssd, xs, st, u):
        b_, L = y_ssd.shape[0], y_ssd.shape[1]
        y = (y_ssd + ssd_D.astype(jnp.float32)[:, None] * xs).reshape(b_, L, SSD_D)
        y = rmsnorm(y * jax.nn.silu(p[0].astype(jnp.float32)), ssd_norm_w)
        x_re, x_im = st
        s = (jnp.einsum('gip,blgp->blgi', s5_C_re.astype(jnp.float32), x_re)
             - jnp.einsum('gip,blgp->blgi', s5_C_im.astype(jnp.float32), x_im))
        s = s.reshape(b_, L, S5_W) + s5_D.astype(jnp.float32) * u[0].reshape(b_, L, S5_W)
        s = jax.nn.gelu(s, approximate=False)
        val, gate = jnp.split(s @ s5_glu_w.astype(jnp.float32), 2, axis=-1)
        ys5 = val * jax.nn.sigmoid(gate)
        return jnp.concatenate([y, ys5], axis=-1).astype(hl.dtype) @ w_out

    yl = finish(pl, yl_ssd, xs_l, stl, ul)
    yc = finish(pc, yc_ssd, xs_c, stc, uc) if need_ctx else None
    return yc, yl


def setup_inputs(seed: int = 0) -> dict:
    key = jax.random.key(seed)
    keys = list(jax.random.split(key, 48))
    n_even = (DEPTH + 1) // 2
    n_odd = DEPTH // 2
    f32 = jnp.float32

    def nrm(shape, scale):
        return scale * jax.random.normal(keys.pop(), shape, f32)

    def unif(shape, lo, hi):
        return jax.random.uniform(keys.pop(), shape, f32, lo, hi)

    D = D_MODEL
    i_bias = nrm((n_even, 2, ML_H), 0.1)
    f_bias = jnp.linspace(3.0, 6.0, ML_H, dtype=f32) + nrm((n_even, 2, ML_H), 0.1)
    dt0 = jnp.exp(unif((n_odd, 2, SSD_H), math.log(1e-3), math.log(1e-1)))
    return {
        'x': nrm((BATCH, SEQ, D), 1.0),
        'c': nrm((BATCH, D), 1.0),
        'ctx': nrm((BATCH, CTX_LEN, D), 1.0),
        'c_ctx': nrm((D,), 1.0),
        'ada_w': nrm((DEPTH, D, 6 * D), D ** -0.5),
        'ada_b': nrm((DEPTH, 6 * D), 0.02),
        'ab_w_in': nrm((n_even, D, AB_IN), D ** -0.5),
        'hy_conv_w': nrm((n_even, SHORT_K, 3 * HY_D), SHORT_K ** -0.5),
        'hy_conv_b': nrm((n_even, 3 * HY_D), 0.02),
        'hy_filt_w1': nrm((n_even, HY_EMB, HY_HID), HY_EMB ** -0.5),
        'hy_filt_b1': nrm((n_even, HY_HID), 0.02),
        'hy_filt_w2': nrm((n_even, HY_HID, HY_HID), HY_HID ** -0.5),
        'hy_filt_b2': nrm((n_even, HY_HID), 0.02),
        'hy_filt_w3': nrm((n_even, HY_HID, 2 * HY_D), HY_HID ** -0.5),
        'hy_filt_freq': 1.0 + nrm((n_even, 2, HY_HID), 0.1),
        'hy_bias': nrm((n_even, HY_D), 0.5),
        'ml_conv_w': nrm((n_even, SHORT_K, 2 * ML_D), SHORT_K ** -0.5),
        'ml_conv_b': nrm((n_even, 2 * ML_D), 0.02),
        'ml_gate_b': jnp.stack([i_bias[:, 0], f_bias[:, 0], i_bias[:, 1], f_bias[:, 1]], axis=1),
        'ml_norm_w': 1.0 + nrm((n_even, ML_D), 0.1),
        'ab_w_out': nrm((n_even, HY_D + ML_D, D), (HY_D + ML_D) ** -0.5),
        'ffn_w_gu': nrm((n_even, D, 2 * FFN_F), D ** -0.5),
        'ffn_w_down': nrm((n_even, FFN_F, D), FFN_F ** -0.5),
        'cd_w_in': nrm((n_odd, D, CD_IN), D ** -0.5),
        'ssd_conv_w': nrm((n_odd, SHORT_K, SSD_XBC), SHORT_K ** -0.5),
        'ssd_conv_b': nrm((n_odd, SSD_XBC), 0.02),
        'ssd_A_log': jnp.log(unif((n_odd, 2, SSD_H), 1.0, 16.0)),
        'ssd_dt_bias': dt0 + jnp.log(-jnp.expm1(-dt0)),
        'ssd_D': 1.0 + nrm((n_odd, SSD_H), 0.1),
        'ssd_norm_w': 1.0 + nrm((n_odd, SSD_D), 0.1),
        's5_lam_re': -0.5 + nrm((n_odd, 2, S5_G, S5_P), 0.01),
        's5_lam_im': math.pi * jnp.arange(S5_P, dtype=f32) + nrm((n_odd, 2, S5_G, S5_P), 0.01),
        's5_log_dt': unif((n_odd, 2, S5_G), math.log(1e-3), math.log(1e-1)),
        's5_B_re': nrm((n_odd, S5_G, S5_P, S5_GS), (2 * S5_GS) ** -0.5),
        's5_B_im': nrm((n_odd, S5_G, S5_P, S5_GS), (2 * S5_GS) ** -0.5),
        's5_C_re': nrm((n_odd, S5_G, S5_GS, S5_P), S5_P ** -0.5),
        's5_C_im': nrm((n_odd, S5_G, S5_GS, S5_P), S5_P ** -0.5),
        's5_D': nrm((n_odd, S5_W), 1.0),
        's5_glu_w': nrm((n_odd, S5_W, 2 * S5_W), S5_W ** -0.5),
        'cd_w_out': nrm((n_odd, SSD_D + S5_W, D), (SSD_D + S5_W) ** -0.5),
        'moe_router': nrm((n_odd, D, N_EXPERTS), D ** -0.5),
        'moe_w_gu': nrm((n_odd, N_EXPERTS, D, 2 * EXPERT_F), D ** -0.5),
        'moe_w_down': nrm((n_odd, N_EXPERTS, EXPERT_F, D), EXPERT_F ** -0.5),
        'final_norm_w': 1.0 + nrm((D,), 0.1),
    }


def reference(x, c, ctx, c_ctx, ada_w, ada_b, ab_w_in, hy_conv_w, hy_conv_b, hy_filt_w1, hy_filt_b1,
              hy_filt_w2, hy_filt_b2, hy_filt_w3, hy_filt_freq, hy_bias, ml_conv_w, ml_conv_b, ml_gate_b,
              ml_norm_w, ab_w_out, ffn_w_gu, ffn_w_down, cd_w_in, ssd_conv_w, ssd_conv_b, ssd_A_log,
              ssd_dt_bias, ssd_D, ssd_norm_w, s5_lam_re, s5_lam_im, s5_log_dt, s5_B_re, s5_B_im, s5_C_re,
              s5_C_im, s5_D, s5_glu_w, cd_w_out, moe_router, moe_w_gu, moe_w_down, final_norm_w):
    rows = x.shape[1] // GRID_W
    xl, xc = x, ctx
    cond_l, cond_c = jax.nn.silu(c), jax.nn.silu(c_ctx)
    for layer in range(DEPTH):
        j = layer // 2
        last = layer == DEPTH - 1
        m_l = jnp.split(cond_l @ ada_w[layer] + ada_b[layer], 6, axis=-1)
        m_c = jnp.split(cond_c @ ada_w[layer] + ada_b[layer], 6, axis=-1)
        hl = modulate(xl, m_l[0][:, None], m_l[1][:, None])
        hc = modulate(xc, m_c[0], m_c[1])
        if layer % 2 == 0:
            yc, yl = mixer_ab(hc, hl, rows, not last, ab_w_in[j], hy_conv_w[j], hy_conv_b[j],
                              (hy_filt_w1[j], hy_filt_b1[j], hy_filt_w2[j], hy_filt_b2[j], hy_filt_w3[j], hy_filt_freq[j]),
                              hy_bias[j], ml_conv_w[j], ml_conv_b[j], ml_gate_b[j], ml_norm_w[j], ab_w_out[j])
            ffn = lambda h: swiglu(h, ffn_w_gu[j], ffn_w_down[j])
        else:
            yc, yl = mixer_cd(hc, hl, rows, not last, cd_w_in[j], ssd_conv_w[j], ssd_conv_b[j], ssd_A_log[j],
                              ssd_dt_bias[j], ssd_D[j], ssd_norm_w[j], s5_lam_re[j], s5_lam_im[j], s5_log_dt[j],
                              s5_B_re[j], s5_B_im[j], s5_C_re[j], s5_C_im[j], s5_D[j], s5_glu_w[j], cd_w_out[j])
            ffn = lambda h: moe_swiglu(h, moe_router[j], moe_w_gu[j], moe_w_down[j])
        xl = xl + m_l[2][:, None] * yl
        xl = xl + m_l[5][:, None] * ffn(modulate(xl, m_l[3][:, None], m_l[4][:, None]))
        if not last:
            xc = xc + m_c[2] * yc
            xc = xc + m_c[5] * ffn(modulate(xc, m_c[3], m_c[4]))
    return rmsnorm(xl, final_norm_w)
```

```python
import functools
import math

import jax
import jax.numpy as jnp
import numpy as np
from jax import lax
from jax.experimental import pallas as pl
from jax.experimental.pallas import tpu as pltpu

F32 = jnp.float32
BF16 = jnp.bfloat16

EPS = 1e-6
GRID_W = 64
CHUNK = 128
LANES = 128
HY_D = 1024
HY_EMB = 33
HY_FAST_DECAY = 0.3
HY_SLOW_DECAY = 1.5
HY_DECAY_TARGET = 1e-2
ML_H = 8
ML_DH = 128
SSD_H = 16
SSD_P = 64
S5_W = 512
S5_G = 32
S5_GS = 16
S5_P = 64
N_EXPERTS = 8
VMEM_LIMIT_BYTES = 56 * 1024 * 1024


def _cparams(n_axes):
    return pltpu.CompilerParams(dimension_semantics=("arbitrary",) * n_axes,
                                vmem_limit_bytes=VMEM_LIMIT_BYTES)


def _sigmoid(x):
    return 1.0 / (1.0 + jnp.exp(-x))


def _silu(x):
    return x * _sigmoid(x)


def _log_sigmoid(x):
    return jnp.minimum(x, 0.0) - jnp.log(1.0 + jnp.exp(-jnp.abs(x)))


def _softplus(x):
    return jnp.maximum(x, 0.0) + jnp.log(1.0 + jnp.exp(-jnp.abs(x)))


def _dot(a, b):
    return jnp.dot(a, b, preferred_element_type=F32)


def _dot_nt(a, b):
    return lax.dot_general(a, b, (((1,), (1,)), ((), ())), preferred_element_type=F32)


def _dot_tn(a, b):
    return lax.dot_general(a, b, (((0,), (0,)), ((), ())), preferred_element_type=F32)


def _dot_hi(a, b):
    return jnp.dot(a, b, preferred_element_type=F32, precision=lax.Precision.HIGHEST)


class Rows:
    def __init__(self, bsz, L, Lc):
        self.B, self.L, self.Lc = bsz, L, Lc
        self.n_lat = bsz * L
        self.n = bsz * (L + Lc)

    def mod_row(self, i, bm):
        n_lat_blocks = self.n_lat // bm
        return jnp.where(i < n_lat_blocks, (i * bm) // self.L, self.B)

    def chunk_block(self, b, c):
        ncc = self.Lc // CHUNK
        return jnp.where(c < ncc, self.n_lat // CHUNK + b * ncc + c, b * (self.L // CHUNK) + c - ncc)

    def scan_chunk(self, d, j):
        ncc = self.Lc // CHUNK
        nct = (self.L + self.Lc) // CHUNK
        back = jnp.where(j < ncc, ncc - 1 - j, nct - 1 + ncc - j)
        return jnp.where(d == 0, j, back)


def _adaln_kernel(c_ref, w_ref, b_ref, o_ref):
    cond = _silu(c_ref[...]).astype(BF16)
    o_ref[...] = _dot(cond, w_ref[...].astype(BF16)) + b_ref[...]


def adaln(cond, ada_w, ada_b, layer, bn=1024):
    bn = min(bn, ada_w.shape[1])
    _, D, N = ada_w.shape
    return pl.pallas_call(
        _adaln_kernel,
        grid=(N // bn,),
        in_specs=[pl.BlockSpec((8, D), lambda n: (0, 0)),
                  pl.BlockSpec((None, D, bn), lambda n: (layer, 0, n)),
                  pl.BlockSpec((None, 1, bn), lambda n: (layer, 0, n))],
        out_specs=pl.BlockSpec((8, bn), lambda n: (0, n)),
        out_shape=jax.ShapeDtypeStruct((8, N), F32),
        compiler_params=_cparams(1),
        name="adaln",
    )(cond, ada_w, ada_b.reshape(ada_b.shape[0], 1, N))


def _modulate_kernel(x_ref, sh_ref, sc_ref, o_ref):
    x = x_ref[...]
    y = x * lax.rsqrt(jnp.mean(x * x, axis=-1, keepdims=True) + EPS)
    o_ref[...] = (y * (1.0 + sc_ref[...]) + sh_ref[...]).astype(o_ref.dtype)


def modulate(x, shift, scale, rows, n_rows, out_dtype, bm=512):
    D = x.shape[1]
    mod_spec = pl.BlockSpec((None, 1, D), lambda i: (rows.mod_row(i, bm), 0, 0))
    return pl.pallas_call(
        _modulate_kernel,
        grid=(n_rows // bm,),
        in_specs=[pl.BlockSpec((bm, D), lambda i: (i, 0)), mod_spec, mod_spec],
        out_specs=pl.BlockSpec((bm, D), lambda i: (i, 0)),
        out_shape=jax.ShapeDtypeStruct((n_rows, D), out_dtype),
        compiler_params=_cparams(1),
        name="modulate",
    )(x, shift, scale)


def _mm_kernel(x_ref, w_ref, o_ref, wb_ref):
    @pl.when(pl.program_id(1) == 0)
    def _():
        wb_ref[...] = w_ref[...].astype(BF16)
    o_ref[...] = _dot(x_ref[...], wb_ref[...]).astype(o_ref.dtype)


def mm(x, w, widx, *, n_cols, col_off=0, bm, bn, out_dtype=F32):
    M, K = x.shape
    off = col_off // bn
    return pl.pallas_call(
        _mm_kernel,
        grid=(n_cols // bn, M // bm),
        in_specs=[pl.BlockSpec((bm, K), lambda n, m: (m, 0)),
                  pl.BlockSpec((None, K, bn), lambda n, m: (widx, 0, n + off))],
        out_specs=pl.BlockSpec((bm, bn), lambda n, m: (m, n)),
        out_shape=jax.ShapeDtypeStruct((M, n_cols), out_dtype),
        scratch_shapes=[pltpu.VMEM((K, bn), BF16)],
        compiler_params=_cparams(2),
        name="mm",
    )(x, w)


def _mm_res_kernel(x_ref, w_ref, r_ref, g_ref, o_ref, wb_ref):
    @pl.when(pl.program_id(1) == 0)
    def _():
        wb_ref[...] = w_ref[...].astype(BF16)
    o_ref[...] = r_ref[...] + g_ref[...] * _dot(x_ref[...], wb_ref[...])


def mm_residual(x, w, widx, res, gate, rows, *, bm, bn):
    M, K = x.shape
    N = w.shape[2]
    return pl.pallas_call(
        _mm_res_kernel,
        grid=(N // bn, M // bm),
        in_specs=[pl.BlockSpec((bm, K), lambda n, m: (m, 0)),
                  pl.BlockSpec((None, K, bn), lambda n, m: (widx, 0, n)),
                  pl.BlockSpec((bm, bn), lambda n, m: (m, n)),
                  pl.BlockSpec((None, 1, bn), lambda n, m: (rows.mod_row(m, bm), 0, n))],
        out_specs=pl.BlockSpec((bm, bn), lambda n, m: (m, n)),
        out_shape=jax.ShapeDtypeStruct((M, N), F32),
        scratch_shapes=[pltpu.VMEM((K, bn), BF16)],
        compiler_params=_cparams(2),
        name="mm_residual",
    )(x, w, res, gate)


def _mm_swiglu_kernel(x_ref, wg_ref, wu_ref, o_ref, wgb_ref, wub_ref):
    @pl.when(pl.program_id(1) == 0)
    def _():
        wgb_ref[...] = wg_ref[...].astype(BF16)
        wub_ref[...] = wu_ref[...].astype(BF16)
    x = x_ref[...]
    g = _dot(x, wgb_ref[...])
    u = _dot(x, wub_ref[...])
    o_ref[...] = (_silu(g) * u).astype(o_ref.dtype)


def mm_swiglu(x, w_gu, widx, *, bm, bn):
    M, K = x.shape
    F = w_gu.shape[2] // 2
    nf = F // bn
    return pl.pallas_call(
        _mm_swiglu_kernel,
        grid=(nf, M // bm),
        in_specs=[pl.BlockSpec((bm, K), lambda n, m: (m, 0)),
                  pl.BlockSpec((None, K, bn), lambda n, m: (widx, 0, n)),
                  pl.BlockSpec((None, K, bn), lambda n, m: (widx, 0, n + nf))],
        out_specs=pl.BlockSpec((bm, bn), lambda n, m: (m, n)),
        out_shape=jax.ShapeDtypeStruct((M, F), BF16),
        scratch_shapes=[pltpu.VMEM((K, bn), BF16), pltpu.VMEM((K, bn), BF16)],
        compiler_params=_cparams(2),
        name="mm_swiglu",
    )(x, w_gu, w_gu)


def _conv3(x, w_ref, b_ref, period):
    n = x.shape[0]
    pos = lax.broadcasted_iota(jnp.int32, x.shape, 0) & (period - 1)
    prev = jnp.where(pos == 0, 0.0, pltpu.roll(x, 1, 0))
    nxt = jnp.where(pos == period - 1, 0.0, pltpu.roll(x, n - 1, 0))
    w = w_ref[...]
    return b_ref[...] + prev * w[0:1] + x * w[1:2] + nxt * w[2:3]


def _conv_silu_kernel(x_ref, w_ref, b_ref, o_ref, *, n_lat_blocks, lat_period, ctx_period):
    period = jnp.where(pl.program_id(0) < n_lat_blocks, lat_period, ctx_period)
    o_ref[...] = _silu(_conv3(x_ref[...], w_ref, b_ref, period)).astype(o_ref.dtype)


def conv_silu(p, col_off, n_cols, w, b, widx, rows, *, bt=256, bc=512):
    off = col_off // bc
    kern = functools.partial(_conv_silu_kernel, n_lat_blocks=rows.n_lat // bt,
                             lat_period=GRID_W, ctx_period=rows.Lc)
    return pl.pallas_call(
        kern,
        grid=(rows.n // bt, n_cols // bc),
        in_specs=[pl.BlockSpec((bt, bc), lambda i, j: (i, j + off)),
                  pl.BlockSpec((None, 3, bc), lambda i, j: (widx, 0, j)),
                  pl.BlockSpec((None, 1, bc), lambda i, j: (widx, 0, j))],
        out_specs=pl.BlockSpec((bt, bc), lambda i, j: (i, j)),
        out_shape=jax.ShapeDtypeStruct((rows.n, n_cols), F32),
        compiler_params=_cparams(2),
        name="conv_silu",
    )(p, w, b.reshape(b.shape[0], 1, b.shape[1]))


def _hyena_conv_kernel(p0_ref, p1_ref, p2_ref, w0_ref, w1_ref, w2_ref, b0_ref, b1_ref, b2_ref,
                       x0_ref, u_ref, *, period):
    x0_ref[...] = _conv3(p0_ref[...], w0_ref, b0_ref, period)
    x1 = _conv3(p1_ref[...], w1_ref, b1_ref, period)
    v = _conv3(p2_ref[...], w2_ref, b2_ref, period)
    u_ref[...] = (x1 * v).astype(u_ref.dtype)


def hyena_conv(p, w, b, widx, *, row0, bsz, seq, period, bt=256, bc=512):
    nt = seq // bt
    ncb = HY_D // bc
    rb0 = row0 // bt

    def pspec(k):
        return pl.BlockSpec((bt, bc), lambda bb, i, j: (rb0 + bb * nt + i, j + k * ncb))

    def wspec(k):
        return pl.BlockSpec((None, 3, bc), lambda bb, i, j: (widx, 0, j + k * ncb))

    def bspec(k):
        return pl.BlockSpec((None, 1, bc), lambda bb, i, j: (widx, 0, j + k * ncb))

    ospec = pl.BlockSpec((bt, bc), lambda bb, i, j: (i, bb * ncb + j))
    b3 = b.reshape(b.shape[0], 1, b.shape[1])
    return pl.pallas_call(
        functools.partial(_hyena_conv_kernel, period=period),
        grid=(bsz, nt, ncb),
        in_specs=[pspec(0), pspec(1), pspec(2), wspec(0), wspec(1), wspec(2), bspec(0), bspec(1), bspec(2)],
        out_specs=[ospec, ospec],
        out_shape=[jax.ShapeDtypeStruct((seq, bsz * HY_D), F32),
                   jax.ShapeDtypeStruct((seq, bsz * HY_D), BF16)],
        compiler_params=_cparams(3),
        name="hyena_conv",
    )(p, p, p, w, w, w, b3, b3, b3)


def _hyena_feats(L):
    pos = np.arange(L, dtype=np.float64)
    t = pos / max(L - 1, 1)
    n_bands = (HY_EMB - 1) // 2
    bands = np.linspace(1e-4, n_bands - 1, n_bands)
    ang = (2 * math.pi / L) * pos[:, None] * bands[None, :]
    feats = np.concatenate([t[:, None], np.cos(ang), -np.sin(ang)], axis=-1)
    feats = np.pad(feats, ((0, 0), (0, LANES - HY_EMB)))
    deltas = np.abs(np.linspace(math.log(HY_DECAY_TARGET) / HY_SLOW_DECAY,
                                math.log(HY_DECAY_TARGET) / HY_FAST_DECAY, HY_D))
    return feats.astype(np.float32), t.astype(np.float32)[:, None], deltas.astype(np.float32)[None, :]


def _hyena_filter_kernel(feats_ref, t_ref, dl_ref, w1_ref, b1_ref, w2_ref, b2_ref, w3f_ref, w3b_ref,
                         fq_ref, hf_ref, hb_ref):
    fq = fq_ref[...]
    h = jnp.sin(fq[0:1] * (_dot_hi(feats_ref[...], w1_ref[...]) + b1_ref[...]))
    h = jnp.sin(fq[1:2] * (_dot_hi(h, w2_ref[...]) + b2_ref[...]))
    win = jnp.exp(-t_ref[...] * dl_ref[...])
    h_f = _dot_hi(h, w3f_ref[...]) * win
    h_b = _dot_hi(h, w3b_ref[...]) * win
    row = lax.broadcasted_iota(jnp.int32, h_b.shape, 0)
    h_b = jnp.where(row == 0, 0.0, h_b)
    l1 = jnp.sum(jnp.abs(h_f), axis=0, keepdims=True) + jnp.sum(jnp.abs(h_b), axis=0, keepdims=True)
    hf_ref[...] = h_f / l1
    hb_ref[...] = h_b / l1


def hyena_filter(L, w1, b1, w2, b2, w3, freq, widx, bc=256):
    feats, t, deltas = _hyena_feats(L)
    hid = w2.shape[1]
    w1p = jnp.pad(w1[widx], ((0, LANES - HY_EMB), (0, 0)))
    ncb = HY_D // bc
    full = lambda shape: pl.BlockSpec(shape, lambda j: (0,) * len(shape))
    return pl.pallas_call(
        _hyena_filter_kernel,
        grid=(ncb,),
        in_specs=[full((L, LANES)), full((L, 1)), pl.BlockSpec((1, bc), lambda j: (0, j)),
                  full((LANES, hid)), full((1, hid)), full((hid, hid)), full((1, hid)),
                  pl.BlockSpec((hid, bc), lambda j: (0, j)),
                  pl.BlockSpec((hid, bc), lambda j: (0, j + ncb)),
                  full((2, hid))],
        out_specs=[pl.BlockSpec((L, bc), lambda j: (0, j)), pl.BlockSpec((L, bc), lambda j: (0, j))],
        out_shape=[jax.ShapeDtypeStruct((L, HY_D), F32), jax.ShapeDtypeStruct((L, HY_D), F32)],
        compiler_params=_cparams(1),
        name="hyena_filter",
    )(jnp.asarray(feats), jnp.asarray(t), jnp.asarray(deltas), w1p, b1[widx][None], w2[widx],
      b2[widx][None], w3[widx], w3[widx], freq[widx])


def _dft_matrix(L):
    f = np.arange(L, dtype=np.int64)[:, None]
    s = np.arange(L, dtype=np.int64)[None, :]
    ang = (math.pi / L) * ((f * s) % (2 * L)).astype(np.float64)
    a_cos = np.cos(ang)
    a_sin = -np.sin(ang)
    a_sin[0, :] = np.where(np.arange(L) % 2 == 0, 1.0, -1.0)
    return np.concatenate([a_cos, a_sin], axis=0).astype(np.float32)


def _spectrum_kernel(ac_ref, as_ref, hf_ref, hb_ref, kre_ref, kim_ref, kny_ref, *, L, bf):
    hf = hf_ref[...].astype(BF16)
    hb = hb_ref[...].astype(BF16)
    ac = ac_ref[...].astype(BF16)
    a_s = as_ref[...].astype(BF16)
    cf, cb = _dot(ac, hf), _dot(ac, hb)
    sf, sb = _dot(a_s, hf), _dot(a_s, hb)
    f = lax.broadcasted_iota(jnp.int32, cf.shape, 0) + pl.program_id(1) * bf
    wgt = jnp.where(f == 0, 0.5 / L, 1.0 / L)
    kre_ref[...] = (cf + cb) * wgt
    kim_ref[...] = jnp.where(f == 0, 0.0, (sf - sb) * wgt)
    kny_ref[...] = jnp.where(f == 0, (sf + sb) * wgt, (cf + cb) * wgt)


def hyena_spectrum(a_mat, hf, hb, L, bf=256, bc=512):
    bf = min(bf, L)
    nf = L // bf
    C = hf.shape[1]
    ospec = pl.BlockSpec((bf, bc), lambda c, i: (i, c))
    hspec = pl.BlockSpec((L, bc), lambda c, i: (0, c))
    return pl.pallas_call(
        functools.partial(_spectrum_kernel, L=L, bf=bf),
        grid=(C // bc, nf),
        in_specs=[pl.BlockSpec((bf, L), lambda c, i: (i, 0)), pl.BlockSpec((bf, L), lambda c, i: (i + nf, 0)),
                  hspec, hspec],
        out_specs=[ospec, ospec, ospec],
        out_shape=[jax.ShapeDtypeStruct((L, C), F32)] * 3,
        compiler_params=_cparams(2),
        name="hyena_spectrum",
    )(a_mat, a_mat, hf, hb)


def _lc_fwd_kernel(ac_ref, as_ref, u_ref, kre_ref, kim_ref, kny_ref, yre_ref, yim_ref, acb_ref, asb_ref):
    @pl.when(pl.program_id(1) == 0)
    def _():
        acb_ref[...] = ac_ref[...].astype(BF16)
        asb_ref[...] = as_ref[...].astype(BF16)
    u = u_ref[...]
    ure = _dot(acb_ref[...], u)
    uim = _dot(asb_ref[...], u)
    kim = kim_ref[...]
    yre_ref[...] = (kre_ref[...] * ure - kim * uim).astype(yre_ref.dtype)
    yim_ref[...] = (kny_ref[...] * uim + kim * ure).astype(yim_ref.dtype)


def long_conv_fwd(a_mat, u, kre, kim, kny, L, bsz, bf=256):
    bf = min(bf, L)
    nf = L // bf
    C = kre.shape[1]
    kspec = pl.BlockSpec((bf, C), lambda i, b: (i, 0))
    ospec = pl.BlockSpec((bf, C), lambda i, b: (i, b))
    return pl.pallas_call(
        _lc_fwd_kernel,
        grid=(nf, bsz),
        in_specs=[pl.BlockSpec((bf, L), lambda i, b: (i, 0)), pl.BlockSpec((bf, L), lambda i, b: (i + nf, 0)),
                  pl.BlockSpec((L, C), lambda i, b: (0, b)), kspec, kspec, kspec],
        out_specs=[ospec, ospec],
        out_shape=[jax.ShapeDtypeStruct((L, bsz * C), BF16)] * 2,
        scratch_shapes=[pltpu.VMEM((bf, L), BF16), pltpu.VMEM((bf, L), BF16)],
        compiler_params=_cparams(2),
        name="long_conv_fwd",
    )(a_mat, a_mat, u, kre, kim, kny)


def _lc_inv_kernel(atc_ref, ats_ref, yre_ref, yim_ref, x0_ref, u_ref, bias_ref, o_ref, atcb_ref, atsb_ref):
    @pl.when(pl.program_id(1) == 0)
    def _():
        atcb_ref[...] = atc_ref[...].astype(BF16)
        atsb_ref[...] = ats_ref[...].astype(BF16)
    y = _dot(atcb_ref[...], yre_ref[...]) + _dot(atsb_ref[...], yim_ref[...])
    u = u_ref[...].astype(F32)
    o_ref[...] = (x0_ref[...] * (y + bias_ref[...] * u)).astype(o_ref.dtype)


def long_conv_inv(at_mat, yre, yim, x0, u, bias, widx, L, bsz, bt=256):
    bt = min(bt, L)
    nt = L // bt
    C = bias.shape[1]
    tspec = pl.BlockSpec((bt, C), lambda i, b: (i, b))
    yspec = pl.BlockSpec((L, C), lambda i, b: (0, b))
    return pl.pallas_call(
        _lc_inv_kernel,
        grid=(nt, bsz),
        in_specs=[pl.BlockSpec((bt, L), lambda i, b: (i, 0)), pl.BlockSpec((bt, L), lambda i, b: (i, 1)),
                  yspec, yspec, tspec, tspec, pl.BlockSpec((None, 1, C), lambda i, b: (widx, 0, 0))],
        out_specs=tspec,
        out_shape=jax.ShapeDtypeStruct((L, bsz * C), BF16),
        scratch_shapes=[pltpu.VMEM((bt, L), BF16), pltpu.VMEM((bt, L), BF16)],
        compiler_params=_cparams(2),
        name="long_conv_inv",
    )(at_mat, at_mat, yre, yim, x0, u, bias.reshape(bias.shape[0], 1, C))


def hyena(p, L, bsz, row0, period, widx, conv_w, conv_b, filt, bias):
    a_np = _dft_matrix(L)
    a_mat = jnp.asarray(a_np)
    at_mat = jnp.asarray(np.ascontiguousarray(a_np.T))
    x0, u = hyena_conv(p, conv_w, conv_b, widx, row0=row0, bsz=bsz, seq=L, period=period, bt=min(256, L))
    hf, hb = hyena_filter(L, *filt, widx)
    kre, kim, kny = hyena_spectrum(a_mat, hf, hb, L)
    yre, yim = long_conv_fwd(a_mat, u, kre, kim, kny, L, bsz)
    return long_conv_inv(at_mat, yre, yim, x0, u, bias, widx, L, bsz)


def _pick_col(g, idx):
    lane = lax.broadcasted_iota(jnp.int32, g.shape, 1)
    return jnp.sum(jnp.where(lane == idx, g, 0.0), axis=1, keepdims=True)


def _chunk_masks(d, n):
    t_i = lax.broadcasted_iota(jnp.int32, (n, n), 0)
    s_i = lax.broadcasted_iota(jnp.int32, (n, n), 1)
    lag = (t_i - s_i) * jnp.where(d == 0, 1, -1)
    return lag >= 0, lag <= 0


def _cumsums(x_row, x_col, causal, causal_t):
    col = jnp.sum(jnp.where(causal, x_row, 0.0), axis=1, keepdims=True)
    row = jnp.sum(jnp.where(causal_t, x_col, 0.0), axis=0, keepdims=True)
    tot = jnp.sum(x_row, axis=1, keepdims=True)
    return col, row, tot


def _mlstm_kernel(gb_ref, q_ref, k_ref, v_ref, gc_ref, gr_ref, o_ref, c_ref, n_ref, m_ref):
    h = pl.program_id(1)
    d = pl.program_id(2)

    @pl.when(pl.program_id(3) == 0)
    def _():
        c_ref[...] = jnp.zeros_like(c_ref)
        n_ref[...] = jnp.zeros_like(n_ref)
        m_ref[...] = jnp.zeros_like(m_ref)

    bi = gb_ref[2 * d, h]
    bf = gb_ref[2 * d + 1, h]
    i_idx = 2 * ML_H * d + h
    f_idx = i_idx + ML_H
    ig_row = gr_ref[pl.ds(i_idx, 1), :] + bi
    lf_row = _log_sigmoid(gr_ref[pl.ds(f_idx, 1), :] + bf)
    gcol = gc_ref[...]
    ig_col = _pick_col(gcol, i_idx) + bi
    lf_col = _log_sigmoid(_pick_col(gcol, f_idx) + bf)

    causal, causal_t = _chunk_masks(d, CHUNK)
    bcum_col, bcum_row, g = _cumsums(lf_row, lf_col, causal, causal_t)

    q = q_ref[...] * (ML_DH ** -0.5)
    k = k_ref[...]
    v = v_ref[...]
    qb, kb, vb = q.astype(BF16), k.astype(BF16), v.astype(BF16)
    c_in, n_in, m_in = c_ref[...], n_ref[...], m_ref[0:1, 0:1]

    dmat = jnp.where(causal, bcum_col - bcum_row + ig_row, -jnp.inf)
    m_inter = bcum_col + m_in
    m_t = jnp.maximum(jnp.max(dmat, axis=1, keepdims=True), m_inter)
    s = _dot_nt(qb, kb) * jnp.exp(dmat - m_t)
    w_inter = jnp.exp(m_inter - m_t)
    num = _dot(s.astype(BF16), vb) + w_inter * _dot(qb, c_in.astype(BF16))
    den = jnp.sum(s, axis=1, keepdims=True) + w_inter * jnp.sum(q * n_in, axis=1, keepdims=True)
    o_ref[...] = num / jnp.maximum(jnp.abs(den), jnp.exp(-m_t))

    a_row = g - bcum_row + ig_row
    a_col = g - bcum_col + ig_col
    m_loc = jnp.max(a_row, axis=1, keepdims=True)
    kw = k * jnp.exp(a_col - m_loc)
    c_loc = _dot_tn(kw.astype(BF16), vb)
    n_loc = jnp.sum(kw, axis=0, keepdims=True)
    m_new = jnp.maximum(g + m_in, m_loc)
    s_prev = jnp.exp(g + m_in - m_new)
    s_loc = jnp.exp(m_loc - m_new)
    c_ref[...] = s_prev * c_in + s_loc * c_loc
    n_ref[...] = s_prev * n_in + s_loc * n_loc
    m_ref[...] = jnp.broadcast_to(m_new, m_ref.shape)


def mlstm(qk, p, v_col_off, gates, gates_t, gate_b, rows):
    nct = (rows.L + rows.Lc) // CHUNK
    voff = v_col_off // ML_DH

    def rb(b, d, j):
        return rows.chunk_block(b, rows.scan_chunk(d, j))

    return pl.pallas_call(
        _mlstm_kernel,
        grid=(rows.B, ML_H, 2, nct),
        in_specs=[pl.BlockSpec(memory_space=pltpu.SMEM),
                  pl.BlockSpec((CHUNK, ML_DH), lambda b, h, d, j: (rb(b, d, j), h)),
                  pl.BlockSpec((CHUNK, ML_DH), lambda b, h, d, j: (rb(b, d, j), ML_H + h)),
                  pl.BlockSpec((CHUNK, ML_DH), lambda b, h, d, j: (rb(b, d, j), voff + h)),
                  pl.BlockSpec((CHUNK, LANES), lambda b, h, d, j: (rb(b, d, j), 0)),
                  pl.BlockSpec((4 * ML_H, CHUNK), lambda b, h, d, j: (0, rb(b, d, j)))],
        out_specs=pl.BlockSpec((None, CHUNK, ML_DH), lambda b, h, d, j: (d, rb(b, d, j), h)),
        out_shape=jax.ShapeDtypeStruct((2, rows.n, ML_H * ML_DH), F32),
        scratch_shapes=[pltpu.VMEM((ML_DH, ML_DH), F32), pltpu.VMEM((1, ML_DH), F32), pltpu.VMEM((8, LANES), F32)],
        compiler_params=_cparams(4),
        name="mlstm",
    )(gate_b, qk, qk, p, gates, gates_t)


def _finish_ab_kernel(yh_ref, h_ref, o_ref, nw_ref, out_ref):
    out_ref[:, 0:HY_D] = yh_ref[...]
    hs = h_ref[0] + h_ref[1]
    og = _sigmoid(o_ref[...])
    nw = nw_ref[...]
    for i in range(ML_H):
        sl = slice(i * ML_DH, (i + 1) * ML_DH)
        x = hs[:, sl]
        mu = jnp.mean(x, axis=-1, keepdims=True)
        xc = x - mu
        var = jnp.mean(xc * xc, axis=-1, keepdims=True)
        y = xc * lax.rsqrt(var + EPS) * nw[:, sl] * og[:, sl]
        out_ref[:, HY_D + i * ML_DH:HY_D + (i + 1) * ML_DH] = y.astype(out_ref.dtype)


def finish_ab(yh_lat, yh_ctx, hdir, p, o_col_off, norm_w, widx, rows, bt=256):
    n_lat_blocks = rows.n_lat // bt
    per_b = rows.L // bt
    ooff = o_col_off // (ML_H * ML_DH)
    W = ML_H * ML_DH

    def call(yh, row_block0, n_blocks, yh_map):
        return pl.pallas_call(
            _finish_ab_kernel,
            grid=(n_blocks,),
            in_specs=[pl.BlockSpec((bt, HY_D), yh_map),
                      pl.BlockSpec((2, bt, W), lambda i: (0, row_block0 + i, 0)),
                      pl.BlockSpec((bt, W), lambda i: (row_block0 + i, ooff)),
                      pl.BlockSpec((None, 1, W), lambda i: (widx, 0, 0))],
            out_specs=pl.BlockSpec((bt, HY_D + W), lambda i: (i, 0)),
            out_shape=jax.ShapeDtypeStruct((n_blocks * bt, HY_D + W), BF16),
            compiler_params=_cparams(1),
            name="finish_ab",
        )(yh, hdir, p, norm_w.reshape(norm_w.shape[0], 1, W))

    lat = call(yh_lat, 0, n_lat_blocks, lambda i: (i % per_b, i // per_b))
    ctx = call(yh_ctx, n_lat_blocks, (rows.n - rows.n_lat) // bt, lambda i: (0, i))
    return jnp.concatenate([lat, ctx], axis=0)


def _ssd_kernel(al_ref, db_ref, x_ref, b_ref, c_ref, dc_ref, dr_ref, o_ref, h_ref):
    d = pl.program_id(1)
    hp = pl.program_id(2)

    @pl.when(pl.program_id(3) == 0)
    def _():
        h_ref[...] = jnp.zeros_like(h_ref)

    causal, causal_t = _chunk_masks(d, CHUNK)
    bmat = b_ref[...].astype(BF16)
    cmat = c_ref[...].astype(BF16)
    cb = _dot_nt(cmat, bmat)
    dcol = dc_ref[...]
    x = x_ref[...]
    for i in range(2):
        hd = 2 * hp + i
        idx = SSD_H * d + hd
        neg_a = -jnp.exp(jnp.full((1, 1), al_ref[d, hd], F32))
        dt_row = _softplus(dr_ref[pl.ds(idx, 1), :] + db_ref[d, hd])
        dt_col = _softplus(_pick_col(dcol, idx) + db_ref[d, hd])
        acum_col, acum_row, tot = _cumsums(dt_row * neg_a, dt_col * neg_a, causal, causal_t)
        decay = jnp.exp(jnp.where(causal, acum_col - acum_row, -jnp.inf))
        xh = x[:, i * SSD_P:(i + 1) * SSD_P]
        h_in = h_ref[i]
        y = _dot((cb * decay * dt_row).astype(BF16), xh.astype(BF16))
        y = y + _dot_nt(cmat, h_in.astype(BF16)) * jnp.exp(acum_col)
        o_ref[:, i * SSD_P:(i + 1) * SSD_P] = y
        xw = xh * (jnp.exp(tot - acum_col) * dt_col)
        h_ref[i] = jnp.exp(tot) * h_in + _dot_tn(xw.astype(BF16), bmat)


def ssd(xbc, dt, dt_t, a_log, dt_bias, rows):
    nct = (rows.L + rows.Lc) // CHUNK
    n_pairs = SSD_H // 2
    pairs_per_group = n_pairs // 2
    xw = SSD_H * SSD_P // LANES

    def rb(b, d, j):
        return rows.chunk_block(b, rows.scan_chunk(d, j))

    return pl.pallas_call(
        _ssd_kernel,
        grid=(rows.B, 2, n_pairs, nct),
        in_specs=[pl.BlockSpec(memory_space=pltpu.SMEM), pl.BlockSpec(memory_space=pltpu.SMEM),
                  pl.BlockSpec((CHUNK, LANES), lambda b, d, hp, j: (rb(b, d, j), hp)),
                  pl.BlockSpec((CHUNK, LANES), lambda b, d, hp, j: (rb(b, d, j), xw + hp // pairs_per_group)),
                  pl.BlockSpec((CHUNK, LANES), lambda b, d, hp, j: (rb(b, d, j), xw + 2 + hp // pairs_per_group)),
                  pl.BlockSpec((CHUNK, LANES), lambda b, d, hp, j: (rb(b, d, j), 0)),
                  pl.BlockSpec((2 * SSD_H, CHUNK), lambda b, d, hp, j: (0, rb(b, d, j)))],
        out_specs=pl.BlockSpec((None, CHUNK, LANES), lambda b, d, hp, j: (d, rb(b, d, j), hp)),
        out_shape=jax.ShapeDtypeStruct((2, rows.n, SSD_H * SSD_P), F32),
        scratch_shapes=[pltpu.VMEM((2, SSD_P, LANES), F32)],
        compiler_params=_cparams(4),
        name="ssd",
    )(a_log, dt_bias, xbc, xbc, xbc, dt, dt_t)


S5_TC = 64
S5_LB = 512
S5_NS = S5_G * S5_P


def _s5_kernel(u_ref, wre_ref, wim_ref, cre_ref, cim_ref, lre_ref, lim_ref, y_ref,
               zre_ref, zim_ref, sre_ref, sim_ref):
    @pl.when(pl.program_id(0) == 0)
    def _():
        sre_ref[...] = jnp.zeros_like(sre_ref)
        sim_ref[...] = jnp.zeros_like(sim_ref)

    n_blk = S5_W // LANES
    sw = S5_NS // n_blk
    u = u_ref[...].astype(BF16)
    for j in range(n_blk):
        uj = u[:, j * LANES:(j + 1) * LANES]
        zre_ref[:, j * sw:(j + 1) * sw] = _dot(uj, wre_ref[j].astype(BF16))
        zim_ref[:, j * sw:(j + 1) * sw] = _dot(uj, wim_ref[j].astype(BF16))

    for lb in range(S5_NS // S5_LB):
        ls = slice(lb * S5_LB, (lb + 1) * S5_LB)
        ar = lre_ref[:, ls]
        ai = lim_ref[:, ls]

        def step(t, carry):
            zr, zi = carry
            r0 = pl.multiple_of(t * 8, 8)
            nr = ar * zr - ai * zi + zre_ref[pl.ds(r0, 8), ls]
            ni = ar * zi + ai * zr + zim_ref[pl.ds(r0, 8), ls]
            zre_ref[pl.ds(r0, 8), ls] = nr
            zim_ref[pl.ds(r0, 8), ls] = ni
            return nr, ni

        zr, zi = lax.fori_loop(0, S5_TC, step, (sre_ref[:, ls], sim_ref[:, ls]))
        sre_ref[:, ls] = zr
        sim_ref[:, ls] = zi

    row = lax.broadcasted_iota(jnp.int32, (S5_TC * 8, LANES), 0)
    is_bwd = (row & 4) != 0
    for j in range(n_blk):
        zr = zre_ref[:, j * sw:(j + 1) * sw].astype(BF16)
        zi = zim_ref[:, j * sw:(j + 1) * sw].astype(BF16)
        y0 = _dot(zr, cre_ref[0, j].astype(BF16)) - _dot(zi, cim_ref[0, j].astype(BF16))
        y1 = _dot(zr, cre_ref[1, j].astype(BF16)) - _dot(zi, cim_ref[1, j].astype(BF16))
        y_ref[:, j * LANES:(j + 1) * LANES] = jnp.where(is_bwd, y1, y0)


def s5_params(lam_re, lam_im, log_dt, b_re, b_im, c_re, c_im):
    dt = jnp.exp(log_dt)[..., None]
    mag = jnp.exp(lam_re * dt)
    lb_re, lb_im = mag * jnp.cos(lam_im * dt), mag * jnp.sin(lam_im * dt)
    den = lam_re * lam_re + lam_im * lam_im
    f_re = ((lb_re - 1) * lam_re + lb_im * lam_im) / den
    f_im = (lb_im * lam_re - (lb_re - 1) * lam_im) / den
    cf_re = c_re[None] * f_re[:, :, None, :] - c_im[None] * f_im[:, :, None, :]
    cf_im = c_re[None] * f_im[:, :, None, :] + c_im[None] * f_re[:, :, None, :]
    gpb = LANES // S5_GS
    n_blk = S5_G // gpb
    eye = jnp.eye(gpb, dtype=F32)

    def w_blocks(b):
        bb = b.reshape(n_blk, gpb, S5_P, S5_GS)
        return jnp.einsum('jgpi,gh->jgihp', bb, eye).reshape(n_blk, gpb * S5_GS, gpb * S5_P)

    def c_blocks(c):
        cc = c.reshape(2, n_blk, gpb, S5_GS, S5_P)
        return jnp.einsum('djgip,gh->djgphi', cc, eye).reshape(2, n_blk, gpb * S5_P, gpb * S5_GS)

    def lam_rows(l, bsz):
        return jnp.repeat(l.reshape(2, S5_NS), bsz, axis=0)

    return w_blocks(b_re), w_blocks(b_im), c_blocks(cf_re), c_blocks(cf_im), lb_re, lb_im, lam_rows


def s5(u_arr, wre, wim, cre, cim, lre, lim, n_steps):
    n_blk = S5_W // LANES
    sw = S5_NS // n_blk
    rt = S5_TC * 8
    full = lambda shape: pl.BlockSpec(shape, lambda i: (0,) * len(shape))
    return pl.pallas_call(
        _s5_kernel,
        grid=(n_steps // S5_TC,),
        in_specs=[pl.BlockSpec((rt, S5_W), lambda i: (i, 0)),
                  full((n_blk, LANES, sw)), full((n_blk, LANES, sw)),
                  full((2, n_blk, sw, LANES)), full((2, n_blk, sw, LANES)),
                  full((8, S5_NS)), full((8, S5_NS))],
        out_specs=pl.BlockSpec((rt, S5_W), lambda i: (i, 0)),
        out_shape=jax.ShapeDtypeStruct((n_steps * 8, S5_W), F32),
        scratch_shapes=[pltpu.VMEM((rt, S5_NS), F32), pltpu.VMEM((rt, S5_NS), F32),
                        pltpu.VMEM((8, S5_NS), F32), pltpu.VMEM((8, S5_NS), F32)],
        compiler_params=_cparams(1),
        name="s5",
    )(u_arr, wre, wim, cre, cim, lre, lim)


def _finish_cd_kernel(y_ref, xs_ref, z_ref, sf_ref, sb_ref, u_ref, dssd_ref, nw_ref, d5_ref, glu_ref, out_ref):
    y = (y_ref[0] + y_ref[1] + dssd_ref[...] * xs_ref[...]) * _silu(z_ref[...])
    y = y * lax.rsqrt(jnp.mean(y * y, axis=-1, keepdims=True) + EPS) * nw_ref[...]
    nd = y.shape[1]
    out_ref[:, 0:nd] = y.astype(out_ref.dtype)
    s = sf_ref[...] + sb_ref[...] + d5_ref[...] * u_ref[...]
    s = 0.5 * s * (1.0 + lax.erf(s * (2.0 ** -0.5)))
    gl = _dot(s.astype(BF16), glu_ref[...].astype(BF16))
    out_ref[:, nd:nd + S5_W] = (gl[:, 0:S5_W] * _sigmoid(gl[:, S5_W:])).astype(out_ref.dtype)


def finish_cd(ydir, xbc, p1, sf, sb, p2, ssd_d_lanes, norm_w, s5_d, glu_w, widx, n_rows, bt=256):
    nd = SSD_H * SSD_P
    vec = lambda n: pl.BlockSpec((None, 1, n), lambda i: (widx, 0, 0))
    return pl.pallas_call(
        _finish_cd_kernel,
        grid=(n_rows // bt,),
        in_specs=[pl.BlockSpec((2, bt, nd), lambda i: (0, i, 0)),
                  pl.BlockSpec((bt, nd), lambda i: (i, 0)),
                  pl.BlockSpec((bt, nd), lambda i: (i, 0)),
                  pl.BlockSpec((bt, S5_W), lambda i: (i, 0)),
                  pl.BlockSpec((bt, S5_W), lambda i: (i, 0)),
                  pl.BlockSpec((bt, S5_W), lambda i: (i, 0)),
                  vec(nd), vec(nd), vec(S5_W),
                  pl.BlockSpec((None, S5_W, 2 * S5_W), lambda i: (widx, 0, 0))],
        out_specs=pl.BlockSpec((bt, nd + S5_W), lambda i: (i, 0)),
        out_shape=jax.ShapeDtypeStruct((n_rows, nd + S5_W), BF16),
        compiler_params=_cparams(1),
        name="finish_cd",
    )(ydir, xbc, p1, sf, sb, p2, ssd_d_lanes, norm_w.reshape(norm_w.shape[0], 1, nd),
      s5_d.reshape(s5_d.shape[0], 1, S5_W), glu_w)


def _router_kernel(h_ref, w_ref, o_ref):
    logits = _dot_hi(h_ref[...], w_ref[...])
    lane = lax.broadcasted_iota(jnp.int32, logits.shape, 1)
    logits = jnp.where(lane < N_EXPERTS, logits, -jnp.inf)
    m1 = jnp.max(logits, axis=1, keepdims=True)
    i1 = jnp.min(jnp.where(logits == m1, lane, LANES), axis=1, keepdims=True)
    rest = jnp.where(lane == i1, -jnp.inf, logits)
    m2 = jnp.max(rest, axis=1, keepdims=True)
    i2 = jnp.min(jnp.where(rest == m2, lane, LANES), axis=1, keepdims=True)
    e2 = jnp.exp(m2 - m1)
    p1 = 1.0 / (1.0 + e2)
    p2 = e2 / (1.0 + e2)
    o_ref[...] = jnp.where(lane == 0, i1.astype(F32),
                           jnp.where(lane == 1, i2.astype(F32),
                                     jnp.where(lane == 2, p1, jnp.where(lane == 3, p2, 0.0))))


def router(h, w_router_padded, bt=512):
    T, D = h.shape
    return pl.pallas_call(
        _router_kernel,
        grid=(T // bt,),
        in_specs=[pl.BlockSpec((bt, D), lambda i: (i, 0)), pl.BlockSpec((D, LANES), lambda i: (0, 0))],
        out_specs=pl.BlockSpec((bt, LANES), lambda i: (i, 0)),
        out_shape=jax.ShapeDtypeStruct((T, LANES), F32),
        compiler_params=_cparams(1),
        name="router",
    )(h, w_router_padded)


def _gather_kernel(idx_ref, src_ref, o_ref, buf_ref, sem, *, bm):
    base = pl.program_id(0) * bm

    def row_copy(r):
        return pltpu.make_async_copy(src_ref.at[pl.ds(idx_ref[base + r], 1), :],
                                     buf_ref.at[pl.ds(r, 1), :], sem)

    def issue(r, c):
        row_copy(r).start()
        return c

    def drain(r, c):
        row_copy(r).wait()
        return c

    lax.fori_loop(0, bm, issue, 0)
    lax.fori_loop(0, bm, drain, 0)
    o_ref[...] = buf_ref[...].astype(o_ref.dtype)


def gather_rows(src, idx, bm=256):
    N = idx.shape[0]
    D = src.shape[1]
    return pl.pallas_call(
        functools.partial(_gather_kernel, bm=bm),
        grid_spec=pltpu.PrefetchScalarGridSpec(
            num_scalar_prefetch=1,
            grid=(N // bm,),
            in_specs=[pl.BlockSpec(memory_space=pl.ANY)],
            out_specs=pl.BlockSpec((bm, D), lambda i, idx: (i, 0)),
            scratch_shapes=[pltpu.VMEM((bm, D), F32), pltpu.SemaphoreType.DMA(())]),
        out_shape=jax.ShapeDtypeStruct((N, D), BF16),
        compiler_params=_cparams(1),
        name="gather_rows",
    )(idx, src)


def _gmm_swiglu_kernel(te_ref, na_ref, x_ref, wg_ref, wu_ref, o_ref, wgb_ref, wub_ref):
    m = pl.program_id(1)
    changed = (m == 0) | (te_ref[m] != te_ref[jnp.maximum(m - 1, 0)])

    @pl.when(changed)
    def _():
        wgb_ref[...] = wg_ref[...].astype(BF16)
        wub_ref[...] = wu_ref[...].astype(BF16)

    @pl.when(m < na_ref[0])
    def _():
        x = x_ref[...]
        g = _dot(x, wgb_ref[...])
        u = _dot(x, wub_ref[...])
        o_ref[...] = (_silu(g) * u).astype(o_ref.dtype)

    @pl.when(m >= na_ref[0])
    def _():
        o_ref[...] = jnp.zeros_like(o_ref)


def gmm_swiglu(x, w_gu, widx, tile_expert, n_active, *, bm, bn):
    M, K = x.shape
    F = w_gu.shape[3] // 2
    nf = F // bn
    return pl.pallas_call(
        _gmm_swiglu_kernel,
        grid_spec=pltpu.PrefetchScalarGridSpec(
            num_scalar_prefetch=2,
            grid=(nf, M // bm),
            in_specs=[pl.BlockSpec((bm, K), lambda n, m, te, na: (m, 0)),
                      pl.BlockSpec((None, None, K, bn), lambda n, m, te, na: (widx, te[m], 0, n)),
                      pl.BlockSpec((None, None, K, bn), lambda n, m, te, na: (widx, te[m], 0, n + nf))],
            out_specs=pl.BlockSpec((bm, bn), lambda n, m, te, na: (m, n)),
            scratch_shapes=[pltpu.VMEM((K, bn), BF16), pltpu.VMEM((K, bn), BF16)]),
        out_shape=jax.ShapeDtypeStruct((M, F), BF16),
        compiler_params=_cparams(2),
        name="gmm_swiglu",
    )(tile_expert, n_active, x, w_gu, w_gu)


def _gmm_kernel(te_ref, na_ref, x_ref, w_ref, o_ref, wb_ref):
    m = pl.program_id(1)
    changed = (m == 0) | (te_ref[m] != te_ref[jnp.maximum(m - 1, 0)])

    @pl.when(changed)
    def _():
        wb_ref[...] = w_ref[...].astype(BF16)

    @pl.when(m < na_ref[0])
    def _():
        o_ref[...] = _dot(x_ref[...], wb_ref[...])

    @pl.when(m >= na_ref[0])
    def _():
        o_ref[...] = jnp.zeros_like(o_ref)


def gmm(x, w, widx, tile_expert, n_active, *, bm, bn):
    M, K = x.shape
    N = w.shape[3]
    return pl.pallas_call(
        _gmm_kernel,
        grid_spec=pltpu.PrefetchScalarGridSpec(
            num_scalar_prefetch=2,
            grid=(N // bn, M // bm),
            in_specs=[pl.BlockSpec((bm, K), lambda n, m, te, na: (m, 0)),
                      pl.BlockSpec((None, None, K, bn), lambda n, m, te, na: (widx, te[m], 0, n))],
            out_specs=pl.BlockSpec((bm, bn), lambda n, m, te, na: (m, n)),
            scratch_shapes=[pltpu.VMEM((K, bn), BF16)]),
        out_shape=jax.ShapeDtypeStruct((M, N), F32),
        compiler_params=_cparams(2),
        name="gmm",
    )(tile_expert, n_active, x, w)


def _combine_kernel(pos_ref, y_ref, x_ref, pr_ref, g_ref, nw_ref, o_ref, b1_ref, b2_ref, sem, *, bm, n_tok):
    base = pl.program_id(0) * bm

    def copies(r):
        c1 = pltpu.make_async_copy(y_ref.at[pl.ds(pos_ref[base + r], 1), :], b1_ref.at[pl.ds(r, 1), :], sem.at[0])
        c2 = pltpu.make_async_copy(y_ref.at[pl.ds(pos_ref[n_tok + base + r], 1), :],
                                   b2_ref.at[pl.ds(r, 1), :], sem.at[1])
        return c1, c2

    def issue(r, c):
        c1, c2 = copies(r)
        c1.start()
        c2.start()
        return c

    def drain(r, c):
        c1, c2 = copies(r)
        c1.wait()
        c2.wait()
        return c

    lax.fori_loop(0, bm, issue, 0)
    lax.fori_loop(0, bm, drain, 0)
    pr = pr_ref[...]
    mix = pr[:, 2:3] * b1_ref[...] + pr[:, 3:4] * b2_ref[...]
    x = x_ref[...] + g_ref[...] * mix
    o_ref[...] = x * lax.rsqrt(jnp.mean(x * x, axis=-1, keepdims=True) + EPS) * nw_ref[...]


def moe_combine(ys, pos, x, route, gate, norm_w, rows, bm=256):
    T, D = x.shape
    return pl.pallas_call(
        functools.partial(_combine_kernel, bm=bm, n_tok=T),
        grid_spec=pltpu.PrefetchScalarGridSpec(
            num_scalar_prefetch=1,
            grid=(T // bm,),
            in_specs=[pl.BlockSpec(memory_space=pl.ANY),
                      pl.BlockSpec((bm, D), lambda i, pos: (i, 0)),
                      pl.BlockSpec((bm, LANES), lambda i, pos: (i, 0)),
                      pl.BlockSpec((None, 1, D), lambda i, pos: (rows.mod_row(i, bm), 0, 0)),
                      pl.BlockSpec((1, D), lambda i, pos: (0, 0))],
            out_specs=pl.BlockSpec((bm, D), lambda i, pos: (i, 0)),
            scratch_shapes=[pltpu.VMEM((bm, D), F32), pltpu.VMEM((bm, D), F32), pltpu.SemaphoreType.DMA((2,))]),
        out_shape=jax.ShapeDtypeStruct((T, D), F32),
        compiler_params=_cparams(1),
        name="moe_combine",
    )(pos, ys, x, route, gate, norm_w[None])


def moe_plan(route, n_tok, bm):
    e = jnp.concatenate([route[:, 0], route[:, 1]]).astype(jnp.int32)
    onehot = (e[:, None] == jnp.arange(N_EXPERTS, dtype=jnp.int32)[None, :]).astype(jnp.int32)
    rank = jnp.take_along_axis(jnp.cumsum(onehot, axis=0), e[:, None], axis=1)[:, 0] - 1
    counts = jnp.sum(onehot, axis=0)
    tiles = (counts + bm - 1) // bm
    tile_end = jnp.cumsum(tiles)
    start = (tile_end - tiles) * bm
    pos = start[e] + rank
    n_tiles = (2 * n_tok) // bm + N_EXPERTS
    tok = jnp.concatenate([jnp.arange(n_tok, dtype=jnp.int32)] * 2)
    src = jnp.zeros((n_tiles * bm,), jnp.int32).at[pos].set(tok)
    n_active = tile_end[-1]
    tile_ids = jnp.minimum(jnp.arange(n_tiles, dtype=jnp.int32), n_active - 1)
    tile_expert = jnp.sum((tile_ids[:, None] >= tile_end[None, :]).astype(jnp.int32), axis=1)
    return pos.astype(jnp.int32), src, tile_expert.astype(jnp.int32), n_active.reshape(1).astype(jnp.int32)


def kernel(x, c, ctx, c_ctx, ada_w, ada_b, ab_w_in, hy_conv_w, hy_conv_b, hy_filt_w1, hy_filt_b1, hy_filt_w2, hy_filt_b2, hy_filt_w3, hy_filt_freq, hy_bias, ml_conv_w, ml_conv_b, ml_gate_b, ml_norm_w, ab_w_out, ffn_w_gu, ffn_w_down, cd_w_in, ssd_conv_w, ssd_conv_b, ssd_A_log, ssd_dt_bias, ssd_D, ssd_norm_w, s5_lam_re, s5_lam_im, s5_log_dt, s5_B_re, s5_B_im, s5_C_re, s5_C_im, s5_D, s5_glu_w, cd_w_out, moe_router, moe_w_gu, moe_w_down, final_norm_w):
    bsz, L, D = x.shape
    Lc = ctx.shape[1]
    rows = Rows(bsz, L, Lc)
    n_lat = rows.n_lat
    assert bsz == 4 and L % 1024 == 0 and Lc == 256 and D % 256 == 0
    bn_d = min(1024, D)
    bn_f = min(512, ffn_w_down.shape[1])

    xs = jnp.concatenate([x.reshape(n_lat, D), ctx.reshape(bsz * Lc, D)], axis=0)
    cond = jnp.concatenate([c, c_ctx[None], jnp.zeros((8 - bsz - 1, D), F32)], axis=0)

    def mods(layer):
        m = adaln(cond, ada_w, ada_b, layer)
        return [m[:, k * D:(k + 1) * D].reshape(8, 1, D) for k in range(6)]

    md = mods(0)
    h = modulate(xs, md[0], md[1], rows, rows.n, BF16)
    n_main = 3 * HY_D + 4 * ML_H * ML_DH
    p = mm(h, ab_w_in, 0, n_cols=n_main, bm=1024, bn=1024)
    w_gate = jnp.pad(ab_w_in[:, :, n_main:], ((0, 0), (0, 0), (0, LANES - 4 * ML_H)))
    gates = mm(h, w_gate, 0, n_cols=LANES, bm=1024, bn=LANES)
    gates_t = gates[:, :4 * ML_H].T

    filt = (hy_filt_w1, hy_filt_b1, hy_filt_w2, hy_filt_b2, hy_filt_w3, hy_filt_freq)
    yh_lat = hyena(p, L, bsz, 0, GRID_W, 0, hy_conv_w, hy_conv_b, filt, hy_bias)
    yh_ctx = hyena(p, Lc, bsz, n_lat, Lc, 0, hy_conv_w, hy_conv_b, filt, hy_bias)

    qk = conv_silu(p, 3 * HY_D, 2 * ML_H * ML_DH, ml_conv_w, ml_conv_b, 0, rows)
    hdir = mlstm(qk, p, 3 * HY_D + 2 * ML_H * ML_DH, gates, gates_t, ml_gate_b[0], rows)
    cat = finish_ab(yh_lat, yh_ctx, hdir, p, 3 * HY_D + 3 * ML_H * ML_DH, ml_norm_w, 0, rows)
    xs = mm_residual(cat, ab_w_out, 0, xs, md[2], rows, bm=1024, bn=bn_d)

    h = modulate(xs, md[3], md[4], rows, rows.n, BF16)
    act = mm_swiglu(h, ffn_w_gu, 0, bm=1024, bn=bn_f)
    xs = mm_residual(act, ffn_w_down, 0, xs, md[5], rows, bm=512, bn=min(512, D))

    md = mods(1)
    h = modulate(xs, md[0], md[1], rows, rows.n, BF16)
    nd = SSD_H * SSD_P
    n_xbc = nd + 4 * LANES
    p1 = mm(h, cd_w_in, 0, n_cols=nd + n_xbc, bm=1024, bn=512)
    w_tail = cd_w_in[:, :, nd + n_xbc:]
    w_tail = jnp.concatenate([w_tail[:, :, 2 * SSD_H:], w_tail[:, :, :2 * SSD_H],
                              jnp.zeros((1, D, LANES - 2 * SSD_H), F32)], axis=2)
    p2 = mm(h, w_tail, 0, n_cols=S5_W + LANES, bm=1024, bn=S5_W + LANES)

    xbc = conv_silu(p1, nd, n_xbc, ssd_conv_w, ssd_conv_b, 0, rows)
    dt = p2[:, S5_W:]
    dt_t = dt[:, :2 * SSD_H].T
    ydir = ssd(xbc, dt, dt_t, ssd_A_log[0], ssd_dt_bias[0], rows)

    wre, wim, cre, cim, lb_re, lb_im, lam_rows = s5_params(
        s5_lam_re[0], s5_lam_im[0], s5_log_dt[0], s5_B_re[0], s5_B_im[0], s5_C_re[0], s5_C_im[0])
    u_lat = p2[:n_lat, :S5_W].reshape(bsz, L, S5_W)
    u_ctx = p2[n_lat:, :S5_W].reshape(bsz, Lc, S5_W)
    seq_f = jnp.concatenate([u_ctx, u_lat], axis=1)
    seq_b = jnp.concatenate([u_ctx[:, ::-1], u_lat[:, ::-1]], axis=1)
    u_arr = jnp.transpose(jnp.stack([seq_f, seq_b]), (2, 0, 1, 3)).reshape((L + Lc) * 2 * bsz, S5_W)
    y_arr = s5(u_arr, wre, wim, cre, cim, lam_rows(lb_re, bsz), lam_rows(lb_im, bsz), L + Lc)
    y_arr = y_arr.reshape(L + Lc, 2, bsz, S5_W)[Lc:]
    sf = jnp.transpose(y_arr[:, 0], (1, 0, 2)).reshape(n_lat, S5_W)
    sb = jnp.transpose(y_arr[::-1, 1], (1, 0, 2)).reshape(n_lat, S5_W)

    ssd_d_lanes = jnp.repeat(ssd_D, SSD_P, axis=1).reshape(ssd_D.shape[0], 1, nd)
    cat = finish_cd(ydir, xbc, p1, sf, sb, p2, ssd_d_lanes, ssd_norm_w, s5_D, s5_glu_w, 0, n_lat)
    xl = mm_residual(cat, cd_w_out, 0, xs[:n_lat], md[2], rows, bm=1024, bn=bn_d)

    h2 = modulate(xl, md[3], md[4], rows, n_lat, F32)
    route = router(h2, jnp.pad(moe_router[0], ((0, 0), (0, LANES - N_EXPERTS))))
    bm_e = 256
    pos, src, tile_expert, n_active = moe_plan(route, n_lat, bm_e)
    xg = gather_rows(h2, src, bm=bm_e)
    act = gmm_swiglu(xg, moe_w_gu, 0, tile_expert, n_active, bm=bm_e, bn=min(512, moe_w_down.shape[2]))
    ys = gmm(act, moe_w_down, 0, tile_expert, n_active, bm=bm_e, bn=min(512, D))
    out = moe_combine(ys, pos, xl, route, md[5], final_norm_w, rows, bm=bm_e)
    return out.reshape(bsz, L, D)
```

```python
import functools
import math

import jax
import jax.numpy as jnp
import numpy as np
from jax import lax
from jax.experimental import pallas as pl
from jax.experimental.pallas import tpu as pltpu

F32 = jnp.float32
BF16 = jnp.bfloat16

EPS = 1e-6
GRID_W = 64
CHUNK = 128
LANES = 128
HY_D = 1024
HY_EMB = 33
HY_FAST_DECAY = 0.3
HY_SLOW_DECAY = 1.5
HY_DECAY_TARGET = 1e-2
ML_H = 8
ML_DH = 128
SSD_H = 16
SSD_P = 64
S5_W = 512
S5_G = 32
S5_GS = 16
S5_P = 64
N_EXPERTS = 8
VMEM_LIMIT_BYTES = 56 * 1024 * 1024


def _cparams(n_axes):
    return pltpu.CompilerParams(dimension_semantics=("arbitrary",) * n_axes,
                                vmem_limit_bytes=VMEM_LIMIT_BYTES)


def _sigmoid(x):
    return 1.0 / (1.0 + jnp.exp(-x))


def _silu(x):
    return x * _sigmoid(x)


def _log_sigmoid(x):
    return jnp.minimum(x, 0.0) - jnp.log(1.0 + jnp.exp(-jnp.abs(x)))


def _softplus(x):
    return jnp.maximum(x, 0.0) + jnp.log(1.0 + jnp.exp(-jnp.abs(x)))


def _dot(a, b):
    return jnp.dot(a, b, preferred_element_type=F32)


def _dot_nt(a, b):
    return lax.dot_general(a, b, (((1,), (1,)), ((), ())), preferred_element_type=F32)


def _dot_tn(a, b):
    return lax.dot_general(a, b, (((0,), (0,)), ((), ())), preferred_element_type=F32)


def _dot_hi(a, b):
    return jnp.dot(a, b, preferred_element_type=F32, precision=lax.Precision.HIGHEST)


class Rows:
    def __init__(self, bsz, L, Lc):
        self.B, self.L, self.Lc = bsz, L, Lc
        self.n_lat = bsz * L
        self.n = bsz * (L + Lc)

    def mod_row(self, i, bm):
        n_lat_blocks = self.n_lat // bm
        return jnp.where(i < n_lat_blocks, (i * bm) // self.L, self.B)

    def chunk_block(self, b, c):
        ncc = self.Lc // CHUNK
        return jnp.where(c < ncc, self.n_lat // CHUNK + b * ncc + c, b * (self.L // CHUNK) + c - ncc)

    def scan_chunk(self, d, j):
        ncc = self.Lc // CHUNK
        nct = (self.L + self.Lc) // CHUNK
        back = jnp.where(j < ncc, ncc - 1 - j, nct - 1 + ncc - j)
        return jnp.where(d == 0, j, back)


def _adaln_kernel(c_ref, w_ref, b_ref, o_ref):
    cond = _silu(c_ref[...]).astype(BF16)
    o_ref[...] = _dot(cond, w_ref[...].astype(BF16)) + b_ref[...]


def adaln(cond, ada_w, ada_b, layer, bn=1024):
    bn = min(bn, ada_w.shape[1])
    _, D, N = ada_w.shape
    return pl.pallas_call(
        _adaln_kernel,
        grid=(N // bn,),
        in_specs=[pl.BlockSpec((8, D), lambda n: (0, 0)),
                  pl.BlockSpec((None, D, bn), lambda n: (layer, 0, n)),
                  pl.BlockSpec((None, 1, bn), lambda n: (layer, 0, n))],
        out_specs=pl.BlockSpec((8, bn), lambda n: (0, n)),
        out_shape=jax.ShapeDtypeStruct((8, N), F32),
        compiler_params=_cparams(1),
        name="adaln",
    )(cond, ada_w, ada_b.reshape(ada_b.shape[0], 1, N))


def _modulate_kernel(x_ref, sh_ref, sc_ref, o_ref):
    x = x_ref[...]
    y = x * lax.rsqrt(jnp.mean(x * x, axis=-1, keepdims=True) + EPS)
    o_ref[...] = (y * (1.0 + sc_ref[...]) + sh_ref[...]).astype(o_ref.dtype)


def modulate(x, shift, scale, rows, n_rows, out_dtype, bm=512):
    D = x.shape[1]
    mod_spec = pl.BlockSpec((None, 1, D), lambda i: (rows.mod_row(i, bm), 0, 0))
    return pl.pallas_call(
        _modulate_kernel,
        grid=(n_rows // bm,),
        in_specs=[pl.BlockSpec((bm, D), lambda i: (i, 0)), mod_spec, mod_spec],
        out_specs=pl.BlockSpec((bm, D), lambda i: (i, 0)),
        out_shape=jax.ShapeDtypeStruct((n_rows, D), out_dtype),
        compiler_params=_cparams(1),
        name="modulate",
    )(x, shift, scale)


def _mm_kernel(x_ref, w_ref, o_ref, wb_ref):
    @pl.when(pl.program_id(1) == 0)
    def _():
        wb_ref[...] = w_ref[...].astype(BF16)
    o_ref[...] = _dot(x_ref[...], wb_ref[...]).astype(o_ref.dtype)


def mm(x, w, widx, *, n_cols, col_off=0, bm, bn, out_dtype=F32):
    M, K = x.shape
    off = col_off // bn
    return pl.pallas_call(
        _mm_kernel,
        grid=(n_cols // bn, M // bm),
        in_specs=[pl.BlockSpec((bm, K), lambda n, m: (m, 0)),
                  pl.BlockSpec((None, K, bn), lambda n, m: (widx, 0, n + off))],
        out_specs=pl.BlockSpec((bm, bn), lambda n, m: (m, n)),
        out_shape=jax.ShapeDtypeStruct((M, n_cols), out_dtype),
        scratch_shapes=[pltpu.VMEM((K, bn), BF16)],
        compiler_params=_cparams(2),
        name="mm",
    )(x, w)


def _mm_res_kernel(x_ref, w_ref, r_ref, g_ref, o_ref, wb_ref):
    @pl.when(pl.program_id(1) == 0)
    def _():
        wb_ref[...] = w_ref[...].astype(BF16)
    o_ref[...] = r_ref[...] + g_ref[...] * _dot(x_ref[...], wb_ref[...])


def mm_residual(x, w, widx, res, gate, rows, *, bm, bn):
    M, K = x.shape
    N = w.shape[2]
    return pl.pallas_call(
        _mm_res_kernel,
        grid=(N // bn, M // bm),
        in_specs=[pl.BlockSpec((bm, K), lambda n, m: (m, 0)),
                  pl.BlockSpec((None, K, bn), lambda n, m: (widx, 0, n)),
                  pl.BlockSpec((bm, bn), lambda n, m: (m, n)),
                  pl.BlockSpec((None, 1, bn), lambda n, m: (rows.mod_row(m, bm), 0, n))],
        out_specs=pl.BlockSpec((bm, bn), lambda n, m: (m, n)),
        out_shape=jax.ShapeDtypeStruct((M, N), F32),
        scratch_shapes=[pltpu.VMEM((K, bn), BF16)],
        compiler_params=_cparams(2),
        name="mm_residual",
    )(x, w, res, gate)


def _mm_swiglu_kernel(x_ref, wg_ref, wu_ref, o_ref, wgb_ref, wub_ref):
    @pl.when(pl.program_id(1) == 0)
    def _():
        wgb_ref[...] = wg_ref[...].astype(BF16)
        wub_ref[...] = wu_ref[...].astype(BF16)
    x = x_ref[...]
    g = _dot(x, wgb_ref[...])
    u = _dot(x, wub_ref[...])
    o_ref[...] = (_silu(g) * u).astype(o_ref.dtype)


def mm_swiglu(x, w_gu, widx, *, bm, bn):
    M, K = x.shape
    F = w_gu.shape[2] // 2
    nf = F // bn
    return pl.pallas_call(
        _mm_swiglu_kernel,
        grid=(nf, M // bm),
        in_specs=[pl.BlockSpec((bm, K), lambda n, m: (m, 0)),
                  pl.BlockSpec((None, K, bn), lambda n, m: (widx, 0, n)),
                  pl.BlockSpec((None, K, bn), lambda n, m: (widx, 0, n + nf))],
        out_specs=pl.BlockSpec((bm, bn), lambda n, m: (m, n)),
        out_shape=jax.ShapeDtypeStruct((M, F), BF16),
        scratch_shapes=[pltpu.VMEM((K, bn), BF16), pltpu.VMEM((K, bn), BF16)],
        compiler_params=_cparams(2),
        name="mm_swiglu",
    )(x, w_gu, w_gu)


def _conv3(x, w_ref, b_ref, period):
    n = x.shape[0]
    pos = lax.broadcasted_iota(jnp.int32, x.shape, 0) & (period - 1)
    prev = jnp.where(pos == 0, 0.0, pltpu.roll(x, 1, 0))
    nxt = jnp.where(pos == period - 1, 0.0, pltpu.roll(x, n - 1, 0))
    w = w_ref[...]
    return b_ref[...] + prev * w[0:1] + x * w[1:2] + nxt * w[2:3]


def _conv_silu_kernel(x_ref, w_ref, b_ref, o_ref, *, n_lat_blocks, lat_period, ctx_period):
    period = jnp.where(pl.program_id(0) < n_lat_blocks, lat_period, ctx_period)
    o_ref[...] = _silu(_conv3(x_ref[...], w_ref, b_ref, period)).astype(o_ref.dtype)


def conv_silu(p, col_off, n_cols, w, b, widx, rows, *, bt=256, bc=512):
    off = col_off // bc
    kern = functools.partial(_conv_silu_kernel, n_lat_blocks=rows.n_lat // bt,
                             lat_period=GRID_W, ctx_period=rows.Lc)
    return pl.pallas_call(
        kern,
        grid=(rows.n // bt, n_cols // bc),
        in_specs=[pl.BlockSpec((bt, bc), lambda i, j: (i, j + off)),
                  pl.BlockSpec((None, 3, bc), lambda i, j: (widx, 0, j)),
                  pl.BlockSpec((None, 1, bc), lambda i, j: (widx, 0, j))],
        out_specs=pl.BlockSpec((bt, bc), lambda i, j: (i, j)),
        out_shape=jax.ShapeDtypeStruct((rows.n, n_cols), F32),
        compiler_params=_cparams(2),
        name="conv_silu",
    )(p, w, b.reshape(b.shape[0], 1, b.shape[1]))


def _hyena_conv_kernel(p0_ref, p1_ref, p2_ref, w0_ref, w1_ref, w2_ref, b0_ref, b1_ref, b2_ref,
                       x0_ref, u_ref, *, period):
    x0_ref[...] = _conv3(p0_ref[...], w0_ref, b0_ref, period)
    x1 = _conv3(p1_ref[...], w1_ref, b1_ref, period)
    v = _conv3(p2_ref[...], w2_ref, b2_ref, period)
    u_ref[...] = (x1 * v).astype(u_ref.dtype)


def hyena_conv(p, w, b, widx, *, row0, bsz, seq, period, bt=256, bc=512):
    nt = seq // bt
    ncb = HY_D // bc
    rb0 = row0 // bt

    def pspec(k):
        return pl.BlockSpec((bt, bc), lambda bb, i, j: (rb0 + bb * nt + i, j + k * ncb))

    def wspec(k):
        return pl.BlockSpec((None, 3, bc), lambda bb, i, j: (widx, 0, j + k * ncb))

    def bspec(k):
        return pl.BlockSpec((None, 1, bc), lambda bb, i, j: (widx, 0, j + k * ncb))

    ospec = pl.BlockSpec((bt, bc), lambda bb, i, j: (i, bb * ncb + j))
    b3 = b.reshape(b.shape[0], 1, b.shape[1])
    return pl.pallas_call(
        functools.partial(_hyena_conv_kernel, period=period),
        grid=(bsz, nt, ncb),
        in_specs=[pspec(0), pspec(1), pspec(2), wspec(0), wspec(1), wspec(2), bspec(0), bspec(1), bspec(2)],
        out_specs=[ospec, ospec],
        out_shape=[jax.ShapeDtypeStruct((seq, bsz * HY_D), F32),
                   jax.ShapeDtypeStruct((seq, bsz * HY_D), BF16)],
        compiler_params=_cparams(3),
        name="hyena_conv",
    )(p, p, p, w, w, w, b3, b3, b3)


def _hyena_feats(L):
    pos = np.arange(L, dtype=np.float64)
    t = pos / max(L - 1, 1)
    n_bands = (HY_EMB - 1) // 2
    bands = np.linspace(1e-4, n_bands - 1, n_bands)
    ang = (2 * math.pi / L) * pos[:, None] * bands[None, :]
    feats = np.concatenate([t[:, None], np.cos(ang), -np.sin(ang)], axis=-1)
    feats = np.pad(feats, ((0, 0), (0, LANES - HY_EMB)))
    deltas = np.abs(np.linspace(math.log(HY_DECAY_TARGET) / HY_SLOW_DECAY,
                                math.log(HY_DECAY_TARGET) / HY_FAST_DECAY, HY_D))
    return feats.astype(np.float32), t.astype(np.float32)[:, None], deltas.astype(np.float32)[None, :]


def _hyena_filter_kernel(feats_ref, t_ref, dl_ref, w1_ref, b1_ref, w2_ref, b2_ref, w3f_ref, w3b_ref,
                         fq_ref, hf_ref, hb_ref):
    fq = fq_ref[...]
    h = jnp.sin(fq[0:1] * (_dot_hi(feats_ref[...], w1_ref[...]) + b1_ref[...]))
    h = jnp.sin(fq[1:2] * (_dot_hi(h, w2_ref[...]) + b2_ref[...]))
    win = jnp.exp(-t_ref[...] * dl_ref[...])
    h_f = _dot_hi(h, w3f_ref[...]) * win
    h_b = _dot_hi(h, w3b_ref[...]) * win
    row = lax.broadcasted_iota(jnp.int32, h_b.shape, 0)
    h_b = jnp.where(row == 0, 0.0, h_b)
    l1 = jnp.sum(jnp.abs(h_f), axis=0, keepdims=True) + jnp.sum(jnp.abs(h_b), axis=0, keepdims=True)
    hf_ref[...] = h_f / l1
    hb_ref[...] = h_b / l1


def hyena_filter(L, w1, b1, w2, b2, w3, freq, widx, bc=256):
    feats, t, deltas = _hyena_feats(L)
    hid = w2.shape[1]
    w1p = jnp.pad(w1[widx], ((0, LANES - HY_EMB), (0, 0)))
    ncb = HY_D // bc
    full = lambda shape: pl.BlockSpec(shape, lambda j: (0,) * len(shape))
    return pl.pallas_call(
        _hyena_filter_kernel,
        grid=(ncb,),
        in_specs=[full((L, LANES)), full((L, 1)), pl.BlockSpec((1, bc), lambda j: (0, j)),
                  full((LANES, hid)), full((1, hid)), full((hid, hid)), full((1, hid)),
                  pl.BlockSpec((hid, bc), lambda j: (0, j)),
                  pl.BlockSpec((hid, bc), lambda j: (0, j + ncb)),
                  full((2, hid))],
        out_specs=[pl.BlockSpec((L, bc), lambda j: (0, j)), pl.BlockSpec((L, bc), lambda j: (0, j))],
        out_shape=[jax.ShapeDtypeStruct((L, HY_D), F32), jax.ShapeDtypeStruct((L, HY_D), F32)],
        compiler_params=_cparams(1),
        name="hyena_filter",
    )(jnp.asarray(feats), jnp.asarray(t), jnp.asarray(deltas), w1p, b1[widx][None], w2[widx],
      b2[widx][None], w3[widx], w3[widx], freq[widx])


def _dft_matrix(L):
    f = np.arange(L, dtype=np.int64)[:, None]
    s = np.arange(L, dtype=np.int64)[None, :]
    ang = (math.pi / L) * ((f * s) % (2 * L)).astype(np.float64)
    a_cos = np.cos(ang)
    a_sin = -np.sin(ang)
    a_sin[0, :] = np.where(np.arange(L) % 2 == 0, 1.0, -1.0)
    return np.concatenate([a_cos, a_sin], axis=0).astype(np.float32)


def _spectrum_kernel(ac_ref, as_ref, hf_ref, hb_ref, kre_ref, kim_ref, kny_ref, *, L, bf):
    hf = hf_ref[...].astype(BF16)
    hb = hb_ref[...].astype(BF16)
    ac = ac_ref[...].astype(BF16)
    a_s = as_ref[...].astype(BF16)
    cf, cb = _dot(ac, hf), _dot(ac, hb)
    sf, sb = _dot(a_s, hf), _dot(a_s, hb)
    f = lax.broadcasted_iota(jnp.int32, cf.shape, 0) + pl.program_id(1) * bf
    wgt = jnp.where(f == 0, 0.5 / L, 1.0 / L)
    kre_ref[...] = (cf + cb) * wgt
    kim_ref[...] = jnp.where(f == 0, 0.0, (sf - sb) * wgt)
    kny_ref[...] = jnp.where(f == 0, (sf + sb) * wgt, (cf + cb) * wgt)


def hyena_spectrum(a_mat, hf, hb, L, bf=256, bc=512):
    bf = min(bf, L)
    nf = L // bf
    C = hf.shape[1]
    ospec = pl.BlockSpec((bf, bc), lambda c, i: (i, c))
    hspec = pl.BlockSpec((L, bc), lambda c, i: (0, c))
    return pl.pallas_call(
        functools.partial(_spectrum_kernel, L=L, bf=bf),
        grid=(C // bc, nf),
        in_specs=[pl.BlockSpec((bf, L), lambda c, i: (i, 0)), pl.BlockSpec((bf, L), lambda c, i: (i + nf, 0)),
                  hspec, hspec],
        out_specs=[ospec, ospec, ospec],
        out_shape=[jax.ShapeDtypeStruct((L, C), F32)] * 3,
        compiler_params=_cparams(2),
        name="hyena_spectrum",
    )(a_mat, a_mat, hf, hb)


def _lc_fwd_kernel(ac_ref, as_ref, u_ref, kre_ref, kim_ref, kny_ref, yre_ref, yim_ref, acb_ref, asb_ref):
    @pl.when(pl.program_id(1) == 0)
    def _():
        acb_ref[...] = ac_ref[...].astype(BF16)
        asb_ref[...] = as_ref[...].astype(BF16)
    u = u_ref[...]
    ure = _dot(acb_ref[...], u)
    uim = _dot(asb_ref[...], u)
    kim = kim_ref[...]
    yre_ref[...] = (kre_ref[...] * ure - kim * uim).astype(yre_ref.dtype)
    yim_ref[...] = (kny_ref[...] * uim + kim * ure).astype(yim_ref.dtype)


def long_conv_fwd(a_mat, u, kre, kim, kny, L, bsz, bf=256):
    bf = min(bf, L)
    nf = L // bf
    C = kre.shape[1]
    kspec = pl.BlockSpec((bf, C), lambda i, b: (i, 0))
    ospec = pl.BlockSpec((bf, C), lambda i, b: (i, b))
    return pl.pallas_call(
        _lc_fwd_kernel,
        grid=(nf, bsz),
        in_specs=[pl.BlockSpec((bf, L), lambda i, b: (i, 0)), pl.BlockSpec((bf, L), lambda i, b: (i + nf, 0)),
                  pl.BlockSpec((L, C), lambda i, b: (0, b)), kspec, kspec, kspec],
        out_specs=[ospec, ospec],
        out_shape=[jax.ShapeDtypeStruct((L, bsz * C), BF16)] * 2,
        scratch_shapes=[pltpu.VMEM((bf, L), BF16), pltpu.VMEM((bf, L), BF16)],
        compiler_params=_cparams(2),
        name="long_conv_fwd",
    )(a_mat, a_mat, u, kre, kim, kny)


def _lc_inv_kernel(atc_ref, ats_ref, yre_ref, yim_ref, x0_ref, u_ref, bias_ref, o_ref, atcb_ref, atsb_ref):
    @pl.when(pl.program_id(1) == 0)
    def _():
        atcb_ref[...] = atc_ref[...].astype(BF16)
        atsb_ref[...] = ats_ref[...].astype(BF16)
    y = _dot(atcb_ref[...], yre_ref[...]) + _dot(atsb_ref[...], yim_ref[...])
    u = u_ref[...].astype(F32)
    o_ref[...] = (x0_ref[...] * (y + bias_ref[...] * u)).astype(o_ref.dtype)


def long_conv_inv(at_mat, yre, yim, x0, u, bias, widx, L, bsz, bt=256):
    bt = min(bt, L)
    nt = L // bt
    C = bias.shape[1]
    tspec = pl.BlockSpec((bt, C), lambda i, b: (i, b))
    yspec = pl.BlockSpec((L, C), lambda i, b: (0, b))
    return pl.pallas_call(
        _lc_inv_kernel,
        grid=(nt, bsz),
        in_specs=[pl.BlockSpec((bt, L), lambda i, b: (i, 0)), pl.BlockSpec((bt, L), lambda i, b: (i, 1)),
                  yspec, yspec, tspec, tspec, pl.BlockSpec((None, 1, C), lambda i, b: (widx, 0, 0))],
        out_specs=tspec,
        out_shape=jax.ShapeDtypeStruct((L, bsz * C), BF16),
        scratch_shapes=[pltpu.VMEM((bt, L), BF16), pltpu.VMEM((bt, L), BF16)],
        compiler_params=_cparams(2),
        name="long_conv_inv",
    )(at_mat, at_mat, yre, yim, x0, u, bias.reshape(bias.shape[0], 1, C))


def hyena(p, L, bsz, row0, period, widx, conv_w, conv_b, filt, bias):
    a_np = _dft_matrix(L)
    a_mat = jnp.asarray(a_np)
    at_mat = jnp.asarray(np.ascontiguousarray(a_np.T))
    x0, u = hyena_conv(p, conv_w, conv_b, widx, row0=row0, bsz=bsz, seq=L, period=period, bt=min(256, L))
    hf, hb = hyena_filter(L, *filt, widx)
    kre, kim, kny = hyena_spectrum(a_mat, hf, hb, L)
    yre, yim = long_conv_fwd(a_mat, u, kre, kim, kny, L, bsz)
    return long_conv_inv(at_mat, yre, yim, x0, u, bias, widx, L, bsz)


def _pick_col(g, idx):
    lane = lax.broadcasted_iota(jnp.int32, g.shape, 1)
    return jnp.sum(jnp.where(lane == idx, g, 0.0), axis=1, keepdims=True)


def _chunk_masks(d, n):
    t_i = lax.broadcasted_iota(jnp.int32, (n, n), 0)
    s_i = lax.broadcasted_iota(jnp.int32, (n, n), 1)
    lag = (t_i - s_i) * jnp.where(d == 0, 1, -1)
    return lag >= 0, lag <= 0


def _cumsums(x_row, x_col, causal, causal_t):
    col = jnp.sum(jnp.where(causal, x_row, 0.0), axis=1, keepdims=True)
    row = jnp.sum(jnp.where(causal_t, x_col, 0.0), axis=0, keepdims=True)
    tot = jnp.sum(x_row, axis=1, keepdims=True)
    return col, row, tot


def _mlstm_kernel(gb_ref, q_ref, k_ref, v_ref, gc_ref, gr_ref, o_ref, c_ref, n_ref, m_ref):
    d = pl.program_id(1)

    @pl.when(pl.program_id(2) == 0)
    def _():
        c_ref[...] = jnp.zeros_like(c_ref)
        n_ref[...] = jnp.zeros_like(n_ref)
        m_ref[...] = jnp.zeros_like(m_ref)

    causal, causal_t = _chunk_masks(d, CHUNK)
    gcol = gc_ref[...]
    for h in range(ML_H):
        sl = slice(h * ML_DH, (h + 1) * ML_DH)
        o_ref[:, sl] = _mlstm_head(h, d, gb_ref, q_ref[:, sl], k_ref[:, sl], v_ref[:, sl], gcol, gr_ref,
                                   causal, causal_t, c_ref.at[h], n_ref.at[h], m_ref.at[h])


def _mlstm_head(h, d, gb_ref, q, k, v, gcol, gr_ref, causal, causal_t, c_ref, n_ref, m_ref):
    bi = gb_ref[2 * d, h]
    bf = gb_ref[2 * d + 1, h]
    i_idx = 2 * ML_H * d + h
    f_idx = i_idx + ML_H
    ig_row = gr_ref[pl.ds(i_idx, 1), :] + bi
    lf_row = _log_sigmoid(gr_ref[pl.ds(f_idx, 1), :] + bf)
    ig_col = _pick_col(gcol, i_idx) + bi
    lf_col = _log_sigmoid(_pick_col(gcol, f_idx) + bf)
    bcum_col, bcum_row, g = _cumsums(lf_row, lf_col, causal, causal_t)

    q = q * (ML_DH ** -0.5)
    qb, kb, vb = q.astype(BF16), k.astype(BF16), v.astype(BF16)
    c_in, n_in, m_in = c_ref[...], n_ref[...], m_ref[0:1, 0:1]

    dmat = jnp.where(causal, bcum_col - bcum_row + ig_row, -jnp.inf)
    m_inter = bcum_col + m_in
    m_t = jnp.maximum(jnp.max(dmat, axis=1, keepdims=True), m_inter)
    s = _dot_nt(qb, kb) * jnp.exp(dmat - m_t)
    w_inter = jnp.exp(m_inter - m_t)
    num = _dot(s.astype(BF16), vb) + w_inter * _dot(qb, c_in.astype(BF16))
    den = jnp.sum(s, axis=1, keepdims=True) + w_inter * jnp.sum(q * n_in, axis=1, keepdims=True)
    out = num / jnp.maximum(jnp.abs(den), jnp.exp(-m_t))

    a_row = g - bcum_row + ig_row
    a_col = g - bcum_col + ig_col
    m_loc = jnp.max(a_row, axis=1, keepdims=True)
    kw = k * jnp.exp(a_col - m_loc)
    c_loc = _dot_tn(kw.astype(BF16), vb)
    n_loc = jnp.sum(kw, axis=0, keepdims=True)
    m_new = jnp.maximum(g + m_in, m_loc)
    s_prev = jnp.exp(g + m_in - m_new)
    s_loc = jnp.exp(m_loc - m_new)
    c_ref[...] = s_prev * c_in + s_loc * c_loc
    n_ref[...] = s_prev * n_in + s_loc * n_loc
    m_ref[...] = jnp.broadcast_to(m_new, m_ref.shape)
    return out


def mlstm(qk, p, v_col_off, gates, gates_t, gate_b, rows):
    nct = (rows.L + rows.Lc) // CHUNK
    W = ML_H * ML_DH
    voff = v_col_off // W

    def rb(b, d, j):
        return rows.chunk_block(b, rows.scan_chunk(d, j))

    return pl.pallas_call(
        _mlstm_kernel,
        grid=(rows.B, 2, nct),
        in_specs=[pl.BlockSpec(memory_space=pltpu.SMEM),
                  pl.BlockSpec((CHUNK, W), lambda b, d, j: (rb(b, d, j), 0)),
                  pl.BlockSpec((CHUNK, W), lambda b, d, j: (rb(b, d, j), 1)),
                  pl.BlockSpec((CHUNK, W), lambda b, d, j: (rb(b, d, j), voff)),
                  pl.BlockSpec((CHUNK, LANES), lambda b, d, j: (rb(b, d, j), 0)),
                  pl.BlockSpec((4 * ML_H, CHUNK), lambda b, d, j: (0, rb(b, d, j)))],
        out_specs=pl.BlockSpec((None, CHUNK, W), lambda b, d, j: (d, rb(b, d, j), 0)),
        out_shape=jax.ShapeDtypeStruct((2, rows.n, W), F32),
        scratch_shapes=[pltpu.VMEM((ML_H, ML_DH, ML_DH), F32), pltpu.VMEM((ML_H, 1, ML_DH), F32),
                        pltpu.VMEM((ML_H, 8, LANES), F32)],
        compiler_params=_cparams(3),
        name="mlstm",
    )(gate_b, qk, qk, p, gates, gates_t)


def _finish_ab_kernel(yh_ref, h_ref, o_ref, nw_ref, out_ref):
    out_ref[:, 0:HY_D] = yh_ref[...]
    hs = h_ref[0] + h_ref[1]
    og = _sigmoid(o_ref[...])
    nw = nw_ref[...]
    for i in range(ML_H):
        sl = slice(i * ML_DH, (i + 1) * ML_DH)
        x = hs[:, sl]
        mu = jnp.mean(x, axis=-1, keepdims=True)
        xc = x - mu
        var = jnp.mean(xc * xc, axis=-1, keepdims=True)
        y = xc * lax.rsqrt(var + EPS) * nw[:, sl] * og[:, sl]
        out_ref[:, HY_D + i * ML_DH:HY_D + (i + 1) * ML_DH] = y.astype(out_ref.dtype)


def finish_ab(yh_lat, yh_ctx, hdir, p, o_col_off, norm_w, widx, rows, bt=256):
    n_lat_blocks = rows.n_lat // bt
    per_b = rows.L // bt
    ooff = o_col_off // (ML_H * ML_DH)
    W = ML_H * ML_DH

    def call(yh, row_block0, n_blocks, yh_map):
        return pl.pallas_call(
            _finish_ab_kernel,
            grid=(n_blocks,),
            in_specs=[pl.BlockSpec((bt, HY_D), yh_map),
                      pl.BlockSpec((2, bt, W), lambda i: (0, row_block0 + i, 0)),
                      pl.BlockSpec((bt, W), lambda i: (row_block0 + i, ooff)),
                      pl.BlockSpec((None, 1, W), lambda i: (widx, 0, 0))],
            out_specs=pl.BlockSpec((bt, HY_D + W), lambda i: (i, 0)),
            out_shape=jax.ShapeDtypeStruct((n_blocks * bt, HY_D + W), BF16),
            compiler_params=_cparams(1),
            name="finish_ab",
        )(yh, hdir, p, norm_w.reshape(norm_w.shape[0], 1, W))

    lat = call(yh_lat, 0, n_lat_blocks, lambda i: (i % per_b, i // per_b))
    ctx = call(yh_ctx, n_lat_blocks, (rows.n - rows.n_lat) // bt, lambda i: (0, i))
    return jnp.concatenate([lat, ctx], axis=0)


def _ssd_kernel(al_ref, db_ref, x_ref, b_ref, c_ref, dc_ref, dr_ref, o_ref, h_ref):
    d = pl.program_id(1)

    @pl.when(pl.program_id(2) == 0)
    def _():
        h_ref[...] = jnp.zeros_like(h_ref)

    causal, causal_t = _chunk_masks(d, CHUNK)
    dcol = dc_ref[...]
    n_groups = b_ref.shape[1] // LANES
    heads_per_group = SSD_H // n_groups
    for grp in range(n_groups):
        gs = slice(grp * LANES, (grp + 1) * LANES)
        bmat = b_ref[:, gs].astype(BF16)
        cmat = c_ref[:, gs].astype(BF16)
        cb = _dot_nt(cmat, bmat)
        for r in range(heads_per_group):
            hd = grp * heads_per_group + r
            idx = SSD_H * d + hd
            neg_a = -jnp.exp(jnp.full((1, 1), al_ref[d, hd], F32))
            dt_row = _softplus(dr_ref[pl.ds(idx, 1), :] + db_ref[d, hd])
            dt_col = _softplus(_pick_col(dcol, idx) + db_ref[d, hd])
            acum_col, acum_row, tot = _cumsums(dt_row * neg_a, dt_col * neg_a, causal, causal_t)
            decay = jnp.exp(jnp.where(causal, acum_col - acum_row, -jnp.inf))
            hs = slice(hd * SSD_P, (hd + 1) * SSD_P)
            xh = x_ref[:, hs]
            h_in = h_ref[hd]
            y = _dot((cb * decay * dt_row).astype(BF16), xh.astype(BF16))
            y = y + _dot_nt(cmat, h_in.astype(BF16)) * jnp.exp(acum_col)
            o_ref[:, hs] = y
            xw = xh * (jnp.exp(tot - acum_col) * dt_col)
            h_ref[hd] = jnp.exp(tot) * h_in + _dot_tn(xw.astype(BF16), bmat)


def ssd(xbc, dt, dt_t, a_log, dt_bias, rows):
    nct = (rows.L + rows.Lc) // CHUNK
    nd = SSD_H * SSD_P
    gw = 2 * LANES

    def rb(b, d, j):
        return rows.chunk_block(b, rows.scan_chunk(d, j))

    return pl.pallas_call(
        _ssd_kernel,
        grid=(rows.B, 2, nct),
        in_specs=[pl.BlockSpec(memory_space=pltpu.SMEM), pl.BlockSpec(memory_space=pltpu.SMEM),
                  pl.BlockSpec((CHUNK, nd), lambda b, d, j: (rb(b, d, j), 0)),
                  pl.BlockSpec((CHUNK, gw), lambda b, d, j: (rb(b, d, j), nd // gw)),
                  pl.BlockSpec((CHUNK, gw), lambda b, d, j: (rb(b, d, j), nd // gw + 1)),
                  pl.BlockSpec((CHUNK, LANES), lambda b, d, j: (rb(b, d, j), 0)),
                  pl.BlockSpec((2 * SSD_H, CHUNK), lambda b, d, j: (0, rb(b, d, j)))],
        out_specs=pl.BlockSpec((None, CHUNK, nd), lambda b, d, j: (d, rb(b, d, j), 0)),
        out_shape=jax.ShapeDtypeStruct((2, rows.n, nd), F32),
        scratch_shapes=[pltpu.VMEM((SSD_H, SSD_P, LANES), F32)],
        compiler_params=_cparams(3),
        name="ssd",
    )(a_log, dt_bias, xbc, xbc, xbc, dt, dt_t)


S5_TC = 64
S5_JB = 2
S5_NS = S5_G * S5_P


def _s5_kernel(u_ref, w_ref, c_ref, lre_ref, lim_ref, y_ref, z_ref, s_ref, wb_ref, cb_ref):
    @pl.when(pl.program_id(0) == 0)
    def _():
        s_ref[...] = jnp.zeros_like(s_ref)
        wb_ref[...] = w_ref[...].astype(BF16)
        cb_ref[...] = c_ref[...].astype(BF16)

    n_blk = S5_W // LANES
    sw = S5_NS // n_blk
    u = u_ref[...].astype(BF16)
    for j in range(n_blk):
        z_ref[:, 2 * j * sw:2 * (j + 1) * sw] = _dot(u[:, j * LANES:(j + 1) * LANES], wb_ref[j])

    for j0 in range(0, n_blk, S5_JB):
        blocks = range(j0, j0 + S5_JB)
        re_sl = [slice(2 * j * sw, (2 * j + 1) * sw) for j in blocks]
        im_sl = [slice((2 * j + 1) * sw, (2 * j + 2) * sw) for j in blocks]
        ar = [lre_ref[:, j * sw:(j + 1) * sw] for j in blocks]
        ai = [lim_ref[:, j * sw:(j + 1) * sw] for j in blocks]

        def step(t, carry):
            r0 = pl.multiple_of(t * 8, 8)
            new = []
            for k in range(S5_JB):
                zr, zi = carry[2 * k], carry[2 * k + 1]
                nr = ar[k] * zr - ai[k] * zi + z_ref[pl.ds(r0, 8), re_sl[k]]
                ni = ar[k] * zi + ai[k] * zr + z_ref[pl.ds(r0, 8), im_sl[k]]
                z_ref[pl.ds(r0, 8), re_sl[k]] = nr
                z_ref[pl.ds(r0, 8), im_sl[k]] = ni
                new += [nr, ni]
            return tuple(new)

        init = []
        for k in range(S5_JB):
            init += [s_ref[:, re_sl[k]], s_ref[:, im_sl[k]]]
        fin = lax.fori_loop(0, S5_TC, step, tuple(init))
        for k in range(S5_JB):
            s_ref[:, re_sl[k]] = fin[2 * k]
            s_ref[:, im_sl[k]] = fin[2 * k + 1]

    row = lax.broadcasted_iota(jnp.int32, (S5_TC * 8, LANES), 0)
    is_bwd = (row & 4) != 0
    for j in range(n_blk):
        yy = _dot(z_ref[:, 2 * j * sw:2 * (j + 1) * sw].astype(BF16), cb_ref[j])
        y_ref[:, j * LANES:(j + 1) * LANES] = jnp.where(is_bwd, yy[:, LANES:], yy[:, :LANES])


def s5_params(lam_re, lam_im, log_dt, b_re, b_im, c_re, c_im):
    dt = jnp.exp(log_dt)[..., None]
    mag = jnp.exp(lam_re * dt)
    lb_re, lb_im = mag * jnp.cos(lam_im * dt), mag * jnp.sin(lam_im * dt)
    den = lam_re * lam_re + lam_im * lam_im
    f_re = ((lb_re - 1) * lam_re + lb_im * lam_im) / den
    f_im = (lb_im * lam_re - (lb_re - 1) * lam_im) / den
    cf_re = c_re[None] * f_re[:, :, None, :] - c_im[None] * f_im[:, :, None, :]
    cf_im = c_re[None] * f_im[:, :, None, :] + c_im[None] * f_re[:, :, None, :]
    gpb = LANES // S5_GS
    n_blk = S5_G // gpb
    eye = jnp.eye(gpb, dtype=F32)

    def w_blocks(b):
        bb = b.reshape(n_blk, gpb, S5_P, S5_GS)
        return jnp.einsum('jgpi,gh->jgihp', bb, eye).reshape(n_blk, gpb * S5_GS, gpb * S5_P)

    def c_blocks(c):
        cc = c.reshape(2, n_blk, gpb, S5_GS, S5_P)
        return jnp.einsum('djgip,gh->jgpdhi', cc, eye).reshape(n_blk, gpb * S5_P, 2 * gpb * S5_GS)

    def lam_rows(l, bsz):
        return jnp.repeat(l.reshape(2, S5_NS), bsz, axis=0)

    w_cat = jnp.concatenate([w_blocks(b_re), w_blocks(b_im)], axis=2)
    c_cat = jnp.concatenate([c_blocks(cf_re), -c_blocks(cf_im)], axis=1)
    return w_cat, c_cat, lb_re, lb_im, lam_rows


def s5(u_arr, w_cat, c_cat, lre, lim, n_steps):
    n_blk = S5_W // LANES
    sw = S5_NS // n_blk
    rt = S5_TC * 8
    full = lambda shape: pl.BlockSpec(shape, lambda i: (0,) * len(shape))
    return pl.pallas_call(
        _s5_kernel,
        grid=(n_steps // S5_TC,),
        in_specs=[pl.BlockSpec((rt, S5_W), lambda i: (i, 0)),
                  full((n_blk, LANES, 2 * sw)), full((n_blk, 2 * sw, 2 * LANES)),
                  full((8, S5_NS)), full((8, S5_NS))],
        out_specs=pl.BlockSpec((rt, S5_W), lambda i: (i, 0)),
        out_shape=jax.ShapeDtypeStruct((n_steps * 8, S5_W), F32),
        scratch_shapes=[pltpu.VMEM((rt, 2 * S5_NS), F32), pltpu.VMEM((8, 2 * S5_NS), F32),
                        pltpu.VMEM((n_blk, LANES, 2 * sw), BF16), pltpu.VMEM((n_blk, 2 * sw, 2 * LANES), BF16)],
        compiler_params=_cparams(1),
        name="s5",
    )(u_arr, w_cat, c_cat, lre, lim)


def _finish_cd_kernel(y_ref, xs_ref, z_ref, sf_ref, sb_ref, u_ref, dssd_ref, nw_ref, d5_ref, glu_ref, out_ref):
    y = (y_ref[0] + y_ref[1] + dssd_ref[...] * xs_ref[...]) * _silu(z_ref[...])
    y = y * lax.rsqrt(jnp.mean(y * y, axis=-1, keepdims=True) + EPS) * nw_ref[...]
    nd = y.shape[1]
    out_ref[:, 0:nd] = y.astype(out_ref.dtype)
    s = sf_ref[...] + sb_ref[...] + d5_ref[...] * u_ref[...]
    s = 0.5 * s * (1.0 + lax.erf(s * (2.0 ** -0.5)))
    gl = _dot(s.astype(BF16), glu_ref[...].astype(BF16))
    out_ref[:, nd:nd + S5_W] = (gl[:, 0:S5_W] * _sigmoid(gl[:, S5_W:])).astype(out_ref.dtype)


def finish_cd(ydir, xbc, p1, sf, sb, p2, ssd_d_lanes, norm_w, s5_d, glu_w, widx, n_rows, bt=256):
    nd = SSD_H * SSD_P
    vec = lambda n: pl.BlockSpec((None, 1, n), lambda i: (widx, 0, 0))
    return pl.pallas_call(
        _finish_cd_kernel,
        grid=(n_rows // bt,),
        in_specs=[pl.BlockSpec((2, bt, nd), lambda i: (0, i, 0)),
                  pl.BlockSpec((bt, nd), lambda i: (i, 0)),
                  pl.BlockSpec((bt, nd), lambda i: (i, 0)),
                  pl.BlockSpec((bt, S5_W), lambda i: (i, 0)),
                  pl.BlockSpec((bt, S5_W), lambda i: (i, 0)),
                  pl.BlockSpec((bt, S5_W), lambda i: (i, 0)),
                  vec(nd), vec(nd), vec(S5_W),
                  pl.BlockSpec((None, S5_W, 2 * S5_W), lambda i: (widx, 0, 0))],
        out_specs=pl.BlockSpec((bt, nd + S5_W), lambda i: (i, 0)),
        out_shape=jax.ShapeDtypeStruct((n_rows, nd + S5_W), BF16),
        compiler_params=_cparams(1),
        name="finish_cd",
    )(ydir, xbc, p1, sf, sb, p2, ssd_d_lanes, norm_w.reshape(norm_w.shape[0], 1, nd),
      s5_d.reshape(s5_d.shape[0], 1, S5_W), glu_w)


def _router_kernel(h_ref, w_ref, o_ref):
    logits = _dot_hi(h_ref[...], w_ref[...])
    lane = lax.broadcasted_iota(jnp.int32, logits.shape, 1)
    logits = jnp.where(lane < N_EXPERTS, logits, -jnp.inf)
    m1 = jnp.max(logits, axis=1, keepdims=True)
    i1 = jnp.min(jnp.where(logits == m1, lane, LANES), axis=1, keepdims=True)
    rest = jnp.where(lane == i1, -jnp.inf, logits)
    m2 = jnp.max(rest, axis=1, keepdims=True)
    i2 = jnp.min(jnp.where(rest == m2, lane, LANES), axis=1, keepdims=True)
    e2 = jnp.exp(m2 - m1)
    p1 = 1.0 / (1.0 + e2)
    p2 = e2 / (1.0 + e2)
    o_ref[...] = jnp.where(lane == 0, i1.astype(F32),
                           jnp.where(lane == 1, i2.astype(F32),
                                     jnp.where(lane == 2, p1, jnp.where(lane == 3, p2, 0.0))))


def router(h, w_router_padded, bt=512):
    T, D = h.shape
    return pl.pallas_call(
        _router_kernel,
        grid=(T // bt,),
        in_specs=[pl.BlockSpec((bt, D), lambda i: (i, 0)), pl.BlockSpec((D, LANES), lambda i: (0, 0))],
        out_specs=pl.BlockSpec((bt, LANES), lambda i: (i, 0)),
        out_shape=jax.ShapeDtypeStruct((T, LANES), F32),
        compiler_params=_cparams(1),
        name="router",
    )(h, w_router_padded)


def _gather_kernel(idx_ref, src_ref, o_ref, buf_ref, sem, *, bm):
    base = pl.program_id(0) * bm

    def row_copy(r):
        return pltpu.make_async_copy(src_ref.at[pl.ds(idx_ref[base + r], 1), :],
                                     buf_ref.at[pl.ds(r, 1), :], sem)

    def issue(r, c):
        row_copy(r).start()
        return c

    def drain(r, c):
        row_copy(r).wait()
        return c

    lax.fori_loop(0, bm, issue, 0)
    lax.fori_loop(0, bm, drain, 0)
    o_ref[...] = buf_ref[...].astype(o_ref.dtype)


def gather_rows(src, idx, bm=256):
    N = idx.shape[0]
    D = src.shape[1]
    return pl.pallas_call(
        functools.partial(_gather_kernel, bm=bm),
        grid_spec=pltpu.PrefetchScalarGridSpec(
            num_scalar_prefetch=1,
            grid=(N // bm,),
            in_specs=[pl.BlockSpec(memory_space=pl.ANY)],
            out_specs=pl.BlockSpec((bm, D), lambda i, idx: (i, 0)),
            scratch_shapes=[pltpu.VMEM((bm, D), F32), pltpu.SemaphoreType.DMA(())]),
        out_shape=jax.ShapeDtypeStruct((N, D), BF16),
        compiler_params=_cparams(1),
        name="gather_rows",
    )(idx, src)


def _gmm_swiglu_kernel(te_ref, na_ref, x_ref, wg_ref, wu_ref, o_ref, wgb_ref, wub_ref):
    m = pl.program_id(1)
    changed = (m == 0) | (te_ref[m] != te_ref[jnp.maximum(m - 1, 0)])

    @pl.when(changed)
    def _():
        wgb_ref[...] = wg_ref[...].astype(BF16)
        wub_ref[...] = wu_ref[...].astype(BF16)

    @pl.when(m < na_ref[0])
    def _():
        x = x_ref[...]
        g = _dot(x, wgb_ref[...])
        u = _dot(x, wub_ref[...])
        o_ref[...] = (_silu(g) * u).astype(o_ref.dtype)

    @pl.when(m >= na_ref[0])
    def _():
        o_ref[...] = jnp.zeros_like(o_ref)


def gmm_swiglu(x, w_gu, widx, tile_expert, n_active, *, bm, bn):
    M, K = x.shape
    F = w_gu.shape[3] // 2
    nf = F // bn
    return pl.pallas_call(
        _gmm_swiglu_kernel,
        grid_spec=pltpu.PrefetchScalarGridSpec(
            num_scalar_prefetch=2,
            grid=(nf, M // bm),
            in_specs=[pl.BlockSpec((bm, K), lambda n, m, te, na: (m, 0)),
                      pl.BlockSpec((None, None, K, bn), lambda n, m, te, na: (widx, te[m], 0, n)),
                      pl.BlockSpec((None, None, K, bn), lambda n, m, te, na: (widx, te[m], 0, n + nf))],
            out_specs=pl.BlockSpec((bm, bn), lambda n, m, te, na: (m, n)),
            scratch_shapes=[pltpu.VMEM((K, bn), BF16), pltpu.VMEM((K, bn), BF16)]),
        out_shape=jax.ShapeDtypeStruct((M, F), BF16),
        compiler_params=_cparams(2),
        name="gmm_swiglu",
    )(tile_expert, n_active, x, w_gu, w_gu)


def _gmm_kernel(te_ref, na_ref, x_ref, w_ref, o_ref, wb_ref):
    m = pl.program_id(1)
    changed = (m == 0) | (te_ref[m] != te_ref[jnp.maximum(m - 1, 0)])

    @pl.when(changed)
    def _():
        wb_ref[...] = w_ref[...].astype(BF16)

    @pl.when(m < na_ref[0])
    def _():
        o_ref[...] = _dot(x_ref[...], wb_ref[...])

    @pl.when(m >= na_ref[0])
    def _():
        o_ref[...] = jnp.zeros_like(o_ref)


def gmm(x, w, widx, tile_expert, n_active, *, bm, bn):
    M, K = x.shape
    N = w.shape[3]
    return pl.pallas_call(
        _gmm_kernel,
        grid_spec=pltpu.PrefetchScalarGridSpec(
            num_scalar_prefetch=2,
            grid=(N // bn, M // bm),
            in_specs=[pl.BlockSpec((bm, K), lambda n, m, te, na: (m, 0)),
                      pl.BlockSpec((None, None, K, bn), lambda n, m, te, na: (widx, te[m], 0, n))],
            out_specs=pl.BlockSpec((bm, bn), lambda n, m, te, na: (m, n)),
            scratch_shapes=[pltpu.VMEM((K, bn), BF16)]),
        out_shape=jax.ShapeDtypeStruct((M, N), F32),
        compiler_params=_cparams(2),
        name="gmm",
    )(tile_expert, n_active, x, w)


def _combine_kernel(pos_ref, y_ref, x_ref, pr_ref, g_ref, nw_ref, o_ref, b1_ref, b2_ref, sem, *, bm, n_tok):
    base = pl.program_id(0) * bm

    def copies(r):
        c1 = pltpu.make_async_copy(y_ref.at[pl.ds(pos_ref[base + r], 1), :], b1_ref.at[pl.ds(r, 1), :], sem.at[0])
        c2 = pltpu.make_async_copy(y_ref.at[pl.ds(pos_ref[n_tok + base + r], 1), :],
                                   b2_ref.at[pl.ds(r, 1), :], sem.at[1])
        return c1, c2

    def issue(r, c):
        c1, c2 = copies(r)
        c1.start()
        c2.start()
        return c

    def drain(r, c):
        c1, c2 = copies(r)
        c1.wait()
        c2.wait()
        return c

    lax.fori_loop(0, bm, issue, 0)
    lax.fori_loop(0, bm, drain, 0)
    pr = pr_ref[...]
    mix = pr[:, 2:3] * b1_ref[...] + pr[:, 3:4] * b2_ref[...]
    x = x_ref[...] + g_ref[...] * mix
    o_ref[...] = x * lax.rsqrt(jnp.mean(x * x, axis=-1, keepdims=True) + EPS) * nw_ref[...]


def moe_combine(ys, pos, x, route, gate, norm_w, rows, bm=256):
    T, D = x.shape
    return pl.pallas_call(
        functools.partial(_combine_kernel, bm=bm, n_tok=T),
        grid_spec=pltpu.PrefetchScalarGridSpec(
            num_scalar_prefetch=1,
            grid=(T // bm,),
            in_specs=[pl.BlockSpec(memory_space=pl.ANY),
                      pl.BlockSpec((bm, D), lambda i, pos: (i, 0)),
                      pl.BlockSpec((bm, LANES), lambda i, pos: (i, 0)),
                      pl.BlockSpec((None, 1, D), lambda i, pos: (rows.mod_row(i, bm), 0, 0)),
                      pl.BlockSpec((1, D), lambda i, pos: (0, 0))],
            out_specs=pl.BlockSpec((bm, D), lambda i, pos: (i, 0)),
            scratch_shapes=[pltpu.VMEM((bm, D), F32), pltpu.VMEM((bm, D), F32), pltpu.SemaphoreType.DMA((2,))]),
        out_shape=jax.ShapeDtypeStruct((T, D), F32),
        compiler_params=_cparams(1),
        name="moe_combine",
    )(pos, ys, x, route, gate, norm_w[None])


def moe_plan(route, n_tok, bm):
    e = jnp.concatenate([route[:, 0], route[:, 1]]).astype(jnp.int32)
    onehot = (e[:, None] == jnp.arange(N_EXPERTS, dtype=jnp.int32)[None, :]).astype(jnp.int32)
    rank = jnp.take_along_axis(jnp.cumsum(onehot, axis=0), e[:, None], axis=1)[:, 0] - 1
    counts = jnp.sum(onehot, axis=0)
    tiles = (counts + bm - 1) // bm
    tile_end = jnp.cumsum(tiles)
    start = (tile_end - tiles) * bm
    pos = start[e] + rank
    n_tiles = (2 * n_tok) // bm + N_EXPERTS
    tok = jnp.concatenate([jnp.arange(n_tok, dtype=jnp.int32)] * 2)
    src = jnp.zeros((n_tiles * bm,), jnp.int32).at[pos].set(tok)
    n_active = tile_end[-1]
    tile_ids = jnp.minimum(jnp.arange(n_tiles, dtype=jnp.int32), n_active - 1)
    tile_expert = jnp.sum((tile_ids[:, None] >= tile_end[None, :]).astype(jnp.int32), axis=1)
    return pos.astype(jnp.int32), src, tile_expert.astype(jnp.int32), n_active.reshape(1).astype(jnp.int32)


def kernel(x, c, ctx, c_ctx, ada_w, ada_b, ab_w_in, hy_conv_w, hy_conv_b, hy_filt_w1, hy_filt_b1, hy_filt_w2, hy_filt_b2, hy_filt_w3, hy_filt_freq, hy_bias, ml_conv_w, ml_conv_b, ml_gate_b, ml_norm_w, ab_w_out, ffn_w_gu, ffn_w_down, cd_w_in, ssd_conv_w, ssd_conv_b, ssd_A_log, ssd_dt_bias, ssd_D, ssd_norm_w, s5_lam_re, s5_lam_im, s5_log_dt, s5_B_re, s5_B_im, s5_C_re, s5_C_im, s5_D, s5_glu_w, cd_w_out, moe_router, moe_w_gu, moe_w_down, final_norm_w):
    bsz, L, D = x.shape
    Lc = ctx.shape[1]
    rows = Rows(bsz, L, Lc)
    n_lat = rows.n_lat
    assert bsz == 4 and L % 1024 == 0 and Lc == 256 and D % 256 == 0
    bn_d = min(1024, D)
    bn_f = min(512, ffn_w_down.shape[1])

    xs = jnp.concatenate([x.reshape(n_lat, D), ctx.reshape(bsz * Lc, D)], axis=0)
    cond = jnp.concatenate([c, c_ctx[None], jnp.zeros((8 - bsz - 1, D), F32)], axis=0)

    def mods(layer):
        m = adaln(cond, ada_w, ada_b, layer)
        return [m[:, k * D:(k + 1) * D].reshape(8, 1, D) for k in range(6)]

    md = mods(0)
    h = modulate(xs, md[0], md[1], rows, rows.n, BF16)
    n_main = 3 * HY_D + 4 * ML_H * ML_DH
    p = mm(h, ab_w_in, 0, n_cols=n_main, bm=1024, bn=1024)
    w_gate = jnp.pad(ab_w_in[:, :, n_main:], ((0, 0), (0, 0), (0, LANES - 4 * ML_H)))
    gates = mm(h, w_gate, 0, n_cols=LANES, bm=1024, bn=LANES)
    gates_t = gates[:, :4 * ML_H].T

    filt = (hy_filt_w1, hy_filt_b1, hy_filt_w2, hy_filt_b2, hy_filt_w3, hy_filt_freq)
    yh_lat = hyena(p, L, bsz, 0, GRID_W, 0, hy_conv_w, hy_conv_b, filt, hy_bias)
    yh_ctx = hyena(p, Lc, bsz, n_lat, Lc, 0, hy_conv_w, hy_conv_b, filt, hy_bias)

    qk = conv_silu(p, 3 * HY_D, 2 * ML_H * ML_DH, ml_conv_w, ml_conv_b, 0, rows)
    hdir = mlstm(qk, p, 3 * HY_D + 2 * ML_H * ML_DH, gates, gates_t, ml_gate_b[0], rows)
    cat = finish_ab(yh_lat, yh_ctx, hdir, p, 3 * HY_D + 3 * ML_H * ML_DH, ml_norm_w, 0, rows)
    xs = mm_residual(cat, ab_w_out, 0, xs, md[2], rows, bm=1024, bn=bn_d)

    h = modulate(xs, md[3], md[4], rows, rows.n, BF16)
    act = mm_swiglu(h, ffn_w_gu, 0, bm=1024, bn=bn_f)
    xs = mm_residual(act, ffn_w_down, 0, xs, md[5], rows, bm=512, bn=min(512, D))

    md = mods(1)
    h = modulate(xs, md[0], md[1], rows, rows.n, BF16)
    nd = SSD_H * SSD_P
    n_xbc = nd + 4 * LANES
    p1 = mm(h, cd_w_in, 0, n_cols=nd + n_xbc, bm=1024, bn=512)
    w_tail = cd_w_in[:, :, nd + n_xbc:]
    w_tail = jnp.concatenate([w_tail[:, :, 2 * SSD_H:], w_tail[:, :, :2 * SSD_H],
                              jnp.zeros((1, D, LANES - 2 * SSD_H), F32)], axis=2)
    p2 = mm(h, w_tail, 0, n_cols=S5_W + LANES, bm=1024, bn=S5_W + LANES)

    xbc = conv_silu(p1, nd, n_xbc, ssd_conv_w, ssd_conv_b, 0, rows)
    dt = p2[:, S5_W:]
    dt_t = dt[:, :2 * SSD_H].T
    ydir = ssd(xbc, dt, dt_t, ssd_A_log[0], ssd_dt_bias[0], rows)

    w_cat, c_cat, lb_re, lb_im, lam_rows = s5_params(
        s5_lam_re[0], s5_lam_im[0], s5_log_dt[0], s5_B_re[0], s5_B_im[0], s5_C_re[0], s5_C_im[0])
    u_lat = p2[:n_lat, :S5_W].reshape(bsz, L, S5_W)
    u_ctx = p2[n_lat:, :S5_W].reshape(bsz, Lc, S5_W)
    seq_f = jnp.concatenate([u_ctx, u_lat], axis=1)
    seq_b = jnp.concatenate([u_ctx[:, ::-1], u_lat[:, ::-1]], axis=1)
    u_arr = jnp.transpose(jnp.stack([seq_f, seq_b]), (2, 0, 1, 3)).reshape((L + Lc) * 2 * bsz, S5_W)
    y_arr = s5(u_arr, w_cat, c_cat, lam_rows(lb_re, bsz), lam_rows(lb_im, bsz), L + Lc)
    y_arr = y_arr.reshape(L + Lc, 2, bsz, S5_W)[Lc:]
    sf = jnp.transpose(y_arr[:, 0], (1, 0, 2)).reshape(n_lat, S5_W)
    sb = jnp.transpose(y_arr[::-1, 1], (1, 0, 2)).reshape(n_lat, S5_W)

    ssd_d_lanes = jnp.repeat(ssd_D, SSD_P, axis=1).reshape(ssd_D.shape[0], 1, nd)
    cat = finish_cd(ydir, xbc, p1, sf, sb, p2, ssd_d_lanes, ssd_norm_w, s5_D, s5_glu_w, 0, n_lat)
    xl = mm_residual(cat, cd_w_out, 0, xs, md[2], rows, bm=1024, bn=bn_d)

    h2 = modulate(xl, md[3], md[4], rows, n_lat, F32)
    route = router(h2, jnp.pad(moe_router[0], ((0, 0), (0, LANES - N_EXPERTS))))
    bm_e = 512
    pos, src, tile_expert, n_active = moe_plan(route, n_lat, bm_e)
    xg = gather_rows(h2, src, bm=256)
    act = gmm_swiglu(xg, moe_w_gu, 0, tile_expert, n_active, bm=bm_e, bn=min(512, moe_w_down.shape[2]))
    ys = gmm(act, moe_w_down, 0, tile_expert, n_active, bm=bm_e, bn=min(512, D))
    out = moe_combine(ys, pos, xl, route, md[5], final_norm_w, rows, bm=256)
    return out.reshape(bsz, L, D)
```

```python
import functools
import math
from typing import NamedTuple

import jax
import jax.numpy as jnp
import numpy as np
from jax import lax
from jax.experimental import pallas as pl
from jax.experimental.pallas import tpu as pltpu

F32 = jnp.float32
BF16 = jnp.bfloat16

EPS = 1e-6
GRID_W = 64
CHUNK = 128
LANES = 128
HY_D = 1024
HY_EMB = 33
HY_FAST_DECAY = 0.3
HY_SLOW_DECAY = 1.5
HY_DECAY_TARGET = 1e-2
ML_H = 8
ML_DH = 128
SSD_H = 16
SSD_P = 64
S5_W = 512
S5_G = 32
S5_GS = 16
S5_P = 64
N_EXPERTS = 8
VMEM_LIMIT_BYTES = 56 * 1024 * 1024


def _cparams(n_axes):
    return pltpu.CompilerParams(dimension_semantics=("arbitrary",) * n_axes,
                                vmem_limit_bytes=VMEM_LIMIT_BYTES)


def _sigmoid(x):
    return 1.0 / (1.0 + jnp.exp(-x))


def _silu(x):
    return x * _sigmoid(x)


def _log_sigmoid(x):
    return jnp.minimum(x, 0.0) - jnp.log(1.0 + jnp.exp(-jnp.abs(x)))


def _softplus(x):
    return jnp.maximum(x, 0.0) + jnp.log(1.0 + jnp.exp(-jnp.abs(x)))


def _dot(a, b):
    return jnp.dot(a, b, preferred_element_type=F32)


def _dot_nt(a, b):
    return lax.dot_general(a, b, (((1,), (1,)), ((), ())), preferred_element_type=F32)


def _dot_tn(a, b):
    return lax.dot_general(a, b, (((0,), (0,)), ((), ())), preferred_element_type=F32)


def _dot_hi(a, b):
    return jnp.dot(a, b, preferred_element_type=F32, precision=lax.Precision.HIGHEST)


class Rows:
    def __init__(self, bsz, L, Lc):
        self.B, self.L, self.Lc = bsz, L, Lc
        self.n_lat = bsz * L
        self.n = bsz * (L + Lc)

    def mod_row(self, i, bm):
        n_lat_blocks = self.n_lat // bm
        return jnp.where(i < n_lat_blocks, (i * bm) // self.L, self.B)

    def chunk_block(self, b, c):
        ncc = self.Lc // CHUNK
        return jnp.where(c < ncc, self.n_lat // CHUNK + b * ncc + c, b * (self.L // CHUNK) + c - ncc)

    def scan_chunk(self, d, j):
        ncc = self.Lc // CHUNK
        nct = (self.L + self.Lc) // CHUNK
        back = jnp.where(j < ncc, ncc - 1 - j, nct - 1 + ncc - j)
        return jnp.where(d == 0, j, back)


def _adaln_kernel(c_ref, w_ref, b_ref, o_ref):
    cond = _silu(c_ref[...]).astype(BF16)
    o_ref[...] = _dot(cond, w_ref[...].astype(BF16)) + b_ref[...]


def adaln(cond, ada_w, ada_b, layer, bn=1024):
    bn = min(bn, ada_w.shape[1])
    _, D, N = ada_w.shape
    return pl.pallas_call(
        _adaln_kernel,
        grid=(N // bn,),
        in_specs=[pl.BlockSpec((8, D), lambda n: (0, 0)),
                  pl.BlockSpec((None, D, bn), lambda n: (layer, 0, n)),
                  pl.BlockSpec((None, 1, bn), lambda n: (layer, 0, n))],
        out_specs=pl.BlockSpec((8, bn), lambda n: (0, n)),
        out_shape=jax.ShapeDtypeStruct((8, N), F32),
        compiler_params=_cparams(1),
        name="adaln",
    )(cond, ada_w, ada_b.reshape(ada_b.shape[0], 1, N))


def _modulate_kernel(x_ref, sh_ref, sc_ref, o_ref):
    x = x_ref[...]
    y = x * lax.rsqrt(jnp.mean(x * x, axis=-1, keepdims=True) + EPS)
    o_ref[...] = (y * (1.0 + sc_ref[...]) + sh_ref[...]).astype(o_ref.dtype)


def modulate(x, shift, scale, rows, n_rows, out_dtype, bm=512):
    D = x.shape[1]
    mod_spec = pl.BlockSpec((None, 1, D), lambda i: (rows.mod_row(i, bm), 0, 0))
    return pl.pallas_call(
        _modulate_kernel,
        grid=(n_rows // bm,),
        in_specs=[pl.BlockSpec((bm, D), lambda i: (i, 0)), mod_spec, mod_spec],
        out_specs=pl.BlockSpec((bm, D), lambda i: (i, 0)),
        out_shape=jax.ShapeDtypeStruct((n_rows, D), out_dtype),
        compiler_params=_cparams(1),
        name="modulate",
    )(x, shift, scale)


def _mm_kernel(x_ref, w_ref, o_ref, wb_ref):
    @pl.when(pl.program_id(1) == 0)
    def _():
        wb_ref[...] = w_ref[...].astype(BF16)
    o_ref[...] = _dot(x_ref[...], wb_ref[...]).astype(o_ref.dtype)


def mm(x, w, widx, *, n_cols, col_off=0, bm, bn, out_dtype=F32):
    M, K = x.shape
    off = col_off // bn
    return pl.pallas_call(
        _mm_kernel,
        grid=(n_cols // bn, M // bm),
        in_specs=[pl.BlockSpec((bm, K), lambda n, m: (m, 0)),
                  pl.BlockSpec((None, K, bn), lambda n, m: (widx, 0, n + off))],
        out_specs=pl.BlockSpec((bm, bn), lambda n, m: (m, n)),
        out_shape=jax.ShapeDtypeStruct((M, n_cols), out_dtype),
        scratch_shapes=[pltpu.VMEM((K, bn), BF16)],
        compiler_params=_cparams(2),
        name="mm",
    )(x, w)


def _mm_res_kernel(x_ref, w_ref, r_ref, g_ref, o_ref, wb_ref):
    @pl.when(pl.program_id(1) == 0)
    def _():
        wb_ref[...] = w_ref[...].astype(BF16)
    o_ref[...] = r_ref[...] + g_ref[...] * _dot(x_ref[...], wb_ref[...])


def mm_residual(x, w, widx, res, gate, rows, *, bm, bn):
    M, K = x.shape
    N = w.shape[2]
    return pl.pallas_call(
        _mm_res_kernel,
        grid=(N // bn, M // bm),
        in_specs=[pl.BlockSpec((bm, K), lambda n, m: (m, 0)),
                  pl.BlockSpec((None, K, bn), lambda n, m: (widx, 0, n)),
                  pl.BlockSpec((bm, bn), lambda n, m: (m, n)),
                  pl.BlockSpec((None, 1, bn), lambda n, m: (rows.mod_row(m, bm), 0, n))],
        out_specs=pl.BlockSpec((bm, bn), lambda n, m: (m, n)),
        out_shape=jax.ShapeDtypeStruct((M, N), F32),
        scratch_shapes=[pltpu.VMEM((K, bn), BF16)],
        compiler_params=_cparams(2),
        name="mm_residual",
    )(x, w, res, gate)


def _mm_swiglu_kernel(x_ref, wg_ref, wu_ref, o_ref, wgb_ref, wub_ref):
    @pl.when(pl.program_id(1) == 0)
    def _():
        wgb_ref[...] = wg_ref[...].astype(BF16)
        wub_ref[...] = wu_ref[...].astype(BF16)
    x = x_ref[...]
    g = _dot(x, wgb_ref[...])
    u = _dot(x, wub_ref[...])
    o_ref[...] = (_silu(g) * u).astype(o_ref.dtype)


def mm_swiglu(x, w_gu, widx, *, bm, bn):
    M, K = x.shape
    F = w_gu.shape[2] // 2
    nf = F // bn
    return pl.pallas_call(
        _mm_swiglu_kernel,
        grid=(nf, M // bm),
        in_specs=[pl.BlockSpec((bm, K), lambda n, m: (m, 0)),
                  pl.BlockSpec((None, K, bn), lambda n, m: (widx, 0, n)),
                  pl.BlockSpec((None, K, bn), lambda n, m: (widx, 0, n + nf))],
        out_specs=pl.BlockSpec((bm, bn), lambda n, m: (m, n)),
        out_shape=jax.ShapeDtypeStruct((M, F), BF16),
        scratch_shapes=[pltpu.VMEM((K, bn), BF16), pltpu.VMEM((K, bn), BF16)],
        compiler_params=_cparams(2),
        name="mm_swiglu",
    )(x, w_gu, w_gu)


def _conv3(x, w_ref, b_ref, period):
    n = x.shape[0]
    pos = lax.broadcasted_iota(jnp.int32, x.shape, 0) & (period - 1)
    prev = jnp.where(pos == 0, 0.0, pltpu.roll(x, 1, 0))
    nxt = jnp.where(pos == period - 1, 0.0, pltpu.roll(x, n - 1, 0))
    w = w_ref[...]
    return b_ref[...] + prev * w[0:1] + x * w[1:2] + nxt * w[2:3]


def _conv_silu_kernel(x_ref, w_ref, b_ref, o_ref, *, n_lat_blocks, lat_period, ctx_period):
    period = jnp.where(pl.program_id(0) < n_lat_blocks, lat_period, ctx_period)
    o_ref[...] = _silu(_conv3(x_ref[...], w_ref, b_ref, period)).astype(o_ref.dtype)


def conv_silu(p, col_off, n_cols, w, b, widx, rows, *, bt=256, bc=512):
    off = col_off // bc
    kern = functools.partial(_conv_silu_kernel, n_lat_blocks=rows.n_lat // bt,
                             lat_period=GRID_W, ctx_period=rows.Lc)
    return pl.pallas_call(
        kern,
        grid=(rows.n // bt, n_cols // bc),
        in_specs=[pl.BlockSpec((bt, bc), lambda i, j: (i, j + off)),
                  pl.BlockSpec((None, 3, bc), lambda i, j: (widx, 0, j)),
                  pl.BlockSpec((None, 1, bc), lambda i, j: (widx, 0, j))],
        out_specs=pl.BlockSpec((bt, bc), lambda i, j: (i, j)),
        out_shape=jax.ShapeDtypeStruct((rows.n, n_cols), F32),
        compiler_params=_cparams(2),
        name="conv_silu",
    )(p, w, b.reshape(b.shape[0], 1, b.shape[1]))


def _hyena_conv_kernel(p0_ref, p1_ref, p2_ref, w0_ref, w1_ref, w2_ref, b0_ref, b1_ref, b2_ref,
                       x0_ref, u_ref, *, period):
    x0_ref[...] = _conv3(p0_ref[...], w0_ref, b0_ref, period)
    x1 = _conv3(p1_ref[...], w1_ref, b1_ref, period)
    v = _conv3(p2_ref[...], w2_ref, b2_ref, period)
    u_ref[...] = (x1 * v).astype(u_ref.dtype)


def hyena_conv(p, w, b, widx, *, row0, bsz, seq, period, bt=256, bc=512):
    nt = seq // bt
    ncb = HY_D // bc
    rb0 = row0 // bt

    def pspec(k):
        return pl.BlockSpec((bt, bc), lambda bb, i, j: (rb0 + bb * nt + i, j + k * ncb))

    def wspec(k):
        return pl.BlockSpec((None, 3, bc), lambda bb, i, j: (widx, 0, j + k * ncb))

    def bspec(k):
        return pl.BlockSpec((None, 1, bc), lambda bb, i, j: (widx, 0, j + k * ncb))

    ospec = pl.BlockSpec((bt, bc), lambda bb, i, j: (i, bb * ncb + j))
    b3 = b.reshape(b.shape[0], 1, b.shape[1])
    return pl.pallas_call(
        functools.partial(_hyena_conv_kernel, period=period),
        grid=(bsz, nt, ncb),
        in_specs=[pspec(0), pspec(1), pspec(2), wspec(0), wspec(1), wspec(2), bspec(0), bspec(1), bspec(2)],
        out_specs=[ospec, ospec],
        out_shape=[jax.ShapeDtypeStruct((seq, bsz * HY_D), F32),
                   jax.ShapeDtypeStruct((seq, bsz * HY_D), BF16)],
        compiler_params=_cparams(3),
        name="hyena_conv",
    )(p, p, p, w, w, w, b3, b3, b3)


def _hyena_feats(L):
    pos = np.arange(L, dtype=np.float64)
    t = pos / max(L - 1, 1)
    n_bands = (HY_EMB - 1) // 2
    bands = np.linspace(1e-4, n_bands - 1, n_bands)
    ang = (2 * math.pi / L) * pos[:, None] * bands[None, :]
    feats = np.concatenate([t[:, None], np.cos(ang), -np.sin(ang)], axis=-1)
    feats = np.pad(feats, ((0, 0), (0, LANES - HY_EMB)))
    deltas = np.abs(np.linspace(math.log(HY_DECAY_TARGET) / HY_SLOW_DECAY,
                                math.log(HY_DECAY_TARGET) / HY_FAST_DECAY, HY_D))
    return feats.astype(np.float32), t.astype(np.float32)[:, None], deltas.astype(np.float32)[None, :]


def _hyena_filter_kernel(feats_ref, t_ref, dl_ref, w1_ref, b1_ref, w2_ref, b2_ref, w3f_ref, w3b_ref,
                         fq_ref, hf_ref, hb_ref):
    fq = fq_ref[...]
    h = jnp.sin(fq[0:1] * (_dot_hi(feats_ref[...], w1_ref[...]) + b1_ref[...]))
    h = jnp.sin(fq[1:2] * (_dot_hi(h, w2_ref[...]) + b2_ref[...]))
    win = jnp.exp(-t_ref[...] * dl_ref[...])
    h_f = _dot_hi(h, w3f_ref[...]) * win
    h_b = _dot_hi(h, w3b_ref[...]) * win
    row = lax.broadcasted_iota(jnp.int32, h_b.shape, 0)
    h_b = jnp.where(row == 0, 0.0, h_b)
    l1 = jnp.sum(jnp.abs(h_f), axis=0, keepdims=True) + jnp.sum(jnp.abs(h_b), axis=0, keepdims=True)
    hf_ref[...] = h_f / l1
    hb_ref[...] = h_b / l1


def hyena_filter(L, w1, b1, w2, b2, w3, freq, widx, bc=256):
    feats, t, deltas = _hyena_feats(L)
    hid = w2.shape[1]
    w1p = jnp.pad(w1[widx], ((0, LANES - HY_EMB), (0, 0)))
    ncb = HY_D // bc
    full = lambda shape: pl.BlockSpec(shape, lambda j: (0,) * len(shape))
    return pl.pallas_call(
        _hyena_filter_kernel,
        grid=(ncb,),
        in_specs=[full((L, LANES)), full((L, 1)), pl.BlockSpec((1, bc), lambda j: (0, j)),
                  full((LANES, hid)), full((1, hid)), full((hid, hid)), full((1, hid)),
                  pl.BlockSpec((hid, bc), lambda j: (0, j)),
                  pl.BlockSpec((hid, bc), lambda j: (0, j + ncb)),
                  full((2, hid))],
        out_specs=[pl.BlockSpec((L, bc), lambda j: (0, j)), pl.BlockSpec((L, bc), lambda j: (0, j))],
        out_shape=[jax.ShapeDtypeStruct((L, HY_D), F32), jax.ShapeDtypeStruct((L, HY_D), F32)],
        compiler_params=_cparams(1),
        name="hyena_filter",
    )(jnp.asarray(feats), jnp.asarray(t), jnp.asarray(deltas), w1p, b1[widx][None], w2[widx],
      b2[widx][None], w3[widx], w3[widx], freq[widx])


def _dft_matrix(L):
    f = np.arange(L, dtype=np.int64)[:, None]
    s = np.arange(L, dtype=np.int64)[None, :]
    ang = (math.pi / L) * ((f * s) % (2 * L)).astype(np.float64)
    a_cos = np.cos(ang)
    a_sin = -np.sin(ang)
    a_sin[0, :] = np.where(np.arange(L) % 2 == 0, 1.0, -1.0)
    return np.concatenate([a_cos, a_sin], axis=0).astype(np.float32)


def _spectrum_kernel(ac_ref, as_ref, hf_ref, hb_ref, kre_ref, kim_ref, kny_ref, *, L, bf):
    hf = hf_ref[...].astype(BF16)
    hb = hb_ref[...].astype(BF16)
    ac = ac_ref[...].astype(BF16)
    a_s = as_ref[...].astype(BF16)
    cf, cb = _dot(ac, hf), _dot(ac, hb)
    sf, sb = _dot(a_s, hf), _dot(a_s, hb)
    f = lax.broadcasted_iota(jnp.int32, cf.shape, 0) + pl.program_id(1) * bf
    wgt = jnp.where(f == 0, 0.5 / L, 1.0 / L)
    kre_ref[...] = (cf + cb) * wgt
    kim_ref[...] = jnp.where(f == 0, 0.0, (sf - sb) * wgt)
    kny_ref[...] = jnp.where(f == 0, (sf + sb) * wgt, (cf + cb) * wgt)


def hyena_spectrum(a_mat, hf, hb, L, bf=256, bc=512):
    bf = min(bf, L)
    nf = L // bf
    C = hf.shape[1]
    ospec = pl.BlockSpec((bf, bc), lambda c, i: (i, c))
    hspec = pl.BlockSpec((L, bc), lambda c, i: (0, c))
    return pl.pallas_call(
        functools.partial(_spectrum_kernel, L=L, bf=bf),
        grid=(C // bc, nf),
        in_specs=[pl.BlockSpec((bf, L), lambda c, i: (i, 0)), pl.BlockSpec((bf, L), lambda c, i: (i + nf, 0)),
                  hspec, hspec],
        out_specs=[ospec, ospec, ospec],
        out_shape=[jax.ShapeDtypeStruct((L, C), F32)] * 3,
        compiler_params=_cparams(2),
        name="hyena_spectrum",
    )(a_mat, a_mat, hf, hb)


def _lc_fwd_kernel(ac_ref, as_ref, u_ref, kre_ref, kim_ref, kny_ref, yre_ref, yim_ref, acb_ref, asb_ref):
    @pl.when(pl.program_id(1) == 0)
    def _():
        acb_ref[...] = ac_ref[...].astype(BF16)
        asb_ref[...] = as_ref[...].astype(BF16)
    u = u_ref[...]
    ure = _dot(acb_ref[...], u)
    uim = _dot(asb_ref[...], u)
    kim = kim_ref[...]
    yre_ref[...] = (kre_ref[...] * ure - kim * uim).astype(yre_ref.dtype)
    yim_ref[...] = (kny_ref[...] * uim + kim * ure).astype(yim_ref.dtype)


def long_conv_fwd(a_mat, u, kre, kim, kny, L, bsz, bf=256):
    bf = min(bf, L)
    nf = L // bf
    C = kre.shape[1]
    kspec = pl.BlockSpec((bf, C), lambda i, b: (i, 0))
    ospec = pl.BlockSpec((bf, C), lambda i, b: (i, b))
    return pl.pallas_call(
        _lc_fwd_kernel,
        grid=(nf, bsz),
        in_specs=[pl.BlockSpec((bf, L), lambda i, b: (i, 0)), pl.BlockSpec((bf, L), lambda i, b: (i + nf, 0)),
                  pl.BlockSpec((L, C), lambda i, b: (0, b)), kspec, kspec, kspec],
        out_specs=[ospec, ospec],
        out_shape=[jax.ShapeDtypeStruct((L, bsz * C), BF16)] * 2,
        scratch_shapes=[pltpu.VMEM((bf, L), BF16), pltpu.VMEM((bf, L), BF16)],
        compiler_params=_cparams(2),
        name="long_conv_fwd",
    )(a_mat, a_mat, u, kre, kim, kny)


def _lc_inv_kernel(atc_ref, ats_ref, yre_ref, yim_ref, x0_ref, u_ref, bias_ref, o_ref, atcb_ref, atsb_ref):
    @pl.when(pl.program_id(1) == 0)
    def _():
        atcb_ref[...] = atc_ref[...].astype(BF16)
        atsb_ref[...] = ats_ref[...].astype(BF16)
    y = _dot(atcb_ref[...], yre_ref[...]) + _dot(atsb_ref[...], yim_ref[...])
    u = u_ref[...].astype(F32)
    o_ref[...] = (x0_ref[...] * (y + bias_ref[...] * u)).astype(o_ref.dtype)


def long_conv_inv(at_mat, yre, yim, x0, u, bias, widx, L, bsz, bt=256):
    bt = min(bt, L)
    nt = L // bt
    C = bias.shape[1]
    tspec = pl.BlockSpec((bt, C), lambda i, b: (i, b))
    yspec = pl.BlockSpec((L, C), lambda i, b: (0, b))
    return pl.pallas_call(
        _lc_inv_kernel,
        grid=(nt, bsz),
        in_specs=[pl.BlockSpec((bt, L), lambda i, b: (i, 0)), pl.BlockSpec((bt, L), lambda i, b: (i, 1)),
                  yspec, yspec, tspec, tspec, pl.BlockSpec((None, 1, C), lambda i, b: (widx, 0, 0))],
        out_specs=tspec,
        out_shape=jax.ShapeDtypeStruct((L, bsz * C), BF16),
        scratch_shapes=[pltpu.VMEM((bt, L), BF16), pltpu.VMEM((bt, L), BF16)],
        compiler_params=_cparams(2),
        name="long_conv_inv",
    )(at_mat, at_mat, yre, yim, x0, u, bias.reshape(bias.shape[0], 1, C))


def hyena(p, L, bsz, row0, period, widx, conv_w, conv_b, filt, bias):
    a_np = _dft_matrix(L)
    a_mat = jnp.asarray(a_np)
    at_mat = jnp.asarray(np.ascontiguousarray(a_np.T))
    x0, u = hyena_conv(p, conv_w, conv_b, widx, row0=row0, bsz=bsz, seq=L, period=period, bt=min(256, L))
    hf, hb = hyena_filter(L, *filt, widx)
    kre, kim, kny = hyena_spectrum(a_mat, hf, hb, L)
    yre, yim = long_conv_fwd(a_mat, u, kre, kim, kny, L, bsz)
    return long_conv_inv(at_mat, yre, yim, x0, u, bias, widx, L, bsz)


def _pick_col(g, idx):
    lane = lax.broadcasted_iota(jnp.int32, g.shape, 1)
    return jnp.sum(jnp.where(lane == idx, g, 0.0), axis=1, keepdims=True)


def _chunk_masks(d, n):
    t_i = lax.broadcasted_iota(jnp.int32, (n, n), 0)
    s_i = lax.broadcasted_iota(jnp.int32, (n, n), 1)
    lag = (t_i - s_i) * jnp.where(d == 0, 1, -1)
    return lag >= 0, lag <= 0


def _running_sums(x_cols, x_rows, causal, causal_t):
    tri = jnp.where(causal, 1.0, 0.0)
    tri_t = jnp.where(causal_t, 1.0, 0.0)
    return _dot_hi(tri, x_cols), _dot_hi(x_rows, tri_t)


def _mlstm_kernel(q_ref, k_ref, v_ref, gc_ref, gr_ref, bc_ref, br_ref, o_ref, c_ref, n_ref, m_ref, gs_ref):
    d = pl.program_id(1)

    @pl.when(pl.program_id(2) == 0)
    def _():
        c_ref[...] = jnp.zeros_like(c_ref)
        n_ref[...] = jnp.zeros_like(n_ref)
        m_ref[...] = jnp.zeros_like(m_ref)

    gcol = gc_ref[...] + bc_ref[...]
    lane = lax.broadcasted_iota(jnp.int32, gcol.shape, 1)
    gcol = jnp.where((lane & ML_H) != 0, _log_sigmoid(gcol), gcol)
    grow = gr_ref[...] + br_ref[...]
    sub = lax.broadcasted_iota(jnp.int32, grow.shape, 0)
    grow = jnp.where((sub & ML_H) != 0, _log_sigmoid(grow), grow)

    causal, causal_t = _chunk_masks(d, CHUNK)
    ccol, crow = _running_sums(gcol, grow, causal, causal_t)
    gs_ref[0] = grow
    gs_ref[1] = crow

    def head(h):
        sl = pl.ds(h * ML_DH, ML_DH)
        return _mlstm_head(h, d, q_ref.at[:, sl], k_ref.at[:, sl], v_ref.at[:, sl], o_ref.at[:, sl], gcol, ccol,
                           gs_ref, causal, c_ref.at[h], n_ref.at[h], m_ref.at[h])

    _run_interleaved([head(h) for h in range(ML_H)])


def _run_interleaved(stages):
    live = list(stages)
    while live:
        still = []
        for gen in live:
            try:
                next(gen)
                still.append(gen)
            except StopIteration:
                pass
        live = still


def _mlstm_head(h, d, q_ref, k_ref, v_ref, o_ref, gcol, ccol, gs_ref, causal, c_ref, n_ref, m_ref):
    i_idx = 2 * ML_H * d + h
    f_idx = i_idx + ML_H
    ig_row = gs_ref[0, pl.ds(i_idx, 1), :]
    ig_col = _pick_col(gcol, i_idx)
    bcum_row = gs_ref[1, pl.ds(f_idx, 1), :]
    bcum_col = _pick_col(ccol, f_idx)
    g = jnp.sum(gs_ref[0, pl.ds(f_idx, 1), :], axis=1, keepdims=True)

    q = q_ref[...] * (ML_DH ** -0.5)
    k = k_ref[...]
    qb, kb, vb = q.astype(BF16), k.astype(BF16), v_ref[...].astype(BF16)
    c_in, n_in, m_in = c_ref[...], n_ref[...], m_ref[0:1, 0:1]

    dmat = jnp.where(causal, bcum_col - bcum_row + ig_row, -jnp.inf)
    m_inter = bcum_col + m_in
    m_t = jnp.maximum(jnp.max(dmat, axis=1, keepdims=True), m_inter)
    decay = jnp.exp(dmat - m_t)
    w_inter = jnp.exp(m_inter - m_t)
    a_row = g - bcum_row + ig_row
    a_col = g - bcum_col + ig_col
    m_loc = jnp.max(a_row, axis=1, keepdims=True)
    kw = k * jnp.exp(a_col - m_loc)
    qk = _dot_nt(qb, kb)
    qc = _dot(qb, c_in.astype(BF16))
    c_loc = _dot_tn(kw.astype(BF16), vb)
    yield

    s = qk * decay
    sv = _dot(s.astype(BF16), vb)
    den = jnp.sum(s, axis=1, keepdims=True) + w_inter * jnp.sum(q * n_in, axis=1, keepdims=True)
    yield

    num = sv + w_inter * qc
    o_ref[...] = num / jnp.maximum(jnp.abs(den), jnp.exp(-m_t))
    n_loc = jnp.sum(kw, axis=0, keepdims=True)
    m_new = jnp.maximum(g + m_in, m_loc)
    s_prev = jnp.exp(g + m_in - m_new)
    s_loc = jnp.exp(m_loc - m_new)
    c_ref[...] = s_prev * c_in + s_loc * c_loc
    n_ref[...] = s_prev * n_in + s_loc * n_loc
    m_ref[...] = jnp.broadcast_to(m_new, m_ref.shape)


def mlstm(qk, p, v_col_off, gates, gates_t, gate_b, rows):
    nct = (rows.L + rows.Lc) // CHUNK
    W = ML_H * ML_DH
    voff = v_col_off // W

    def rb(b, d, j):
        return rows.chunk_block(b, rows.scan_chunk(d, j))

    ng = 4 * ML_H
    bias = gate_b.reshape(ng)
    bias_row = jnp.pad(bias, (0, LANES - ng))[None]
    return pl.pallas_call(
        _mlstm_kernel,
        grid=(rows.B, 2, nct),
        in_specs=[pl.BlockSpec((CHUNK, W), lambda b, d, j: (rb(b, d, j), 0)),
                  pl.BlockSpec((CHUNK, W), lambda b, d, j: (rb(b, d, j), 1)),
                  pl.BlockSpec((CHUNK, W), lambda b, d, j: (rb(b, d, j), voff)),
                  pl.BlockSpec((CHUNK, LANES), lambda b, d, j: (rb(b, d, j), 0)),
                  pl.BlockSpec((ng, CHUNK), lambda b, d, j: (0, rb(b, d, j))),
                  pl.BlockSpec((1, LANES), lambda b, d, j: (0, 0)),
                  pl.BlockSpec((ng, 1), lambda b, d, j: (0, 0))],
        out_specs=pl.BlockSpec((None, CHUNK, W), lambda b, d, j: (d, rb(b, d, j), 0)),
        out_shape=jax.ShapeDtypeStruct((2, rows.n, W), F32),
        scratch_shapes=[pltpu.VMEM((ML_H, ML_DH, ML_DH), F32), pltpu.VMEM((ML_H, 1, ML_DH), F32),
                        pltpu.VMEM((ML_H, 8, LANES), F32), pltpu.VMEM((2, ng, CHUNK), F32)],
        compiler_params=_cparams(3),
        name="mlstm",
    )(qk, qk, p, gates, gates_t, bias_row, bias[:, None])


def _finish_ab_kernel(yh_ref, h_ref, o_ref, nw_ref, out_ref):
    out_ref[:, 0:HY_D] = yh_ref[...]
    hs = h_ref[0] + h_ref[1]
    og = _sigmoid(o_ref[...])
    nw = nw_ref[...]
    for i in range(ML_H):
        sl = slice(i * ML_DH, (i + 1) * ML_DH)
        x = hs[:, sl]
        mu = jnp.mean(x, axis=-1, keepdims=True)
        xc = x - mu
        var = jnp.mean(xc * xc, axis=-1, keepdims=True)
        y = xc * lax.rsqrt(var + EPS) * nw[:, sl] * og[:, sl]
        out_ref[:, HY_D + i * ML_DH:HY_D + (i + 1) * ML_DH] = y.astype(out_ref.dtype)


def finish_ab(yh_lat, yh_ctx, hdir, p, o_col_off, norm_w, widx, rows, bt=256):
    n_lat_blocks = rows.n_lat // bt
    per_b = rows.L // bt
    ooff = o_col_off // (ML_H * ML_DH)
    W = ML_H * ML_DH

    def call(yh, row_block0, n_blocks, yh_map):
        return pl.pallas_call(
            _finish_ab_kernel,
            grid=(n_blocks,),
            in_specs=[pl.BlockSpec((bt, HY_D), yh_map),
                      pl.BlockSpec((2, bt, W), lambda i: (0, row_block0 + i, 0)),
                      pl.BlockSpec((bt, W), lambda i: (row_block0 + i, ooff)),
                      pl.BlockSpec((None, 1, W), lambda i: (widx, 0, 0))],
            out_specs=pl.BlockSpec((bt, HY_D + W), lambda i: (i, 0)),
            out_shape=jax.ShapeDtypeStruct((n_blocks * bt, HY_D + W), BF16),
            compiler_params=_cparams(1),
            name="finish_ab",
        )(yh, hdir, p, norm_w.reshape(norm_w.shape[0], 1, W))

    lat = call(yh_lat, 0, n_lat_blocks, lambda i: (i % per_b, i // per_b))
    ctx = call(yh_ctx, n_lat_blocks, (rows.n - rows.n_lat) // bt, lambda i: (0, i))
    return jnp.concatenate([lat, ctx], axis=0)


def _ssd_kernel(x_ref, b_ref, c_ref, dc_ref, dr_ref, pr_ref, pc_ref, o_ref, h_ref, ds_ref):
    d = pl.program_id(1)

    @pl.when(pl.program_id(2) == 0)
    def _():
        h_ref[...] = jnp.zeros_like(h_ref)

    dt_cols = _softplus(dc_ref[...] + pr_ref[1:2, :])
    la_cols = dt_cols * -jnp.exp(pr_ref[0:1, :])
    dt_rows = _softplus(dr_ref[...] + pc_ref[:, 1:2])
    la_rows = dt_rows * -jnp.exp(pc_ref[:, 0:1])

    causal, causal_t = _chunk_masks(d, CHUNK)
    acum_cols, acum_rows = _running_sums(la_cols, la_rows, causal, causal_t)
    ds_ref[0] = dt_rows
    ds_ref[1] = la_rows
    ds_ref[2] = acum_rows
    n_groups = b_ref.shape[1] // LANES
    heads_per_group = SSD_H // n_groups
    bmats = [b_ref[:, g * LANES:(g + 1) * LANES].astype(BF16) for g in range(n_groups)]
    cmats = [c_ref[:, g * LANES:(g + 1) * LANES].astype(BF16) for g in range(n_groups)]
    cbs = [_dot_nt(cmats[g], bmats[g]) for g in range(n_groups)]

    def head(hd):
        grp = hd // heads_per_group
        idx = SSD_H * d + hd
        hs = pl.ds(hd * SSD_P, SSD_P)
        dt_row = ds_ref[0, pl.ds(idx, 1), :]
        dt_col = _pick_col(dt_cols, idx)
        acum_col = _pick_col(acum_cols, idx)
        acum_row = ds_ref[2, pl.ds(idx, 1), :]
        tot = jnp.sum(ds_ref[1, pl.ds(idx, 1), :], axis=1, keepdims=True)
        decay = jnp.exp(jnp.where(causal, acum_col - acum_row, -jnp.inf))
        xh = x_ref[:, hs]
        h_in = h_ref[hd]
        xw = xh * (jnp.exp(tot - acum_col) * dt_col)
        y_off = _dot_nt(cmats[grp], h_in.astype(BF16))
        st = _dot_tn(xw.astype(BF16), bmats[grp])
        y_diag = _dot((cbs[grp] * decay * dt_row).astype(BF16), xh.astype(BF16))
        yield
        o_ref[:, hs] = y_diag + y_off * jnp.exp(acum_col)
        h_ref[hd] = jnp.exp(tot) * h_in + st

    _run_interleaved([head(hd) for hd in range(SSD_H)])


def ssd(xbc, dt, dt_t, a_log, dt_bias, rows):
    nct = (rows.L + rows.Lc) // CHUNK
    nd = SSD_H * SSD_P
    gw = 2 * LANES

    def rb(b, d, j):
        return rows.chunk_block(b, rows.scan_chunk(d, j))

    nh = 2 * SSD_H
    par = jnp.stack([a_log.reshape(nh), dt_bias.reshape(nh)])
    par_rows = jnp.pad(par, ((0, 0), (0, LANES - nh)))
    return pl.pallas_call(
        _ssd_kernel,
        grid=(rows.B, 2, nct),
        in_specs=[pl.BlockSpec((CHUNK, nd), lambda b, d, j: (rb(b, d, j), 0)),
                  pl.BlockSpec((CHUNK, gw), lambda b, d, j: (rb(b, d, j), nd // gw)),
                  pl.BlockSpec((CHUNK, gw), lambda b, d, j: (rb(b, d, j), nd // gw + 1)),
                  pl.BlockSpec((CHUNK, LANES), lambda b, d, j: (rb(b, d, j), 0)),
                  pl.BlockSpec((nh, CHUNK), lambda b, d, j: (0, rb(b, d, j))),
                  pl.BlockSpec((2, LANES), lambda b, d, j: (0, 0)),
                  pl.BlockSpec((nh, 2), lambda b, d, j: (0, 0))],
        out_specs=pl.BlockSpec((None, CHUNK, nd), lambda b, d, j: (d, rb(b, d, j), 0)),
        out_shape=jax.ShapeDtypeStruct((2, rows.n, nd), F32),
        scratch_shapes=[pltpu.VMEM((SSD_H, SSD_P, LANES), F32), pltpu.VMEM((3, nh, CHUNK), F32)],
        compiler_params=_cparams(3),
        name="ssd",
    )(xbc, xbc, xbc, dt, dt_t, par_rows, par.T)


S5_TC = 64
S5_JB = 2
S5_NS = S5_G * S5_P


def _s5_kernel(u_ref, w_ref, c_ref, lre_ref, lim_ref, y_ref, z_ref, s_ref, wb_ref, cb_ref):
    @pl.when(pl.program_id(0) == 0)
    def _():
        s_ref[...] = jnp.zeros_like(s_ref)
        wb_ref[...] = w_ref[...].astype(BF16)
        cb_ref[...] = c_ref[...].astype(BF16)

    n_blk = S5_W // LANES
    sw = S5_NS // n_blk
    u = u_ref[...].astype(BF16)
    for j in range(n_blk):
        z_ref[:, 2 * j * sw:2 * (j + 1) * sw] = _dot(u[:, j * LANES:(j + 1) * LANES], wb_ref[j])

    for j0 in range(0, n_blk, S5_JB):
        blocks = range(j0, j0 + S5_JB)
        re_sl = [slice(2 * j * sw, (2 * j + 1) * sw) for j in blocks]
        im_sl = [slice((2 * j + 1) * sw, (2 * j + 2) * sw) for j in blocks]
        ar = [lre_ref[:, j * sw:(j + 1) * sw] for j in blocks]
        ai = [lim_ref[:, j * sw:(j + 1) * sw] for j in blocks]

        def step(t, carry):
            r0 = pl.multiple_of(t * 8, 8)
            new = []
            for k in range(S5_JB):
                zr, zi = carry[2 * k], carry[2 * k + 1]
                nr = ar[k] * zr - ai[k] * zi + z_ref[pl.ds(r0, 8), re_sl[k]]
                ni = ar[k] * zi + ai[k] * zr + z_ref[pl.ds(r0, 8), im_sl[k]]
                z_ref[pl.ds(r0, 8), re_sl[k]] = nr
                z_ref[pl.ds(r0, 8), im_sl[k]] = ni
                new += [nr, ni]
            return tuple(new)

        init = []
        for k in range(S5_JB):
            init += [s_ref[:, re_sl[k]], s_ref[:, im_sl[k]]]
        fin = lax.fori_loop(0, S5_TC, step, tuple(init))
        for k in range(S5_JB):
            s_ref[:, re_sl[k]] = fin[2 * k]
            s_ref[:, im_sl[k]] = fin[2 * k + 1]

    row = lax.broadcasted_iota(jnp.int32, (S5_TC * 8, LANES), 0)
    is_bwd = (row & 4) != 0
    for j in range(n_blk):
        yy = _dot(z_ref[:, 2 * j * sw:2 * (j + 1) * sw].astype(BF16), cb_ref[j])
        y_ref[:, j * LANES:(j + 1) * LANES] = jnp.where(is_bwd, yy[:, LANES:], yy[:, :LANES])


def s5_params(lam_re, lam_im, log_dt, b_re, b_im, c_re, c_im):
    dt = jnp.exp(log_dt)[..., None]
    mag = jnp.exp(lam_re * dt)
    lb_re, lb_im = mag * jnp.cos(lam_im * dt), mag * jnp.sin(lam_im * dt)
    den = lam_re * lam_re + lam_im * lam_im
    f_re = ((lb_re - 1) * lam_re + lb_im * lam_im) / den
    f_im = (lb_im * lam_re - (lb_re - 1) * lam_im) / den
    cf_re = c_re[None] * f_re[:, :, None, :] - c_im[None] * f_im[:, :, None, :]
    cf_im = c_re[None] * f_im[:, :, None, :] + c_im[None] * f_re[:, :, None, :]
    gpb = LANES // S5_GS
    n_blk = S5_G // gpb
    eye = jnp.eye(gpb, dtype=F32)

    def w_blocks(b):
        bb = b.reshape(n_blk, gpb, S5_P, S5_GS)
        return jnp.einsum('jgpi,gh->jgihp', bb, eye).reshape(n_blk, gpb * S5_GS, gpb * S5_P)

    def c_blocks(c):
        cc = c.reshape(2, n_blk, gpb, S5_GS, S5_P)
        return jnp.einsum('djgip,gh->jgpdhi', cc, eye).reshape(n_blk, gpb * S5_P, 2 * gpb * S5_GS)

    def lam_rows(l, bsz):
        return jnp.repeat(l.reshape(2, S5_NS), bsz, axis=0)

    w_cat = jnp.concatenate([w_blocks(b_re), w_blocks(b_im)], axis=2)
    c_cat = jnp.concatenate([c_blocks(cf_re), -c_blocks(cf_im)], axis=1)
    return w_cat, c_cat, lb_re, lb_im, lam_rows


def s5(u_arr, w_cat, c_cat, lre, lim, n_steps):
    n_blk = S5_W // LANES
    sw = S5_NS // n_blk
    rt = S5_TC * 8
    full = lambda shape: pl.BlockSpec(shape, lambda i: (0,) * len(shape))
    return pl.pallas_call(
        _s5_kernel,
        grid=(n_steps // S5_TC,),
        in_specs=[pl.BlockSpec((rt, S5_W), lambda i: (i, 0)),
                  full((n_blk, LANES, 2 * sw)), full((n_blk, 2 * sw, 2 * LANES)),
                  full((8, S5_NS)), full((8, S5_NS))],
        out_specs=pl.BlockSpec((rt, S5_W), lambda i: (i, 0)),
        out_shape=jax.ShapeDtypeStruct((n_steps * 8, S5_W), F32),
        scratch_shapes=[pltpu.VMEM((rt, 2 * S5_NS), F32), pltpu.VMEM((8, 2 * S5_NS), F32),
                        pltpu.VMEM((n_blk, LANES, 2 * sw), BF16), pltpu.VMEM((n_blk, 2 * sw, 2 * LANES), BF16)],
        compiler_params=_cparams(1),
        name="s5",
    )(u_arr, w_cat, c_cat, lre, lim)


def _finish_cd_kernel(y_ref, xs_ref, z_ref, sf_ref, sb_ref, u_ref, dssd_ref, nw_ref, d5_ref, glu_ref, out_ref):
    y = (y_ref[0] + y_ref[1] + dssd_ref[...] * xs_ref[...]) * _silu(z_ref[...])
    y = y * lax.rsqrt(jnp.mean(y * y, axis=-1, keepdims=True) + EPS) * nw_ref[...]
    nd = y.shape[1]
    out_ref[:, 0:nd] = y.astype(out_ref.dtype)
    s = sf_ref[...] + sb_ref[...] + d5_ref[...] * u_ref[...]
    s = 0.5 * s * (1.0 + lax.erf(s * (2.0 ** -0.5)))
    gl = _dot(s.astype(BF16), glu_ref[...].astype(BF16))
    out_ref[:, nd:nd + S5_W] = (gl[:, 0:S5_W] * _sigmoid(gl[:, S5_W:])).astype(out_ref.dtype)


def finish_cd(ydir, xbc, p1, sf, sb, p2, ssd_d_lanes, norm_w, s5_d, glu_w, widx, n_rows, bt=256):
    nd = SSD_H * SSD_P
    vec = lambda n: pl.BlockSpec((None, 1, n), lambda i: (widx, 0, 0))
    return pl.pallas_call(
        _finish_cd_kernel,
        grid=(n_rows // bt,),
        in_specs=[pl.BlockSpec((2, bt, nd), lambda i: (0, i, 0)),
                  pl.BlockSpec((bt, nd), lambda i: (i, 0)),
                  pl.BlockSpec((bt, nd), lambda i: (i, 0)),
                  pl.BlockSpec((bt, S5_W), lambda i: (i, 0)),
                  pl.BlockSpec((bt, S5_W), lambda i: (i, 0)),
                  pl.BlockSpec((bt, S5_W), lambda i: (i, 0)),
                  vec(nd), vec(nd), vec(S5_W),
                  pl.BlockSpec((None, S5_W, 2 * S5_W), lambda i: (widx, 0, 0))],
        out_specs=pl.BlockSpec((bt, nd + S5_W), lambda i: (i, 0)),
        out_shape=jax.ShapeDtypeStruct((n_rows, nd + S5_W), BF16),
        compiler_params=_cparams(1),
        name="finish_cd",
    )(ydir, xbc, p1, sf, sb, p2, ssd_d_lanes, norm_w.reshape(norm_w.shape[0], 1, nd),
      s5_d.reshape(s5_d.shape[0], 1, S5_W), glu_w)


def _router_kernel(h_ref, w_ref, o_ref):
    logits = _dot_hi(h_ref[...], w_ref[...])
    lane = lax.broadcasted_iota(jnp.int32, logits.shape, 1)
    logits = jnp.where(lane < N_EXPERTS, logits, -jnp.inf)
    m1 = jnp.max(logits, axis=1, keepdims=True)
    i1 = jnp.min(jnp.where(logits == m1, lane, LANES), axis=1, keepdims=True)
    rest = jnp.where(lane == i1, -jnp.inf, logits)
    m2 = jnp.max(rest, axis=1, keepdims=True)
    i2 = jnp.min(jnp.where(rest == m2, lane, LANES), axis=1, keepdims=True)
    e2 = jnp.exp(m2 - m1)
    p1 = 1.0 / (1.0 + e2)
    p2 = e2 / (1.0 + e2)
    o_ref[...] = jnp.where(lane == 0, i1.astype(F32),
                           jnp.where(lane == 1, i2.astype(F32),
                                     jnp.where(lane == 2, p1, jnp.where(lane == 3, p2, 0.0))))


def router(h, w_router_padded, bt=512):
    T, D = h.shape
    return pl.pallas_call(
        _router_kernel,
        grid=(T // bt,),
        in_specs=[pl.BlockSpec((bt, D), lambda i: (i, 0)), pl.BlockSpec((D, LANES), lambda i: (0, 0))],
        out_specs=pl.BlockSpec((bt, LANES), lambda i: (i, 0)),
        out_shape=jax.ShapeDtypeStruct((T, LANES), F32),
        compiler_params=_cparams(1),
        name="router",
    )(h, w_router_padded)


def _gather_kernel(idx_ref, src_ref, o_ref, buf_ref, sem, *, bm, s):
    base = pl.program_id(0) * bm

    def row_copy(r):
        return pltpu.make_async_copy(src_ref.at[idx_ref[base + r]],
                                     buf_ref.at[pl.ds(pl.multiple_of(r * s, s), s), :], sem)

    def issue(r, c):
        row_copy(r).start()
        return c

    def drain(r, c):
        row_copy(r).wait()
        return c

    lax.fori_loop(0, bm, issue, 0, unroll=8)
    lax.fori_loop(0, bm, drain, 0, unroll=8)
    for j in range(s):
        o_ref[:, j * LANES:(j + 1) * LANES] = buf_ref[pl.ds(j, bm, stride=s), :].astype(o_ref.dtype)


def gather_rows(src, idx, bm=256):
    N = idx.shape[0]
    T, D = src.shape
    s = D // LANES
    return pl.pallas_call(
        functools.partial(_gather_kernel, bm=bm, s=s),
        grid_spec=pltpu.PrefetchScalarGridSpec(
            num_scalar_prefetch=1,
            grid=(N // bm,),
            in_specs=[pl.BlockSpec(memory_space=pl.ANY)],
            out_specs=pl.BlockSpec((bm, D), lambda i, idx: (i, 0)),
            scratch_shapes=[pltpu.VMEM((bm * s, LANES), F32), pltpu.SemaphoreType.DMA(())]),
        out_shape=jax.ShapeDtypeStruct((N, D), BF16),
        compiler_params=_cparams(1),
        name="gather_rows",
    )(idx, src.reshape(T, s, LANES))


def _gmm_kernel(te_ref, na_ref, nx_ref, x_ref, w_ref, o_ref, wf_ref, wb_ref, sem, *, widx, bn, nf, parts):
    n = pl.program_id(0)
    m = pl.program_id(1)
    e = te_ref[m]
    first = (m == 0) | (e != te_ref[jnp.maximum(m - 1, 0)])

    def fetch(expert, nblk):
        return [pltpu.make_async_copy(
            w_ref.at[widx, expert, :, pl.ds(pl.multiple_of((nblk + p * nf) * bn, bn), bn)],
            wf_ref.at[p], sem.at[p]) for p in range(parts)]

    @pl.when(first)
    def _():
        @pl.when((n == 0) & (m == 0))
        def _():
            for cp in fetch(e, n):
                cp.start()

        for cp in fetch(e, n):
            cp.wait()
        wb_ref[...] = wf_ref[...].astype(BF16)
        nxt = nx_ref[m]

        @pl.when(nxt >= 0)
        def _():
            for cp in fetch(nxt, n):
                cp.start()

        @pl.when((nxt < 0) & (n + 1 < nf))
        def _():
            for cp in fetch(te_ref[0], n + 1):
                cp.start()

    @pl.when(m < na_ref[0])
    def _():
        x = x_ref[...]
        if parts == 2:
            o_ref[...] = (_silu(_dot(x, wb_ref[0])) * _dot(x, wb_ref[1])).astype(o_ref.dtype)
        else:
            o_ref[...] = _dot(x, wb_ref[0]).astype(o_ref.dtype)

    @pl.when(m >= na_ref[0])
    def _():
        o_ref[...] = jnp.zeros_like(o_ref)


def gmm(x, w, widx, plan, *, bm, bn, swiglu):
    M, K = x.shape
    parts = 2 if swiglu else 1
    N = w.shape[3] // parts
    nf = N // bn
    return pl.pallas_call(
        functools.partial(_gmm_kernel, widx=widx, bn=bn, nf=nf, parts=parts),
        grid_spec=pltpu.PrefetchScalarGridSpec(
            num_scalar_prefetch=3,
            grid=(nf, M // bm),
            in_specs=[pl.BlockSpec((bm, K), lambda n, m, te, na, nx: (m, 0)),
                      pl.BlockSpec(memory_space=pl.ANY)],
            out_specs=pl.BlockSpec((bm, bn), lambda n, m, te, na, nx: (m, n)),
            scratch_shapes=[pltpu.VMEM((parts, K, bn), F32), pltpu.VMEM((parts, K, bn), BF16),
                            pltpu.SemaphoreType.DMA((parts,))]),
        out_shape=jax.ShapeDtypeStruct((M, N), BF16 if swiglu else F32),
        compiler_params=_cparams(2),
        name="gmm_swiglu" if swiglu else "gmm",
    )(plan.tile_expert, plan.n_active, plan.next_expert, x, w)


def _combine_kernel(pos_ref, y_ref, x_ref, pr_ref, g_ref, nw_ref, o_ref, b1_ref, b2_ref, sem, *, bm, n_tok):
    base = pl.program_id(0) * bm

    def copies(r):
        c1 = pltpu.make_async_copy(y_ref.at[pl.ds(pos_ref[base + r], 1), :], b1_ref.at[pl.ds(r, 1), :], sem.at[0])
        c2 = pltpu.make_async_copy(y_ref.at[pl.ds(pos_ref[n_tok + base + r], 1), :],
                                   b2_ref.at[pl.ds(r, 1), :], sem.at[1])
        return c1, c2

    def issue(r, c):
        c1, c2 = copies(r)
        c1.start()
        c2.start()
        return c

    def drain(r, c):
        c1, c2 = copies(r)
        c1.wait()
        c2.wait()
        return c

    lax.fori_loop(0, bm, issue, 0)
    lax.fori_loop(0, bm, drain, 0)
    pr = pr_ref[...]
    mix = pr[:, 2:3] * b1_ref[...] + pr[:, 3:4] * b2_ref[...]
    x = x_ref[...] + g_ref[...] * mix
    o_ref[...] = x * lax.rsqrt(jnp.mean(x * x, axis=-1, keepdims=True) + EPS) * nw_ref[...]


def moe_combine(ys, pos, x, route, gate, norm_w, rows, bm=256):
    T, D = x.shape
    return pl.pallas_call(
        functools.partial(_combine_kernel, bm=bm, n_tok=T),
        grid_spec=pltpu.PrefetchScalarGridSpec(
            num_scalar_prefetch=1,
            grid=(T // bm,),
            in_specs=[pl.BlockSpec(memory_space=pl.ANY),
                      pl.BlockSpec((bm, D), lambda i, pos: (i, 0)),
                      pl.BlockSpec((bm, LANES), lambda i, pos: (i, 0)),
                      pl.BlockSpec((None, 1, D), lambda i, pos: (rows.mod_row(i, bm), 0, 0)),
                      pl.BlockSpec((1, D), lambda i, pos: (0, 0))],
            out_specs=pl.BlockSpec((bm, D), lambda i, pos: (i, 0)),
            scratch_shapes=[pltpu.VMEM((bm, D), F32), pltpu.VMEM((bm, D), F32), pltpu.SemaphoreType.DMA((2,))]),
        out_shape=jax.ShapeDtypeStruct((T, D), F32),
        compiler_params=_cparams(1),
        name="moe_combine",
    )(pos, ys, x, route, gate, norm_w[None])


class MoePlan(NamedTuple):
    pos: jax.Array
    src: jax.Array
    tile_expert: jax.Array
    n_active: jax.Array
    next_expert: jax.Array


def moe_plan(route, n_tok, bm):
    e = jnp.concatenate([route[:, 0], route[:, 1]]).astype(jnp.int32)
    onehot = (e[:, None] == jnp.arange(N_EXPERTS, dtype=jnp.int32)[None, :]).astype(jnp.int32)
    rank = jnp.take_along_axis(jnp.cumsum(onehot, axis=0), e[:, None], axis=1)[:, 0] - 1
    counts = jnp.sum(onehot, axis=0)
    tiles = (counts + bm - 1) // bm
    tile_end = jnp.cumsum(tiles)
    start = (tile_end - tiles) * bm
    pos = start[e] + rank
    n_tiles = (2 * n_tok) // bm + N_EXPERTS
    tok = jnp.concatenate([jnp.arange(n_tok, dtype=jnp.int32)] * 2)
    src = jnp.zeros((n_tiles * bm,), jnp.int32).at[pos].set(tok)
    n_active = tile_end[-1]
    tile_ids = jnp.minimum(jnp.arange(n_tiles, dtype=jnp.int32), n_active - 1)
    tile_expert = jnp.sum((tile_ids[:, None] >= tile_end[None, :]).astype(jnp.int32), axis=1)
    ids = jnp.arange(N_EXPERTS, dtype=jnp.int32)
    later_used = (ids[None, :] > ids[:, None]) & (tiles[None, :] > 0)
    next_of = jnp.min(jnp.where(later_used, ids[None, :], N_EXPERTS), axis=1)
    next_of = jnp.where(next_of == N_EXPERTS, -1, next_of)
    return MoePlan(pos.astype(jnp.int32), src, tile_expert.astype(jnp.int32),
                   n_active.reshape(1).astype(jnp.int32), next_of[tile_expert].astype(jnp.int32))


def kernel(x, c, ctx, c_ctx, ada_w, ada_b, ab_w_in, hy_conv_w, hy_conv_b, hy_filt_w1, hy_filt_b1, hy_filt_w2, hy_filt_b2, hy_filt_w3, hy_filt_freq, hy_bias, ml_conv_w, ml_conv_b, ml_gate_b, ml_norm_w, ab_w_out, ffn_w_gu, ffn_w_down, cd_w_in, ssd_conv_w, ssd_conv_b, ssd_A_log, ssd_dt_bias, ssd_D, ssd_norm_w, s5_lam_re, s5_lam_im, s5_log_dt, s5_B_re, s5_B_im, s5_C_re, s5_C_im, s5_D, s5_glu_w, cd_w_out, moe_router, moe_w_gu, moe_w_down, final_norm_w):
    bsz, L, D = x.shape
    Lc = ctx.shape[1]
    rows = Rows(bsz, L, Lc)
    n_lat = rows.n_lat
    assert bsz == 4 and L % 1024 == 0 and Lc == 256 and D % 256 == 0
    bn_d = min(1024, D)
    bn_f = min(512, ffn_w_down.shape[1])

    xs = jnp.concatenate([x.reshape(n_lat, D), ctx.reshape(bsz * Lc, D)], axis=0)
    cond = jnp.concatenate([c, c_ctx[None], jnp.zeros((8 - bsz - 1, D), F32)], axis=0)

    def mods(layer):
        m = adaln(cond, ada_w, ada_b, layer)
        return [m[:, k * D:(k + 1) * D].reshape(8, 1, D) for k in range(6)]

    md = mods(0)
    h = modulate(xs, md[0], md[1], rows, rows.n, BF16)
    n_main = 3 * HY_D + 4 * ML_H * ML_DH
    p = mm(h, ab_w_in, 0, n_cols=n_main, bm=1024, bn=1024)
    w_gate = jnp.pad(ab_w_in[:, :, n_main:], ((0, 0), (0, 0), (0, LANES - 4 * ML_H)))
    gates = mm(h, w_gate, 0, n_cols=LANES, bm=1024, bn=LANES)
    gates_t = gates[:, :4 * ML_H].T

    filt = (hy_filt_w1, hy_filt_b1, hy_filt_w2, hy_filt_b2, hy_filt_w3, hy_filt_freq)
    yh_lat = hyena(p, L, bsz, 0, GRID_W, 0, hy_conv_w, hy_conv_b, filt, hy_bias)
    yh_ctx = hyena(p, Lc, bsz, n_lat, Lc, 0, hy_conv_w, hy_conv_b, filt, hy_bias)

    qk = conv_silu(p, 3 * HY_D, 2 * ML_H * ML_DH, ml_conv_w, ml_conv_b, 0, rows)
    hdir = mlstm(qk, p, 3 * HY_D + 2 * ML_H * ML_DH, gates, gates_t, ml_gate_b[0], rows)
    cat = finish_ab(yh_lat, yh_ctx, hdir, p, 3 * HY_D + 3 * ML_H * ML_DH, ml_norm_w, 0, rows)
    xs = mm_residual(cat, ab_w_out, 0, xs, md[2], rows, bm=1024, bn=bn_d)

    h = modulate(xs, md[3], md[4], rows, rows.n, BF16)
    act = mm_swiglu(h, ffn_w_gu, 0, bm=1024, bn=bn_f)
    xs = mm_residual(act, ffn_w_down, 0, xs, md[5], rows, bm=512, bn=min(512, D))

    md = mods(1)
    h = modulate(xs, md[0], md[1], rows, rows.n, BF16)
    nd = SSD_H * SSD_P
    n_xbc = nd + 4 * LANES
    p1 = mm(h, cd_w_in, 0, n_cols=nd + n_xbc, bm=1024, bn=512)
    w_tail = cd_w_in[:, :, nd + n_xbc:]
    w_tail = jnp.concatenate([w_tail[:, :, 2 * SSD_H:], w_tail[:, :, :2 * SSD_H],
                              jnp.zeros((1, D, LANES - 2 * SSD_H), F32)], axis=2)
    p2 = mm(h, w_tail, 0, n_cols=S5_W + LANES, bm=1024, bn=S5_W + LANES)

    xbc = conv_silu(p1, nd, n_xbc, ssd_conv_w, ssd_conv_b, 0, rows)
    dt = p2[:, S5_W:]
    dt_t = dt[:, :2 * SSD_H].T
    ydir = ssd(xbc, dt, dt_t, ssd_A_log[0], ssd_dt_bias[0], rows)

    w_cat, c_cat, lb_re, lb_im, lam_rows = s5_params(
        s5_lam_re[0], s5_lam_im[0], s5_log_dt[0], s5_B_re[0], s5_B_im[0], s5_C_re[0], s5_C_im[0])
    u_lat = p2[:n_lat, :S5_W].reshape(bsz, L, S5_W)
    u_ctx = p2[n_lat:, :S5_W].reshape(bsz, Lc, S5_W)
    seq_f = jnp.concatenate([u_ctx, u_lat], axis=1)
    seq_b = jnp.concatenate([u_ctx[:, ::-1], u_lat[:, ::-1]], axis=1)
    u_arr = jnp.transpose(jnp.stack([seq_f, seq_b]), (2, 0, 1, 3)).reshape((L + Lc) * 2 * bsz, S5_W)
    y_arr = s5(u_arr, w_cat, c_cat, lam_rows(lb_re, bsz), lam_rows(lb_im, bsz), L + Lc)
    y_arr = y_arr.reshape(L + Lc, 2, bsz, S5_W)[Lc:]
    sf = jnp.transpose(y_arr[:, 0], (1, 0, 2)).reshape(n_lat, S5_W)
    sb = jnp.transpose(y_arr[::-1, 1], (1, 0, 2)).reshape(n_lat, S5_W)

    ssd_d_lanes = jnp.repeat(ssd_D, SSD_P, axis=1).reshape(ssd_D.shape[0], 1, nd)
    cat = finish_cd(ydir, xbc, p1, sf, sb, p2, ssd_d_lanes, ssd_norm_w, s5_D, s5_glu_w, 0, n_lat)
    xl = mm_residual(cat, cd_w_out, 0, xs, md[2], rows, bm=1024, bn=bn_d)

    h2 = modulate(xl, md[3], md[4], rows, n_lat, F32)
    route = router(h2, jnp.pad(moe_router[0], ((0, 0), (0, LANES - N_EXPERTS))))
    bm_e = 512
    plan = moe_plan(route, n_lat, bm_e)
    xg = gather_rows(h2, plan.src, bm=256)
    act = gmm(xg, moe_w_gu, 0, plan, bm=bm_e, bn=min(512, moe_w_down.shape[2]), swiglu=True)
    ys = gmm(act, moe_w_down, 0, plan, bm=bm_e, bn=min(512, D), swiglu=False)
    out = moe_combine(ys, plan.pos, xl, route, md[5], final_norm_w, rows, bm=256)
    return out.reshape(bsz, L, D)
```

```python
import functools
import math
from typing import NamedTuple

import jax
import jax.numpy as jnp
import numpy as np
from jax import lax
from jax.experimental import pallas as pl
from jax.experimental.pallas import tpu as pltpu

F32 = jnp.float32
BF16 = jnp.bfloat16

EPS = 1e-6
GRID_W = 64
CHUNK = 128
LANES = 128
HY_D = 1024
HY_EMB = 33
HY_FAST_DECAY = 0.3
HY_SLOW_DECAY = 1.5
HY_DECAY_TARGET = 1e-2
ML_H = 8
ML_DH = 128
SSD_H = 16
SSD_P = 64
S5_W = 512
S5_G = 32
S5_GS = 16
S5_P = 64
N_EXPERTS = 8
VMEM_LIMIT_BYTES = 56 * 1024 * 1024


def _cparams(n_axes):
    return pltpu.CompilerParams(dimension_semantics=("arbitrary",) * n_axes,
                                vmem_limit_bytes=VMEM_LIMIT_BYTES)


def _sigmoid(x):
    return 1.0 / (1.0 + jnp.exp(-x))


def _silu(x):
    return x * _sigmoid(x)


def _log_sigmoid(x):
    return jnp.minimum(x, 0.0) - jnp.log(1.0 + jnp.exp(-jnp.abs(x)))


def _softplus(x):
    return jnp.maximum(x, 0.0) + jnp.log(1.0 + jnp.exp(-jnp.abs(x)))


def _dot(a, b):
    return jnp.dot(a, b, preferred_element_type=F32)


def _dot_nt(a, b):
    return lax.dot_general(a, b, (((1,), (1,)), ((), ())), preferred_element_type=F32)


def _dot_tn(a, b):
    return lax.dot_general(a, b, (((0,), (0,)), ((), ())), preferred_element_type=F32)


def _dot_hi(a, b):
    return jnp.dot(a, b, preferred_element_type=F32, precision=lax.Precision.HIGHEST)


class Rows:
    def __init__(self, bsz, L, Lc):
        self.B, self.L, self.Lc = bsz, L, Lc
        self.n_lat = bsz * L
        self.n = bsz * (L + Lc)

    def mod_row(self, i, bm):
        n_lat_blocks = self.n_lat // bm
        return jnp.where(i < n_lat_blocks, (i * bm) // self.L, self.B)

    def chunk_block(self, b, c):
        ncc = self.Lc // CHUNK
        return jnp.where(c < ncc, self.n_lat // CHUNK + b * ncc + c, b * (self.L // CHUNK) + c - ncc)

    def scan_chunk(self, d, j):
        ncc = self.Lc // CHUNK
        nct = (self.L + self.Lc) // CHUNK
        back = jnp.where(j < ncc, ncc - 1 - j, nct - 1 + ncc - j)
        return jnp.where(d == 0, j, back)


def _adaln_kernel(c_ref, w_ref, b_ref, o_ref):
    cond = _silu(c_ref[...]).astype(BF16)
    o_ref[...] = _dot(cond, w_ref[...].astype(BF16)) + b_ref[...]


def adaln(cond, ada_w, ada_b, layer, bn=1024):
    bn = min(bn, ada_w.shape[1])
    _, D, N = ada_w.shape
    return pl.pallas_call(
        _adaln_kernel,
        grid=(N // bn,),
        in_specs=[pl.BlockSpec((8, D), lambda n: (0, 0)),
                  pl.BlockSpec((None, D, bn), lambda n: (layer, 0, n)),
                  pl.BlockSpec((None, 1, bn), lambda n: (layer, 0, n))],
        out_specs=pl.BlockSpec((8, bn), lambda n: (0, n)),
        out_shape=jax.ShapeDtypeStruct((8, N), F32),
        compiler_params=_cparams(1),
        name="adaln",
    )(cond, ada_w, ada_b.reshape(ada_b.shape[0], 1, N))


def _modulate_kernel(x_ref, sh_ref, sc_ref, o_ref):
    x = x_ref[...]
    y = x * lax.rsqrt(jnp.mean(x * x, axis=-1, keepdims=True) + EPS)
    o_ref[...] = (y * (1.0 + sc_ref[...]) + sh_ref[...]).astype(o_ref.dtype)


def modulate(x, shift, scale, rows, n_rows, out_dtype, bm=512):
    D = x.shape[1]
    mod_spec = pl.BlockSpec((None, 1, D), lambda i: (rows.mod_row(i, bm), 0, 0))
    return pl.pallas_call(
        _modulate_kernel,
        grid=(n_rows // bm,),
        in_specs=[pl.BlockSpec((bm, D), lambda i: (i, 0)), mod_spec, mod_spec],
        out_specs=pl.BlockSpec((bm, D), lambda i: (i, 0)),
        out_shape=jax.ShapeDtypeStruct((n_rows, D), out_dtype),
        compiler_params=_cparams(1),
        name="modulate",
    )(x, shift, scale)


def _mm_kernel(x_ref, w_ref, o_ref, wb_ref):
    @pl.when(pl.program_id(1) == 0)
    def _():
        wb_ref[...] = w_ref[...].astype(BF16)
    o_ref[...] = _dot(x_ref[...], wb_ref[...]).astype(o_ref.dtype)


def mm(x, w, widx, *, n_cols, col_off=0, bm, bn, out_dtype=F32):
    M, K = x.shape
    off = col_off // bn
    return pl.pallas_call(
        _mm_kernel,
        grid=(n_cols // bn, M // bm),
        in_specs=[pl.BlockSpec((bm, K), lambda n, m: (m, 0)),
                  pl.BlockSpec((None, K, bn), lambda n, m: (widx, 0, n + off))],
        out_specs=pl.BlockSpec((bm, bn), lambda n, m: (m, n)),
        out_shape=jax.ShapeDtypeStruct((M, n_cols), out_dtype),
        scratch_shapes=[pltpu.VMEM((K, bn), BF16)],
        compiler_params=_cparams(2),
        name="mm",
    )(x, w)


def _mm_res_kernel(x_ref, w_ref, r_ref, g_ref, o_ref, wb_ref):
    @pl.when(pl.program_id(1) == 0)
    def _():
        wb_ref[...] = w_ref[...].astype(BF16)
    o_ref[...] = r_ref[...] + g_ref[...] * _dot(x_ref[...], wb_ref[...])


def mm_residual(x, w, widx, res, gate, rows, *, bm, bn):
    M, K = x.shape
    N = w.shape[2]
    return pl.pallas_call(
        _mm_res_kernel,
        grid=(N // bn, M // bm),
        in_specs=[pl.BlockSpec((bm, K), lambda n, m: (m, 0)),
                  pl.BlockSpec((None, K, bn), lambda n, m: (widx, 0, n)),
                  pl.BlockSpec((bm, bn), lambda n, m: (m, n)),
                  pl.BlockSpec((None, 1, bn), lambda n, m: (rows.mod_row(m, bm), 0, n))],
        out_specs=pl.BlockSpec((bm, bn), lambda n, m: (m, n)),
        out_shape=jax.ShapeDtypeStruct((M, N), F32),
        scratch_shapes=[pltpu.VMEM((K, bn), BF16)],
        compiler_params=_cparams(2),
        name="mm_residual",
    )(x, w, res, gate)


def _mm_swiglu_kernel(x_ref, wg_ref, wu_ref, o_ref, wgb_ref, wub_ref):
    @pl.when(pl.program_id(1) == 0)
    def _():
        wgb_ref[...] = wg_ref[...].astype(BF16)
        wub_ref[...] = wu_ref[...].astype(BF16)
    x = x_ref[...]
    g = _dot(x, wgb_ref[...])
    u = _dot(x, wub_ref[...])
    o_ref[...] = (_silu(g) * u).astype(o_ref.dtype)


def mm_swiglu(x, w_gu, widx, *, bm, bn):
    M, K = x.shape
    F = w_gu.shape[2] // 2
    nf = F // bn
    return pl.pallas_call(
        _mm_swiglu_kernel,
        grid=(nf, M // bm),
        in_specs=[pl.BlockSpec((bm, K), lambda n, m: (m, 0)),
                  pl.BlockSpec((None, K, bn), lambda n, m: (widx, 0, n)),
                  pl.BlockSpec((None, K, bn), lambda n, m: (widx, 0, n + nf))],
        out_specs=pl.BlockSpec((bm, bn), lambda n, m: (m, n)),
        out_shape=jax.ShapeDtypeStruct((M, F), BF16),
        scratch_shapes=[pltpu.VMEM((K, bn), BF16), pltpu.VMEM((K, bn), BF16)],
        compiler_params=_cparams(2),
        name="mm_swiglu",
    )(x, w_gu, w_gu)


def _conv3(x, w_ref, b_ref, period):
    n = x.shape[0]
    pos = lax.broadcasted_iota(jnp.int32, x.shape, 0) & (period - 1)
    prev = jnp.where(pos == 0, 0.0, pltpu.roll(x, 1, 0))
    nxt = jnp.where(pos == period - 1, 0.0, pltpu.roll(x, n - 1, 0))
    w = w_ref[...]
    return b_ref[...] + prev * w[0:1] + x * w[1:2] + nxt * w[2:3]


def _conv_silu_kernel(x_ref, w_ref, b_ref, o_ref, *, n_lat_blocks, lat_period, ctx_period):
    period = jnp.where(pl.program_id(0) < n_lat_blocks, lat_period, ctx_period)
    o_ref[...] = _silu(_conv3(x_ref[...], w_ref, b_ref, period)).astype(o_ref.dtype)


def conv_silu(p, col_off, n_cols, w, b, widx, rows, *, bt=256, bc=512):
    off = col_off // bc
    kern = functools.partial(_conv_silu_kernel, n_lat_blocks=rows.n_lat // bt,
                             lat_period=GRID_W, ctx_period=rows.Lc)
    return pl.pallas_call(
        kern,
        grid=(rows.n // bt, n_cols // bc),
        in_specs=[pl.BlockSpec((bt, bc), lambda i, j: (i, j + off)),
                  pl.BlockSpec((None, 3, bc), lambda i, j: (widx, 0, j)),
                  pl.BlockSpec((None, 1, bc), lambda i, j: (widx, 0, j))],
        out_specs=pl.BlockSpec((bt, bc), lambda i, j: (i, j)),
        out_shape=jax.ShapeDtypeStruct((rows.n, n_cols), F32),
        compiler_params=_cparams(2),
        name="conv_silu",
    )(p, w, b.reshape(b.shape[0], 1, b.shape[1]))


def _hyena_conv_kernel(p0_ref, p1_ref, p2_ref, w0_ref, w1_ref, w2_ref, b0_ref, b1_ref, b2_ref,
                       x0_ref, u_ref, *, period):
    x0_ref[...] = _conv3(p0_ref[...], w0_ref, b0_ref, period)
    x1 = _conv3(p1_ref[...], w1_ref, b1_ref, period)
    v = _conv3(p2_ref[...], w2_ref, b2_ref, period)
    u_ref[...] = (x1 * v).astype(u_ref.dtype)


def hyena_conv(p, w, b, widx, *, row0, bsz, seq, period, bt=256, bc=512):
    nt = seq // bt
    ncb = HY_D // bc
    rb0 = row0 // bt

    def pspec(k):
        return pl.BlockSpec((bt, bc), lambda bb, i, j: (rb0 + bb * nt + i, j + k * ncb))

    def wspec(k):
        return pl.BlockSpec((None, 3, bc), lambda bb, i, j: (widx, 0, j + k * ncb))

    def bspec(k):
        return pl.BlockSpec((None, 1, bc), lambda bb, i, j: (widx, 0, j + k * ncb))

    ospec = pl.BlockSpec((bt, bc), lambda bb, i, j: (i, bb * ncb + j))
    b3 = b.reshape(b.shape[0], 1, b.shape[1])
    return pl.pallas_call(
        functools.partial(_hyena_conv_kernel, period=period),
        grid=(bsz, nt, ncb),
        in_specs=[pspec(0), pspec(1), pspec(2), wspec(0), wspec(1), wspec(2), bspec(0), bspec(1), bspec(2)],
        out_specs=[ospec, ospec],
        out_shape=[jax.ShapeDtypeStruct((seq, bsz * HY_D), F32),
                   jax.ShapeDtypeStruct((seq, bsz * HY_D), BF16)],
        compiler_params=_cparams(3),
        name="hyena_conv",
    )(p, p, p, w, w, w, b3, b3, b3)


def _hyena_feats(L):
    pos = np.arange(L, dtype=np.float64)
    t = pos / max(L - 1, 1)
    n_bands = (HY_EMB - 1) // 2
    bands = np.linspace(1e-4, n_bands - 1, n_bands)
    ang = (2 * math.pi / L) * pos[:, None] * bands[None, :]
    feats = np.concatenate([t[:, None], np.cos(ang), -np.sin(ang)], axis=-1)
    feats = np.pad(feats, ((0, 0), (0, LANES - HY_EMB)))
    deltas = np.abs(np.linspace(math.log(HY_DECAY_TARGET) / HY_SLOW_DECAY,
                                math.log(HY_DECAY_TARGET) / HY_FAST_DECAY, HY_D))
    return feats.astype(np.float32), t.astype(np.float32)[:, None], deltas.astype(np.float32)[None, :]


def _hyena_filter_kernel(feats_ref, t_ref, dl_ref, w1_ref, b1_ref, w2_ref, b2_ref, w3f_ref, w3b_ref,
                         fq_ref, hf_ref, hb_ref):
    fq = fq_ref[...]
    h = jnp.sin(fq[0:1] * (_dot_hi(feats_ref[...], w1_ref[...]) + b1_ref[...]))
    h = jnp.sin(fq[1:2] * (_dot_hi(h, w2_ref[...]) + b2_ref[...]))
    win = jnp.exp(-t_ref[...] * dl_ref[...])
    h_f = _dot_hi(h, w3f_ref[...]) * win
    h_b = _dot_hi(h, w3b_ref[...]) * win
    row = lax.broadcasted_iota(jnp.int32, h_b.shape, 0)
    h_b = jnp.where(row == 0, 0.0, h_b)
    l1 = jnp.sum(jnp.abs(h_f), axis=0, keepdims=True) + jnp.sum(jnp.abs(h_b), axis=0, keepdims=True)
    hf_ref[...] = h_f / l1
    hb_ref[...] = h_b / l1


def hyena_filter(L, w1, b1, w2, b2, w3, freq, widx, bc=256):
    feats, t, deltas = _hyena_feats(L)
    hid = w2.shape[1]
    w1p = jnp.pad(w1[widx], ((0, LANES - HY_EMB), (0, 0)))
    ncb = HY_D // bc
    full = lambda shape: pl.BlockSpec(shape, lambda j: (0,) * len(shape))
    return pl.pallas_call(
        _hyena_filter_kernel,
        grid=(ncb,),
        in_specs=[full((L, LANES)), full((L, 1)), pl.BlockSpec((1, bc), lambda j: (0, j)),
                  full((LANES, hid)), full((1, hid)), full((hid, hid)), full((1, hid)),
                  pl.BlockSpec((hid, bc), lambda j: (0, j)),
                  pl.BlockSpec((hid, bc), lambda j: (0, j + ncb)),
                  full((2, hid))],
        out_specs=[pl.BlockSpec((L, bc), lambda j: (0, j)), pl.BlockSpec((L, bc), lambda j: (0, j))],
        out_shape=[jax.ShapeDtypeStruct((L, HY_D), F32), jax.ShapeDtypeStruct((L, HY_D), F32)],
        compiler_params=_cparams(1),
        name="hyena_filter",
    )(jnp.asarray(feats), jnp.asarray(t), jnp.asarray(deltas), w1p, b1[widx][None], w2[widx],
      b2[widx][None], w3[widx], w3[widx], freq[widx])


def _dft_matrix(L):
    f = np.arange(L, dtype=np.int64)[:, None]
    s = np.arange(L, dtype=np.int64)[None, :]
    ang = (math.pi / L) * ((f * s) % (2 * L)).astype(np.float64)
    a_cos = np.cos(ang)
    a_sin = -np.sin(ang)
    a_sin[0, :] = np.where(np.arange(L) % 2 == 0, 1.0, -1.0)
    return np.concatenate([a_cos, a_sin], axis=0).astype(np.float32)


def _spectrum_kernel(ac_ref, as_ref, hf_ref, hb_ref, kre_ref, kim_ref, kny_ref, *, L, bf):
    hf = hf_ref[...].astype(BF16)
    hb = hb_ref[...].astype(BF16)
    ac = ac_ref[...].astype(BF16)
    a_s = as_ref[...].astype(BF16)
    cf, cb = _dot(ac, hf), _dot(ac, hb)
    sf, sb = _dot(a_s, hf), _dot(a_s, hb)
    f = lax.broadcasted_iota(jnp.int32, cf.shape, 0) + pl.program_id(1) * bf
    wgt = jnp.where(f == 0, 0.5 / L, 1.0 / L)
    kre_ref[...] = (cf + cb) * wgt
    kim_ref[...] = jnp.where(f == 0, 0.0, (sf - sb) * wgt)
    kny_ref[...] = jnp.where(f == 0, (sf + sb) * wgt, (cf + cb) * wgt)


def hyena_spectrum(a_mat, hf, hb, L, bf=256, bc=512):
    bf = min(bf, L)
    nf = L // bf
    C = hf.shape[1]
    ospec = pl.BlockSpec((bf, bc), lambda c, i: (i, c))
    hspec = pl.BlockSpec((L, bc), lambda c, i: (0, c))
    return pl.pallas_call(
        functools.partial(_spectrum_kernel, L=L, bf=bf),
        grid=(C // bc, nf),
        in_specs=[pl.BlockSpec((bf, L), lambda c, i: (i, 0)), pl.BlockSpec((bf, L), lambda c, i: (i + nf, 0)),
                  hspec, hspec],
        out_specs=[ospec, ospec, ospec],
        out_shape=[jax.ShapeDtypeStruct((L, C), F32)] * 3,
        compiler_params=_cparams(2),
        name="hyena_spectrum",
    )(a_mat, a_mat, hf, hb)


def _lc_fwd_kernel(ac_ref, as_ref, u_ref, kre_ref, kim_ref, kny_ref, yre_ref, yim_ref, acb_ref, asb_ref):
    @pl.when(pl.program_id(1) == 0)
    def _():
        acb_ref[...] = ac_ref[...].astype(BF16)
        asb_ref[...] = as_ref[...].astype(BF16)
    u = u_ref[...]
    ure = _dot(acb_ref[...], u)
    uim = _dot(asb_ref[...], u)
    kim = kim_ref[...]
    yre_ref[...] = (kre_ref[...] * ure - kim * uim).astype(yre_ref.dtype)
    yim_ref[...] = (kny_ref[...] * uim + kim * ure).astype(yim_ref.dtype)


def long_conv_fwd(a_mat, u, kre, kim, kny, L, bsz, bf=256):
    bf = min(bf, L)
    nf = L // bf
    C = kre.shape[1]
    kspec = pl.BlockSpec((bf, C), lambda i, b: (i, 0))
    ospec = pl.BlockSpec((bf, C), lambda i, b: (i, b))
    return pl.pallas_call(
        _lc_fwd_kernel,
        grid=(nf, bsz),
        in_specs=[pl.BlockSpec((bf, L), lambda i, b: (i, 0)), pl.BlockSpec((bf, L), lambda i, b: (i + nf, 0)),
                  pl.BlockSpec((L, C), lambda i, b: (0, b)), kspec, kspec, kspec],
        out_specs=[ospec, ospec],
        out_shape=[jax.ShapeDtypeStruct((L, bsz * C), BF16)] * 2,
        scratch_shapes=[pltpu.VMEM((bf, L), BF16), pltpu.VMEM((bf, L), BF16)],
        compiler_params=_cparams(2),
        name="long_conv_fwd",
    )(a_mat, a_mat, u, kre, kim, kny)


def _lc_inv_kernel(atc_ref, ats_ref, yre_ref, yim_ref, x0_ref, u_ref, bias_ref, o_ref, atcb_ref, atsb_ref):
    @pl.when(pl.program_id(1) == 0)
    def _():
        atcb_ref[...] = atc_ref[...].astype(BF16)
        atsb_ref[...] = ats_ref[...].astype(BF16)
    y = _dot(atcb_ref[...], yre_ref[...]) + _dot(atsb_ref[...], yim_ref[...])
    u = u_ref[...].astype(F32)
    o_ref[...] = (x0_ref[...] * (y + bias_ref[...] * u)).astype(o_ref.dtype)


def long_conv_inv(at_mat, yre, yim, x0, u, bias, widx, L, bsz, bt=256):
    bt = min(bt, L)
    nt = L // bt
    C = bias.shape[1]
    tspec = pl.BlockSpec((bt, C), lambda i, b: (i, b))
    yspec = pl.BlockSpec((L, C), lambda i, b: (0, b))
    return pl.pallas_call(
        _lc_inv_kernel,
        grid=(nt, bsz),
        in_specs=[pl.BlockSpec((bt, L), lambda i, b: (i, 0)), pl.BlockSpec((bt, L), lambda i, b: (i, 1)),
                  yspec, yspec, tspec, tspec, pl.BlockSpec((None, 1, C), lambda i, b: (widx, 0, 0))],
        out_specs=tspec,
        out_shape=jax.ShapeDtypeStruct((L, bsz * C), BF16),
        scratch_shapes=[pltpu.VMEM((bt, L), BF16), pltpu.VMEM((bt, L), BF16)],
        compiler_params=_cparams(2),
        name="long_conv_inv",
    )(at_mat, at_mat, yre, yim, x0, u, bias.reshape(bias.shape[0], 1, C))


def hyena(p, L, bsz, row0, period, widx, conv_w, conv_b, filt, bias):
    a_np = _dft_matrix(L)
    a_mat = jnp.asarray(a_np)
    at_mat = jnp.asarray(np.ascontiguousarray(a_np.T))
    x0, u = hyena_conv(p, conv_w, conv_b, widx, row0=row0, bsz=bsz, seq=L, period=period, bt=min(256, L))
    hf, hb = hyena_filter(L, *filt, widx)
    kre, kim, kny = hyena_spectrum(a_mat, hf, hb, L)
    yre, yim = long_conv_fwd(a_mat, u, kre, kim, kny, L, bsz)
    return long_conv_inv(at_mat, yre, yim, x0, u, bias, widx, L, bsz)


def _pick_col(g, idx):
    lane = lax.broadcasted_iota(jnp.int32, g.shape, 1)
    return jnp.sum(jnp.where(lane == idx, g, 0.0), axis=1, keepdims=True)


def _chunk_masks(d, n):
    t_i = lax.broadcasted_iota(jnp.int32, (n, n), 0)
    s_i = lax.broadcasted_iota(jnp.int32, (n, n), 1)
    lag = (t_i - s_i) * jnp.where(d == 0, 1, -1)
    return lag >= 0, lag <= 0


def _running_sums(x_cols, x_rows, causal, causal_t):
    tri = jnp.where(causal, 1.0, 0.0)
    tri_t = jnp.where(causal_t, 1.0, 0.0)
    return _dot_hi(tri, x_cols), _dot_hi(x_rows, tri_t)


def _mlstm_kernel(q_ref, k_ref, v_ref, gc_ref, gr_ref, bc_ref, br_ref, o_ref, c_ref, n_ref, m_ref, gs_ref):
    d = pl.program_id(1)

    @pl.when(pl.program_id(2) == 0)
    def _():
        c_ref[...] = jnp.zeros_like(c_ref)
        n_ref[...] = jnp.zeros_like(n_ref)
        m_ref[...] = jnp.zeros_like(m_ref)

    gcol = gc_ref[...] + bc_ref[...]
    lane = lax.broadcasted_iota(jnp.int32, gcol.shape, 1)
    gcol = jnp.where((lane & ML_H) != 0, _log_sigmoid(gcol), gcol)
    grow = gr_ref[...] + br_ref[...]
    sub = lax.broadcasted_iota(jnp.int32, grow.shape, 0)
    grow = jnp.where((sub & ML_H) != 0, _log_sigmoid(grow), grow)

    causal, causal_t = _chunk_masks(d, CHUNK)
    ccol, crow = _running_sums(gcol, grow, causal, causal_t)
    gs_ref[0] = grow
    gs_ref[1] = crow

    def head(h):
        sl = pl.ds(h * ML_DH, ML_DH)
        return _mlstm_head(h, d, q_ref.at[:, sl], k_ref.at[:, sl], v_ref.at[:, sl], o_ref.at[:, sl], gcol, ccol,
                           gs_ref, causal, c_ref.at[h], n_ref.at[h], m_ref.at[h])

    _run_interleaved([head(h) for h in range(ML_H)])


def _run_interleaved(stages):
    live = list(stages)
    while live:
        still = []
        for gen in live:
            try:
                next(gen)
                still.append(gen)
            except StopIteration:
                pass
        live = still


def _mlstm_head(h, d, q_ref, k_ref, v_ref, o_ref, gcol, ccol, gs_ref, causal, c_ref, n_ref, m_ref):
    i_idx = 2 * ML_H * d + h
    f_idx = i_idx + ML_H
    ig_row = gs_ref[0, pl.ds(i_idx, 1), :]
    ig_col = _pick_col(gcol, i_idx)
    bcum_row = gs_ref[1, pl.ds(f_idx, 1), :]
    bcum_col = _pick_col(ccol, f_idx)
    g = jnp.sum(gs_ref[0, pl.ds(f_idx, 1), :], axis=1, keepdims=True)

    q = q_ref[...] * (ML_DH ** -0.5)
    k = k_ref[...]
    qb, kb, vb = q.astype(BF16), k.astype(BF16), v_ref[...].astype(BF16)
    c_in, n_in, m_in = c_ref[...], n_ref[...], m_ref[0:1, 0:1]

    dmat = jnp.where(causal, bcum_col - bcum_row + ig_row, -jnp.inf)
    m_inter = bcum_col + m_in
    m_t = jnp.maximum(jnp.max(dmat, axis=1, keepdims=True), m_inter)
    decay = jnp.exp(dmat - m_t)
    w_inter = jnp.exp(m_inter - m_t)
    a_row = g - bcum_row + ig_row
    a_col = g - bcum_col + ig_col
    m_loc = jnp.max(a_row, axis=1, keepdims=True)
    kw = k * jnp.exp(a_col - m_loc)
    qk = _dot_nt(qb, kb)
    qc = _dot(qb, c_in.astype(BF16))
    c_loc = _dot_tn(kw.astype(BF16), vb)
    yield

    s = qk * decay
    sv = _dot(s.astype(BF16), vb)
    den = jnp.sum(s, axis=1, keepdims=True) + w_inter * jnp.sum(q * n_in, axis=1, keepdims=True)
    yield

    num = sv + w_inter * qc
    o_ref[...] = num / jnp.maximum(jnp.abs(den), jnp.exp(-m_t))
    n_loc = jnp.sum(kw, axis=0, keepdims=True)
    m_new = jnp.maximum(g + m_in, m_loc)
    s_prev = jnp.exp(g + m_in - m_new)
    s_loc = jnp.exp(m_loc - m_new)
    c_ref[...] = s_prev * c_in + s_loc * c_loc
    n_ref[...] = s_prev * n_in + s_loc * n_loc
    m_ref[...] = jnp.broadcast_to(m_new, m_ref.shape)


def mlstm(qk, p, v_col_off, gates, gates_t, gate_b, rows):
    nct = (rows.L + rows.Lc) // CHUNK
    W = ML_H * ML_DH
    voff = v_col_off // W

    def rb(b, d, j):
        return rows.chunk_block(b, rows.scan_chunk(d, j))

    ng = 4 * ML_H
    bias = gate_b.reshape(ng)
    bias_row = jnp.pad(bias, (0, LANES - ng))[None]
    return pl.pallas_call(
        _mlstm_kernel,
        grid=(rows.B, 2, nct),
        in_specs=[pl.BlockSpec((CHUNK, W), lambda b, d, j: (rb(b, d, j), 0)),
                  pl.BlockSpec((CHUNK, W), lambda b, d, j: (rb(b, d, j), 1)),
                  pl.BlockSpec((CHUNK, W), lambda b, d, j: (rb(b, d, j), voff)),
                  pl.BlockSpec((CHUNK, LANES), lambda b, d, j: (rb(b, d, j), 0)),
                  pl.BlockSpec((ng, CHUNK), lambda b, d, j: (0, rb(b, d, j))),
                  pl.BlockSpec((1, LANES), lambda b, d, j: (0, 0)),
                  pl.BlockSpec((ng, 1), lambda b, d, j: (0, 0))],
        out_specs=pl.BlockSpec((None, CHUNK, W), lambda b, d, j: (d, rb(b, d, j), 0)),
        out_shape=jax.ShapeDtypeStruct((2, rows.n, W), F32),
        scratch_shapes=[pltpu.VMEM((ML_H, ML_DH, ML_DH), F32), pltpu.VMEM((ML_H, 1, ML_DH), F32),
                        pltpu.VMEM((ML_H, 8, LANES), F32), pltpu.VMEM((2, ng, CHUNK), F32)],
        compiler_params=_cparams(3),
        name="mlstm",
    )(qk, qk, p, gates, gates_t, bias_row, bias[:, None])


def _finish_ab_kernel(yl_ref, yc_ref, h_ref, o_ref, nw_ref, out_ref, *, n_lat_blocks):
    @pl.when(pl.program_id(0) < n_lat_blocks)
    def _():
        out_ref[:, 0:HY_D] = yl_ref[...]

    @pl.when(pl.program_id(0) >= n_lat_blocks)
    def _():
        out_ref[:, 0:HY_D] = yc_ref[...]

    hs = h_ref[0] + h_ref[1]
    og = _sigmoid(o_ref[...])
    nw = nw_ref[...]
    for i in range(ML_H):
        sl = slice(i * ML_DH, (i + 1) * ML_DH)
        x = hs[:, sl]
        mu = jnp.mean(x, axis=-1, keepdims=True)
        xc = x - mu
        var = jnp.mean(xc * xc, axis=-1, keepdims=True)
        y = xc * lax.rsqrt(var + EPS) * nw[:, sl] * og[:, sl]
        out_ref[:, HY_D + i * ML_DH:HY_D + (i + 1) * ML_DH] = y.astype(out_ref.dtype)


def finish_ab(yh_lat, yh_ctx, hdir, p, o_col_off, norm_w, widx, rows, bt=256):
    n_lat_blocks = rows.n_lat // bt
    per_b = rows.L // bt
    ooff = o_col_off // (ML_H * ML_DH)
    W = ML_H * ML_DH

    assert rows.Lc == bt
    n_blocks = rows.n // bt

    def lat_map(i):
        j = jnp.minimum(i, n_lat_blocks - 1)
        return (j % per_b, j // per_b)

    return pl.pallas_call(
        functools.partial(_finish_ab_kernel, n_lat_blocks=n_lat_blocks),
        grid=(n_blocks,),
        in_specs=[pl.BlockSpec((bt, HY_D), lat_map),
                  pl.BlockSpec((bt, HY_D), lambda i: (0, jnp.maximum(i - n_lat_blocks, 0))),
                  pl.BlockSpec((2, bt, W), lambda i: (0, i, 0)),
                  pl.BlockSpec((bt, W), lambda i: (i, ooff)),
                  pl.BlockSpec((None, 1, W), lambda i: (widx, 0, 0))],
        out_specs=pl.BlockSpec((bt, HY_D + W), lambda i: (i, 0)),
        out_shape=jax.ShapeDtypeStruct((rows.n, HY_D + W), BF16),
        compiler_params=_cparams(1),
        name="finish_ab",
    )(yh_lat, yh_ctx, hdir, p, norm_w.reshape(norm_w.shape[0], 1, W))


def _ssd_kernel(x_ref, b_ref, c_ref, dc_ref, dr_ref, pr_ref, pc_ref, o_ref, h_ref, ds_ref):
    d = pl.program_id(1)

    @pl.when(pl.program_id(2) == 0)
    def _():
        h_ref[...] = jnp.zeros_like(h_ref)

    dt_cols = _softplus(dc_ref[...] + pr_ref[1:2, :])
    la_cols = dt_cols * -jnp.exp(pr_ref[0:1, :])
    dt_rows = _softplus(dr_ref[...] + pc_ref[:, 1:2])
    la_rows = dt_rows * -jnp.exp(pc_ref[:, 0:1])

    causal, causal_t = _chunk_masks(d, CHUNK)
    acum_cols, acum_rows = _running_sums(la_cols, la_rows, causal, causal_t)
    ds_ref[0] = dt_rows
    ds_ref[1] = la_rows
    ds_ref[2] = acum_rows
    n_groups = b_ref.shape[1] // LANES
    heads_per_group = SSD_H // n_groups
    bmats = [b_ref[:, g * LANES:(g + 1) * LANES].astype(BF16) for g in range(n_groups)]
    cmats = [c_ref[:, g * LANES:(g + 1) * LANES].astype(BF16) for g in range(n_groups)]
    cbs = [_dot_nt(cmats[g], bmats[g]) for g in range(n_groups)]

    def head(hd):
        grp = hd // heads_per_group
        idx = SSD_H * d + hd
        hs = pl.ds(hd * SSD_P, SSD_P)
        dt_row = ds_ref[0, pl.ds(idx, 1), :]
        dt_col = _pick_col(dt_cols, idx)
        acum_col = _pick_col(acum_cols, idx)
        acum_row = ds_ref[2, pl.ds(idx, 1), :]
        tot = jnp.sum(ds_ref[1, pl.ds(idx, 1), :], axis=1, keepdims=True)
        decay = jnp.exp(jnp.where(causal, acum_col - acum_row, -jnp.inf))
        xh = x_ref[:, hs]
        h_in = h_ref[hd]
        xw = xh * (jnp.exp(tot - acum_col) * dt_col)
        y_off = _dot_nt(cmats[grp], h_in.astype(BF16))
        st = _dot_tn(xw.astype(BF16), bmats[grp])
        y_diag = _dot((cbs[grp] * decay * dt_row).astype(BF16), xh.astype(BF16))
        yield
        o_ref[:, hs] = y_diag + y_off * jnp.exp(acum_col)
        h_ref[hd] = jnp.exp(tot) * h_in + st

    _run_interleaved([head(hd) for hd in range(SSD_H)])


def ssd(xbc, dt, dt_t, a_log, dt_bias, rows):
    nct = (rows.L + rows.Lc) // CHUNK
    nd = SSD_H * SSD_P
    gw = 2 * LANES

    def rb(b, d, j):
        return rows.chunk_block(b, rows.scan_chunk(d, j))

    nh = 2 * SSD_H
    par = jnp.stack([a_log.reshape(nh), dt_bias.reshape(nh)])
    par_rows = jnp.pad(par, ((0, 0), (0, LANES - nh)))
    return pl.pallas_call(
        _ssd_kernel,
        grid=(rows.B, 2, nct),
        in_specs=[pl.BlockSpec((CHUNK, nd), lambda b, d, j: (rb(b, d, j), 0)),
                  pl.BlockSpec((CHUNK, gw), lambda b, d, j: (rb(b, d, j), nd // gw)),
                  pl.BlockSpec((CHUNK, gw), lambda b, d, j: (rb(b, d, j), nd // gw + 1)),
                  pl.BlockSpec((CHUNK, LANES), lambda b, d, j: (rb(b, d, j), 0)),
                  pl.BlockSpec((nh, CHUNK), lambda b, d, j: (0, rb(b, d, j))),
                  pl.BlockSpec((2, LANES), lambda b, d, j: (0, 0)),
                  pl.BlockSpec((nh, 2), lambda b, d, j: (0, 0))],
        out_specs=pl.BlockSpec((None, CHUNK, nd), lambda b, d, j: (d, rb(b, d, j), 0)),
        out_shape=jax.ShapeDtypeStruct((2, rows.n, nd), F32),
        scratch_shapes=[pltpu.VMEM((SSD_H, SSD_P, LANES), F32), pltpu.VMEM((3, nh, CHUNK), F32)],
        compiler_params=_cparams(3),
        name="ssd",
    )(xbc, xbc, xbc, dt, dt_t, par_rows, par.T)


S5_TC = 64
S5_JB = 2
S5_NS = S5_G * S5_P


def _s5_kernel(*refs, bsz):
    uf_refs, ub_refs = refs[:bsz], refs[bsz:2 * bsz]
    w_ref, c_ref, lre_ref, lim_ref, sf_ref, sb_ref = refs[2 * bsz:2 * bsz + 6]
    z_ref, s_ref, wb_ref, cb_ref, u_ref, y_ref = refs[2 * bsz + 6:]

    @pl.when(pl.program_id(0) == 0)
    def _():
        s_ref[...] = jnp.zeros_like(s_ref)
        wb_ref[...] = w_ref[...].astype(BF16)
        cb_ref[...] = c_ref[...].astype(BF16)

    r_i = lax.broadcasted_iota(jnp.int32, (S5_TC, S5_TC), 0)
    c_i = lax.broadcasted_iota(jnp.int32, (S5_TC, S5_TC), 1)
    flip = jnp.where(r_i + c_i == S5_TC - 1, 1.0, 0.0)
    group = 2 * bsz
    n_blk = S5_W // LANES
    sw = S5_NS // n_blk
    for b in range(bsz):
        uf = uf_refs[b][...]
        ub = _dot(flip.astype(BF16), ub_refs[b][...].astype(BF16))
        for j in range(n_blk):
            u_ref[j, pl.ds(b, S5_TC, stride=group), :] = uf[:, j * LANES:(j + 1) * LANES]
            u_ref[j, pl.ds(bsz + b, S5_TC, stride=group), :] = ub[:, j * LANES:(j + 1) * LANES]

    for j in range(n_blk):
        z_ref[:, 2 * j * sw:2 * (j + 1) * sw] = _dot(u_ref[j].astype(BF16), wb_ref[j])

    for j0 in range(0, n_blk, S5_JB):
        blocks = range(j0, j0 + S5_JB)
        re_sl = [slice(2 * j * sw, (2 * j + 1) * sw) for j in blocks]
        im_sl = [slice((2 * j + 1) * sw, (2 * j + 2) * sw) for j in blocks]
        ar = [lre_ref[:, j * sw:(j + 1) * sw] for j in blocks]
        ai = [lim_ref[:, j * sw:(j + 1) * sw] for j in blocks]

        def step(t, carry):
            r0 = pl.multiple_of(t * 8, 8)
            new = []
            for k in range(S5_JB):
                zr, zi = carry[2 * k], carry[2 * k + 1]
                nr = ar[k] * zr - ai[k] * zi + z_ref[pl.ds(r0, 8), re_sl[k]]
                ni = ar[k] * zi + ai[k] * zr + z_ref[pl.ds(r0, 8), im_sl[k]]
                z_ref[pl.ds(r0, 8), re_sl[k]] = nr
                z_ref[pl.ds(r0, 8), im_sl[k]] = ni
                new += [nr, ni]
            return tuple(new)

        init = []
        for k in range(S5_JB):
            init += [s_ref[:, re_sl[k]], s_ref[:, im_sl[k]]]
        fin = lax.fori_loop(0, S5_TC, step, tuple(init))
        for k in range(S5_JB):
            s_ref[:, re_sl[k]] = fin[2 * k]
            s_ref[:, im_sl[k]] = fin[2 * k + 1]

    row = lax.broadcasted_iota(jnp.int32, (S5_TC * 8, LANES), 0)
    is_bwd = (row & 4) != 0
    for j in range(n_blk):
        yy = _dot(z_ref[:, 2 * j * sw:2 * (j + 1) * sw].astype(BF16), cb_ref[j])
        y_ref[j] = jnp.where(is_bwd, yy[:, LANES:], yy[:, :LANES])
    for b in range(bsz):
        for j in range(n_blk):
            ls = slice(j * LANES, (j + 1) * LANES)
            sf_ref[b, :, ls] = y_ref[j, pl.ds(b, S5_TC, stride=group), :]
            sb_ref[b, :, ls] = _dot_hi(flip, y_ref[j, pl.ds(bsz + b, S5_TC, stride=group), :])


def s5_params(lam_re, lam_im, log_dt, b_re, b_im, c_re, c_im):
    dt = jnp.exp(log_dt)[..., None]
    mag = jnp.exp(lam_re * dt)
    lb_re, lb_im = mag * jnp.cos(lam_im * dt), mag * jnp.sin(lam_im * dt)
    den = lam_re * lam_re + lam_im * lam_im
    f_re = ((lb_re - 1) * lam_re + lb_im * lam_im) / den
    f_im = (lb_im * lam_re - (lb_re - 1) * lam_im) / den
    cf_re = c_re[None] * f_re[:, :, None, :] - c_im[None] * f_im[:, :, None, :]
    cf_im = c_re[None] * f_im[:, :, None, :] + c_im[None] * f_re[:, :, None, :]
    gpb = LANES // S5_GS
    n_blk = S5_G // gpb
    eye = jnp.eye(gpb, dtype=F32)

    def w_blocks(b):
        bb = b.reshape(n_blk, gpb, S5_P, S5_GS)
        return jnp.einsum('jgpi,gh->jgihp', bb, eye).reshape(n_blk, gpb * S5_GS, gpb * S5_P)

    def c_blocks(c):
        cc = c.reshape(2, n_blk, gpb, S5_GS, S5_P)
        return jnp.einsum('djgip,gh->jgpdhi', cc, eye).reshape(n_blk, gpb * S5_P, 2 * gpb * S5_GS)

    def lam_rows(l, bsz):
        return jnp.repeat(l.reshape(2, S5_NS), bsz, axis=0)

    w_cat = jnp.concatenate([w_blocks(b_re), w_blocks(b_im)], axis=2)
    c_cat = jnp.concatenate([c_blocks(cf_re), -c_blocks(cf_im)], axis=1)
    return w_cat, c_cat, lb_re, lb_im, lam_rows


def s5(p, w_cat, c_cat, lre, lim, rows):
    bsz, L, Lc = rows.B, rows.L, rows.Lc
    assert bsz == 4
    n_blk = S5_W // LANES
    sw = S5_NS // n_blk
    rt = S5_TC * 2 * bsz
    ncc, ncl = Lc // S5_TC, L // S5_TC
    lat0 = rows.n_lat // S5_TC

    def fwd_block(b):
        return lambda i: (jnp.where(i < ncc, lat0 + b * ncc + i, b * ncl + i - ncc), 0)

    def bwd_block(b):
        return lambda i: (jnp.where(i < ncc, lat0 + b * ncc + ncc - 1 - i, b * ncl + ncl - 1 + ncc - i), 0)

    full = lambda shape: pl.BlockSpec(shape, lambda i: (0,) * len(shape))
    u_specs = ([pl.BlockSpec((S5_TC, S5_W), fwd_block(b)) for b in range(bsz)]
               + [pl.BlockSpec((S5_TC, S5_W), bwd_block(b)) for b in range(bsz)])
    out_specs = [pl.BlockSpec((bsz, S5_TC, S5_W), lambda i: (0, jnp.maximum(i - ncc, 0), 0)),
                 pl.BlockSpec((bsz, S5_TC, S5_W), lambda i: (0, jnp.minimum(ncl - 1 + ncc - i, ncl - 1), 0))]
    return pl.pallas_call(
        functools.partial(_s5_kernel, bsz=bsz),
        grid=((L + Lc) // S5_TC,),
        in_specs=u_specs + [full((n_blk, LANES, 2 * sw)), full((n_blk, 2 * sw, 2 * LANES)),
                            full((2 * bsz, S5_NS)), full((2 * bsz, S5_NS))],
        out_specs=out_specs,
        out_shape=[jax.ShapeDtypeStruct((bsz, L, S5_W), F32)] * 2,
        scratch_shapes=[pltpu.VMEM((rt, 2 * S5_NS), F32), pltpu.VMEM((2 * bsz, 2 * S5_NS), F32),
                        pltpu.VMEM((n_blk, LANES, 2 * sw), BF16), pltpu.VMEM((n_blk, 2 * sw, 2 * LANES), BF16),
                        pltpu.VMEM((n_blk, rt, LANES), F32), pltpu.VMEM((n_blk, rt, LANES), F32)],
        compiler_params=_cparams(1),
        name="s5",
    )(*([p] * (2 * bsz)), w_cat, c_cat, lre, lim)


def _finish_cd_kernel(y_ref, xs_ref, z_ref, sf_ref, sb_ref, u_ref, dssd_ref, nw_ref, d5_ref, glu_ref, out_ref):
    y = (y_ref[0] + y_ref[1] + dssd_ref[...] * xs_ref[...]) * _silu(z_ref[...])
    y = y * lax.rsqrt(jnp.mean(y * y, axis=-1, keepdims=True) + EPS) * nw_ref[...]
    nd = y.shape[1]
    out_ref[:, 0:nd] = y.astype(out_ref.dtype)
    s = sf_ref[...] + sb_ref[...] + d5_ref[...] * u_ref[...]
    s = 0.5 * s * (1.0 + lax.erf(s * (2.0 ** -0.5)))
    gl = _dot(s.astype(BF16), glu_ref[...].astype(BF16))
    out_ref[:, nd:nd + S5_W] = (gl[:, 0:S5_W] * _sigmoid(gl[:, S5_W:])).astype(out_ref.dtype)


def finish_cd(ydir, xbc, p1, sf, sb, p2, ssd_d_lanes, norm_w, s5_d, glu_w, widx, n_rows, bt=256):
    nd = SSD_H * SSD_P
    vec = lambda n: pl.BlockSpec((None, 1, n), lambda i: (widx, 0, 0))
    return pl.pallas_call(
        _finish_cd_kernel,
        grid=(n_rows // bt,),
        in_specs=[pl.BlockSpec((2, bt, nd), lambda i: (0, i, 0)),
                  pl.BlockSpec((bt, nd), lambda i: (i, 0)),
                  pl.BlockSpec((bt, nd), lambda i: (i, 0)),
                  pl.BlockSpec((bt, S5_W), lambda i: (i, 0)),
                  pl.BlockSpec((bt, S5_W), lambda i: (i, 0)),
                  pl.BlockSpec((bt, S5_W), lambda i: (i, 0)),
                  vec(nd), vec(nd), vec(S5_W),
                  pl.BlockSpec((None, S5_W, 2 * S5_W), lambda i: (widx, 0, 0))],
        out_specs=pl.BlockSpec((bt, nd + S5_W), lambda i: (i, 0)),
        out_shape=jax.ShapeDtypeStruct((n_rows, nd + S5_W), BF16),
        compiler_params=_cparams(1),
        name="finish_cd",
    )(ydir, xbc, p1, sf, sb, p2, ssd_d_lanes, norm_w.reshape(norm_w.shape[0], 1, nd),
      s5_d.reshape(s5_d.shape[0], 1, S5_W), glu_w)


def _router_kernel(h_ref, w_ref, o_ref):
    logits = _dot_hi(h_ref[...], w_ref[...])
    lane = lax.broadcasted_iota(jnp.int32, logits.shape, 1)
    logits = jnp.where(lane < N_EXPERTS, logits, -jnp.inf)
    m1 = jnp.max(logits, axis=1, keepdims=True)
    i1 = jnp.min(jnp.where(logits == m1, lane, LANES), axis=1, keepdims=True)
    rest = jnp.where(lane == i1, -jnp.inf, logits)
    m2 = jnp.max(rest, axis=1, keepdims=True)
    i2 = jnp.min(jnp.where(rest == m2, lane, LANES), axis=1, keepdims=True)
    e2 = jnp.exp(m2 - m1)
    p1 = 1.0 / (1.0 + e2)
    p2 = e2 / (1.0 + e2)
    o_ref[...] = jnp.where(lane == 0, i1.astype(F32),
                           jnp.where(lane == 1, i2.astype(F32),
                                     jnp.where(lane == 2, p1, jnp.where(lane == 3, p2, 0.0))))


def router(h, w_router_padded, bt=512):
    T, D = h.shape
    return pl.pallas_call(
        _router_kernel,
        grid=(T // bt,),
        in_specs=[pl.BlockSpec((bt, D), lambda i: (i, 0)), pl.BlockSpec((D, LANES), lambda i: (0, 0))],
        out_specs=pl.BlockSpec((bt, LANES), lambda i: (i, 0)),
        out_shape=jax.ShapeDtypeStruct((T, LANES), F32),
        compiler_params=_cparams(1),
        name="router",
    )(h, w_router_padded)


def _gather_kernel(idx_ref, nt_ref, src_ref, o_ref, buf_ref, sem, *, bm, s):
    i = pl.program_id(0)
    n_used = nt_ref[0]
    slot = i % 2

    def row_copy(tile, slot_, r):
        return pltpu.make_async_copy(src_ref.at[idx_ref[tile * bm + r]],
                                     buf_ref.at[slot_, pl.ds(pl.multiple_of(r * s, s), s), :], sem.at[slot_])

    def issue(tile, slot_):
        def body(r, c):
            row_copy(tile, slot_, r).start()
            return c
        lax.fori_loop(0, bm, body, 0, unroll=8)

    @pl.when(i == 0)
    def _():
        issue(0, 0)

    @pl.when(i + 1 < n_used)
    def _():
        issue(i + 1, 1 - slot)

    @pl.when(i < n_used)
    def _():
        def body(r, c):
            row_copy(i, slot, r).wait()
            return c
        lax.fori_loop(0, bm, body, 0, unroll=8)
        for j in range(s):
            o_ref[:, j * LANES:(j + 1) * LANES] = buf_ref[slot, pl.ds(j, bm, stride=s), :].astype(o_ref.dtype)

    @pl.when(i >= n_used)
    def _():
        o_ref[...] = jnp.zeros_like(o_ref)


def gather_rows(src, idx, n_used_tiles, bm=256):
    N = idx.shape[0]
    T, D = src.shape
    s = D // LANES
    return pl.pallas_call(
        functools.partial(_gather_kernel, bm=bm, s=s),
        grid_spec=pltpu.PrefetchScalarGridSpec(
            num_scalar_prefetch=2,
            grid=(N // bm,),
            in_specs=[pl.BlockSpec(memory_space=pl.ANY)],
            out_specs=pl.BlockSpec((bm, D), lambda i, idx, nt: (i, 0)),
            scratch_shapes=[pltpu.VMEM((2, bm * s, LANES), F32), pltpu.SemaphoreType.DMA((2,))]),
        out_shape=jax.ShapeDtypeStruct((N, D), BF16),
        compiler_params=_cparams(1),
        name="gather_rows",
    )(idx, n_used_tiles, src.reshape(T, s, LANES))


def _gmm_kernel(te_ref, na_ref, nx_ref, x_ref, w_ref, o_ref, wf_ref, wb_ref, sem, *, widx, bn, nf, parts):
    n = pl.program_id(0)
    m = pl.program_id(1)
    e = te_ref[m]
    first = (m == 0) | (e != te_ref[jnp.maximum(m - 1, 0)])

    def fetch(expert, nblk):
        return [pltpu.make_async_copy(
            w_ref.at[widx, expert, :, pl.ds(pl.multiple_of((nblk + p * nf) * bn, bn), bn)],
            wf_ref.at[p], sem.at[p]) for p in range(parts)]

    @pl.when(first)
    def _():
        @pl.when((n == 0) & (m == 0))
        def _():
            for cp in fetch(e, n):
                cp.start()

        for cp in fetch(e, n):
            cp.wait()
        wb_ref[...] = wf_ref[...].astype(BF16)
        nxt = nx_ref[m]

        @pl.when(nxt >= 0)
        def _():
            for cp in fetch(nxt, n):
                cp.start()

        @pl.when((nxt < 0) & (n + 1 < nf))
        def _():
            for cp in fetch(te_ref[0], n + 1):
                cp.start()

    @pl.when(m < na_ref[0])
    def _():
        x = x_ref[...]
        if parts == 2:
            o_ref[...] = (_silu(_dot(x, wb_ref[0])) * _dot(x, wb_ref[1])).astype(o_ref.dtype)
        else:
            o_ref[...] = _dot(x, wb_ref[0]).astype(o_ref.dtype)

    @pl.when(m >= na_ref[0])
    def _():
        o_ref[...] = jnp.zeros_like(o_ref)


def gmm(x, w, widx, plan, *, bm, bn, swiglu):
    M, K = x.shape
    parts = 2 if swiglu else 1
    N = w.shape[3] // parts
    nf = N // bn
    return pl.pallas_call(
        functools.partial(_gmm_kernel, widx=widx, bn=bn, nf=nf, parts=parts),
        grid_spec=pltpu.PrefetchScalarGridSpec(
            num_scalar_prefetch=3,
            grid=(nf, M // bm),
            in_specs=[pl.BlockSpec((bm, K), lambda n, m, te, na, nx: (m, 0)),
                      pl.BlockSpec(memory_space=pl.ANY)],
            out_specs=pl.BlockSpec((bm, bn), lambda n, m, te, na, nx: (m, n)),
            scratch_shapes=[pltpu.VMEM((parts, K, bn), F32), pltpu.VMEM((parts, K, bn), BF16),
                            pltpu.SemaphoreType.DMA((parts,))]),
        out_shape=jax.ShapeDtypeStruct((M, N), BF16 if swiglu else F32),
        compiler_params=_cparams(2),
        name="gmm_swiglu" if swiglu else "gmm",
    )(plan.tile_expert, plan.n_active, plan.next_expert, x, w)


def _combine_kernel(pos_ref, y_ref, x_ref, pr_ref, g_ref, nw_ref, o_ref, buf_ref, sem, *, bm, n_tok):
    i = pl.program_id(0)
    slot = i % 2

    def copies(tile, slot_, r):
        return [pltpu.make_async_copy(y_ref.at[pl.ds(pos_ref[c * n_tok + tile * bm + r], 1), :],
                                      buf_ref.at[slot_, c, pl.ds(r, 1), :], sem.at[slot_, c]) for c in range(2)]

    def issue(tile, slot_):
        def body(r, carry):
            for cp in copies(tile, slot_, r):
                cp.start()
            return carry
        lax.fori_loop(0, bm, body, 0, unroll=4)

    @pl.when(i == 0)
    def _():
        issue(0, 0)

    @pl.when(i + 1 < pl.num_programs(0))
    def _():
        issue(i + 1, 1 - slot)

    def drain(r, carry):
        for cp in copies(i, slot, r):
            cp.wait()
        return carry

    lax.fori_loop(0, bm, drain, 0, unroll=4)
    pr = pr_ref[...]
    mix = pr[:, 2:3] * buf_ref[slot, 0] + pr[:, 3:4] * buf_ref[slot, 1]
    x = x_ref[...] + g_ref[...] * mix
    o_ref[...] = x * lax.rsqrt(jnp.mean(x * x, axis=-1, keepdims=True) + EPS) * nw_ref[...]


def moe_combine(ys, pos, x, route, gate, norm_w, rows, bm=256):
    T, D = x.shape
    return pl.pallas_call(
        functools.partial(_combine_kernel, bm=bm, n_tok=T),
        grid_spec=pltpu.PrefetchScalarGridSpec(
            num_scalar_prefetch=1,
            grid=(T // bm,),
            in_specs=[pl.BlockSpec(memory_space=pl.ANY),
                      pl.BlockSpec((bm, D), lambda i, pos: (i, 0)),
                      pl.BlockSpec((bm, LANES), lambda i, pos: (i, 0)),
                      pl.BlockSpec((None, 1, D), lambda i, pos: (rows.mod_row(i, bm), 0, 0)),
                      pl.BlockSpec((1, D), lambda i, pos: (0, 0))],
            out_specs=pl.BlockSpec((bm, D), lambda i, pos: (i, 0)),
            scratch_shapes=[pltpu.VMEM((2, 2, bm, D), F32), pltpu.SemaphoreType.DMA((2, 2))]),
        out_shape=jax.ShapeDtypeStruct((T, D), F32),
        compiler_params=_cparams(1),
        name="moe_combine",
    )(pos, ys, x, route, gate, norm_w[None])


class MoePlan(NamedTuple):
    pos: jax.Array
    src: jax.Array
    tile_expert: jax.Array
    n_active: jax.Array
    next_expert: jax.Array


def moe_plan(route, n_tok, bm):
    e = jnp.concatenate([route[:, 0], route[:, 1]]).astype(jnp.int32)
    onehot = (e[:, None] == jnp.arange(N_EXPERTS, dtype=jnp.int32)[None, :]).astype(jnp.int32)
    rank = jnp.take_along_axis(jnp.cumsum(onehot, axis=0), e[:, None], axis=1)[:, 0] - 1
    counts = jnp.sum(onehot, axis=0)
    tiles = (counts + bm - 1) // bm
    tile_end = jnp.cumsum(tiles)
    start = (tile_end - tiles) * bm
    pos = start[e] + rank
    n_tiles = (2 * n_tok) // bm + N_EXPERTS
    tok = jnp.concatenate([jnp.arange(n_tok, dtype=jnp.int32)] * 2)
    src = jnp.zeros((n_tiles * bm,), jnp.int32).at[pos].set(tok)
    n_active = tile_end[-1]
    tile_ids = jnp.minimum(jnp.arange(n_tiles, dtype=jnp.int32), n_active - 1)
    tile_expert = jnp.sum((tile_ids[:, None] >= tile_end[None, :]).astype(jnp.int32), axis=1)
    ids = jnp.arange(N_EXPERTS, dtype=jnp.int32)
    later_used = (ids[None, :] > ids[:, None]) & (tiles[None, :] > 0)
    next_of = jnp.min(jnp.where(later_used, ids[None, :], N_EXPERTS), axis=1)
    next_of = jnp.where(next_of == N_EXPERTS, -1, next_of)
    return MoePlan(pos.astype(jnp.int32), src, tile_expert.astype(jnp.int32),
                   n_active.reshape(1).astype(jnp.int32), next_of[tile_expert].astype(jnp.int32))


def kernel(x, c, ctx, c_ctx, ada_w, ada_b, ab_w_in, hy_conv_w, hy_conv_b, hy_filt_w1, hy_filt_b1, hy_filt_w2, hy_filt_b2, hy_filt_w3, hy_filt_freq, hy_bias, ml_conv_w, ml_conv_b, ml_gate_b, ml_norm_w, ab_w_out, ffn_w_gu, ffn_w_down, cd_w_in, ssd_conv_w, ssd_conv_b, ssd_A_log, ssd_dt_bias, ssd_D, ssd_norm_w, s5_lam_re, s5_lam_im, s5_log_dt, s5_B_re, s5_B_im, s5_C_re, s5_C_im, s5_D, s5_glu_w, cd_w_out, moe_router, moe_w_gu, moe_w_down, final_norm_w):
    bsz, L, D = x.shape
    Lc = ctx.shape[1]
    rows = Rows(bsz, L, Lc)
    n_lat = rows.n_lat
    assert bsz == 4 and L % 1024 == 0 and Lc == 256 and D % 256 == 0
    bn_d = min(1024, D)
    bn_f = min(512, ffn_w_down.shape[1])

    xs = jnp.concatenate([x.reshape(n_lat, D), ctx.reshape(bsz * Lc, D)], axis=0)
    cond = jnp.concatenate([c, c_ctx[None], jnp.zeros((8 - bsz - 1, D), F32)], axis=0)

    def mods(layer):
        m = adaln(cond, ada_w, ada_b, layer)
        return [m[:, k * D:(k + 1) * D].reshape(8, 1, D) for k in range(6)]

    md = mods(0)
    h = modulate(xs, md[0], md[1], rows, rows.n, BF16)
    n_main = 3 * HY_D + 4 * ML_H * ML_DH
    p = mm(h, ab_w_in, 0, n_cols=n_main, bm=1024, bn=1024)
    w_gate = jnp.pad(ab_w_in[:, :, n_main:], ((0, 0), (0, 0), (0, LANES - 4 * ML_H)))
    gates = mm(h, w_gate, 0, n_cols=LANES, bm=1024, bn=LANES)
    gates_t = gates[:, :4 * ML_H].T

    filt = (hy_filt_w1, hy_filt_b1, hy_filt_w2, hy_filt_b2, hy_filt_w3, hy_filt_freq)
    yh_lat = hyena(p, L, bsz, 0, GRID_W, 0, hy_conv_w, hy_conv_b, filt, hy_bias)
    yh_ctx = hyena(p, Lc, bsz, n_lat, Lc, 0, hy_conv_w, hy_conv_b, filt, hy_bias)

    qk = conv_silu(p, 3 * HY_D, 2 * ML_H * ML_DH, ml_conv_w, ml_conv_b, 0, rows)
    hdir = mlstm(qk, p, 3 * HY_D + 2 * ML_H * ML_DH, gates, gates_t, ml_gate_b[0], rows)
    cat = finish_ab(yh_lat, yh_ctx, hdir, p, 3 * HY_D + 3 * ML_H * ML_DH, ml_norm_w, 0, rows)
    xs = mm_residual(cat, ab_w_out, 0, xs, md[2], rows, bm=1024, bn=bn_d)

    h = modulate(xs, md[3], md[4], rows, rows.n, BF16)
    act = mm_swiglu(h, ffn_w_gu, 0, bm=1024, bn=bn_f)
    xs = mm_residual(act, ffn_w_down, 0, xs, md[5], rows, bm=512, bn=min(512, D))

    md = mods(1)
    h = modulate(xs, md[0], md[1], rows, rows.n, BF16)
    nd = SSD_H * SSD_P
    n_xbc = nd + 4 * LANES
    p1 = mm(h, cd_w_in, 0, n_cols=nd + n_xbc, bm=1024, bn=512)
    w_tail = cd_w_in[:, :, nd + n_xbc:]
    w_tail = jnp.concatenate([w_tail[:, :, 2 * SSD_H:], w_tail[:, :, :2 * SSD_H],
                              jnp.zeros((1, D, LANES - 2 * SSD_H), F32)], axis=2)
    p2 = mm(h, w_tail, 0, n_cols=S5_W + LANES, bm=1024, bn=S5_W + LANES)

    xbc = conv_silu(p1, nd, n_xbc, ssd_conv_w, ssd_conv_b, 0, rows)
    dt = p2[:, S5_W:]
    dt_t = dt[:, :2 * SSD_H].T
    ydir = ssd(xbc, dt, dt_t, ssd_A_log[0], ssd_dt_bias[0], rows)

    w_cat, c_cat, lb_re, lb_im, lam_rows = s5_params(
        s5_lam_re[0], s5_lam_im[0], s5_log_dt[0], s5_B_re[0], s5_B_im[0], s5_C_re[0], s5_C_im[0])
    sf, sb = s5(p2, w_cat, c_cat, lam_rows(lb_re, bsz), lam_rows(lb_im, bsz), rows)
    sf, sb = sf.reshape(n_lat, S5_W), sb.reshape(n_lat, S5_W)

    ssd_d_lanes = jnp.repeat(ssd_D, SSD_P, axis=1).reshape(ssd_D.shape[0], 1, nd)
    cat = finish_cd(ydir, xbc, p1, sf, sb, p2, ssd_d_lanes, ssd_norm_w, s5_D, s5_glu_w, 0, n_lat)
    xl = mm_residual(cat, cd_w_out, 0, xs, md[2], rows, bm=1024, bn=bn_d)

    h2 = modulate(xl, md[3], md[4], rows, n_lat, F32)
    route = router(h2, jnp.pad(moe_router[0], ((0, 0), (0, LANES - N_EXPERTS))))
    bm_e = 512
    plan = moe_plan(route, n_lat, bm_e)
    xg = gather_rows(h2, plan.src, plan.n_active * (bm_e // 256), bm=256)
    act = gmm(xg, moe_w_gu, 0, plan, bm=bm_e, bn=min(512, moe_w_down.shape[2]), swiglu=True)
    ys = gmm(act, moe_w_down, 0, plan, bm=bm_e, bn=min(512, D), swiglu=False)
    out = moe_combine(ys, plan.pos, xl, route, md[5], final_norm_w, rows, bm=256)
    return out.reshape(bsz, L, D)
```

```python
import functools
import math
from typing import NamedTuple

import jax
import jax.numpy as jnp
import numpy as np
from jax import lax
from jax.experimental import pallas as pl
from jax.experimental.pallas import tpu as pltpu

F32 = jnp.float32
BF16 = jnp.bfloat16

EPS = 1e-6
GRID_W = 64
CHUNK = 128
LANES = 128
HY_D = 1024
HY_EMB = 33
HY_FAST_DECAY = 0.3
HY_SLOW_DECAY = 1.5
HY_DECAY_TARGET = 1e-2
ML_H = 8
ML_DH = 128
SSD_H = 16
SSD_P = 64
S5_W = 512
S5_G = 32
S5_GS = 16
S5_P = 64
N_EXPERTS = 8
VMEM_LIMIT_BYTES = 56 * 1024 * 1024


def _cparams(n_axes):
    return pltpu.CompilerParams(dimension_semantics=("arbitrary",) * n_axes,
                                vmem_limit_bytes=VMEM_LIMIT_BYTES)


def _sigmoid(x):
    return 1.0 / (1.0 + jnp.exp(-x))


def _silu(x):
    return x * _sigmoid(x)


def _log_sigmoid(x):
    return jnp.minimum(x, 0.0) - jnp.log(1.0 + jnp.exp(-jnp.abs(x)))


def _softplus(x):
    return jnp.maximum(x, 0.0) + jnp.log(1.0 + jnp.exp(-jnp.abs(x)))


def _dot(a, b):
    return jnp.dot(a, b, preferred_element_type=F32)


def _dot_nt(a, b):
    return lax.dot_general(a, b, (((1,), (1,)), ((), ())), preferred_element_type=F32)


def _dot_tn(a, b):
    return lax.dot_general(a, b, (((0,), (0,)), ((), ())), preferred_element_type=F32)


def _dot_hi(a, b):
    return jnp.dot(a, b, preferred_element_type=F32, precision=lax.Precision.HIGHEST)


class Rows:
    def __init__(self, bsz, L, Lc):
        self.B, self.L, self.Lc = bsz, L, Lc
        self.n_lat = bsz * L
        self.n = bsz * (L + Lc)

    def mod_row(self, i, bm):
        n_lat_blocks = self.n_lat // bm
        return jnp.where(i < n_lat_blocks, (i * bm) // self.L, self.B)

    def chunk_block(self, b, c):
        ncc = self.Lc // CHUNK
        return jnp.where(c < ncc, self.n_lat // CHUNK + b * ncc + c, b * (self.L // CHUNK) + c - ncc)

    def scan_chunk(self, d, j):
        ncc = self.Lc // CHUNK
        nct = (self.L + self.Lc) // CHUNK
        back = jnp.where(j < ncc, ncc - 1 - j, nct - 1 + ncc - j)
        return jnp.where(d == 0, j, back)


def _adaln_kernel(c_ref, w_ref, b_ref, o_ref):
    cond = _silu(c_ref[...]).astype(BF16)
    o_ref[...] = _dot(cond, w_ref[...].astype(BF16)) + b_ref[...]


def adaln(cond, ada_w, ada_b, layer, bn=1024):
    bn = min(bn, ada_w.shape[1])
    _, D, N = ada_w.shape
    return pl.pallas_call(
        _adaln_kernel,
        grid=(N // bn,),
        in_specs=[pl.BlockSpec((8, D), lambda n: (0, 0)),
                  pl.BlockSpec((None, D, bn), lambda n: (layer, 0, n)),
                  pl.BlockSpec((None, 1, bn), lambda n: (layer, 0, n))],
        out_specs=pl.BlockSpec((8, bn), lambda n: (0, n)),
        out_shape=jax.ShapeDtypeStruct((8, N), F32),
        compiler_params=_cparams(1),
        name="adaln",
    )(cond, ada_w, ada_b.reshape(ada_b.shape[0], 1, N))


def _head_tail_specs(block, n_head_blocks, row_col):
    def head(*ids):
        r, c = row_col(*ids)
        return jnp.minimum(r, n_head_blocks - 1), c

    def tail(*ids):
        r, c = row_col(*ids)
        return jnp.maximum(r - n_head_blocks, 0), c

    return pl.BlockSpec(block, head), pl.BlockSpec(block, tail)


def _modulate_kernel(*refs, n_head_blocks):
    x_ref, sh_ref, sc_ref, o_ref = refs[0], refs[-3], refs[-2], refs[-1]
    x = x_ref[...]
    if len(refs) == 5:
        x = jnp.where(pl.program_id(0) < n_head_blocks, x, refs[1][...])
    y = x * lax.rsqrt(jnp.mean(x * x, axis=-1, keepdims=True) + EPS)
    o_ref[...] = (y * (1.0 + sc_ref[...]) + sh_ref[...]).astype(o_ref.dtype)


def modulate(x, shift, scale, rows, n_rows, out_dtype, bm=512, x_tail=None):
    D = x.shape[1]
    mod_spec = pl.BlockSpec((None, 1, D), lambda i: (rows.mod_row(i, bm), 0, 0))
    if x_tail is None:
        xs, x_specs = [x], [pl.BlockSpec((bm, D), lambda i: (i, 0))]
    else:
        xs, x_specs = [x, x_tail], list(_head_tail_specs((bm, D), x.shape[0] // bm, lambda i: (i, 0)))
    return pl.pallas_call(
        functools.partial(_modulate_kernel, n_head_blocks=x.shape[0] // bm),
        grid=(n_rows // bm,),
        in_specs=x_specs + [mod_spec, mod_spec],
        out_specs=pl.BlockSpec((bm, D), lambda i: (i, 0)),
        out_shape=jax.ShapeDtypeStruct((n_rows, D), out_dtype),
        compiler_params=_cparams(1),
        name="modulate",
    )(*xs, shift, scale)


def _mm_kernel(x_ref, w_ref, o_ref, wb_ref):
    @pl.when(pl.program_id(1) == 0)
    def _():
        wb_ref[...] = w_ref[...].astype(BF16)
    o_ref[...] = _dot(x_ref[...], wb_ref[...]).astype(o_ref.dtype)


def mm(x, w, widx, *, n_cols, col_off=0, bm, bn, out_dtype=F32):
    M, K = x.shape
    off = col_off // bn
    return pl.pallas_call(
        _mm_kernel,
        grid=(n_cols // bn, M // bm),
        in_specs=[pl.BlockSpec((bm, K), lambda n, m: (m, 0)),
                  pl.BlockSpec((None, K, bn), lambda n, m: (widx, 0, n + off))],
        out_specs=pl.BlockSpec((bm, bn), lambda n, m: (m, n)),
        out_shape=jax.ShapeDtypeStruct((M, n_cols), out_dtype),
        scratch_shapes=[pltpu.VMEM((K, bn), BF16)],
        compiler_params=_cparams(2),
        name="mm",
    )(x, w)


def _mm_res_kernel(*refs, n_head_blocks):
    x_ref, w_ref, r_ref = refs[:3]
    g_ref, o_ref, wb_ref = refs[-3:]

    @pl.when(pl.program_id(1) == 0)
    def _():
        wb_ref[...] = w_ref[...].astype(BF16)
    res = r_ref[...]
    if len(refs) == 7:
        res = jnp.where(pl.program_id(1) < n_head_blocks, res, refs[3][...])
    o_ref[...] = res + g_ref[...] * _dot(x_ref[...], wb_ref[...])


def mm_residual(x, w, widx, res, gate, rows, *, bm, bn, res_tail=None):
    M, K = x.shape
    N = w.shape[2]
    if res_tail is None:
        rs, r_specs = [res], [pl.BlockSpec((bm, bn), lambda n, m: (m, n))]
    else:
        rs, r_specs = [res, res_tail], list(_head_tail_specs((bm, bn), res.shape[0] // bm, lambda n, m: (m, n)))
    return pl.pallas_call(
        functools.partial(_mm_res_kernel, n_head_blocks=res.shape[0] // bm),
        grid=(N // bn, M // bm),
        in_specs=[pl.BlockSpec((bm, K), lambda n, m: (m, 0)),
                  pl.BlockSpec((None, K, bn), lambda n, m: (widx, 0, n))] + r_specs
                 + [pl.BlockSpec((None, 1, bn), lambda n, m: (rows.mod_row(m, bm), 0, n))],
        out_specs=pl.BlockSpec((bm, bn), lambda n, m: (m, n)),
        out_shape=jax.ShapeDtypeStruct((M, N), F32),
        scratch_shapes=[pltpu.VMEM((K, bn), BF16)],
        compiler_params=_cparams(2),
        name="mm_residual",
    )(x, w, *rs, gate)


def _mm_swiglu_kernel(x_ref, wg_ref, wu_ref, o_ref, wgb_ref, wub_ref):
    @pl.when(pl.program_id(1) == 0)
    def _():
        wgb_ref[...] = wg_ref[...].astype(BF16)
        wub_ref[...] = wu_ref[...].astype(BF16)
    x = x_ref[...]
    g = _dot(x, wgb_ref[...])
    u = _dot(x, wub_ref[...])
    o_ref[...] = (_silu(g) * u).astype(o_ref.dtype)


def mm_swiglu(x, w_gu, widx, *, bm, bn):
    M, K = x.shape
    F = w_gu.shape[2] // 2
    nf = F // bn
    return pl.pallas_call(
        _mm_swiglu_kernel,
        grid=(nf, M // bm),
        in_specs=[pl.BlockSpec((bm, K), lambda n, m: (m, 0)),
                  pl.BlockSpec((None, K, bn), lambda n, m: (widx, 0, n)),
                  pl.BlockSpec((None, K, bn), lambda n, m: (widx, 0, n + nf))],
        out_specs=pl.BlockSpec((bm, bn), lambda n, m: (m, n)),
        out_shape=jax.ShapeDtypeStruct((M, F), BF16),
        scratch_shapes=[pltpu.VMEM((K, bn), BF16), pltpu.VMEM((K, bn), BF16)],
        compiler_params=_cparams(2),
        name="mm_swiglu",
    )(x, w_gu, w_gu)


def _conv3(x, w_ref, b_ref, period):
    n = x.shape[0]
    pos = lax.broadcasted_iota(jnp.int32, x.shape, 0) & (period - 1)
    prev = jnp.where(pos == 0, 0.0, pltpu.roll(x, 1, 0))
    nxt = jnp.where(pos == period - 1, 0.0, pltpu.roll(x, n - 1, 0))
    w = w_ref[...]
    return b_ref[...] + prev * w[0:1] + x * w[1:2] + nxt * w[2:3]


def _conv_silu_kernel(x_ref, w_ref, b_ref, o_ref, *, n_lat_blocks, lat_period, ctx_period):
    period = jnp.where(pl.program_id(0) < n_lat_blocks, lat_period, ctx_period)
    o_ref[...] = _silu(_conv3(x_ref[...], w_ref, b_ref, period)).astype(o_ref.dtype)


def conv_silu(p, col_off, n_cols, w, b, widx, rows, *, bt=1024, bc=512):
    assert rows.n_lat % bt == 0 and bt % rows.Lc == 0 and bt % GRID_W == 0
    off = col_off // bc
    kern = functools.partial(_conv_silu_kernel, n_lat_blocks=rows.n_lat // bt,
                             lat_period=GRID_W, ctx_period=rows.Lc)
    return pl.pallas_call(
        kern,
        grid=(rows.n // bt, n_cols // bc),
        in_specs=[pl.BlockSpec((bt, bc), lambda i, j: (i, j + off)),
                  pl.BlockSpec((None, 3, bc), lambda i, j: (widx, 0, j)),
                  pl.BlockSpec((None, 1, bc), lambda i, j: (widx, 0, j))],
        out_specs=pl.BlockSpec((bt, bc), lambda i, j: (i, j)),
        out_shape=jax.ShapeDtypeStruct((rows.n, n_cols), F32),
        compiler_params=_cparams(2),
        name="conv_silu",
    )(p, w, b.reshape(b.shape[0], 1, b.shape[1]))


def _hyena_conv_kernel(p0_ref, p1_ref, p2_ref, w0_ref, w1_ref, w2_ref, b0_ref, b1_ref, b2_ref,
                       x0_ref, u_ref, *, period):
    x0_ref[...] = _conv3(p0_ref[...], w0_ref, b0_ref, period)
    x1 = _conv3(p1_ref[...], w1_ref, b1_ref, period)
    v = _conv3(p2_ref[...], w2_ref, b2_ref, period)
    u_ref[...] = (x1 * v).astype(u_ref.dtype)


def hyena_conv(p, w, b, widx, *, row0, bsz, seq, period, bt=256, bc=512):
    nt = seq // bt
    ncb = HY_D // bc
    rb0 = row0 // bt

    def pspec(k):
        return pl.BlockSpec((bt, bc), lambda bb, i, j: (rb0 + bb * nt + i, j + k * ncb))

    def wspec(k):
        return pl.BlockSpec((None, 3, bc), lambda bb, i, j: (widx, 0, j + k * ncb))

    def bspec(k):
        return pl.BlockSpec((None, 1, bc), lambda bb, i, j: (widx, 0, j + k * ncb))

    ospec = pl.BlockSpec((bt, bc), lambda bb, i, j: (i, bb * ncb + j))
    b3 = b.reshape(b.shape[0], 1, b.shape[1])
    return pl.pallas_call(
        functools.partial(_hyena_conv_kernel, period=period),
        grid=(bsz, nt, ncb),
        in_specs=[pspec(0), pspec(1), pspec(2), wspec(0), wspec(1), wspec(2), bspec(0), bspec(1), bspec(2)],
        out_specs=[ospec, ospec],
        out_shape=[jax.ShapeDtypeStruct((seq, bsz * HY_D), F32),
                   jax.ShapeDtypeStruct((seq, bsz * HY_D), BF16)],
        compiler_params=_cparams(3),
        name="hyena_conv",
    )(p, p, p, w, w, w, b3, b3, b3)


def _hyena_feats(L):
    pos = np.arange(L, dtype=np.float64)
    t = pos / max(L - 1, 1)
    n_bands = (HY_EMB - 1) // 2
    bands = np.linspace(1e-4, n_bands - 1, n_bands)
    ang = (2 * math.pi / L) * pos[:, None] * bands[None, :]
    feats = np.concatenate([t[:, None], np.cos(ang), -np.sin(ang)], axis=-1)
    feats = np.pad(feats, ((0, 0), (0, LANES - HY_EMB)))
    deltas = np.abs(np.linspace(math.log(HY_DECAY_TARGET) / HY_SLOW_DECAY,
                                math.log(HY_DECAY_TARGET) / HY_FAST_DECAY, HY_D))
    return feats.astype(np.float32), t.astype(np.float32)[:, None], deltas.astype(np.float32)[None, :]


def _hyena_filter_kernel(feats_ref, t_ref, dl_ref, w1_ref, b1_ref, w2_ref, b2_ref, w3f_ref, w3b_ref,
                         fq_ref, hf_ref, hb_ref):
    fq = fq_ref[...]
    h = jnp.sin(fq[0:1] * (_dot_hi(feats_ref[...], w1_ref[...]) + b1_ref[...]))
    h = jnp.sin(fq[1:2] * (_dot_hi(h, w2_ref[...]) + b2_ref[...]))
    win = jnp.exp(-t_ref[...] * dl_ref[...])
    h_f = _dot_hi(h, w3f_ref[...]) * win
    h_b = _dot_hi(h, w3b_ref[...]) * win
    row = lax.broadcasted_iota(jnp.int32, h_b.shape, 0)
    h_b = jnp.where(row == 0, 0.0, h_b)
    l1 = jnp.sum(jnp.abs(h_f), axis=0, keepdims=True) + jnp.sum(jnp.abs(h_b), axis=0, keepdims=True)
    hf_ref[...] = h_f / l1
    hb_ref[...] = h_b / l1


def hyena_filter(L, w1, b1, w2, b2, w3, freq, widx, bc=256):
    feats, t, deltas = _hyena_feats(L)
    hid = w2.shape[1]
    w1p = jnp.pad(w1[widx], ((0, LANES - HY_EMB), (0, 0)))
    ncb = HY_D // bc
    full = lambda shape: pl.BlockSpec(shape, lambda j: (0,) * len(shape))
    return pl.pallas_call(
        _hyena_filter_kernel,
        grid=(ncb,),
        in_specs=[full((L, LANES)), full((L, 1)), pl.BlockSpec((1, bc), lambda j: (0, j)),
                  full((LANES, hid)), full((1, hid)), full((hid, hid)), full((1, hid)),
                  pl.BlockSpec((hid, bc), lambda j: (0, j)),
                  pl.BlockSpec((hid, bc), lambda j: (0, j + ncb)),
                  full((2, hid))],
        out_specs=[pl.BlockSpec((L, bc), lambda j: (0, j)), pl.BlockSpec((L, bc), lambda j: (0, j))],
        out_shape=[jax.ShapeDtypeStruct((L, HY_D), F32), jax.ShapeDtypeStruct((L, HY_D), F32)],
        compiler_params=_cparams(1),
        name="hyena_filter",
    )(jnp.asarray(feats), jnp.asarray(t), jnp.asarray(deltas), w1p, b1[widx][None], w2[widx],
      b2[widx][None], w3[widx], w3[widx], freq[widx])


def _dft_matrix(L):
    f = np.arange(L, dtype=np.int64)[:, None]
    s = np.arange(L, dtype=np.int64)[None, :]
    ang = (math.pi / L) * ((f * s) % (2 * L)).astype(np.float64)
    a_cos = np.cos(ang)
    a_sin = -np.sin(ang)
    a_sin[0, :] = np.where(np.arange(L) % 2 == 0, 1.0, -1.0)
    return np.concatenate([a_cos, a_sin], axis=0).astype(np.float32)


def _spectrum_kernel(ac_ref, as_ref, hf_ref, hb_ref, kre_ref, kim_ref, kny_ref, *, L, bf):
    hf = hf_ref[...].astype(BF16)
    hb = hb_ref[...].astype(BF16)
    ac = ac_ref[...].astype(BF16)
    a_s = as_ref[...].astype(BF16)
    cf, cb = _dot(ac, hf), _dot(ac, hb)
    sf, sb = _dot(a_s, hf), _dot(a_s, hb)
    f = lax.broadcasted_iota(jnp.int32, cf.shape, 0) + pl.program_id(1) * bf
    wgt = jnp.where(f == 0, 0.5 / L, 1.0 / L)
    kre_ref[...] = (cf + cb) * wgt
    kim_ref[...] = jnp.where(f == 0, 0.0, (sf - sb) * wgt)
    kny_ref[...] = jnp.where(f == 0, (sf + sb) * wgt, (cf + cb) * wgt)


def hyena_spectrum(a_mat, hf, hb, L, bf=256, bc=512):
    bf = min(bf, L)
    nf = L // bf
    C = hf.shape[1]
    ospec = pl.BlockSpec((bf, bc), lambda c, i: (i, c))
    hspec = pl.BlockSpec((L, bc), lambda c, i: (0, c))
    return pl.pallas_call(
        functools.partial(_spectrum_kernel, L=L, bf=bf),
        grid=(C // bc, nf),
        in_specs=[pl.BlockSpec((bf, L), lambda c, i: (i, 0)), pl.BlockSpec((bf, L), lambda c, i: (i + nf, 0)),
                  hspec, hspec],
        out_specs=[ospec, ospec, ospec],
        out_shape=[jax.ShapeDtypeStruct((L, C), F32)] * 3,
        compiler_params=_cparams(2),
        name="hyena_spectrum",
    )(a_mat, a_mat, hf, hb)


def _lc_fwd_kernel(ac_ref, as_ref, u_ref, kre_ref, kim_ref, kny_ref, yre_ref, yim_ref, acb_ref, asb_ref):
    @pl.when(pl.program_id(1) == 0)
    def _():
        acb_ref[...] = ac_ref[...].astype(BF16)
        asb_ref[...] = as_ref[...].astype(BF16)
    u = u_ref[...]
    ure = _dot(acb_ref[...], u)
    uim = _dot(asb_ref[...], u)
    kim = kim_ref[...]
    yre_ref[...] = (kre_ref[...] * ure - kim * uim).astype(yre_ref.dtype)
    yim_ref[...] = (kny_ref[...] * uim + kim * ure).astype(yim_ref.dtype)


def long_conv_fwd(a_mat, u, kre, kim, kny, L, bsz, bf=512):
    bf = min(bf, L)
    nf = L // bf
    C = kre.shape[1]
    kspec = pl.BlockSpec((bf, C), lambda i, b: (i, 0))
    ospec = pl.BlockSpec((bf, C), lambda i, b: (i, b))
    return pl.pallas_call(
        _lc_fwd_kernel,
        grid=(nf, bsz),
        in_specs=[pl.BlockSpec((bf, L), lambda i, b: (i, 0)), pl.BlockSpec((bf, L), lambda i, b: (i + nf, 0)),
                  pl.BlockSpec((L, C), lambda i, b: (0, b)), kspec, kspec, kspec],
        out_specs=[ospec, ospec],
        out_shape=[jax.ShapeDtypeStruct((L, bsz * C), BF16)] * 2,
        scratch_shapes=[pltpu.VMEM((bf, L), BF16), pltpu.VMEM((bf, L), BF16)],
        compiler_params=_cparams(2),
        name="long_conv_fwd",
    )(a_mat, a_mat, u, kre, kim, kny)


def _lc_inv_kernel(atc_ref, ats_ref, yre_ref, yim_ref, x0_ref, u_ref, bias_ref, o_ref, atcb_ref, atsb_ref):
    @pl.when(pl.program_id(1) == 0)
    def _():
        atcb_ref[...] = atc_ref[...].astype(BF16)
        atsb_ref[...] = ats_ref[...].astype(BF16)
    y = _dot(atcb_ref[...], yre_ref[...]) + _dot(atsb_ref[...], yim_ref[...])
    u = u_ref[...].astype(F32)
    o_ref[...] = (x0_ref[...] * (y + bias_ref[...] * u)).astype(o_ref.dtype)


def long_conv_inv(at_mat, yre, yim, x0, u, bias, widx, L, bsz, bt=512):
    bt = min(bt, L)
    nt = L // bt
    C = bias.shape[1]
    tspec = pl.BlockSpec((bt, C), lambda i, b: (i, b))
    yspec = pl.BlockSpec((L, C), lambda i, b: (0, b))
    return pl.pallas_call(
        _lc_inv_kernel,
        grid=(nt, bsz),
        in_specs=[pl.BlockSpec((bt, L), lambda i, b: (i, 0)), pl.BlockSpec((bt, L), lambda i, b: (i, 1)),
                  yspec, yspec, tspec, tspec, pl.BlockSpec((None, 1, C), lambda i, b: (widx, 0, 0))],
        out_specs=tspec,
        out_shape=jax.ShapeDtypeStruct((L, bsz * C), BF16),
        scratch_shapes=[pltpu.VMEM((bt, L), BF16), pltpu.VMEM((bt, L), BF16)],
        compiler_params=_cparams(2),
        name="long_conv_inv",
    )(at_mat, at_mat, yre, yim, x0, u, bias.reshape(bias.shape[0], 1, C))


def hyena(p, L, bsz, row0, period, widx, conv_w, conv_b, filt, bias):
    a_np = _dft_matrix(L)
    a_mat = jnp.asarray(a_np)
    at_mat = jnp.asarray(np.ascontiguousarray(a_np.T))
    x0, u = hyena_conv(p, conv_w, conv_b, widx, row0=row0, bsz=bsz, seq=L, period=period, bt=min(1024, L))
    hf, hb = hyena_filter(L, *filt, widx)
    kre, kim, kny = hyena_spectrum(a_mat, hf, hb, L)
    yre, yim = long_conv_fwd(a_mat, u, kre, kim, kny, L, bsz)
    return long_conv_inv(at_mat, yre, yim, x0, u, bias, widx, L, bsz)


def _pick_col(g, idx):
    lane = lax.broadcasted_iota(jnp.int32, g.shape, 1)
    return jnp.sum(jnp.where(lane == idx, g, 0.0), axis=1, keepdims=True)


def _chunk_masks(d, n):
    t_i = lax.broadcasted_iota(jnp.int32, (n, n), 0)
    s_i = lax.broadcasted_iota(jnp.int32, (n, n), 1)
    lag = (t_i - s_i) * jnp.where(d == 0, 1, -1)
    return lag >= 0, lag <= 0


def _running_sums(x_cols, x_rows, causal, causal_t):
    tri = jnp.where(causal, 1.0, 0.0)
    tri_t = jnp.where(causal_t, 1.0, 0.0)
    return _dot_hi(tri, x_cols), _dot_hi(x_rows, tri_t)


def _mlstm_kernel(q_ref, k_ref, v_ref, gc_ref, gr_ref, bc_ref, br_ref, o_ref, c_ref, n_ref, m_ref, gs_ref):
    d = pl.program_id(1)

    @pl.when(pl.program_id(2) == 0)
    def _():
        c_ref[...] = jnp.zeros_like(c_ref)
        n_ref[...] = jnp.zeros_like(n_ref)
        m_ref[...] = jnp.zeros_like(m_ref)

    gcol = gc_ref[...] + bc_ref[...]
    lane = lax.broadcasted_iota(jnp.int32, gcol.shape, 1)
    gcol = jnp.where((lane & ML_H) != 0, _log_sigmoid(gcol), gcol)
    grow = gr_ref[...] + br_ref[...]
    sub = lax.broadcasted_iota(jnp.int32, grow.shape, 0)
    grow = jnp.where((sub & ML_H) != 0, _log_sigmoid(grow), grow)

    causal, causal_t = _chunk_masks(d, CHUNK)
    ccol, crow = _running_sums(gcol, grow, causal, causal_t)
    gs_ref[0] = grow
    gs_ref[1] = crow

    def head(h):
        sl = pl.ds(h * ML_DH, ML_DH)
        return _mlstm_head(h, d, q_ref.at[:, sl], k_ref.at[:, sl], v_ref.at[:, sl], o_ref.at[:, sl], gcol, ccol,
                           gs_ref, causal, c_ref.at[h], n_ref.at[h], m_ref.at[h])

    _run_interleaved([head(h) for h in range(ML_H)])


def _run_interleaved(stages):
    live = list(stages)
    while live:
        still = []
        for gen in live:
            try:
                next(gen)
                still.append(gen)
            except StopIteration:
                pass
        live = still


def _mlstm_head(h, d, q_ref, k_ref, v_ref, o_ref, gcol, ccol, gs_ref, causal, c_ref, n_ref, m_ref):
    i_idx = 2 * ML_H * d + h
    f_idx = i_idx + ML_H
    ig_row = gs_ref[0, pl.ds(i_idx, 1), :]
    ig_col = _pick_col(gcol, i_idx)
    bcum_row = gs_ref[1, pl.ds(f_idx, 1), :]
    bcum_col = _pick_col(ccol, f_idx)
    g = jnp.sum(gs_ref[0, pl.ds(f_idx, 1), :], axis=1, keepdims=True)

    q = q_ref[...] * (ML_DH ** -0.5)
    k = k_ref[...]
    qb, kb, vb = q.astype(BF16), k.astype(BF16), v_ref[...].astype(BF16)
    c_in, n_in, m_in = c_ref[...], n_ref[...], m_ref[0:1, 0:1]

    dmat = jnp.where(causal, bcum_col - bcum_row + ig_row, -jnp.inf)
    m_inter = bcum_col + m_in
    m_t = jnp.maximum(jnp.max(dmat, axis=1, keepdims=True), m_inter)
    decay = jnp.exp(dmat - m_t)
    w_inter = jnp.exp(m_inter - m_t)
    a_row = g - bcum_row + ig_row
    a_col = g - bcum_col + ig_col
    m_loc = jnp.max(a_row, axis=1, keepdims=True)
    kw = k * jnp.exp(a_col - m_loc)
    qk = _dot_nt(qb, kb)
    qc = _dot(qb, c_in.astype(BF16))
    c_loc = _dot_tn(kw.astype(BF16), vb)
    yield

    s = qk * decay
    sv = _dot(s.astype(BF16), vb)
    den = jnp.sum(s, axis=1, keepdims=True) + w_inter * jnp.sum(q * n_in, axis=1, keepdims=True)
    yield

    num = sv + w_inter * qc
    o_ref[...] = num / jnp.maximum(jnp.abs(den), jnp.exp(-m_t))
    n_loc = jnp.sum(kw, axis=0, keepdims=True)
    m_new = jnp.maximum(g + m_in, m_loc)
    s_prev = jnp.exp(g + m_in - m_new)
    s_loc = jnp.exp(m_loc - m_new)
    c_ref[...] = s_prev * c_in + s_loc * c_loc
    n_ref[...] = s_prev * n_in + s_loc * n_loc
    m_ref[...] = jnp.broadcast_to(m_new, m_ref.shape)


def mlstm(qk, p, v_col_off, gates, gates_t, gate_b, rows):
    nct = (rows.L + rows.Lc) // CHUNK
    W = ML_H * ML_DH
    voff = v_col_off // W

    def rb(b, d, j):
        return rows.chunk_block(b, rows.scan_chunk(d, j))

    ng = 4 * ML_H
    bias = gate_b.reshape(ng)
    bias_row = jnp.pad(bias, (0, LANES - ng))[None]
    return pl.pallas_call(
        _mlstm_kernel,
        grid=(rows.B, 2, nct),
        in_specs=[pl.BlockSpec((CHUNK, W), lambda b, d, j: (rb(b, d, j), 0)),
                  pl.BlockSpec((CHUNK, W), lambda b, d, j: (rb(b, d, j), 1)),
                  pl.BlockSpec((CHUNK, W), lambda b, d, j: (rb(b, d, j), voff)),
                  pl.BlockSpec((CHUNK, LANES), lambda b, d, j: (rb(b, d, j), 0)),
                  pl.BlockSpec((ng, CHUNK), lambda b, d, j: (0, rb(b, d, j))),
                  pl.BlockSpec((1, LANES), lambda b, d, j: (0, 0)),
                  pl.BlockSpec((ng, 1), lambda b, d, j: (0, 0))],
        out_specs=pl.BlockSpec((None, CHUNK, W), lambda b, d, j: (d, rb(b, d, j), 0)),
        out_shape=jax.ShapeDtypeStruct((2, rows.n, W), F32),
        scratch_shapes=[pltpu.VMEM((ML_H, ML_DH, ML_DH), F32), pltpu.VMEM((ML_H, 1, ML_DH), F32),
                        pltpu.VMEM((ML_H, 8, LANES), F32), pltpu.VMEM((2, ng, CHUNK), F32)],
        compiler_params=_cparams(3),
        name="mlstm",
    )(qk, qk, p, gates, gates_t, bias_row, bias[:, None])


def _finish_ab_kernel(yl_ref, yc_ref, h_ref, o_ref, nw_ref, out_ref, *, n_lat_blocks):
    @pl.when(pl.program_id(0) < n_lat_blocks)
    def _():
        out_ref[:, 0:HY_D] = yl_ref[...]

    @pl.when(pl.program_id(0) >= n_lat_blocks)
    def _():
        out_ref[:, 0:HY_D] = yc_ref[...]

    hs = h_ref[0] + h_ref[1]
    og = _sigmoid(o_ref[...])
    nw = nw_ref[...]
    for i in range(ML_H):
        sl = slice(i * ML_DH, (i + 1) * ML_DH)
        x = hs[:, sl]
        mu = jnp.mean(x, axis=-1, keepdims=True)
        xc = x - mu
        var = jnp.mean(xc * xc, axis=-1, keepdims=True)
        y = xc * lax.rsqrt(var + EPS) * nw[:, sl] * og[:, sl]
        out_ref[:, HY_D + i * ML_DH:HY_D + (i + 1) * ML_DH] = y.astype(out_ref.dtype)


def finish_ab(yh_lat, yh_ctx, hdir, p, o_col_off, norm_w, widx, rows, bt=256):
    n_lat_blocks = rows.n_lat // bt
    per_b = rows.L // bt
    ooff = o_col_off // (ML_H * ML_DH)
    W = ML_H * ML_DH

    assert rows.Lc == bt
    n_blocks = rows.n // bt

    def lat_map(i):
        j = jnp.minimum(i, n_lat_blocks - 1)
        return (j % per_b, j // per_b)

    return pl.pallas_call(
        functools.partial(_finish_ab_kernel, n_lat_blocks=n_lat_blocks),
        grid=(n_blocks,),
        in_specs=[pl.BlockSpec((bt, HY_D), lat_map),
                  pl.BlockSpec((bt, HY_D), lambda i: (0, jnp.maximum(i - n_lat_blocks, 0))),
                  pl.BlockSpec((2, bt, W), lambda i: (0, i, 0)),
                  pl.BlockSpec((bt, W), lambda i: (i, ooff)),
                  pl.BlockSpec((None, 1, W), lambda i: (widx, 0, 0))],
        out_specs=pl.BlockSpec((bt, HY_D + W), lambda i: (i, 0)),
        out_shape=jax.ShapeDtypeStruct((rows.n, HY_D + W), BF16),
        compiler_params=_cparams(1),
        name="finish_ab",
    )(yh_lat, yh_ctx, hdir, p, norm_w.reshape(norm_w.shape[0], 1, W))


def _ssd_kernel(x_ref, b_ref, c_ref, dc_ref, dr_ref, pr_ref, pc_ref, o_ref, h_ref, ds_ref):
    d = pl.program_id(1)

    @pl.when(pl.program_id(2) == 0)
    def _():
        h_ref[...] = jnp.zeros_like(h_ref)

    dt_cols = _softplus(dc_ref[...] + pr_ref[1:2, :])
    la_cols = dt_cols * -jnp.exp(pr_ref[0:1, :])
    dt_rows = _softplus(dr_ref[...] + pc_ref[:, 1:2])
    la_rows = dt_rows * -jnp.exp(pc_ref[:, 0:1])

    causal, causal_t = _chunk_masks(d, CHUNK)
    acum_cols, acum_rows = _running_sums(la_cols, la_rows, causal, causal_t)
    ds_ref[0] = dt_rows
    ds_ref[1] = la_rows
    ds_ref[2] = acum_rows
    n_groups = b_ref.shape[1] // LANES
    heads_per_group = SSD_H // n_groups
    bmats = [b_ref[:, g * LANES:(g + 1) * LANES].astype(BF16) for g in range(n_groups)]
    cmats = [c_ref[:, g * LANES:(g + 1) * LANES].astype(BF16) for g in range(n_groups)]
    cbs = [_dot_nt(cmats[g], bmats[g]) for g in range(n_groups)]

    def head(hd):
        grp = hd // heads_per_group
        idx = SSD_H * d + hd
        hs = pl.ds(hd * SSD_P, SSD_P)
        dt_row = ds_ref[0, pl.ds(idx, 1), :]
        dt_col = _pick_col(dt_cols, idx)
        acum_col = _pick_col(acum_cols, idx)
        acum_row = ds_ref[2, pl.ds(idx, 1), :]
        tot = jnp.sum(ds_ref[1, pl.ds(idx, 1), :], axis=1, keepdims=True)
        decay = jnp.exp(jnp.where(causal, acum_col - acum_row, -jnp.inf))
        xh = x_ref[:, hs]
        h_in = h_ref[hd]
        xw = xh * (jnp.exp(tot - acum_col) * dt_col)
        y_off = _dot_nt(cmats[grp], h_in.astype(BF16))
        st = _dot_tn(xw.astype(BF16), bmats[grp])
        y_diag = _dot((cbs[grp] * decay * dt_row).astype(BF16), xh.astype(BF16))
        yield
        o_ref[:, hs] = y_diag + y_off * jnp.exp(acum_col)
        h_ref[hd] = jnp.exp(tot) * h_in + st

    _run_interleaved([head(hd) for hd in range(SSD_H)])


def ssd(xbc, dt, dt_t, a_log, dt_bias, rows):
    nct = (rows.L + rows.Lc) // CHUNK
    nd = SSD_H * SSD_P
    gw = 2 * LANES

    def rb(b, d, j):
        return rows.chunk_block(b, rows.scan_chunk(d, j))

    nh = 2 * SSD_H
    par = jnp.stack([a_log.reshape(nh), dt_bias.reshape(nh)])
    par_rows = jnp.pad(par, ((0, 0), (0, LANES - nh)))
    return pl.pallas_call(
        _ssd_kernel,
        grid=(rows.B, 2, nct),
        in_specs=[pl.BlockSpec((CHUNK, nd), lambda b, d, j: (rb(b, d, j), 0)),
                  pl.BlockSpec((CHUNK, gw), lambda b, d, j: (rb(b, d, j), nd // gw)),
                  pl.BlockSpec((CHUNK, gw), lambda b, d, j: (rb(b, d, j), nd // gw + 1)),
                  pl.BlockSpec((CHUNK, LANES), lambda b, d, j: (rb(b, d, j), 0)),
                  pl.BlockSpec((nh, CHUNK), lambda b, d, j: (0, rb(b, d, j))),
                  pl.BlockSpec((2, LANES), lambda b, d, j: (0, 0)),
                  pl.BlockSpec((nh, 2), lambda b, d, j: (0, 0))],
        out_specs=pl.BlockSpec((None, CHUNK, nd), lambda b, d, j: (d, rb(b, d, j), 0)),
        out_shape=jax.ShapeDtypeStruct((2, rows.n, nd), F32),
        scratch_shapes=[pltpu.VMEM((SSD_H, SSD_P, LANES), F32), pltpu.VMEM((3, nh, CHUNK), F32)],
        compiler_params=_cparams(3),
        name="ssd",
    )(xbc, xbc, xbc, dt, dt_t, par_rows, par.T)


S5_TC = 64
S5_JB = 2
S5_NS = S5_G * S5_P


def _s5_kernel(*refs, bsz):
    uf_refs, ub_refs = refs[:bsz], refs[bsz:2 * bsz]
    w_ref, c_ref, lre_ref, lim_ref, sf_ref, sb_ref = refs[2 * bsz:2 * bsz + 6]
    z_ref, s_ref, wb_ref, cb_ref, u_ref, y_ref = refs[2 * bsz + 6:]

    @pl.when(pl.program_id(0) == 0)
    def _():
        s_ref[...] = jnp.zeros_like(s_ref)
        wb_ref[...] = w_ref[...].astype(BF16)
        cb_ref[...] = c_ref[...].astype(BF16)

    r_i = lax.broadcasted_iota(jnp.int32, (S5_TC, S5_TC), 0)
    c_i = lax.broadcasted_iota(jnp.int32, (S5_TC, S5_TC), 1)
    flip = jnp.where(r_i + c_i == S5_TC - 1, 1.0, 0.0)
    group = 2 * bsz
    n_blk = S5_W // LANES
    sw = S5_NS // n_blk
    for b in range(bsz):
        uf = uf_refs[b][...]
        ub = _dot(flip.astype(BF16), ub_refs[b][...].astype(BF16))
        for j in range(n_blk):
            u_ref[j, pl.ds(b, S5_TC, stride=group), :] = uf[:, j * LANES:(j + 1) * LANES]
            u_ref[j, pl.ds(bsz + b, S5_TC, stride=group), :] = ub[:, j * LANES:(j + 1) * LANES]

    for j in range(n_blk):
        z_ref[:, 2 * j * sw:2 * (j + 1) * sw] = _dot(u_ref[j].astype(BF16), wb_ref[j])

    for j0 in range(0, n_blk, S5_JB):
        blocks = range(j0, j0 + S5_JB)
        re_sl = [slice(2 * j * sw, (2 * j + 1) * sw) for j in blocks]
        im_sl = [slice((2 * j + 1) * sw, (2 * j + 2) * sw) for j in blocks]
        ar = [lre_ref[:, j * sw:(j + 1) * sw] for j in blocks]
        ai = [lim_ref[:, j * sw:(j + 1) * sw] for j in blocks]

        def step(t, carry):
            r0 = pl.multiple_of(t * 8, 8)
            new = []
            for k in range(S5_JB):
                zr, zi = carry[2 * k], carry[2 * k + 1]
                nr = ar[k] * zr - ai[k] * zi + z_ref[pl.ds(r0, 8), re_sl[k]]
                ni = ar[k] * zi + ai[k] * zr + z_ref[pl.ds(r0, 8), im_sl[k]]
                z_ref[pl.ds(r0, 8), re_sl[k]] = nr
                z_ref[pl.ds(r0, 8), im_sl[k]] = ni
                new += [nr, ni]
            return tuple(new)

        init = []
        for k in range(S5_JB):
            init += [s_ref[:, re_sl[k]], s_ref[:, im_sl[k]]]
        fin = lax.fori_loop(0, S5_TC, step, tuple(init))
        for k in range(S5_JB):
            s_ref[:, re_sl[k]] = fin[2 * k]
            s_ref[:, im_sl[k]] = fin[2 * k + 1]

    row = lax.broadcasted_iota(jnp.int32, (S5_TC * 8, LANES), 0)
    is_bwd = (row & 4) != 0
    for j in range(n_blk):
        yy = _dot(z_ref[:, 2 * j * sw:2 * (j + 1) * sw].astype(BF16), cb_ref[j])
        y_ref[j] = jnp.where(is_bwd, yy[:, LANES:], yy[:, :LANES])
    for b in range(bsz):
        for j in range(n_blk):
            ls = slice(j * LANES, (j + 1) * LANES)
            sf_ref[b, :, ls] = y_ref[j, pl.ds(b, S5_TC, stride=group), :]
            sb_ref[b, :, ls] = _dot_hi(flip, y_ref[j, pl.ds(bsz + b, S5_TC, stride=group), :])


def s5_params(lam_re, lam_im, log_dt, b_re, b_im, c_re, c_im):
    dt = jnp.exp(log_dt)[..., None]
    mag = jnp.exp(lam_re * dt)
    lb_re, lb_im = mag * jnp.cos(lam_im * dt), mag * jnp.sin(lam_im * dt)
    den = lam_re * lam_re + lam_im * lam_im
    f_re = ((lb_re - 1) * lam_re + lb_im * lam_im) / den
    f_im = (lb_im * lam_re - (lb_re - 1) * lam_im) / den
    cf_re = c_re[None] * f_re[:, :, None, :] - c_im[None] * f_im[:, :, None, :]
    cf_im = c_re[None] * f_im[:, :, None, :] + c_im[None] * f_re[:, :, None, :]
    gpb = LANES // S5_GS
    n_blk = S5_G // gpb
    eye = jnp.eye(gpb, dtype=F32)

    def w_blocks(b):
        bb = b.reshape(n_blk, gpb, S5_P, S5_GS)
        return jnp.einsum('jgpi,gh->jgihp', bb, eye).reshape(n_blk, gpb * S5_GS, gpb * S5_P)

    def c_blocks(c):
        cc = c.reshape(2, n_blk, gpb, S5_GS, S5_P)
        return jnp.einsum('djgip,gh->jgpdhi', cc, eye).reshape(n_blk, gpb * S5_P, 2 * gpb * S5_GS)

    def lam_rows(l, bsz):
        return jnp.repeat(l.reshape(2, S5_NS), bsz, axis=0)

    w_cat = jnp.concatenate([w_blocks(b_re), w_blocks(b_im)], axis=2)
    c_cat = jnp.concatenate([c_blocks(cf_re), -c_blocks(cf_im)], axis=1)
    return w_cat, c_cat, lb_re, lb_im, lam_rows


def s5(p, w_cat, c_cat, lre, lim, rows):
    bsz, L, Lc = rows.B, rows.L, rows.Lc
    assert bsz == 4
    n_blk = S5_W // LANES
    sw = S5_NS // n_blk
    rt = S5_TC * 2 * bsz
    ncc, ncl = Lc // S5_TC, L // S5_TC
    lat0 = rows.n_lat // S5_TC

    def fwd_block(b):
        return lambda i: (jnp.where(i < ncc, lat0 + b * ncc + i, b * ncl + i - ncc), 0)

    def bwd_block(b):
        return lambda i: (jnp.where(i < ncc, lat0 + b * ncc + ncc - 1 - i, b * ncl + ncl - 1 + ncc - i), 0)

    full = lambda shape: pl.BlockSpec(shape, lambda i: (0,) * len(shape))
    u_specs = ([pl.BlockSpec((S5_TC, S5_W), fwd_block(b)) for b in range(bsz)]
               + [pl.BlockSpec((S5_TC, S5_W), bwd_block(b)) for b in range(bsz)])
    out_specs = [pl.BlockSpec((bsz, S5_TC, S5_W), lambda i: (0, jnp.maximum(i - ncc, 0), 0)),
                 pl.BlockSpec((bsz, S5_TC, S5_W), lambda i: (0, jnp.minimum(ncl - 1 + ncc - i, ncl - 1), 0))]
    return pl.pallas_call(
        functools.partial(_s5_kernel, bsz=bsz),
        grid=((L + Lc) // S5_TC,),
        in_specs=u_specs + [full((n_blk, LANES, 2 * sw)), full((n_blk, 2 * sw, 2 * LANES)),
                            full((2 * bsz, S5_NS)), full((2 * bsz, S5_NS))],
        out_specs=out_specs,
        out_shape=[jax.ShapeDtypeStruct((bsz, L, S5_W), F32)] * 2,
        scratch_shapes=[pltpu.VMEM((rt, 2 * S5_NS), F32), pltpu.VMEM((2 * bsz, 2 * S5_NS), F32),
                        pltpu.VMEM((n_blk, LANES, 2 * sw), BF16), pltpu.VMEM((n_blk, 2 * sw, 2 * LANES), BF16),
                        pltpu.VMEM((n_blk, rt, LANES), F32), pltpu.VMEM((n_blk, rt, LANES), F32)],
        compiler_params=_cparams(1),
        name="s5",
    )(*([p] * (2 * bsz)), w_cat, c_cat, lre, lim)


def _finish_cd_kernel(y_ref, xs_ref, z_ref, sf_ref, sb_ref, u_ref, dssd_ref, nw_ref, d5_ref, glu_ref, out_ref):
    y = (y_ref[0] + y_ref[1] + dssd_ref[...] * xs_ref[...]) * _silu(z_ref[...])
    y = y * lax.rsqrt(jnp.mean(y * y, axis=-1, keepdims=True) + EPS) * nw_ref[...]
    nd = y.shape[1]
    out_ref[:, 0:nd] = y.astype(out_ref.dtype)
    s = sf_ref[...] + sb_ref[...] + d5_ref[...] * u_ref[...]
    s = 0.5 * s * (1.0 + lax.erf(s * (2.0 ** -0.5)))
    gl = _dot(s.astype(BF16), glu_ref[...].astype(BF16))
    out_ref[:, nd:nd + S5_W] = (gl[:, 0:S5_W] * _sigmoid(gl[:, S5_W:])).astype(out_ref.dtype)


def finish_cd(ydir, xbc, p1, sf, sb, p2, ssd_d_lanes, norm_w, s5_d, glu_w, widx, n_rows, bt=256):
    nd = SSD_H * SSD_P
    vec = lambda n: pl.BlockSpec((None, 1, n), lambda i: (widx, 0, 0))
    return pl.pallas_call(
        _finish_cd_kernel,
        grid=(n_rows // bt,),
        in_specs=[pl.BlockSpec((2, bt, nd), lambda i: (0, i, 0)),
                  pl.BlockSpec((bt, nd), lambda i: (i, 0)),
                  pl.BlockSpec((bt, nd), lambda i: (i, 0)),
                  pl.BlockSpec((bt, S5_W), lambda i: (i, 0)),
                  pl.BlockSpec((bt, S5_W), lambda i: (i, 0)),
                  pl.BlockSpec((bt, S5_W), lambda i: (i, 0)),
                  vec(nd), vec(nd), vec(S5_W),
                  pl.BlockSpec((None, S5_W, 2 * S5_W), lambda i: (widx, 0, 0))],
        out_specs=pl.BlockSpec((bt, nd + S5_W), lambda i: (i, 0)),
        out_shape=jax.ShapeDtypeStruct((n_rows, nd + S5_W), BF16),
        compiler_params=_cparams(1),
        name="finish_cd",
    )(ydir, xbc, p1, sf, sb, p2, ssd_d_lanes, norm_w.reshape(norm_w.shape[0], 1, nd),
      s5_d.reshape(s5_d.shape[0], 1, S5_W), glu_w)


def _router_kernel(x_ref, sh_ref, sc_ref, w_ref, h_ref, o_ref):
    x = x_ref[...]
    h = x * lax.rsqrt(jnp.mean(x * x, axis=-1, keepdims=True) + EPS) * (1.0 + sc_ref[...]) + sh_ref[...]
    h_ref[...] = h
    logits = _dot_hi(h, w_ref[...])
    lane = lax.broadcasted_iota(jnp.int32, logits.shape, 1)
    logits = jnp.where(lane < N_EXPERTS, logits, -jnp.inf)
    m1 = jnp.max(logits, axis=1, keepdims=True)
    i1 = jnp.min(jnp.where(logits == m1, lane, LANES), axis=1, keepdims=True)
    rest = jnp.where(lane == i1, -jnp.inf, logits)
    m2 = jnp.max(rest, axis=1, keepdims=True)
    i2 = jnp.min(jnp.where(rest == m2, lane, LANES), axis=1, keepdims=True)
    e2 = jnp.exp(m2 - m1)
    p1 = 1.0 / (1.0 + e2)
    p2 = e2 / (1.0 + e2)
    o_ref[...] = jnp.where(lane == 0, i1.astype(F32),
                           jnp.where(lane == 1, i2.astype(F32),
                                     jnp.where(lane == 2, p1, jnp.where(lane == 3, p2, 0.0))))


def modulate_route(x, shift, scale, w_router_padded, rows, bt=512):
    T, D = x.shape
    mod_spec = pl.BlockSpec((None, 1, D), lambda i: (rows.mod_row(i, bt), 0, 0))
    return pl.pallas_call(
        _router_kernel,
        grid=(T // bt,),
        in_specs=[pl.BlockSpec((bt, D), lambda i: (i, 0)), mod_spec, mod_spec,
                  pl.BlockSpec((D, LANES), lambda i: (0, 0))],
        out_specs=[pl.BlockSpec((bt, D), lambda i: (i, 0)), pl.BlockSpec((bt, LANES), lambda i: (i, 0))],
        out_shape=[jax.ShapeDtypeStruct((T, D), F32), jax.ShapeDtypeStruct((T, LANES), F32)],
        compiler_params=_cparams(1),
        name="modulate_route",
    )(x, shift, scale, w_router_padded)


def _gather_kernel(idx_ref, nt_ref, src_ref, o_ref, buf_ref, sem, *, bm, s):
    i = pl.program_id(0)
    n_used = nt_ref[0]
    slot = i % 2

    def row_copy(tile, slot_, r):
        return pltpu.make_async_copy(src_ref.at[idx_ref[tile * bm + r]],
                                     buf_ref.at[slot_, pl.ds(pl.multiple_of(r * s, s), s), :], sem.at[slot_])

    def issue(tile, slot_):
        def body(r, c):
            row_copy(tile, slot_, r).start()
            return c
        lax.fori_loop(0, bm, body, 0, unroll=8)

    @pl.when(i == 0)
    def _():
        issue(0, 0)

    @pl.when(i + 1 < n_used)
    def _():
        issue(i + 1, 1 - slot)

    @pl.when(i < n_used)
    def _():
        def body(r, c):
            row_copy(i, slot, r).wait()
            return c
        lax.fori_loop(0, bm, body, 0, unroll=8)
        for j in range(s):
            o_ref[:, j * LANES:(j + 1) * LANES] = buf_ref[slot, pl.ds(j, bm, stride=s), :].astype(o_ref.dtype)

    @pl.when(i >= n_used)
    def _():
        o_ref[...] = jnp.zeros_like(o_ref)


def gather_rows(src, idx, n_used_tiles, bm=256):
    N = idx.shape[0]
    T, D = src.shape
    s = D // LANES
    return pl.pallas_call(
        functools.partial(_gather_kernel, bm=bm, s=s),
        grid_spec=pltpu.PrefetchScalarGridSpec(
            num_scalar_prefetch=2,
            grid=(N // bm,),
            in_specs=[pl.BlockSpec(memory_space=pl.ANY)],
            out_specs=pl.BlockSpec((bm, D), lambda i, idx, nt: (i, 0)),
            scratch_shapes=[pltpu.VMEM((2, bm * s, LANES), F32), pltpu.SemaphoreType.DMA((2,))]),
        out_shape=jax.ShapeDtypeStruct((N, D), BF16),
        compiler_params=_cparams(1),
        name="gather_rows",
    )(idx, n_used_tiles, src.reshape(T, s, LANES))


def _gmm_kernel(te_ref, na_ref, nx_ref, x_ref, w_ref, o_ref, wf_ref, wb_ref, sem, *, widx, bn, nf, parts):
    n = pl.program_id(0)
    m = pl.program_id(1)
    e = te_ref[m]
    first = (m == 0) | (e != te_ref[jnp.maximum(m - 1, 0)])

    def fetch(expert, nblk):
        return [pltpu.make_async_copy(
            w_ref.at[widx, expert, :, pl.ds(pl.multiple_of((nblk + p * nf) * bn, bn), bn)],
            wf_ref.at[p], sem.at[p]) for p in range(parts)]

    @pl.when(first)
    def _():
        @pl.when((n == 0) & (m == 0))
        def _():
            for cp in fetch(e, n):
                cp.start()

        for cp in fetch(e, n):
            cp.wait()
        wb_ref[...] = wf_ref[...].astype(BF16)
        nxt = nx_ref[m]

        @pl.when(nxt >= 0)
        def _():
            for cp in fetch(nxt, n):
                cp.start()

        @pl.when((nxt < 0) & (n + 1 < nf))
        def _():
            for cp in fetch(te_ref[0], n + 1):
                cp.start()

    @pl.when(m < na_ref[0])
    def _():
        x = x_ref[...]
        if parts == 2:
            o_ref[...] = (_silu(_dot(x, wb_ref[0])) * _dot(x, wb_ref[1])).astype(o_ref.dtype)
        else:
            o_ref[...] = _dot(x, wb_ref[0]).astype(o_ref.dtype)

    @pl.when(m >= na_ref[0])
    def _():
        o_ref[...] = jnp.zeros_like(o_ref)


def gmm(x, w, widx, plan, *, bm, bn, swiglu):
    M, K = x.shape
    parts = 2 if swiglu else 1
    N = w.shape[3] // parts
    nf = N // bn
    return pl.pallas_call(
        functools.partial(_gmm_kernel, widx=widx, bn=bn, nf=nf, parts=parts),
        grid_spec=pltpu.PrefetchScalarGridSpec(
            num_scalar_prefetch=3,
            grid=(nf, M // bm),
            in_specs=[pl.BlockSpec((bm, K), lambda n, m, te, na, nx: (m, 0)),
                      pl.BlockSpec(memory_space=pl.ANY)],
            out_specs=pl.BlockSpec((bm, bn), lambda n, m, te, na, nx: (m, n)),
            scratch_shapes=[pltpu.VMEM((parts, K, bn), F32), pltpu.VMEM((parts, K, bn), BF16),
                            pltpu.SemaphoreType.DMA((parts,))]),
        out_shape=jax.ShapeDtypeStruct((M, N), BF16 if swiglu else F32),
        compiler_params=_cparams(2),
        name="gmm_swiglu" if swiglu else "gmm",
    )(plan.tile_expert, plan.n_active, plan.next_expert, x, w)


def _combine_kernel(pos_ref, y_ref, x_ref, pr_ref, g_ref, nw_ref, o_ref, buf_ref, sem, *, bm, n_tok):
    i = pl.program_id(0)
    slot = i % 2

    def copies(tile, slot_, r):
        return [pltpu.make_async_copy(y_ref.at[pl.ds(pos_ref[c * n_tok + tile * bm + r], 1), :],
                                      buf_ref.at[slot_, c, pl.ds(r, 1), :], sem.at[slot_, c]) for c in range(2)]

    def issue(tile, slot_):
        def body(r, carry):
            for cp in copies(tile, slot_, r):
                cp.start()
            return carry
        lax.fori_loop(0, bm, body, 0, unroll=4)

    @pl.when(i == 0)
    def _():
        issue(0, 0)

    @pl.when(i + 1 < pl.num_programs(0))
    def _():
        issue(i + 1, 1 - slot)

    def drain(r, carry):
        for cp in copies(i, slot, r):
            cp.wait()
        return carry

    lax.fori_loop(0, bm, drain, 0, unroll=4)
    pr = pr_ref[...]
    mix = pr[:, 2:3] * buf_ref[slot, 0] + pr[:, 3:4] * buf_ref[slot, 1]
    x = x_ref[...] + g_ref[...] * mix
    o_ref[...] = x * lax.rsqrt(jnp.mean(x * x, axis=-1, keepdims=True) + EPS) * nw_ref[...]


def moe_combine(ys, pos, x, route, gate, norm_w, rows, bm=256):
    T, D = x.shape
    return pl.pallas_call(
        functools.partial(_combine_kernel, bm=bm, n_tok=T),
        grid_spec=pltpu.PrefetchScalarGridSpec(
            num_scalar_prefetch=1,
            grid=(T // bm,),
            in_specs=[pl.BlockSpec(memory_space=pl.ANY),
                      pl.BlockSpec((bm, D), lambda i, pos: (i, 0)),
                      pl.BlockSpec((bm, LANES), lambda i, pos: (i, 0)),
                      pl.BlockSpec((None, 1, D), lambda i, pos: (rows.mod_row(i, bm), 0, 0)),
                      pl.BlockSpec((1, D), lambda i, pos: (0, 0))],
            out_specs=pl.BlockSpec((bm, D), lambda i, pos: (i, 0)),
            scratch_shapes=[pltpu.VMEM((2, 2, bm, D), F32), pltpu.SemaphoreType.DMA((2, 2))]),
        out_shape=jax.ShapeDtypeStruct((T, D), F32),
        compiler_params=_cparams(1),
        name="moe_combine",
    )(pos, ys, x, route, gate, norm_w[None])


class MoePlan(NamedTuple):
    pos: jax.Array
    src: jax.Array
    tile_expert: jax.Array
    n_active: jax.Array
    next_expert: jax.Array


def moe_plan(route, n_tok, bm):
    e = jnp.concatenate([route[:, 0], route[:, 1]]).astype(jnp.int32)
    onehot = (e[:, None] == jnp.arange(N_EXPERTS, dtype=jnp.int32)[None, :]).astype(jnp.int32)
    rank = jnp.take_along_axis(jnp.cumsum(onehot, axis=0), e[:, None], axis=1)[:, 0] - 1
    counts = jnp.sum(onehot, axis=0)
    tiles = (counts + bm - 1) // bm
    tile_end = jnp.cumsum(tiles)
    start = (tile_end - tiles) * bm
    pos = start[e] + rank
    n_tiles = (2 * n_tok) // bm + N_EXPERTS
    tok = jnp.concatenate([jnp.arange(n_tok, dtype=jnp.int32)] * 2)
    src = jnp.zeros((n_tiles * bm,), jnp.int32).at[pos].set(tok)
    n_active = tile_end[-1]
    tile_ids = jnp.minimum(jnp.arange(n_tiles, dtype=jnp.int32), n_active - 1)
    tile_expert = jnp.sum((tile_ids[:, None] >= tile_end[None, :]).astype(jnp.int32), axis=1)
    ids = jnp.arange(N_EXPERTS, dtype=jnp.int32)
    later_used = (ids[None, :] > ids[:, None]) & (tiles[None, :] > 0)
    next_of = jnp.min(jnp.where(later_used, ids[None, :], N_EXPERTS), axis=1)
    next_of = jnp.where(next_of == N_EXPERTS, -1, next_of)
    return MoePlan(pos.astype(jnp.int32), src, tile_expert.astype(jnp.int32),
                   n_active.reshape(1).astype(jnp.int32), next_of[tile_expert].astype(jnp.int32))


def kernel(x, c, ctx, c_ctx, ada_w, ada_b, ab_w_in, hy_conv_w, hy_conv_b, hy_filt_w1, hy_filt_b1, hy_filt_w2, hy_filt_b2, hy_filt_w3, hy_filt_freq, hy_bias, ml_conv_w, ml_conv_b, ml_gate_b, ml_norm_w, ab_w_out, ffn_w_gu, ffn_w_down, cd_w_in, ssd_conv_w, ssd_conv_b, ssd_A_log, ssd_dt_bias, ssd_D, ssd_norm_w, s5_lam_re, s5_lam_im, s5_log_dt, s5_B_re, s5_B_im, s5_C_re, s5_C_im, s5_D, s5_glu_w, cd_w_out, moe_router, moe_w_gu, moe_w_down, final_norm_w):
    bsz, L, D = x.shape
    Lc = ctx.shape[1]
    rows = Rows(bsz, L, Lc)
    n_lat = rows.n_lat
    assert bsz == 4 and L % 1024 == 0 and Lc == 256 and D % 256 == 0
    bn_d = min(1024, D)
    bn_f = min(512, ffn_w_down.shape[1])

    x_lat, x_ctx = x.reshape(n_lat, D), ctx.reshape(bsz * Lc, D)
    cond = jnp.concatenate([c, c_ctx[None], jnp.zeros((8 - bsz - 1, D), F32)], axis=0)

    def mods(layer):
        m = adaln(cond, ada_w, ada_b, layer)
        return [m[:, k * D:(k + 1) * D].reshape(8, 1, D) for k in range(6)]

    md = mods(0)
    h = modulate(x_lat, md[0], md[1], rows, rows.n, BF16, x_tail=x_ctx)
    n_main = 3 * HY_D + 4 * ML_H * ML_DH
    p = mm(h, ab_w_in, 0, n_cols=n_main, bm=1024, bn=1024)
    w_gate = jnp.pad(ab_w_in[:, :, n_main:], ((0, 0), (0, 0), (0, LANES - 4 * ML_H)))
    gates = mm(h, w_gate, 0, n_cols=LANES, bm=1024, bn=LANES)
    gates_t = gates[:, :4 * ML_H].T

    filt = (hy_filt_w1, hy_filt_b1, hy_filt_w2, hy_filt_b2, hy_filt_w3, hy_filt_freq)
    yh_lat = hyena(p, L, bsz, 0, GRID_W, 0, hy_conv_w, hy_conv_b, filt, hy_bias)
    yh_ctx = hyena(p, Lc, bsz, n_lat, Lc, 0, hy_conv_w, hy_conv_b, filt, hy_bias)

    qk = conv_silu(p, 3 * HY_D, 2 * ML_H * ML_DH, ml_conv_w, ml_conv_b, 0, rows)
    hdir = mlstm(qk, p, 3 * HY_D + 2 * ML_H * ML_DH, gates, gates_t, ml_gate_b[0], rows)
    cat = finish_ab(yh_lat, yh_ctx, hdir, p, 3 * HY_D + 3 * ML_H * ML_DH, ml_norm_w, 0, rows)
    xs = mm_residual(cat, ab_w_out, 0, x_lat, md[2], rows, bm=1024, bn=bn_d, res_tail=x_ctx)

    h = modulate(xs, md[3], md[4], rows, rows.n, BF16)
    act = mm_swiglu(h, ffn_w_gu, 0, bm=1024, bn=bn_f)
    xs = mm_residual(act, ffn_w_down, 0, xs, md[5], rows, bm=512, bn=min(512, D))

    md = mods(1)
    h = modulate(xs, md[0], md[1], rows, rows.n, BF16)
    nd = SSD_H * SSD_P
    n_xbc = nd + 4 * LANES
    p1 = mm(h, cd_w_in, 0, n_cols=nd + n_xbc, bm=1024, bn=512)
    w_tail = cd_w_in[:, :, nd + n_xbc:]
    w_tail = jnp.concatenate([w_tail[:, :, 2 * SSD_H:], w_tail[:, :, :2 * SSD_H],
                              jnp.zeros((1, D, LANES - 2 * SSD_H), F32)], axis=2)
    p2 = mm(h, w_tail, 0, n_cols=S5_W + LANES, bm=1024, bn=S5_W + LANES)

    xbc = conv_silu(p1, nd, n_xbc, ssd_conv_w, ssd_conv_b, 0, rows)
    dt = p2[:, S5_W:]
    dt_t = dt[:, :2 * SSD_H].T
    ydir = ssd(xbc, dt, dt_t, ssd_A_log[0], ssd_dt_bias[0], rows)

    w_cat, c_cat, lb_re, lb_im, lam_rows = s5_params(
        s5_lam_re[0], s5_lam_im[0], s5_log_dt[0], s5_B_re[0], s5_B_im[0], s5_C_re[0], s5_C_im[0])
    sf, sb = s5(p2, w_cat, c_cat, lam_rows(lb_re, bsz), lam_rows(lb_im, bsz), rows)
    sf, sb = sf.reshape(n_lat, S5_W), sb.reshape(n_lat, S5_W)

    ssd_d_lanes = jnp.repeat(ssd_D, SSD_P, axis=1).reshape(ssd_D.shape[0], 1, nd)
    cat = finish_cd(ydir, xbc, p1, sf, sb, p2, ssd_d_lanes, ssd_norm_w, s5_D, s5_glu_w, 0, n_lat)
    xl = mm_residual(cat, cd_w_out, 0, xs, md[2], rows, bm=1024, bn=bn_d)

    h2, route = modulate_route(xl, md[3], md[4], jnp.pad(moe_router[0], ((0, 0), (0, LANES - N_EXPERTS))), rows)
    bm_e = 512
    plan = moe_plan(route, n_lat, bm_e)
    xg = gather_rows(h2, plan.src, plan.n_active * (bm_e // 256), bm=256)
    act = gmm(xg, moe_w_gu, 0, plan, bm=bm_e, bn=min(512, moe_w_down.shape[2]), swiglu=True)
    ys = gmm(act, moe_w_down, 0, plan, bm=bm_e, bn=min(512, D), swiglu=False)
    out = moe_combine(ys, plan.pos, xl, route, md[5], final_norm_w, rows, bm=256)
    return out.reshape(bsz, L, D)
```

```python
import functools
import math
from typing import NamedTuple

import jax
import jax.numpy as jnp
import numpy as np
from jax import lax
from jax.experimental import pallas as pl
from jax.experimental.pallas import tpu as pltpu

F32 = jnp.float32
BF16 = jnp.bfloat16

EPS = 1e-6
GRID_W = 64
CHUNK = 128
LANES = 128
HY_D = 1024
HY_EMB = 33
HY_FAST_DECAY = 0.3
HY_SLOW_DECAY = 1.5
HY_DECAY_TARGET = 1e-2
ML_H = 8
ML_DH = 128
SSD_H = 16
SSD_P = 64
S5_W = 512
S5_G = 32
S5_GS = 16
S5_P = 64
N_EXPERTS = 8
VMEM_LIMIT_BYTES = 56 * 1024 * 1024


def _cparams(n_axes):
    return pltpu.CompilerParams(dimension_semantics=("arbitrary",) * n_axes,
                                vmem_limit_bytes=VMEM_LIMIT_BYTES)


def _sigmoid(x):
    return 1.0 / (1.0 + jnp.exp(-x))


def _silu(x):
    return x * _sigmoid(x)


def _log_sigmoid(x):
    return jnp.minimum(x, 0.0) - jnp.log(1.0 + jnp.exp(-jnp.abs(x)))


def _softplus(x):
    return jnp.maximum(x, 0.0) + jnp.log(1.0 + jnp.exp(-jnp.abs(x)))


def _dot(a, b):
    return jnp.dot(a, b, preferred_element_type=F32)


def _dot_nt(a, b):
    return lax.dot_general(a, b, (((1,), (1,)), ((), ())), preferred_element_type=F32)


def _dot_tn(a, b):
    return lax.dot_general(a, b, (((0,), (0,)), ((), ())), preferred_element_type=F32)


def _dot_hi(a, b):
    return jnp.dot(a, b, preferred_element_type=F32, precision=lax.Precision.HIGHEST)


class Rows:
    def __init__(self, bsz, L, Lc):
        self.B, self.L, self.Lc = bsz, L, Lc
        self.n_lat = bsz * L
        self.n = bsz * (L + Lc)

    def mod_row(self, i, bm):
        n_lat_blocks = self.n_lat // bm
        return jnp.where(i < n_lat_blocks, (i * bm) // self.L, self.B)

    def chunk_block(self, b, c):
        ncc = self.Lc // CHUNK
        return jnp.where(c < ncc, self.n_lat // CHUNK + b * ncc + c, b * (self.L // CHUNK) + c - ncc)

    def scan_chunk(self, d, j):
        ncc = self.Lc // CHUNK
        nct = (self.L + self.Lc) // CHUNK
        back = jnp.where(j < ncc, ncc - 1 - j, nct - 1 + ncc - j)
        return jnp.where(d == 0, j, back)


def _adaln_kernel(c_ref, w_ref, b_ref, o_ref):
    cond = _silu(c_ref[...]).astype(BF16)
    o_ref[...] = _dot(cond, w_ref[...].astype(BF16)) + b_ref[...]


def adaln(cond, ada_w, ada_b, layer, bn=1024):
    bn = min(bn, ada_w.shape[1])
    _, D, N = ada_w.shape
    return pl.pallas_call(
        _adaln_kernel,
        grid=(N // bn,),
        in_specs=[pl.BlockSpec((8, D), lambda n: (0, 0)),
                  pl.BlockSpec((None, D, bn), lambda n: (layer, 0, n)),
                  pl.BlockSpec((None, 1, bn), lambda n: (layer, 0, n))],
        out_specs=pl.BlockSpec((8, bn), lambda n: (0, n)),
        out_shape=jax.ShapeDtypeStruct((8, N), F32),
        compiler_params=_cparams(1),
        name="adaln",
    )(cond, ada_w, ada_b.reshape(ada_b.shape[0], 1, N))


def _head_tail_specs(block, n_head_blocks, row_col):
    def head(*ids):
        r, c = row_col(*ids)
        return jnp.minimum(r, n_head_blocks - 1), c

    def tail(*ids):
        r, c = row_col(*ids)
        return jnp.maximum(r - n_head_blocks, 0), c

    return pl.BlockSpec(block, head), pl.BlockSpec(block, tail)


def _modulate_kernel(*refs, n_head_blocks):
    x_ref, sh_ref, sc_ref, o_ref = refs[0], refs[-3], refs[-2], refs[-1]
    x = x_ref[...]
    if len(refs) == 5:
        x = jnp.where(pl.program_id(0) < n_head_blocks, x, refs[1][...])
    y = x * lax.rsqrt(jnp.mean(x * x, axis=-1, keepdims=True) + EPS)
    o_ref[...] = (y * (1.0 + sc_ref[...]) + sh_ref[...]).astype(o_ref.dtype)


def modulate(x, shift, scale, rows, n_rows, out_dtype, bm=512, x_tail=None):
    D = x.shape[1]
    mod_spec = pl.BlockSpec((None, 1, D), lambda i: (rows.mod_row(i, bm), 0, 0))
    if x_tail is None:
        xs, x_specs = [x], [pl.BlockSpec((bm, D), lambda i: (i, 0))]
    else:
        xs, x_specs = [x, x_tail], list(_head_tail_specs((bm, D), x.shape[0] // bm, lambda i: (i, 0)))
    return pl.pallas_call(
        functools.partial(_modulate_kernel, n_head_blocks=x.shape[0] // bm),
        grid=(n_rows // bm,),
        in_specs=x_specs + [mod_spec, mod_spec],
        out_specs=pl.BlockSpec((bm, D), lambda i: (i, 0)),
        out_shape=jax.ShapeDtypeStruct((n_rows, D), out_dtype),
        compiler_params=_cparams(1),
        name="modulate",
    )(*xs, shift, scale)


def _mm_kernel(x_ref, w_ref, o_ref, wb_ref):
    @pl.when(pl.program_id(1) == 0)
    def _():
        wb_ref[...] = w_ref[...].astype(BF16)
    o_ref[...] = _dot(x_ref[...], wb_ref[...]).astype(o_ref.dtype)


def mm(x, w, widx, *, n_cols, col_off=0, bm, bn, out_dtype=F32):
    M, K = x.shape
    off = col_off // bn
    return pl.pallas_call(
        _mm_kernel,
        grid=(n_cols // bn, M // bm),
        in_specs=[pl.BlockSpec((bm, K), lambda n, m: (m, 0)),
                  pl.BlockSpec((None, K, bn), lambda n, m: (widx, 0, n + off))],
        out_specs=pl.BlockSpec((bm, bn), lambda n, m: (m, n)),
        out_shape=jax.ShapeDtypeStruct((M, n_cols), out_dtype),
        scratch_shapes=[pltpu.VMEM((K, bn), BF16)],
        compiler_params=_cparams(2),
        name="mm",
    )(x, w)


def _mm_res_kernel(*refs, n_head_blocks):
    x_ref, w_ref, r_ref = refs[:3]
    g_ref, o_ref, wb_ref = refs[-3:]

    @pl.when(pl.program_id(1) == 0)
    def _():
        wb_ref[...] = w_ref[...].astype(BF16)
    res = r_ref[...]
    if len(refs) == 7:
        res = jnp.where(pl.program_id(1) < n_head_blocks, res, refs[3][...])
    o_ref[...] = res + g_ref[...] * _dot(x_ref[...], wb_ref[...])


def mm_residual(x, w, widx, res, gate, rows, *, bm, bn, res_tail=None):
    M, K = x.shape
    N = w.shape[2]
    if res_tail is None:
        rs, r_specs = [res], [pl.BlockSpec((bm, bn), lambda n, m: (m, n))]
    else:
        rs, r_specs = [res, res_tail], list(_head_tail_specs((bm, bn), res.shape[0] // bm, lambda n, m: (m, n)))
    return pl.pallas_call(
        functools.partial(_mm_res_kernel, n_head_blocks=res.shape[0] // bm),
        grid=(N // bn, M // bm),
        in_specs=[pl.BlockSpec((bm, K), lambda n, m: (m, 0)),
                  pl.BlockSpec((None, K, bn), lambda n, m: (widx, 0, n))] + r_specs
                 + [pl.BlockSpec((None, 1, bn), lambda n, m: (rows.mod_row(m, bm), 0, n))],
        out_specs=pl.BlockSpec((bm, bn), lambda n, m: (m, n)),
        out_shape=jax.ShapeDtypeStruct((M, N), F32),
        scratch_shapes=[pltpu.VMEM((K, bn), BF16)],
        compiler_params=_cparams(2),
        name="mm_residual",
    )(x, w, *rs, gate)


def _mm_swiglu_kernel(x_ref, wg_ref, wu_ref, o_ref, wgb_ref, wub_ref):
    @pl.when(pl.program_id(1) == 0)
    def _():
        wgb_ref[...] = wg_ref[...].astype(BF16)
        wub_ref[...] = wu_ref[...].astype(BF16)
    x = x_ref[...]
    g = _dot(x, wgb_ref[...])
    u = _dot(x, wub_ref[...])
    o_ref[...] = (_silu(g) * u).astype(o_ref.dtype)


def mm_swiglu(x, w_gu, widx, *, bm, bn):
    M, K = x.shape
    F = w_gu.shape[2] // 2
    nf = F // bn
    return pl.pallas_call(
        _mm_swiglu_kernel,
        grid=(nf, M // bm),
        in_specs=[pl.BlockSpec((bm, K), lambda n, m: (m, 0)),
                  pl.BlockSpec((None, K, bn), lambda n, m: (widx, 0, n)),
                  pl.BlockSpec((None, K, bn), lambda n, m: (widx, 0, n + nf))],
        out_specs=pl.BlockSpec((bm, bn), lambda n, m: (m, n)),
        out_shape=jax.ShapeDtypeStruct((M, F), BF16),
        scratch_shapes=[pltpu.VMEM((K, bn), BF16), pltpu.VMEM((K, bn), BF16)],
        compiler_params=_cparams(2),
        name="mm_swiglu",
    )(x, w_gu, w_gu)


def _conv3(x, w_ref, b_ref, period):
    n = x.shape[0]
    pos = lax.broadcasted_iota(jnp.int32, x.shape, 0) & (period - 1)
    prev = jnp.where(pos == 0, 0.0, pltpu.roll(x, 1, 0))
    nxt = jnp.where(pos == period - 1, 0.0, pltpu.roll(x, n - 1, 0))
    w = w_ref[...]
    return b_ref[...] + prev * w[0:1] + x * w[1:2] + nxt * w[2:3]


def _conv_silu_kernel(x_ref, w_ref, b_ref, o_ref, *, n_lat_blocks, lat_period, ctx_period):
    period = jnp.where(pl.program_id(0) < n_lat_blocks, lat_period, ctx_period)
    o_ref[...] = _silu(_conv3(x_ref[...], w_ref, b_ref, period)).astype(o_ref.dtype)


def conv_silu(p, col_off, n_cols, w, b, widx, rows, *, bt=1024, bc=512):
    assert rows.n_lat % bt == 0 and bt % rows.Lc == 0 and bt % GRID_W == 0
    off = col_off // bc
    kern = functools.partial(_conv_silu_kernel, n_lat_blocks=rows.n_lat // bt,
                             lat_period=GRID_W, ctx_period=rows.Lc)
    return pl.pallas_call(
        kern,
        grid=(rows.n // bt, n_cols // bc),
        in_specs=[pl.BlockSpec((bt, bc), lambda i, j: (i, j + off)),
                  pl.BlockSpec((None, 3, bc), lambda i, j: (widx, 0, j)),
                  pl.BlockSpec((None, 1, bc), lambda i, j: (widx, 0, j))],
        out_specs=pl.BlockSpec((bt, bc), lambda i, j: (i, j)),
        out_shape=jax.ShapeDtypeStruct((rows.n, n_cols), F32),
        compiler_params=_cparams(2),
        name="conv_silu",
    )(p, w, b.reshape(b.shape[0], 1, b.shape[1]))


def _hyena_conv_kernel(p0_ref, p1_ref, p2_ref, w0_ref, w1_ref, w2_ref, b0_ref, b1_ref, b2_ref,
                       x0_ref, u_ref, *, period):
    x0_ref[...] = _conv3(p0_ref[...], w0_ref, b0_ref, period)
    x1 = _conv3(p1_ref[...], w1_ref, b1_ref, period)
    v = _conv3(p2_ref[...], w2_ref, b2_ref, period)
    u_ref[...] = (x1 * v).astype(u_ref.dtype)


def hyena_conv(p, w, b, widx, *, row0, bsz, seq, period, bt=256, bc=512):
    nt = seq // bt
    ncb = HY_D // bc
    rb0 = row0 // bt

    def pspec(k):
        return pl.BlockSpec((bt, bc), lambda bb, i, j: (rb0 + bb * nt + i, j + k * ncb))

    def wspec(k):
        return pl.BlockSpec((None, 3, bc), lambda bb, i, j: (widx, 0, j + k * ncb))

    def bspec(k):
        return pl.BlockSpec((None, 1, bc), lambda bb, i, j: (widx, 0, j + k * ncb))

    ospec = pl.BlockSpec((bt, bc), lambda bb, i, j: (i, bb * ncb + j))
    b3 = b.reshape(b.shape[0], 1, b.shape[1])
    return pl.pallas_call(
        functools.partial(_hyena_conv_kernel, period=period),
        grid=(bsz, nt, ncb),
        in_specs=[pspec(0), pspec(1), pspec(2), wspec(0), wspec(1), wspec(2), bspec(0), bspec(1), bspec(2)],
        out_specs=[ospec, ospec],
        out_shape=[jax.ShapeDtypeStruct((seq, bsz * HY_D), F32),
                   jax.ShapeDtypeStruct((seq, bsz * HY_D), BF16)],
        compiler_params=_cparams(3),
        name="hyena_conv",
    )(p, p, p, w, w, w, b3, b3, b3)


def _hyena_feats(L):
    pos = np.arange(L, dtype=np.float64)
    t = pos / max(L - 1, 1)
    n_bands = (HY_EMB - 1) // 2
    bands = np.linspace(1e-4, n_bands - 1, n_bands)
    ang = (2 * math.pi / L) * pos[:, None] * bands[None, :]
    feats = np.concatenate([t[:, None], np.cos(ang), -np.sin(ang)], axis=-1)
    feats = np.pad(feats, ((0, 0), (0, LANES - HY_EMB)))
    deltas = np.abs(np.linspace(math.log(HY_DECAY_TARGET) / HY_SLOW_DECAY,
                                math.log(HY_DECAY_TARGET) / HY_FAST_DECAY, HY_D))
    return feats.astype(np.float32), t.astype(np.float32)[:, None], deltas.astype(np.float32)[None, :]


def _hyena_filter_kernel(feats_ref, t_ref, dl_ref, w1_ref, b1_ref, w2_ref, b2_ref, w3f_ref, w3b_ref,
                         fq_ref, hf_ref, hb_ref):
    fq = fq_ref[...]
    h = jnp.sin(fq[0:1] * (_dot_hi(feats_ref[...], w1_ref[...]) + b1_ref[...]))
    h = jnp.sin(fq[1:2] * (_dot_hi(h, w2_ref[...]) + b2_ref[...]))
    win = jnp.exp(-t_ref[...] * dl_ref[...])
    h_f = _dot_hi(h, w3f_ref[...]) * win
    h_b = _dot_hi(h, w3b_ref[...]) * win
    row = lax.broadcasted_iota(jnp.int32, h_b.shape, 0)
    h_b = jnp.where(row == 0, 0.0, h_b)
    l1 = jnp.sum(jnp.abs(h_f), axis=0, keepdims=True) + jnp.sum(jnp.abs(h_b), axis=0, keepdims=True)
    hf_ref[...] = h_f / l1
    hb_ref[...] = h_b / l1


def hyena_filter(L, w1, b1, w2, b2, w3, freq, widx, bc=256):
    feats, t, deltas = _hyena_feats(L)
    hid = w2.shape[1]
    w1p = jnp.pad(w1[widx], ((0, LANES - HY_EMB), (0, 0)))
    ncb = HY_D // bc
    full = lambda shape: pl.BlockSpec(shape, lambda j: (0,) * len(shape))
    return pl.pallas_call(
        _hyena_filter_kernel,
        grid=(ncb,),
        in_specs=[full((L, LANES)), full((L, 1)), pl.BlockSpec((1, bc), lambda j: (0, j)),
                  full((LANES, hid)), full((1, hid)), full((hid, hid)), full((1, hid)),
                  pl.BlockSpec((hid, bc), lambda j: (0, j)),
                  pl.BlockSpec((hid, bc), lambda j: (0, j + ncb)),
                  full((2, hid))],
        out_specs=[pl.BlockSpec((L, bc), lambda j: (0, j)), pl.BlockSpec((L, bc), lambda j: (0, j))],
        out_shape=[jax.ShapeDtypeStruct((L, HY_D), F32), jax.ShapeDtypeStruct((L, HY_D), F32)],
        compiler_params=_cparams(1),
        name="hyena_filter",
    )(jnp.asarray(feats), jnp.asarray(t), jnp.asarray(deltas), w1p, b1[widx][None], w2[widx],
      b2[widx][None], w3[widx], w3[widx], freq[widx])


def _dft_matrix(L):
    f = np.arange(L, dtype=np.int64)[:, None]
    s = np.arange(L, dtype=np.int64)[None, :]
    ang = (math.pi / L) * ((f * s) % (2 * L)).astype(np.float64)
    a_cos = np.cos(ang)
    a_sin = -np.sin(ang)
    a_sin[0, :] = np.where(np.arange(L) % 2 == 0, 1.0, -1.0)
    return np.concatenate([a_cos, a_sin], axis=0).astype(np.float32)


def _spectrum_kernel(ac_ref, as_ref, hf_ref, hb_ref, kre_ref, kim_ref, kny_ref, *, L, bf):
    hf = hf_ref[...].astype(BF16)
    hb = hb_ref[...].astype(BF16)
    ac = ac_ref[...].astype(BF16)
    a_s = as_ref[...].astype(BF16)
    cf, cb = _dot(ac, hf), _dot(ac, hb)
    sf, sb = _dot(a_s, hf), _dot(a_s, hb)
    f = lax.broadcasted_iota(jnp.int32, cf.shape, 0) + pl.program_id(1) * bf
    wgt = jnp.where(f == 0, 0.5 / L, 1.0 / L)
    kre_ref[...] = (cf + cb) * wgt
    kim_ref[...] = jnp.where(f == 0, 0.0, (sf - sb) * wgt)
    kny_ref[...] = jnp.where(f == 0, (sf + sb) * wgt, (cf + cb) * wgt)


def hyena_spectrum(a_mat, hf, hb, L, bf=256, bc=512):
    bf = min(bf, L)
    nf = L // bf
    C = hf.shape[1]
    ospec = pl.BlockSpec((bf, bc), lambda c, i: (i, c))
    hspec = pl.BlockSpec((L, bc), lambda c, i: (0, c))
    return pl.pallas_call(
        functools.partial(_spectrum_kernel, L=L, bf=bf),
        grid=(C // bc, nf),
        in_specs=[pl.BlockSpec((bf, L), lambda c, i: (i, 0)), pl.BlockSpec((bf, L), lambda c, i: (i + nf, 0)),
                  hspec, hspec],
        out_specs=[ospec, ospec, ospec],
        out_shape=[jax.ShapeDtypeStruct((L, C), F32)] * 3,
        compiler_params=_cparams(2),
        name="hyena_spectrum",
    )(a_mat, a_mat, hf, hb)


def _lc_fwd_kernel(ac_ref, as_ref, u_ref, kre_ref, kim_ref, kny_ref, yre_ref, yim_ref, acb_ref, asb_ref):
    @pl.when(pl.program_id(1) == 0)
    def _():
        acb_ref[...] = ac_ref[...].astype(BF16)
        asb_ref[...] = as_ref[...].astype(BF16)
    u = u_ref[...]
    ure = _dot(acb_ref[...], u)
    uim = _dot(asb_ref[...], u)
    kim = kim_ref[...]
    yre_ref[...] = (kre_ref[...] * ure - kim * uim).astype(yre_ref.dtype)
    yim_ref[...] = (kny_ref[...] * uim + kim * ure).astype(yim_ref.dtype)


def long_conv_fwd(a_mat, u, kre, kim, kny, L, bsz, bf=512):
    bf = min(bf, L)
    nf = L // bf
    C = kre.shape[1]
    kspec = pl.BlockSpec((bf, C), lambda i, b: (i, 0))
    ospec = pl.BlockSpec((bf, C), lambda i, b: (i, b))
    return pl.pallas_call(
        _lc_fwd_kernel,
        grid=(nf, bsz),
        in_specs=[pl.BlockSpec((bf, L), lambda i, b: (i, 0)), pl.BlockSpec((bf, L), lambda i, b: (i + nf, 0)),
                  pl.BlockSpec((L, C), lambda i, b: (0, b)), kspec, kspec, kspec],
        out_specs=[ospec, ospec],
        out_shape=[jax.ShapeDtypeStruct((L, bsz * C), BF16)] * 2,
        scratch_shapes=[pltpu.VMEM((bf, L), BF16), pltpu.VMEM((bf, L), BF16)],
        compiler_params=_cparams(2),
        name="long_conv_fwd",
    )(a_mat, a_mat, u, kre, kim, kny)


def _lc_inv_kernel(atc_ref, ats_ref, yre_ref, yim_ref, x0_ref, u_ref, bias_ref, o_ref, atcb_ref, atsb_ref):
    @pl.when(pl.program_id(1) == 0)
    def _():
        atcb_ref[...] = atc_ref[...].astype(BF16)
        atsb_ref[...] = ats_ref[...].astype(BF16)
    y = _dot(atcb_ref[...], yre_ref[...]) + _dot(atsb_ref[...], yim_ref[...])
    u = u_ref[...].astype(F32)
    o_ref[...] = (x0_ref[...] * (y + bias_ref[...] * u)).astype(o_ref.dtype)


def long_conv_inv(at_mat, yre, yim, x0, u, bias, widx, L, bsz, bt=512):
    bt = min(bt, L)
    nt = L // bt
    C = bias.shape[1]
    tspec = pl.BlockSpec((bt, C), lambda i, b: (i, b))
    yspec = pl.BlockSpec((L, C), lambda i, b: (0, b))
    return pl.pallas_call(
        _lc_inv_kernel,
        grid=(nt, bsz),
        in_specs=[pl.BlockSpec((bt, L), lambda i, b: (i, 0)), pl.BlockSpec((bt, L), lambda i, b: (i, 1)),
                  yspec, yspec, tspec, tspec, pl.BlockSpec((None, 1, C), lambda i, b: (widx, 0, 0))],
        out_specs=tspec,
        out_shape=jax.ShapeDtypeStruct((L, bsz * C), BF16),
        scratch_shapes=[pltpu.VMEM((bt, L), BF16), pltpu.VMEM((bt, L), BF16)],
        compiler_params=_cparams(2),
        name="long_conv_inv",
    )(at_mat, at_mat, yre, yim, x0, u, bias.reshape(bias.shape[0], 1, C))


def hyena(p, L, bsz, row0, period, widx, conv_w, conv_b, filt, bias):
    a_np = _dft_matrix(L)
    a_mat = jnp.asarray(a_np)
    at_mat = jnp.asarray(np.ascontiguousarray(a_np.T))
    x0, u = hyena_conv(p, conv_w, conv_b, widx, row0=row0, bsz=bsz, seq=L, period=period, bt=min(1024, L))
    hf, hb = hyena_filter(L, *filt, widx)
    kre, kim, kny = hyena_spectrum(a_mat, hf, hb, L)
    yre, yim = long_conv_fwd(a_mat, u, kre, kim, kny, L, bsz)
    return long_conv_inv(at_mat, yre, yim, x0, u, bias, widx, L, bsz)


def _pick_col(g, idx):
    lane = lax.broadcasted_iota(jnp.int32, g.shape, 1)
    return jnp.sum(jnp.where(lane == idx, g, 0.0), axis=1, keepdims=True)


def _chunk_masks(d, n):
    t_i = lax.broadcasted_iota(jnp.int32, (n, n), 0)
    s_i = lax.broadcasted_iota(jnp.int32, (n, n), 1)
    lag = (t_i - s_i) * jnp.where(d == 0, 1, -1)
    return lag >= 0, lag <= 0


def _running_sums(x_cols, x_rows, causal, causal_t):
    tri = jnp.where(causal, 1.0, 0.0)
    tri_t = jnp.where(causal_t, 1.0, 0.0)
    return _dot_hi(tri, x_cols), _dot_hi(x_rows, tri_t)


def _mlstm_kernel(q_ref, k_ref, v_ref, gc_ref, gr_ref, bc_ref, br_ref, o_ref, c_ref, n_ref, m_ref, gs_ref):
    d = pl.program_id(1)

    @pl.when(pl.program_id(2) == 0)
    def _():
        c_ref[...] = jnp.zeros_like(c_ref)
        n_ref[...] = jnp.zeros_like(n_ref)
        m_ref[...] = jnp.zeros_like(m_ref)

    gcol = gc_ref[...] + bc_ref[...]
    lane = lax.broadcasted_iota(jnp.int32, gcol.shape, 1)
    gcol = jnp.where((lane & ML_H) != 0, _log_sigmoid(gcol), gcol)
    grow = gr_ref[...] + br_ref[...]
    sub = lax.broadcasted_iota(jnp.int32, grow.shape, 0)
    grow = jnp.where((sub & ML_H) != 0, _log_sigmoid(grow), grow)

    causal, causal_t = _chunk_masks(d, CHUNK)
    ccol, crow = _running_sums(gcol, grow, causal, causal_t)
    gs_ref[0] = grow
    gs_ref[1] = crow

    def head(h):
        sl = pl.ds(h * ML_DH, ML_DH)
        return _mlstm_head(h, d, q_ref.at[:, sl], k_ref.at[:, sl], v_ref.at[:, sl], o_ref.at[:, sl], gcol, ccol,
                           gs_ref, causal, c_ref.at[h], n_ref.at[h], m_ref.at[h])

    _run_interleaved([head(h) for h in range(ML_H)])


def _run_interleaved(stages):
    live = list(stages)
    while live:
        still = []
        for gen in live:
            try:
                next(gen)
                still.append(gen)
            except StopIteration:
                pass
        live = still


def _mlstm_head(h, d, q_ref, k_ref, v_ref, o_ref, gcol, ccol, gs_ref, causal, c_ref, n_ref, m_ref):
    i_idx = 2 * ML_H * d + h
    f_idx = i_idx + ML_H
    ig_row = gs_ref[0, pl.ds(i_idx, 1), :]
    ig_col = _pick_col(gcol, i_idx)
    bcum_row = gs_ref[1, pl.ds(f_idx, 1), :]
    bcum_col = _pick_col(ccol, f_idx)
    g = jnp.sum(gs_ref[0, pl.ds(f_idx, 1), :], axis=1, keepdims=True)

    q = q_ref[...] * (ML_DH ** -0.5)
    k = k_ref[...]
    qb, kb, vb = q.astype(BF16), k.astype(BF16), v_ref[...].astype(BF16)
    c_in, n_in, m_in = c_ref[...], n_ref[...], m_ref[0:1, 0:1]

    dmat = jnp.where(causal, bcum_col - bcum_row + ig_row, -jnp.inf)
    m_inter = bcum_col + m_in
    m_t = jnp.maximum(jnp.max(dmat, axis=1, keepdims=True), m_inter)
    decay = jnp.exp(dmat - m_t)
    w_inter = jnp.exp(m_inter - m_t)
    a_row = g - bcum_row + ig_row
    a_col = g - bcum_col + ig_col
    m_loc = jnp.max(a_row, axis=1, keepdims=True)
    kw = k * jnp.exp(a_col - m_loc)
    qk = _dot_nt(qb, kb)
    qc = _dot(qb, c_in.astype(BF16))
    c_loc = _dot_tn(kw.astype(BF16), vb)
    yield

    s = qk * decay
    sv = _dot(s.astype(BF16), vb)
    den = jnp.sum(s, axis=1, keepdims=True) + w_inter * jnp.sum(q * n_in, axis=1, keepdims=True)
    yield

    num = sv + w_inter * qc
    o_ref[...] = num / jnp.maximum(jnp.abs(den), jnp.exp(-m_t))
    n_loc = jnp.sum(kw, axis=0, keepdims=True)
    m_new = jnp.maximum(g + m_in, m_loc)
    s_prev = jnp.exp(g + m_in - m_new)
    s_loc = jnp.exp(m_loc - m_new)
    c_ref[...] = s_prev * c_in + s_loc * c_loc
    n_ref[...] = s_prev * n_in + s_loc * n_loc
    m_ref[...] = jnp.broadcast_to(m_new, m_ref.shape)


def mlstm(qk, p, v_col_off, gates, gates_t, gate_b, rows):
    nct = (rows.L + rows.Lc) // CHUNK
    W = ML_H * ML_DH
    voff = v_col_off // W

    def rb(b, d, j):
        return rows.chunk_block(b, rows.scan_chunk(d, j))

    ng = 4 * ML_H
    bias = gate_b.reshape(ng)
    bias_row = jnp.pad(bias, (0, LANES - ng))[None]
    return pl.pallas_call(
        _mlstm_kernel,
        grid=(rows.B, 2, nct),
        in_specs=[pl.BlockSpec((CHUNK, W), lambda b, d, j: (rb(b, d, j), 0)),
                  pl.BlockSpec((CHUNK, W), lambda b, d, j: (rb(b, d, j), 1)),
                  pl.BlockSpec((CHUNK, W), lambda b, d, j: (rb(b, d, j), voff)),
                  pl.BlockSpec((CHUNK, LANES), lambda b, d, j: (rb(b, d, j), 0)),
                  pl.BlockSpec((ng, CHUNK), lambda b, d, j: (0, rb(b, d, j))),
                  pl.BlockSpec((1, LANES), lambda b, d, j: (0, 0)),
                  pl.BlockSpec((ng, 1), lambda b, d, j: (0, 0))],
        out_specs=pl.BlockSpec((None, CHUNK, W), lambda b, d, j: (d, rb(b, d, j), 0)),
        out_shape=jax.ShapeDtypeStruct((2, rows.n, W), F32),
        scratch_shapes=[pltpu.VMEM((ML_H, ML_DH, ML_DH), F32), pltpu.VMEM((ML_H, 1, ML_DH), F32),
                        pltpu.VMEM((ML_H, 8, LANES), F32), pltpu.VMEM((2, ng, CHUNK), F32)],
        compiler_params=_cparams(3),
        name="mlstm",
    )(qk, qk, p, gates, gates_t, bias_row, bias[:, None])


def _finish_ab_kernel(yl_ref, yc_ref, h_ref, o_ref, nw_ref, out_ref, *, n_lat_blocks):
    @pl.when(pl.program_id(0) < n_lat_blocks)
    def _():
        out_ref[:, 0:HY_D] = yl_ref[...]

    @pl.when(pl.program_id(0) >= n_lat_blocks)
    def _():
        out_ref[:, 0:HY_D] = yc_ref[...]

    hs = h_ref[0] + h_ref[1]
    og = _sigmoid(o_ref[...])
    nw = nw_ref[...]
    for i in range(ML_H):
        sl = slice(i * ML_DH, (i + 1) * ML_DH)
        x = hs[:, sl]
        mu = jnp.mean(x, axis=-1, keepdims=True)
        xc = x - mu
        var = jnp.mean(xc * xc, axis=-1, keepdims=True)
        y = xc * lax.rsqrt(var + EPS) * nw[:, sl] * og[:, sl]
        out_ref[:, HY_D + i * ML_DH:HY_D + (i + 1) * ML_DH] = y.astype(out_ref.dtype)


def finish_ab(yh_lat, yh_ctx, hdir, p, o_col_off, norm_w, widx, rows, bt=256):
    n_lat_blocks = rows.n_lat // bt
    per_b = rows.L // bt
    ooff = o_col_off // (ML_H * ML_DH)
    W = ML_H * ML_DH

    assert rows.Lc == bt
    n_blocks = rows.n // bt

    def lat_map(i):
        j = jnp.minimum(i, n_lat_blocks - 1)
        return (j % per_b, j // per_b)

    return pl.pallas_call(
        functools.partial(_finish_ab_kernel, n_lat_blocks=n_lat_blocks),
        grid=(n_blocks,),
        in_specs=[pl.BlockSpec((bt, HY_D), lat_map),
                  pl.BlockSpec((bt, HY_D), lambda i: (0, jnp.maximum(i - n_lat_blocks, 0))),
                  pl.BlockSpec((2, bt, W), lambda i: (0, i, 0)),
                  pl.BlockSpec((bt, W), lambda i: (i, ooff)),
                  pl.BlockSpec((None, 1, W), lambda i: (widx, 0, 0))],
        out_specs=pl.BlockSpec((bt, HY_D + W), lambda i: (i, 0)),
        out_shape=jax.ShapeDtypeStruct((rows.n, HY_D + W), BF16),
        compiler_params=_cparams(1),
        name="finish_ab",
    )(yh_lat, yh_ctx, hdir, p, norm_w.reshape(norm_w.shape[0], 1, W))


def _ssd_kernel(x_ref, b_ref, c_ref, dc_ref, dr_ref, pr_ref, pc_ref, o_ref, h_ref, ds_ref):
    d = pl.program_id(1)

    @pl.when(pl.program_id(2) == 0)
    def _():
        h_ref[...] = jnp.zeros_like(h_ref)

    dt_cols = _softplus(dc_ref[...] + pr_ref[1:2, :])
    la_cols = dt_cols * -jnp.exp(pr_ref[0:1, :])
    dt_rows = _softplus(dr_ref[...] + pc_ref[:, 1:2])
    la_rows = dt_rows * -jnp.exp(pc_ref[:, 0:1])

    causal, causal_t = _chunk_masks(d, CHUNK)
    acum_cols, acum_rows = _running_sums(la_cols, la_rows, causal, causal_t)
    ds_ref[0] = dt_rows
    ds_ref[1] = la_rows
    ds_ref[2] = acum_rows
    n_groups = b_ref.shape[1] // LANES
    heads_per_group = SSD_H // n_groups
    bmats = [b_ref[:, g * LANES:(g + 1) * LANES].astype(BF16) for g in range(n_groups)]
    cmats = [c_ref[:, g * LANES:(g + 1) * LANES].astype(BF16) for g in range(n_groups)]
    cbs = [_dot_nt(cmats[g], bmats[g]) for g in range(n_groups)]

    def head(hd):
        grp = hd // heads_per_group
        idx = SSD_H * d + hd
        hs = pl.ds(hd * SSD_P, SSD_P)
        dt_row = ds_ref[0, pl.ds(idx, 1), :]
        dt_col = _pick_col(dt_cols, idx)
        acum_col = _pick_col(acum_cols, idx)
        acum_row = ds_ref[2, pl.ds(idx, 1), :]
        tot = jnp.sum(ds_ref[1, pl.ds(idx, 1), :], axis=1, keepdims=True)
        decay = jnp.exp(jnp.where(causal, acum_col - acum_row, -jnp.inf))
        xh = x_ref[:, hs]
        h_in = h_ref[hd]
        xw = xh * (jnp.exp(tot - acum_col) * dt_col)
        y_off = _dot_nt(cmats[grp], h_in.astype(BF16))
        st = _dot_tn(xw.astype(BF16), bmats[grp])
        y_diag = _dot((cbs[grp] * decay * dt_row).astype(BF16), xh.astype(BF16))
        yield
        o_ref[:, hs] = y_diag + y_off * jnp.exp(acum_col)
        h_ref[hd] = jnp.exp(tot) * h_in + st

    _run_interleaved([head(hd) for hd in range(SSD_H)])


def ssd(xbc, dt, dt_t, a_log, dt_bias, rows):
    nct = (rows.L + rows.Lc) // CHUNK
    nd = SSD_H * SSD_P
    gw = 2 * LANES

    def rb(b, d, j):
        return rows.chunk_block(b, rows.scan_chunk(d, j))

    nh = 2 * SSD_H
    par = jnp.stack([a_log.reshape(nh), dt_bias.reshape(nh)])
    par_rows = jnp.pad(par, ((0, 0), (0, LANES - nh)))
    return pl.pallas_call(
        _ssd_kernel,
        grid=(rows.B, 2, nct),
        in_specs=[pl.BlockSpec((CHUNK, nd), lambda b, d, j: (rb(b, d, j), 0)),
                  pl.BlockSpec((CHUNK, gw), lambda b, d, j: (rb(b, d, j), nd // gw)),
                  pl.BlockSpec((CHUNK, gw), lambda b, d, j: (rb(b, d, j), nd // gw + 1)),
                  pl.BlockSpec((CHUNK, LANES), lambda b, d, j: (rb(b, d, j), 0)),
                  pl.BlockSpec((nh, CHUNK), lambda b, d, j: (0, rb(b, d, j))),
                  pl.BlockSpec((2, LANES), lambda b, d, j: (0, 0)),
                  pl.BlockSpec((nh, 2), lambda b, d, j: (0, 0))],
        out_specs=pl.BlockSpec((None, CHUNK, nd), lambda b, d, j: (d, rb(b, d, j), 0)),
        out_shape=jax.ShapeDtypeStruct((2, rows.n, nd), F32),
        scratch_shapes=[pltpu.VMEM((SSD_H, SSD_P, LANES), F32), pltpu.VMEM((3, nh, CHUNK), F32)],
        compiler_params=_cparams(3),
        name="ssd",
    )(xbc, xbc, xbc, dt, dt_t, par_rows, par.T)


S5_TC = 64
S5_JB = 2
S5_NS = S5_G * S5_P


def _s5_kernel(*refs, bsz):
    uf_refs, ub_refs = refs[:bsz], refs[bsz:2 * bsz]
    w_ref, c_ref, lre_ref, lim_ref, sf_ref, sb_ref = refs[2 * bsz:2 * bsz + 6]
    z_ref, s_ref, wb_ref, cb_ref, u_ref, y_ref = refs[2 * bsz + 6:]

    @pl.when(pl.program_id(0) == 0)
    def _():
        s_ref[...] = jnp.zeros_like(s_ref)
        wb_ref[...] = w_ref[...].astype(BF16)
        cb_ref[...] = c_ref[...].astype(BF16)

    r_i = lax.broadcasted_iota(jnp.int32, (S5_TC, S5_TC), 0)
    c_i = lax.broadcasted_iota(jnp.int32, (S5_TC, S5_TC), 1)
    flip = jnp.where(r_i + c_i == S5_TC - 1, 1.0, 0.0)
    group = 2 * bsz
    n_blk = S5_W // LANES
    sw = S5_NS // n_blk
    for b in range(bsz):
        uf = uf_refs[b][...]
        ub = _dot(flip.astype(BF16), ub_refs[b][...].astype(BF16))
        for j in range(n_blk):
            u_ref[j, pl.ds(b, S5_TC, stride=group), :] = uf[:, j * LANES:(j + 1) * LANES]
            u_ref[j, pl.ds(bsz + b, S5_TC, stride=group), :] = ub[:, j * LANES:(j + 1) * LANES]

    for j in range(n_blk):
        z_ref[:, 2 * j * sw:2 * (j + 1) * sw] = _dot(u_ref[j].astype(BF16), wb_ref[j])

    for j0 in range(0, n_blk, S5_JB):
        blocks = range(j0, j0 + S5_JB)
        re_sl = [slice(2 * j * sw, (2 * j + 1) * sw) for j in blocks]
        im_sl = [slice((2 * j + 1) * sw, (2 * j + 2) * sw) for j in blocks]
        ar = [lre_ref[:, j * sw:(j + 1) * sw] for j in blocks]
        ai = [lim_ref[:, j * sw:(j + 1) * sw] for j in blocks]

        def step(t, carry):
            r0 = pl.multiple_of(t * 8, 8)
            new = []
            for k in range(S5_JB):
                zr, zi = carry[2 * k], carry[2 * k + 1]
                nr = ar[k] * zr - ai[k] * zi + z_ref[pl.ds(r0, 8), re_sl[k]]
                ni = ar[k] * zi + ai[k] * zr + z_ref[pl.ds(r0, 8), im_sl[k]]
                z_ref[pl.ds(r0, 8), re_sl[k]] = nr
                z_ref[pl.ds(r0, 8), im_sl[k]] = ni
                new += [nr, ni]
            return tuple(new)

        init = []
        for k in range(S5_JB):
            init += [s_ref[:, re_sl[k]], s_ref[:, im_sl[k]]]
        fin = lax.fori_loop(0, S5_TC, step, tuple(init))
        for k in range(S5_JB):
            s_ref[:, re_sl[k]] = fin[2 * k]
            s_ref[:, im_sl[k]] = fin[2 * k + 1]

    row = lax.broadcasted_iota(jnp.int32, (S5_TC * 8, LANES), 0)
    is_bwd = (row & 4) != 0
    for j in range(n_blk):
        yy = _dot(z_ref[:, 2 * j * sw:2 * (j + 1) * sw].astype(BF16), cb_ref[j])
        y_ref[j] = jnp.where(is_bwd, yy[:, LANES:], yy[:, :LANES])
    for b in range(bsz):
        for j in range(n_blk):
            ls = slice(j * LANES, (j + 1) * LANES)
            sf_ref[b, :, ls] = y_ref[j, pl.ds(b, S5_TC, stride=group), :]
            sb_ref[b, :, ls] = _dot_hi(flip, y_ref[j, pl.ds(bsz + b, S5_TC, stride=group), :])


def s5_params(lam_re, lam_im, log_dt, b_re, b_im, c_re, c_im):
    dt = jnp.exp(log_dt)[..., None]
    mag = jnp.exp(lam_re * dt)
    lb_re, lb_im = mag * jnp.cos(lam_im * dt), mag * jnp.sin(lam_im * dt)
    den = lam_re * lam_re + lam_im * lam_im
    f_re = ((lb_re - 1) * lam_re + lb_im * lam_im) / den
    f_im = (lb_im * lam_re - (lb_re - 1) * lam_im) / den
    cf_re = c_re[None] * f_re[:, :, None, :] - c_im[None] * f_im[:, :, None, :]
    cf_im = c_re[None] * f_im[:, :, None, :] + c_im[None] * f_re[:, :, None, :]
    gpb = LANES // S5_GS
    n_blk = S5_G // gpb
    eye = jnp.eye(gpb, dtype=F32)

    def w_blocks(b):
        bb = b.reshape(n_blk, gpb, S5_P, S5_GS)
        return jnp.einsum('jgpi,gh->jgihp', bb, eye).reshape(n_blk, gpb * S5_GS, gpb * S5_P)

    def c_blocks(c):
        cc = c.reshape(2, n_blk, gpb, S5_GS, S5_P)
        return jnp.einsum('djgip,gh->jgpdhi', cc, eye).reshape(n_blk, gpb * S5_P, 2 * gpb * S5_GS)

    def lam_rows(l, bsz):
        return jnp.repeat(l.reshape(2, S5_NS), bsz, axis=0)

    w_cat = jnp.concatenate([w_blocks(b_re), w_blocks(b_im)], axis=2)
    c_cat = jnp.concatenate([c_blocks(cf_re), -c_blocks(cf_im)], axis=1)
    return w_cat, c_cat, lb_re, lb_im, lam_rows


def s5(p, w_cat, c_cat, lre, lim, rows):
    bsz, L, Lc = rows.B, rows.L, rows.Lc
    assert bsz == 4
    n_blk = S5_W // LANES
    sw = S5_NS // n_blk
    rt = S5_TC * 2 * bsz
    ncc, ncl = Lc // S5_TC, L // S5_TC
    lat0 = rows.n_lat // S5_TC

    def fwd_block(b):
        return lambda i: (jnp.where(i < ncc, lat0 + b * ncc + i, b * ncl + i - ncc), 0)

    def bwd_block(b):
        return lambda i: (jnp.where(i < ncc, lat0 + b * ncc + ncc - 1 - i, b * ncl + ncl - 1 + ncc - i), 0)

    full = lambda shape: pl.BlockSpec(shape, lambda i: (0,) * len(shape))
    u_specs = ([pl.BlockSpec((S5_TC, S5_W), fwd_block(b)) for b in range(bsz)]
               + [pl.BlockSpec((S5_TC, S5_W), bwd_block(b)) for b in range(bsz)])
    out_specs = [pl.BlockSpec((bsz, S5_TC, S5_W), lambda i: (0, jnp.maximum(i - ncc, 0), 0)),
                 pl.BlockSpec((bsz, S5_TC, S5_W), lambda i: (0, jnp.minimum(ncl - 1 + ncc - i, ncl - 1), 0))]
    return pl.pallas_call(
        functools.partial(_s5_kernel, bsz=bsz),
        grid=((L + Lc) // S5_TC,),
        in_specs=u_specs + [full((n_blk, LANES, 2 * sw)), full((n_blk, 2 * sw, 2 * LANES)),
                            full((2 * bsz, S5_NS)), full((2 * bsz, S5_NS))],
        out_specs=out_specs,
        out_shape=[jax.ShapeDtypeStruct((bsz, L, S5_W), F32)] * 2,
        scratch_shapes=[pltpu.VMEM((rt, 2 * S5_NS), F32), pltpu.VMEM((2 * bsz, 2 * S5_NS), F32),
                        pltpu.VMEM((n_blk, LANES, 2 * sw), BF16), pltpu.VMEM((n_blk, 2 * sw, 2 * LANES), BF16),
                        pltpu.VMEM((n_blk, rt, LANES), F32), pltpu.VMEM((n_blk, rt, LANES), F32)],
        compiler_params=_cparams(1),
        name="s5",
    )(*([p] * (2 * bsz)), w_cat, c_cat, lre, lim)


def _finish_cd_kernel(y_ref, xs_ref, z_ref, sf_ref, sb_ref, u_ref, dssd_ref, nw_ref, d5_ref, glu_ref, out_ref):
    y = (y_ref[0] + y_ref[1] + dssd_ref[...] * xs_ref[...]) * _silu(z_ref[...])
    y = y * lax.rsqrt(jnp.mean(y * y, axis=-1, keepdims=True) + EPS) * nw_ref[...]
    nd = y.shape[1]
    out_ref[:, 0:nd] = y.astype(out_ref.dtype)
    s = sf_ref[...] + sb_ref[...] + d5_ref[...] * u_ref[...]
    s = 0.5 * s * (1.0 + lax.erf(s * (2.0 ** -0.5)))
    gl = _dot(s.astype(BF16), glu_ref[...].astype(BF16))
    out_ref[:, nd:nd + S5_W] = (gl[:, 0:S5_W] * _sigmoid(gl[:, S5_W:])).astype(out_ref.dtype)


def finish_cd(ydir, xbc, p1, sf, sb, p2, ssd_d_lanes, norm_w, s5_d, glu_w, widx, n_rows, bt=256):
    nd = SSD_H * SSD_P
    vec = lambda n: pl.BlockSpec((None, 1, n), lambda i: (widx, 0, 0))
    return pl.pallas_call(
        _finish_cd_kernel,
        grid=(n_rows // bt,),
        in_specs=[pl.BlockSpec((2, bt, nd), lambda i: (0, i, 0)),
                  pl.BlockSpec((bt, nd), lambda i: (i, 0)),
                  pl.BlockSpec((bt, nd), lambda i: (i, 0)),
                  pl.BlockSpec((bt, S5_W), lambda i: (i, 0)),
                  pl.BlockSpec((bt, S5_W), lambda i: (i, 0)),
                  pl.BlockSpec((bt, S5_W), lambda i: (i, 0)),
                  vec(nd), vec(nd), vec(S5_W),
                  pl.BlockSpec((None, S5_W, 2 * S5_W), lambda i: (widx, 0, 0))],
        out_specs=pl.BlockSpec((bt, nd + S5_W), lambda i: (i, 0)),
        out_shape=jax.ShapeDtypeStruct((n_rows, nd + S5_W), BF16),
        compiler_params=_cparams(1),
        name="finish_cd",
    )(ydir, xbc, p1, sf, sb, p2, ssd_d_lanes, norm_w.reshape(norm_w.shape[0], 1, nd),
      s5_d.reshape(s5_d.shape[0], 1, S5_W), glu_w)


def _router_kernel(x_ref, sh_ref, sc_ref, w_ref, h_ref, o_ref):
    x = x_ref[...]
    h = x * lax.rsqrt(jnp.mean(x * x, axis=-1, keepdims=True) + EPS) * (1.0 + sc_ref[...]) + sh_ref[...]
    h_ref[...] = h
    logits = _dot_hi(h, w_ref[...])
    lane = lax.broadcasted_iota(jnp.int32, logits.shape, 1)
    logits = jnp.where(lane < N_EXPERTS, logits, -jnp.inf)
    m1 = jnp.max(logits, axis=1, keepdims=True)
    i1 = jnp.min(jnp.where(logits == m1, lane, LANES), axis=1, keepdims=True)
    rest = jnp.where(lane == i1, -jnp.inf, logits)
    m2 = jnp.max(rest, axis=1, keepdims=True)
    i2 = jnp.min(jnp.where(rest == m2, lane, LANES), axis=1, keepdims=True)
    e2 = jnp.exp(m2 - m1)
    p1 = 1.0 / (1.0 + e2)
    p2 = e2 / (1.0 + e2)
    o_ref[...] = jnp.where(lane == 0, i1.astype(F32),
                           jnp.where(lane == 1, i2.astype(F32),
                                     jnp.where(lane == 2, p1, jnp.where(lane == 3, p2, 0.0))))


def modulate_route(x, shift, scale, w_router_padded, rows, bt=512):
    T, D = x.shape
    mod_spec = pl.BlockSpec((None, 1, D), lambda i: (rows.mod_row(i, bt), 0, 0))
    return pl.pallas_call(
        _router_kernel,
        grid=(T // bt,),
        in_specs=[pl.BlockSpec((bt, D), lambda i: (i, 0)), mod_spec, mod_spec,
                  pl.BlockSpec((D, LANES), lambda i: (0, 0))],
        out_specs=[pl.BlockSpec((bt, D), lambda i: (i, 0)), pl.BlockSpec((bt, LANES), lambda i: (i, 0))],
        out_shape=[jax.ShapeDtypeStruct((T, D), F32), jax.ShapeDtypeStruct((T, LANES), F32)],
        compiler_params=_cparams(1),
        name="modulate_route",
    )(x, shift, scale, w_router_padded)


def _gather_kernel(idx_ref, nt_ref, src_ref, o_ref, buf_ref, sem, *, bm, s):
    i = pl.program_id(0)
    n_used = nt_ref[0]
    slot = i % 2

    def row_copy(tile, slot_, r):
        return pltpu.make_async_copy(src_ref.at[idx_ref[tile * bm + r]],
                                     buf_ref.at[slot_, pl.ds(pl.multiple_of(r * s, s), s), :], sem.at[slot_])

    def issue(tile, slot_):
        def body(r, c):
            row_copy(tile, slot_, r).start()
            return c
        lax.fori_loop(0, bm, body, 0, unroll=8)

    @pl.when(i == 0)
    def _():
        issue(0, 0)

    @pl.when(i + 1 < n_used)
    def _():
        issue(i + 1, 1 - slot)

    @pl.when(i < n_used)
    def _():
        def body(r, c):
            row_copy(i, slot, r).wait()
            return c
        lax.fori_loop(0, bm, body, 0, unroll=8)
        for j in range(s):
            o_ref[:, j * LANES:(j + 1) * LANES] = buf_ref[slot, pl.ds(j, bm, stride=s), :].astype(o_ref.dtype)

    @pl.when(i >= n_used)
    def _():
        o_ref[...] = jnp.zeros_like(o_ref)


def gather_rows(src, idx, n_used_tiles, bm=256):
    N = idx.shape[0]
    T, D = src.shape
    s = D // LANES
    return pl.pallas_call(
        functools.partial(_gather_kernel, bm=bm, s=s),
        grid_spec=pltpu.PrefetchScalarGridSpec(
            num_scalar_prefetch=2,
            grid=(N // bm,),
            in_specs=[pl.BlockSpec(memory_space=pl.ANY)],
            out_specs=pl.BlockSpec((bm, D), lambda i, idx, nt: (i, 0)),
            scratch_shapes=[pltpu.VMEM((2, bm * s, LANES), F32), pltpu.SemaphoreType.DMA((2,))]),
        out_shape=jax.ShapeDtypeStruct((N, D), BF16),
        compiler_params=_cparams(1),
        name="gather_rows",
    )(idx, n_used_tiles, src.reshape(T, s, LANES))


def _gmm_kernel(hdr_ref, grp_ref, last_ref, rows_ref, x_ref, w_ref, o_ref, wf_ref, wb_ref, sem,
                *, widx, bn, nf, parts, bm):
    n = pl.program_id(0)
    m = pl.program_id(1)
    n_used, n_grp = hdr_ref[0], hdr_ref[1]
    total = nf * n_grp
    grp = grp_ref[m]
    order = n * n_grp + grp
    half = bm // 2

    def group_after(nq, gq):
        wrap = gq + 1 >= n_grp
        return jnp.where(wrap, nq + 1, nq), jnp.where(wrap, 0, gq + 1)

    def fetch(nq, gq):
        expert = hdr_ref[2 + gq]
        return [pltpu.make_async_copy(
            w_ref.at[widx, expert, :, pl.ds(pl.multiple_of((nq + p * nf) * bn, bn), bn)],
            wf_ref.at[p], sem.at[p]) for p in range(parts)]

    def rows_times_weights(r0, nr):
        x = x_ref[r0:r0 + nr, :]
        if parts == 2:
            o_ref[r0:r0 + nr, :] = (_silu(_dot(x, wb_ref[0])) * _dot(x, wb_ref[1])).astype(o_ref.dtype)
        else:
            o_ref[r0:r0 + nr, :] = _dot(x, wb_ref[0]).astype(o_ref.dtype)

    @pl.when((n == 0) & (m == 0))
    def _():
        for cp in fetch(0, 0):
            cp.start()
        for cp in fetch(0, 0):
            cp.wait()
        wb_ref[...] = wf_ref[...].astype(BF16)

        @pl.when(total > 1)
        def _():
            for cp in fetch(*group_after(0, 0)):
                cp.start()

    is_last = last_ref[m] == 1
    n1, g1 = group_after(n, grp)
    n2, g2 = group_after(n1, g1)

    @pl.when((m < n_used) & jnp.logical_not(is_last))
    def _():
        rows_times_weights(0, bm)

    @pl.when(is_last)
    def _():
        @pl.when(order + 1 < total)
        def _():
            for cp in fetch(n1, g1):
                cp.wait()

        rows_times_weights(0, half)

        @pl.when(rows_ref[m] > half)
        def _():
            rows_times_weights(half, half)

        @pl.when(rows_ref[m] <= half)
        def _():
            o_ref[half:, :] = jnp.zeros((half, bn), o_ref.dtype)

        @pl.when(order + 1 < total)
        def _():
            wb_ref[...] = wf_ref[...].astype(BF16)

        @pl.when(order + 2 < total)
        def _():
            for cp in fetch(n2, g2):
                cp.start()

    @pl.when(m >= n_used)
    def _():
        o_ref[...] = jnp.zeros_like(o_ref)


def gmm(x, w, widx, plan, *, bm, bn, swiglu):
    M, K = x.shape
    parts = 2 if swiglu else 1
    N = w.shape[3] // parts
    nf = N // bn
    return pl.pallas_call(
        functools.partial(_gmm_kernel, widx=widx, bn=bn, nf=nf, parts=parts, bm=bm),
        grid_spec=pltpu.PrefetchScalarGridSpec(
            num_scalar_prefetch=4,
            grid=(nf, M // bm),
            in_specs=[pl.BlockSpec((bm, K), lambda n, m, *_: (m, 0)),
                      pl.BlockSpec(memory_space=pl.ANY)],
            out_specs=pl.BlockSpec((bm, bn), lambda n, m, *_: (m, n)),
            scratch_shapes=[pltpu.VMEM((parts, K, bn), F32), pltpu.VMEM((parts, K, bn), BF16),
                            pltpu.SemaphoreType.DMA((parts,))]),
        out_shape=jax.ShapeDtypeStruct((M, N), BF16 if swiglu else F32),
        compiler_params=_cparams(2),
        name="gmm_swiglu" if swiglu else "gmm",
    )(plan.header, plan.tile_group, plan.tile_last, plan.tile_rows, x, w)


def _combine_kernel(pos_ref, y_ref, x_ref, pr_ref, g_ref, nw_ref, o_ref, buf_ref, sem, *, bm, n_tok):
    i = pl.program_id(0)
    slot = i % 2

    def copies(tile, slot_, r):
        return [pltpu.make_async_copy(y_ref.at[pl.ds(pos_ref[c * n_tok + tile * bm + r], 1), :],
                                      buf_ref.at[slot_, c, pl.ds(r, 1), :], sem.at[slot_, c]) for c in range(2)]

    def issue(tile, slot_):
        def body(r, carry):
            for cp in copies(tile, slot_, r):
                cp.start()
            return carry
        lax.fori_loop(0, bm, body, 0, unroll=4)

    @pl.when(i == 0)
    def _():
        issue(0, 0)

    @pl.when(i + 1 < pl.num_programs(0))
    def _():
        issue(i + 1, 1 - slot)

    def drain(r, carry):
        for cp in copies(i, slot, r):
            cp.wait()
        return carry

    lax.fori_loop(0, bm, drain, 0, unroll=4)
    pr = pr_ref[...]
    mix = pr[:, 2:3] * buf_ref[slot, 0] + pr[:, 3:4] * buf_ref[slot, 1]
    x = x_ref[...] + g_ref[...] * mix
    o_ref[...] = x * lax.rsqrt(jnp.mean(x * x, axis=-1, keepdims=True) + EPS) * nw_ref[...]


def moe_combine(ys, pos, x, route, gate, norm_w, rows, bm=256):
    T, D = x.shape
    return pl.pallas_call(
        functools.partial(_combine_kernel, bm=bm, n_tok=T),
        grid_spec=pltpu.PrefetchScalarGridSpec(
            num_scalar_prefetch=1,
            grid=(T // bm,),
            in_specs=[pl.BlockSpec(memory_space=pl.ANY),
                      pl.BlockSpec((bm, D), lambda i, pos: (i, 0)),
                      pl.BlockSpec((bm, LANES), lambda i, pos: (i, 0)),
                      pl.BlockSpec((None, 1, D), lambda i, pos: (rows.mod_row(i, bm), 0, 0)),
                      pl.BlockSpec((1, D), lambda i, pos: (0, 0))],
            out_specs=pl.BlockSpec((bm, D), lambda i, pos: (i, 0)),
            scratch_shapes=[pltpu.VMEM((2, 2, bm, D), F32), pltpu.SemaphoreType.DMA((2, 2))]),
        out_shape=jax.ShapeDtypeStruct((T, D), F32),
        compiler_params=_cparams(1),
        name="moe_combine",
    )(pos, ys, x, route, gate, norm_w[None])


class MoePlan(NamedTuple):
    pos: jax.Array
    src: jax.Array
    n_active: jax.Array
    header: jax.Array
    tile_group: jax.Array
    tile_last: jax.Array
    tile_rows: jax.Array


def moe_plan(route, n_tok, bm):
    e = jnp.concatenate([route[:, 0], route[:, 1]]).astype(jnp.int32)
    onehot = (e[:, None] == jnp.arange(N_EXPERTS, dtype=jnp.int32)[None, :]).astype(jnp.int32)
    rank = jnp.take_along_axis(jnp.cumsum(onehot, axis=0), e[:, None], axis=1)[:, 0] - 1
    counts = jnp.sum(onehot, axis=0)
    tiles = (counts + bm - 1) // bm
    tile_end = jnp.cumsum(tiles)
    start = (tile_end - tiles) * bm
    pos = start[e] + rank
    n_tiles = (2 * n_tok) // bm + N_EXPERTS
    tok = jnp.concatenate([jnp.arange(n_tok, dtype=jnp.int32)] * 2)
    src = jnp.zeros((n_tiles * bm,), jnp.int32).at[pos].set(tok)
    n_active = tile_end[-1]
    all_tiles = jnp.arange(n_tiles, dtype=jnp.int32)
    used = all_tiles < n_active
    tile_ids = jnp.minimum(all_tiles, n_active - 1)
    tile_expert = jnp.sum((tile_ids[:, None] >= tile_end[None, :]).astype(jnp.int32), axis=1)
    ids = jnp.arange(N_EXPERTS, dtype=jnp.int32)
    in_use = tiles > 0
    group_of = jnp.cumsum(in_use.astype(jnp.int32)) - 1
    n_groups = jnp.sum(in_use.astype(jnp.int32))
    experts_in_use = jnp.sum(jnp.where(in_use[None, :] & (group_of[None, :] == ids[:, None]), ids[None, :], 0), axis=1)
    tile_last = used & (all_tiles + 1 == tile_end[tile_expert])
    tile_rows = jnp.clip(counts[tile_expert] - (all_tiles - (tile_end - tiles)[tile_expert]) * bm, 0, bm)
    header = jnp.concatenate([n_active.reshape(1), n_groups.reshape(1), experts_in_use])
    return MoePlan(pos.astype(jnp.int32), src, n_active.reshape(1).astype(jnp.int32), header.astype(jnp.int32),
                   group_of[tile_expert].astype(jnp.int32), tile_last.astype(jnp.int32),
                   jnp.where(used, tile_rows, 0).astype(jnp.int32))


def kernel(x, c, ctx, c_ctx, ada_w, ada_b, ab_w_in, hy_conv_w, hy_conv_b, hy_filt_w1, hy_filt_b1, hy_filt_w2, hy_filt_b2, hy_filt_w3, hy_filt_freq, hy_bias, ml_conv_w, ml_conv_b, ml_gate_b, ml_norm_w, ab_w_out, ffn_w_gu, ffn_w_down, cd_w_in, ssd_conv_w, ssd_conv_b, ssd_A_log, ssd_dt_bias, ssd_D, ssd_norm_w, s5_lam_re, s5_lam_im, s5_log_dt, s5_B_re, s5_B_im, s5_C_re, s5_C_im, s5_D, s5_glu_w, cd_w_out, moe_router, moe_w_gu, moe_w_down, final_norm_w):
    bsz, L, D = x.shape
    Lc = ctx.shape[1]
    rows = Rows(bsz, L, Lc)
    n_lat = rows.n_lat
    assert bsz == 4 and L % 1024 == 0 and Lc == 256 and D % 256 == 0
    bn_d = min(1024, D)
    bn_f = min(512, ffn_w_down.shape[1])

    x_lat, x_ctx = x.reshape(n_lat, D), ctx.reshape(bsz * Lc, D)
    cond = jnp.concatenate([c, c_ctx[None], jnp.zeros((8 - bsz - 1, D), F32)], axis=0)

    def mods(layer):
        m = adaln(cond, ada_w, ada_b, layer)
        return [m[:, k * D:(k + 1) * D].reshape(8, 1, D) for k in range(6)]

    md = mods(0)
    h = modulate(x_lat, md[0], md[1], rows, rows.n, BF16, x_tail=x_ctx)
    n_main = 3 * HY_D + 4 * ML_H * ML_DH
    p = mm(h, ab_w_in, 0, n_cols=n_main, bm=1024, bn=1024)
    w_gate = jnp.pad(ab_w_in[:, :, n_main:], ((0, 0), (0, 0), (0, LANES - 4 * ML_H)))
    gates = mm(h, w_gate, 0, n_cols=LANES, bm=1024, bn=LANES)
    gates_t = gates[:, :4 * ML_H].T

    filt = (hy_filt_w1, hy_filt_b1, hy_filt_w2, hy_filt_b2, hy_filt_w3, hy_filt_freq)
    yh_lat = hyena(p, L, bsz, 0, GRID_W, 0, hy_conv_w, hy_conv_b, filt, hy_bias)
    yh_ctx = hyena(p, Lc, bsz, n_lat, Lc, 0, hy_conv_w, hy_conv_b, filt, hy_bias)

    qk = conv_silu(p, 3 * HY_D, 2 * ML_H * ML_DH, ml_conv_w, ml_conv_b, 0, rows)
    hdir = mlstm(qk, p, 3 * HY_D + 2 * ML_H * ML_DH, gates, gates_t, ml_gate_b[0], rows)
    cat = finish_ab(yh_lat, yh_ctx, hdir, p, 3 * HY_D + 3 * ML_H * ML_DH, ml_norm_w, 0, rows)
    xs = mm_residual(cat, ab_w_out, 0, x_lat, md[2], rows, bm=1024, bn=bn_d, res_tail=x_ctx)

    h = modulate(xs, md[3], md[4], rows, rows.n, BF16)
    act = mm_swiglu(h, ffn_w_gu, 0, bm=1024, bn=bn_f)
    xs = mm_residual(act, ffn_w_down, 0, xs, md[5], rows, bm=512, bn=min(512, D))

    md = mods(1)
    h = modulate(xs, md[0], md[1], rows, rows.n, BF16)
    nd = SSD_H * SSD_P
    n_xbc = nd + 4 * LANES
    p1 = mm(h, cd_w_in, 0, n_cols=nd + n_xbc, bm=1024, bn=512)
    w_tail = cd_w_in[:, :, nd + n_xbc:]
    w_tail = jnp.concatenate([w_tail[:, :, 2 * SSD_H:], w_tail[:, :, :2 * SSD_H],
                              jnp.zeros((1, D, LANES - 2 * SSD_H), F32)], axis=2)
    p2 = mm(h, w_tail, 0, n_cols=S5_W + LANES, bm=1024, bn=S5_W + LANES)

    xbc = conv_silu(p1, nd, n_xbc, ssd_conv_w, ssd_conv_b, 0, rows)
    dt = p2[:, S5_W:]
    dt_t = dt[:, :2 * SSD_H].T
    ydir = ssd(xbc, dt, dt_t, ssd_A_log[0], ssd_dt_bias[0], rows)

    w_cat, c_cat, lb_re, lb_im, lam_rows = s5_params(
        s5_lam_re[0], s5_lam_im[0], s5_log_dt[0], s5_B_re[0], s5_B_im[0], s5_C_re[0], s5_C_im[0])
    sf, sb = s5(p2, w_cat, c_cat, lam_rows(lb_re, bsz), lam_rows(lb_im, bsz), rows)
    sf, sb = sf.reshape(n_lat, S5_W), sb.reshape(n_lat, S5_W)

    ssd_d_lanes = jnp.repeat(ssd_D, SSD_P, axis=1).reshape(ssd_D.shape[0], 1, nd)
    cat = finish_cd(ydir, xbc, p1, sf, sb, p2, ssd_d_lanes, ssd_norm_w, s5_D, s5_glu_w, 0, n_lat)
    xl = mm_residual(cat, cd_w_out, 0, xs, md[2], rows, bm=1024, bn=bn_d)

    h2, route = modulate_route(xl, md[3], md[4], jnp.pad(moe_router[0], ((0, 0), (0, LANES - N_EXPERTS))), rows)
    bm_e = 512
    plan = moe_plan(route, n_lat, bm_e)
    xg = gather_rows(h2, plan.src, plan.n_active * (bm_e // 256), bm=256)
    act = gmm(xg, moe_w_gu, 0, plan, bm=bm_e, bn=min(512, moe_w_down.shape[2]), swiglu=True)
    ys = gmm(act, moe_w_down, 0, plan, bm=bm_e, bn=min(512, D), swiglu=False)
    out = moe_combine(ys, plan.pos, xl, route, md[5], final_norm_w, rows, bm=256)
    return out.reshape(bsz, L, D)
```

```python
import functools
import math
from typing import NamedTuple

import jax
import jax.numpy as jnp
import numpy as np
from jax import lax
from jax.experimental import pallas as pl
from jax.experimental.pallas import tpu as pltpu

F32 = jnp.float32
BF16 = jnp.bfloat16

EPS = 1e-6
GRID_W = 64
CHUNK = 128
LANES = 128
HY_D = 1024
HY_EMB = 33
HY_FAST_DECAY = 0.3
HY_SLOW_DECAY = 1.5
HY_DECAY_TARGET = 1e-2
ML_H = 8
ML_DH = 128
SSD_H = 16
SSD_P = 64
S5_W = 512
S5_G = 32
S5_GS = 16
S5_P = 64
N_EXPERTS = 8
VMEM_LIMIT_BYTES = 56 * 1024 * 1024


def _cparams(n_axes):
    return pltpu.CompilerParams(dimension_semantics=("arbitrary",) * n_axes,
                                vmem_limit_bytes=VMEM_LIMIT_BYTES)


def _sigmoid(x):
    return 1.0 / (1.0 + jnp.exp(-x))


def _silu(x):
    return x * _sigmoid(x)


def _log_sigmoid(x):
    return jnp.minimum(x, 0.0) - jnp.log(1.0 + jnp.exp(-jnp.abs(x)))


def _softplus(x):
    return jnp.maximum(x, 0.0) + jnp.log(1.0 + jnp.exp(-jnp.abs(x)))


def _dot(a, b):
    return jnp.dot(a, b, preferred_element_type=F32)


def _dot_nt(a, b):
    return lax.dot_general(a, b, (((1,), (1,)), ((), ())), preferred_element_type=F32)


def _dot_tn(a, b):
    return lax.dot_general(a, b, (((0,), (0,)), ((), ())), preferred_element_type=F32)


def _dot_hi(a, b):
    return jnp.dot(a, b, preferred_element_type=F32, precision=lax.Precision.HIGHEST)


class Rows:
    def __init__(self, bsz, L, Lc):
        self.B, self.L, self.Lc = bsz, L, Lc
        self.n_lat = bsz * L
        self.n = bsz * (L + Lc)

    def mod_row(self, i, bm):
        n_lat_blocks = self.n_lat // bm
        return jnp.where(i < n_lat_blocks, (i * bm) // self.L, self.B)

    def chunk_block(self, b, c):
        ncc = self.Lc // CHUNK
        return jnp.where(c < ncc, self.n_lat // CHUNK + b * ncc + c, b * (self.L // CHUNK) + c - ncc)

    def scan_chunk(self, d, j):
        ncc = self.Lc // CHUNK
        nct = (self.L + self.Lc) // CHUNK
        back = jnp.where(j < ncc, ncc - 1 - j, nct - 1 + ncc - j)
        return jnp.where(d == 0, j, back)


def _adaln_kernel(c_ref, w_ref, b_ref, o_ref):
    cond = _silu(c_ref[...]).astype(BF16)
    o_ref[...] = _dot(cond, w_ref[...].astype(BF16)) + b_ref[...]


def adaln(cond, ada_w, ada_b, layer, bn=1024):
    bn = min(bn, ada_w.shape[1])
    _, D, N = ada_w.shape
    return pl.pallas_call(
        _adaln_kernel,
        grid=(N // bn,),
        in_specs=[pl.BlockSpec((8, D), lambda n: (0, 0)),
                  pl.BlockSpec((None, D, bn), lambda n: (layer, 0, n)),
                  pl.BlockSpec((None, 1, bn), lambda n: (layer, 0, n))],
        out_specs=pl.BlockSpec((8, bn), lambda n: (0, n)),
        out_shape=jax.ShapeDtypeStruct((8, N), F32),
        compiler_params=_cparams(1),
        name="adaln",
    )(cond, ada_w, ada_b.reshape(ada_b.shape[0], 1, N))


def _head_tail_specs(block, n_head_blocks, row_col):
    def head(*ids):
        r, c = row_col(*ids)
        return jnp.minimum(r, n_head_blocks - 1), c

    def tail(*ids):
        r, c = row_col(*ids)
        return jnp.maximum(r - n_head_blocks, 0), c

    return pl.BlockSpec(block, head), pl.BlockSpec(block, tail)


def _modulate_kernel(*refs, n_head_blocks):
    x_ref, sh_ref, sc_ref, o_ref = refs[0], refs[-3], refs[-2], refs[-1]
    x = x_ref[...]
    if len(refs) == 5:
        x = jnp.where(pl.program_id(0) < n_head_blocks, x, refs[1][...])
    y = x * lax.rsqrt(jnp.mean(x * x, axis=-1, keepdims=True) + EPS)
    o_ref[...] = (y * (1.0 + sc_ref[...]) + sh_ref[...]).astype(o_ref.dtype)


def modulate(x, shift, scale, rows, n_rows, out_dtype, bm=512, x_tail=None):
    D = x.shape[1]
    mod_spec = pl.BlockSpec((None, 1, D), lambda i: (rows.mod_row(i, bm), 0, 0))
    if x_tail is None:
        xs, x_specs = [x], [pl.BlockSpec((bm, D), lambda i: (i, 0))]
    else:
        xs, x_specs = [x, x_tail], list(_head_tail_specs((bm, D), x.shape[0] // bm, lambda i: (i, 0)))
    return pl.pallas_call(
        functools.partial(_modulate_kernel, n_head_blocks=x.shape[0] // bm),
        grid=(n_rows // bm,),
        in_specs=x_specs + [mod_spec, mod_spec],
        out_specs=pl.BlockSpec((bm, D), lambda i: (i, 0)),
        out_shape=jax.ShapeDtypeStruct((n_rows, D), out_dtype),
        compiler_params=_cparams(1),
        name="modulate",
    )(*xs, shift, scale)


def _mm_kernel(x_ref, w_ref, o_ref, wb_ref):
    @pl.when(pl.program_id(1) == 0)
    def _():
        wb_ref[...] = w_ref[...].astype(BF16)
    o_ref[...] = _dot(x_ref[...], wb_ref[...]).astype(o_ref.dtype)


def mm(x, w, widx, *, n_cols, col_off=0, bm, bn, out_dtype=F32):
    M, K = x.shape
    off = col_off // bn
    return pl.pallas_call(
        _mm_kernel,
        grid=(n_cols // bn, M // bm),
        in_specs=[pl.BlockSpec((bm, K), lambda n, m: (m, 0)),
                  pl.BlockSpec((None, K, bn), lambda n, m: (widx, 0, n + off))],
        out_specs=pl.BlockSpec((bm, bn), lambda n, m: (m, n)),
        out_shape=jax.ShapeDtypeStruct((M, n_cols), out_dtype),
        scratch_shapes=[pltpu.VMEM((K, bn), BF16)],
        compiler_params=_cparams(2),
        name="mm",
    )(x, w)


def _mm_t_kernel(x_ref, wt_ref, o_ref, wb_ref):
    @pl.when(pl.program_id(1) == 0)
    def _():
        wb_ref[...] = wt_ref[...].astype(BF16)
    o_ref[...] = _dot_nt(x_ref[...], wb_ref[...]).astype(o_ref.dtype)


def mm_t(x, wt, widx, *, n_cols, bm, bn, out_dtype=F32):
    M, K = x.shape
    return pl.pallas_call(
        _mm_t_kernel,
        grid=(n_cols // bn, M // bm),
        in_specs=[pl.BlockSpec((bm, K), lambda n, m: (m, 0)),
                  pl.BlockSpec((None, bn, K), lambda n, m: (widx, n, 0))],
        out_specs=pl.BlockSpec((bm, bn), lambda n, m: (m, n)),
        out_shape=jax.ShapeDtypeStruct((M, n_cols), out_dtype),
        scratch_shapes=[pltpu.VMEM((bn, K), BF16)],
        compiler_params=_cparams(2),
        name="mm_t",
    )(x, wt)


def _mm_res_kernel(*refs, n_head_blocks):
    x_ref, w_ref, r_ref = refs[:3]
    g_ref, o_ref, wb_ref = refs[-3:]

    @pl.when(pl.program_id(1) == 0)
    def _():
        wb_ref[...] = w_ref[...].astype(BF16)
    res = r_ref[...]
    if len(refs) == 7:
        res = jnp.where(pl.program_id(1) < n_head_blocks, res, refs[3][...])
    o_ref[...] = res + g_ref[...] * _dot(x_ref[...], wb_ref[...])


def mm_residual(x, w, widx, res, gate, rows, *, bm, bn, res_tail=None):
    M, K = x.shape
    N = w.shape[2]
    if res_tail is None:
        rs, r_specs = [res], [pl.BlockSpec((bm, bn), lambda n, m: (m, n))]
    else:
        rs, r_specs = [res, res_tail], list(_head_tail_specs((bm, bn), res.shape[0] // bm, lambda n, m: (m, n)))
    return pl.pallas_call(
        functools.partial(_mm_res_kernel, n_head_blocks=res.shape[0] // bm),
        grid=(N // bn, M // bm),
        in_specs=[pl.BlockSpec((bm, K), lambda n, m: (m, 0)),
                  pl.BlockSpec((None, K, bn), lambda n, m: (widx, 0, n))] + r_specs
                 + [pl.BlockSpec((None, 1, bn), lambda n, m: (rows.mod_row(m, bm), 0, n))],
        out_specs=pl.BlockSpec((bm, bn), lambda n, m: (m, n)),
        out_shape=jax.ShapeDtypeStruct((M, N), F32),
        scratch_shapes=[pltpu.VMEM((K, bn), BF16)],
        compiler_params=_cparams(2),
        name="mm_residual",
    )(x, w, *rs, gate)


def _mm_swiglu_kernel(x_ref, wg_ref, wu_ref, o_ref, wgb_ref, wub_ref):
    @pl.when(pl.program_id(1) == 0)
    def _():
        wgb_ref[...] = wg_ref[...].astype(BF16)
        wub_ref[...] = wu_ref[...].astype(BF16)
    x = x_ref[...]
    g = _dot(x, wgb_ref[...])
    u = _dot(x, wub_ref[...])
    o_ref[...] = (_silu(g) * u).astype(o_ref.dtype)


def mm_swiglu(x, w_gu, widx, *, bm, bn):
    M, K = x.shape
    F = w_gu.shape[2] // 2
    nf = F // bn
    return pl.pallas_call(
        _mm_swiglu_kernel,
        grid=(nf, M // bm),
        in_specs=[pl.BlockSpec((bm, K), lambda n, m: (m, 0)),
                  pl.BlockSpec((None, K, bn), lambda n, m: (widx, 0, n)),
                  pl.BlockSpec((None, K, bn), lambda n, m: (widx, 0, n + nf))],
        out_specs=pl.BlockSpec((bm, bn), lambda n, m: (m, n)),
        out_shape=jax.ShapeDtypeStruct((M, F), BF16),
        scratch_shapes=[pltpu.VMEM((K, bn), BF16), pltpu.VMEM((K, bn), BF16)],
        compiler_params=_cparams(2),
        name="mm_swiglu",
    )(x, w_gu, w_gu)


def _conv3(x, w_ref, b_ref, period):
    n = x.shape[0]
    pos = lax.broadcasted_iota(jnp.int32, x.shape, 0) & (period - 1)
    prev = jnp.where(pos == 0, 0.0, pltpu.roll(x, 1, 0))
    nxt = jnp.where(pos == period - 1, 0.0, pltpu.roll(x, n - 1, 0))
    w = w_ref[...]
    return b_ref[...] + prev * w[0:1] + x * w[1:2] + nxt * w[2:3]


def _conv_silu_kernel(x_ref, w_ref, b_ref, o_ref, *, n_lat_blocks, lat_period, ctx_period):
    period = jnp.where(pl.program_id(0) < n_lat_blocks, lat_period, ctx_period)
    o_ref[...] = _silu(_conv3(x_ref[...], w_ref, b_ref, period)).astype(o_ref.dtype)


def conv_silu(p, col_off, n_cols, w, b, widx, rows, *, bt=1024, bc=512):
    assert rows.n_lat % bt == 0 and bt % rows.Lc == 0 and bt % GRID_W == 0
    off = col_off // bc
    kern = functools.partial(_conv_silu_kernel, n_lat_blocks=rows.n_lat // bt,
                             lat_period=GRID_W, ctx_period=rows.Lc)
    return pl.pallas_call(
        kern,
        grid=(rows.n // bt, n_cols // bc),
        in_specs=[pl.BlockSpec((bt, bc), lambda i, j: (i, j + off)),
                  pl.BlockSpec((None, 3, bc), lambda i, j: (widx, 0, j)),
                  pl.BlockSpec((None, 1, bc), lambda i, j: (widx, 0, j))],
        out_specs=pl.BlockSpec((bt, bc), lambda i, j: (i, j)),
        out_shape=jax.ShapeDtypeStruct((rows.n, n_cols), F32),
        compiler_params=_cparams(2),
        name="conv_silu",
    )(p, w, b.reshape(b.shape[0], 1, b.shape[1]))


def _hyena_conv_kernel(p0_ref, p1_ref, p2_ref, w0_ref, w1_ref, w2_ref, b0_ref, b1_ref, b2_ref,
                       x0_ref, u_ref, *, period):
    x0_ref[...] = _conv3(p0_ref[...], w0_ref, b0_ref, period)
    x1 = _conv3(p1_ref[...], w1_ref, b1_ref, period)
    v = _conv3(p2_ref[...], w2_ref, b2_ref, period)
    u_ref[...] = (x1 * v).astype(u_ref.dtype)


def hyena_conv(p, w, b, widx, *, row0, bsz, seq, period, bt=256, bc=512):
    nt = seq // bt
    ncb = HY_D // bc
    rb0 = row0 // bt

    def pspec(k):
        return pl.BlockSpec((bt, bc), lambda bb, i, j: (rb0 + bb * nt + i, j + k * ncb))

    def wspec(k):
        return pl.BlockSpec((None, 3, bc), lambda bb, i, j: (widx, 0, j + k * ncb))

    def bspec(k):
        return pl.BlockSpec((None, 1, bc), lambda bb, i, j: (widx, 0, j + k * ncb))

    ospec = pl.BlockSpec((bt, bc), lambda bb, i, j: (i, bb * ncb + j))
    b3 = b.reshape(b.shape[0], 1, b.shape[1])
    return pl.pallas_call(
        functools.partial(_hyena_conv_kernel, period=period),
        grid=(bsz, nt, ncb),
        in_specs=[pspec(0), pspec(1), pspec(2), wspec(0), wspec(1), wspec(2), bspec(0), bspec(1), bspec(2)],
        out_specs=[ospec, ospec],
        out_shape=[jax.ShapeDtypeStruct((seq, bsz * HY_D), F32),
                   jax.ShapeDtypeStruct((seq, bsz * HY_D), BF16)],
        compiler_params=_cparams(3),
        name="hyena_conv",
    )(p, p, p, w, w, w, b3, b3, b3)


def _hyena_feats(L):
    pos = np.arange(L, dtype=np.float64)
    t = pos / max(L - 1, 1)
    n_bands = (HY_EMB - 1) // 2
    bands = np.linspace(1e-4, n_bands - 1, n_bands)
    ang = (2 * math.pi / L) * pos[:, None] * bands[None, :]
    feats = np.concatenate([t[:, None], np.cos(ang), -np.sin(ang)], axis=-1)
    feats = np.pad(feats, ((0, 0), (0, LANES - HY_EMB)))
    deltas = np.abs(np.linspace(math.log(HY_DECAY_TARGET) / HY_SLOW_DECAY,
                                math.log(HY_DECAY_TARGET) / HY_FAST_DECAY, HY_D))
    return feats.astype(np.float32), t.astype(np.float32)[:, None], deltas.astype(np.float32)[None, :]


def _hyena_filter_kernel(feats_ref, t_ref, dl_ref, w1_ref, b1_ref, w2_ref, b2_ref, w3f_ref, w3b_ref,
                         fq_ref, hf_ref, hb_ref):
    fq = fq_ref[...]
    h = jnp.sin(fq[0:1] * (_dot_hi(feats_ref[...], w1_ref[...]) + b1_ref[...]))
    h = jnp.sin(fq[1:2] * (_dot_hi(h, w2_ref[...]) + b2_ref[...]))
    win = jnp.exp(-t_ref[...] * dl_ref[...])
    h_f = _dot_hi(h, w3f_ref[...]) * win
    h_b = _dot_hi(h, w3b_ref[...]) * win
    row = lax.broadcasted_iota(jnp.int32, h_b.shape, 0)
    h_b = jnp.where(row == 0, 0.0, h_b)
    l1 = jnp.sum(jnp.abs(h_f), axis=0, keepdims=True) + jnp.sum(jnp.abs(h_b), axis=0, keepdims=True)
    hf_ref[...] = h_f / l1
    hb_ref[...] = h_b / l1


def hyena_filter(L, w1, b1, w2, b2, w3, freq, widx, bc=256):
    feats, t, deltas = _hyena_feats(L)
    hid = w2.shape[1]
    w1p = jnp.pad(w1[widx], ((0, LANES - HY_EMB), (0, 0)))
    ncb = HY_D // bc
    full = lambda shape: pl.BlockSpec(shape, lambda j: (0,) * len(shape))
    return pl.pallas_call(
        _hyena_filter_kernel,
        grid=(ncb,),
        in_specs=[full((L, LANES)), full((L, 1)), pl.BlockSpec((1, bc), lambda j: (0, j)),
                  full((LANES, hid)), full((1, hid)), full((hid, hid)), full((1, hid)),
                  pl.BlockSpec((hid, bc), lambda j: (0, j)),
                  pl.BlockSpec((hid, bc), lambda j: (0, j + ncb)),
                  full((2, hid))],
        out_specs=[pl.BlockSpec((L, bc), lambda j: (0, j)), pl.BlockSpec((L, bc), lambda j: (0, j))],
        out_shape=[jax.ShapeDtypeStruct((L, HY_D), F32), jax.ShapeDtypeStruct((L, HY_D), F32)],
        compiler_params=_cparams(1),
        name="hyena_filter",
    )(jnp.asarray(feats), jnp.asarray(t), jnp.asarray(deltas), w1p, b1[widx][None], w2[widx],
      b2[widx][None], w3[widx], w3[widx], freq[widx])


def _dft_matrix(L):
    f = np.arange(L, dtype=np.int64)[:, None]
    s = np.arange(L, dtype=np.int64)[None, :]
    ang = (math.pi / L) * ((f * s) % (2 * L)).astype(np.float64)
    a_cos = np.cos(ang)
    a_sin = -np.sin(ang)
    a_sin[0, :] = np.where(np.arange(L) % 2 == 0, 1.0, -1.0)
    return np.concatenate([a_cos, a_sin], axis=0).astype(np.float32)


def _spectrum_kernel(ac_ref, as_ref, hf_ref, hb_ref, kre_ref, kim_ref, kny_ref, *, L, bf):
    hf = hf_ref[...].astype(BF16)
    hb = hb_ref[...].astype(BF16)
    ac = ac_ref[...].astype(BF16)
    a_s = as_ref[...].astype(BF16)
    cf, cb = _dot(ac, hf), _dot(ac, hb)
    sf, sb = _dot(a_s, hf), _dot(a_s, hb)
    f = lax.broadcasted_iota(jnp.int32, cf.shape, 0) + pl.program_id(1) * bf
    wgt = jnp.where(f == 0, 0.5 / L, 1.0 / L)
    kre_ref[...] = (cf + cb) * wgt
    kim_ref[...] = jnp.where(f == 0, 0.0, (sf - sb) * wgt)
    kny_ref[...] = jnp.where(f == 0, (sf + sb) * wgt, (cf + cb) * wgt)


def hyena_spectrum(a_mat, hf, hb, L, bf=256, bc=512):
    bf = min(bf, L)
    nf = L // bf
    C = hf.shape[1]
    ospec = pl.BlockSpec((bf, bc), lambda c, i: (i, c))
    hspec = pl.BlockSpec((L, bc), lambda c, i: (0, c))
    return pl.pallas_call(
        functools.partial(_spectrum_kernel, L=L, bf=bf),
        grid=(C // bc, nf),
        in_specs=[pl.BlockSpec((bf, L), lambda c, i: (i, 0)), pl.BlockSpec((bf, L), lambda c, i: (i + nf, 0)),
                  hspec, hspec],
        out_specs=[ospec, ospec, ospec],
        out_shape=[jax.ShapeDtypeStruct((L, C), F32)] * 3,
        compiler_params=_cparams(2),
        name="hyena_spectrum",
    )(a_mat, a_mat, hf, hb)


def _lc_fwd_kernel(ac_ref, as_ref, u_ref, kre_ref, kim_ref, kny_ref, yre_ref, yim_ref, acb_ref, asb_ref):
    @pl.when(pl.program_id(1) == 0)
    def _():
        acb_ref[...] = ac_ref[...].astype(BF16)
        asb_ref[...] = as_ref[...].astype(BF16)
    u = u_ref[...]
    ure = _dot(acb_ref[...], u)
    uim = _dot(asb_ref[...], u)
    kim = kim_ref[...]
    yre_ref[...] = (kre_ref[...] * ure - kim * uim).astype(yre_ref.dtype)
    yim_ref[...] = (kny_ref[...] * uim + kim * ure).astype(yim_ref.dtype)


def long_conv_fwd(a_mat, u, kre, kim, kny, L, bsz, bf=512):
    bf = min(bf, L)
    nf = L // bf
    C = kre.shape[1]
    kspec = pl.BlockSpec((bf, C), lambda i, b: (i, 0))
    ospec = pl.BlockSpec((bf, C), lambda i, b: (i, b))
    return pl.pallas_call(
        _lc_fwd_kernel,
        grid=(nf, bsz),
        in_specs=[pl.BlockSpec((bf, L), lambda i, b: (i, 0)), pl.BlockSpec((bf, L), lambda i, b: (i + nf, 0)),
                  pl.BlockSpec((L, C), lambda i, b: (0, b)), kspec, kspec, kspec],
        out_specs=[ospec, ospec],
        out_shape=[jax.ShapeDtypeStruct((L, bsz * C), BF16)] * 2,
        scratch_shapes=[pltpu.VMEM((bf, L), BF16), pltpu.VMEM((bf, L), BF16)],
        compiler_params=_cparams(2),
        name="long_conv_fwd",
    )(a_mat, a_mat, u, kre, kim, kny)


def _lc_inv_kernel(atc_ref, ats_ref, yre_ref, yim_ref, x0_ref, u_ref, bias_ref, o_ref, atcb_ref, atsb_ref):
    @pl.when(pl.program_id(1) == 0)
    def _():
        atcb_ref[...] = atc_ref[...].astype(BF16)
        atsb_ref[...] = ats_ref[...].astype(BF16)
    y = _dot(atcb_ref[...], yre_ref[...]) + _dot(atsb_ref[...], yim_ref[...])
    u = u_ref[...].astype(F32)
    o_ref[...] = (x0_ref[...] * (y + bias_ref[...] * u)).astype(o_ref.dtype)


def long_conv_inv(at_mat, yre, yim, x0, u, bias, widx, L, bsz, bt=512):
    bt = min(bt, L)
    nt = L // bt
    C = bias.shape[1]
    tspec = pl.BlockSpec((bt, C), lambda i, b: (i, b))
    yspec = pl.BlockSpec((L, C), lambda i, b: (0, b))
    return pl.pallas_call(
        _lc_inv_kernel,
        grid=(nt, bsz),
        in_specs=[pl.BlockSpec((bt, L), lambda i, b: (i, 0)), pl.BlockSpec((bt, L), lambda i, b: (i, 1)),
                  yspec, yspec, tspec, tspec, pl.BlockSpec((None, 1, C), lambda i, b: (widx, 0, 0))],
        out_specs=tspec,
        out_shape=jax.ShapeDtypeStruct((L, bsz * C), BF16),
        scratch_shapes=[pltpu.VMEM((bt, L), BF16), pltpu.VMEM((bt, L), BF16)],
        compiler_params=_cparams(2),
        name="long_conv_inv",
    )(at_mat, at_mat, yre, yim, x0, u, bias.reshape(bias.shape[0], 1, C))


def hyena(p, L, bsz, row0, period, widx, conv_w, conv_b, filt, bias):
    a_np = _dft_matrix(L)
    a_mat = jnp.asarray(a_np)
    at_mat = jnp.asarray(np.ascontiguousarray(a_np.T))
    x0, u = hyena_conv(p, conv_w, conv_b, widx, row0=row0, bsz=bsz, seq=L, period=period, bt=min(1024, L))
    hf, hb = hyena_filter(L, *filt, widx)
    kre, kim, kny = hyena_spectrum(a_mat, hf, hb, L)
    yre, yim = long_conv_fwd(a_mat, u, kre, kim, kny, L, bsz)
    return long_conv_inv(at_mat, yre, yim, x0, u, bias, widx, L, bsz)


def _pick_col(g, idx):
    lane = lax.broadcasted_iota(jnp.int32, g.shape, 1)
    return jnp.sum(jnp.where(lane == idx, g, 0.0), axis=1, keepdims=True)


def _chunk_masks(d, n):
    t_i = lax.broadcasted_iota(jnp.int32, (n, n), 0)
    s_i = lax.broadcasted_iota(jnp.int32, (n, n), 1)
    lag = (t_i - s_i) * jnp.where(d == 0, 1, -1)
    return lag >= 0, lag <= 0


def _running_sums(x_cols, x_rows, causal, causal_t):
    tri = jnp.where(causal, 1.0, 0.0)
    tri_t = jnp.where(causal_t, 1.0, 0.0)
    return _dot_hi(tri, x_cols), _dot_hi(x_rows, tri_t)


def _mlstm_kernel(q_ref, k_ref, v_ref, gc_ref, gr_ref, bc_ref, br_ref, o_ref, c_ref, n_ref, m_ref, gs_ref):
    d = pl.program_id(1)

    @pl.when(pl.program_id(2) == 0)
    def _():
        c_ref[...] = jnp.zeros_like(c_ref)
        n_ref[...] = jnp.zeros_like(n_ref)
        m_ref[...] = jnp.zeros_like(m_ref)

    gcol = gc_ref[...] + bc_ref[...]
    lane = lax.broadcasted_iota(jnp.int32, gcol.shape, 1)
    gcol = jnp.where((lane & ML_H) != 0, _log_sigmoid(gcol), gcol)
    grow = gr_ref[...] + br_ref[...]
    sub = lax.broadcasted_iota(jnp.int32, grow.shape, 0)
    grow = jnp.where((sub & ML_H) != 0, _log_sigmoid(grow), grow)

    causal, causal_t = _chunk_masks(d, CHUNK)
    ccol, crow = _running_sums(gcol, grow, causal, causal_t)
    gs_ref[0] = grow
    gs_ref[1] = crow

    def head(h):
        sl = pl.ds(h * ML_DH, ML_DH)
        return _mlstm_head(h, d, q_ref.at[:, sl], k_ref.at[:, sl], v_ref.at[:, sl], o_ref.at[:, sl], gcol, ccol,
                           gs_ref, causal, c_ref.at[h], n_ref.at[h], m_ref.at[h])

    _run_interleaved([head(h) for h in range(ML_H)])


def _run_interleaved(stages):
    live = list(stages)
    while live:
        still = []
        for gen in live:
            try:
                next(gen)
                still.append(gen)
            except StopIteration:
                pass
        live = still


def _mlstm_head(h, d, q_ref, k_ref, v_ref, o_ref, gcol, ccol, gs_ref, causal, c_ref, n_ref, m_ref):
    i_idx = 2 * ML_H * d + h
    f_idx = i_idx + ML_H
    ig_row = gs_ref[0, pl.ds(i_idx, 1), :]
    ig_col = _pick_col(gcol, i_idx)
    bcum_row = gs_ref[1, pl.ds(f_idx, 1), :]
    bcum_col = _pick_col(ccol, f_idx)
    g = jnp.sum(gs_ref[0, pl.ds(f_idx, 1), :], axis=1, keepdims=True)

    q = q_ref[...] * (ML_DH ** -0.5)
    k = k_ref[...]
    qb, kb, vb = q.astype(BF16), k.astype(BF16), v_ref[...].astype(BF16)
    c_in, n_in, m_in = c_ref[...], n_ref[...], m_ref[0:1, 0:1]

    dmat = jnp.where(causal, bcum_col - bcum_row + ig_row, -jnp.inf)
    m_inter = bcum_col + m_in
    m_t = jnp.maximum(jnp.max(dmat, axis=1, keepdims=True), m_inter)
    decay = jnp.exp(dmat - m_t)
    w_inter = jnp.exp(m_inter - m_t)
    a_row = g - bcum_row + ig_row
    a_col = g - bcum_col + ig_col
    m_loc = jnp.max(a_row, axis=1, keepdims=True)
    kw = k * jnp.exp(a_col - m_loc)
    qk = _dot_nt(qb, kb)
    qc = _dot(qb, c_in.astype(BF16))
    c_loc = _dot_tn(kw.astype(BF16), vb)
    yield

    s = qk * decay
    sv = _dot(s.astype(BF16), vb)
    den = jnp.sum(s, axis=1, keepdims=True) + w_inter * jnp.sum(q * n_in, axis=1, keepdims=True)
    yield

    num = sv + w_inter * qc
    o_ref[...] = num / jnp.maximum(jnp.abs(den), jnp.exp(-m_t))
    n_loc = jnp.sum(kw, axis=0, keepdims=True)
    m_new = jnp.maximum(g + m_in, m_loc)
    s_prev = jnp.exp(g + m_in - m_new)
    s_loc = jnp.exp(m_loc - m_new)
    c_ref[...] = s_prev * c_in + s_loc * c_loc
    n_ref[...] = s_prev * n_in + s_loc * n_loc
    m_ref[...] = jnp.broadcast_to(m_new, m_ref.shape)


def mlstm(qk, p, v_col_off, gates, gates_t, gate_b, rows):
    nct = (rows.L + rows.Lc) // CHUNK
    W = ML_H * ML_DH
    voff = v_col_off // W

    def rb(b, d, j):
        return rows.chunk_block(b, rows.scan_chunk(d, j))

    ng = 4 * ML_H
    bias = gate_b.reshape(ng)
    bias_row = jnp.pad(bias, (0, LANES - ng))[None]
    return pl.pallas_call(
        _mlstm_kernel,
        grid=(rows.B, 2, nct),
        in_specs=[pl.BlockSpec((CHUNK, W), lambda b, d, j: (rb(b, d, j), 0)),
                  pl.BlockSpec((CHUNK, W), lambda b, d, j: (rb(b, d, j), 1)),
                  pl.BlockSpec((CHUNK, W), lambda b, d, j: (rb(b, d, j), voff)),
                  pl.BlockSpec((CHUNK, LANES), lambda b, d, j: (rb(b, d, j), 0)),
                  pl.BlockSpec((ng, CHUNK), lambda b, d, j: (0, rb(b, d, j))),
                  pl.BlockSpec((1, LANES), lambda b, d, j: (0, 0)),
                  pl.BlockSpec((ng, 1), lambda b, d, j: (0, 0))],
        out_specs=pl.BlockSpec((None, CHUNK, W), lambda b, d, j: (d, rb(b, d, j), 0)),
        out_shape=jax.ShapeDtypeStruct((2, rows.n, W), F32),
        scratch_shapes=[pltpu.VMEM((ML_H, ML_DH, ML_DH), F32), pltpu.VMEM((ML_H, 1, ML_DH), F32),
                        pltpu.VMEM((ML_H, 8, LANES), F32), pltpu.VMEM((2, ng, CHUNK), F32)],
        compiler_params=_cparams(3),
        name="mlstm",
    )(qk, qk, p, gates, gates_t, bias_row, bias[:, None])


def _finish_ab_kernel(yl_ref, yc_ref, h_ref, o_ref, nw_ref, out_ref, *, n_lat_blocks):
    @pl.when(pl.program_id(0) < n_lat_blocks)
    def _():
        out_ref[:, 0:HY_D] = yl_ref[...]

    @pl.when(pl.program_id(0) >= n_lat_blocks)
    def _():
        out_ref[:, 0:HY_D] = yc_ref[...]

    hs = h_ref[0] + h_ref[1]
    og = _sigmoid(o_ref[...])
    nw = nw_ref[...]
    for i in range(ML_H):
        sl = slice(i * ML_DH, (i + 1) * ML_DH)
        x = hs[:, sl]
        mu = jnp.mean(x, axis=-1, keepdims=True)
        xc = x - mu
        var = jnp.mean(xc * xc, axis=-1, keepdims=True)
        y = xc * lax.rsqrt(var + EPS) * nw[:, sl] * og[:, sl]
        out_ref[:, HY_D + i * ML_DH:HY_D + (i + 1) * ML_DH] = y.astype(out_ref.dtype)


def finish_ab(yh_lat, yh_ctx, hdir, p, o_col_off, norm_w, widx, rows, bt=256):
    n_lat_blocks = rows.n_lat // bt
    per_b = rows.L // bt
    ooff = o_col_off // (ML_H * ML_DH)
    W = ML_H * ML_DH

    assert rows.Lc == bt
    n_blocks = rows.n // bt

    def lat_map(i):
        j = jnp.minimum(i, n_lat_blocks - 1)
        return (j % per_b, j // per_b)

    return pl.pallas_call(
        functools.partial(_finish_ab_kernel, n_lat_blocks=n_lat_blocks),
        grid=(n_blocks,),
        in_specs=[pl.BlockSpec((bt, HY_D), lat_map),
                  pl.BlockSpec((bt, HY_D), lambda i: (0, jnp.maximum(i - n_lat_blocks, 0))),
                  pl.BlockSpec((2, bt, W), lambda i: (0, i, 0)),
                  pl.BlockSpec((bt, W), lambda i: (i, ooff)),
                  pl.BlockSpec((None, 1, W), lambda i: (widx, 0, 0))],
        out_specs=pl.BlockSpec((bt, HY_D + W), lambda i: (i, 0)),
        out_shape=jax.ShapeDtypeStruct((rows.n, HY_D + W), BF16),
        compiler_params=_cparams(1),
        name="finish_ab",
    )(yh_lat, yh_ctx, hdir, p, norm_w.reshape(norm_w.shape[0], 1, W))


def _ssd_kernel(x_ref, b_ref, c_ref, dc_ref, dr_ref, pr_ref, pc_ref, o_ref, h_ref, ds_ref):
    d = pl.program_id(1)

    @pl.when(pl.program_id(2) == 0)
    def _():
        h_ref[...] = jnp.zeros_like(h_ref)

    dt_cols = _softplus(dc_ref[...] + pr_ref[1:2, :])
    la_cols = dt_cols * -jnp.exp(pr_ref[0:1, :])
    dt_rows = _softplus(dr_ref[...] + pc_ref[:, 1:2])
    la_rows = dt_rows * -jnp.exp(pc_ref[:, 0:1])

    causal, causal_t = _chunk_masks(d, CHUNK)
    acum_cols, acum_rows = _running_sums(la_cols, la_rows, causal, causal_t)
    ds_ref[0] = dt_rows
    ds_ref[1] = la_rows
    ds_ref[2] = acum_rows
    n_groups = b_ref.shape[1] // LANES
    heads_per_group = SSD_H // n_groups
    bmats = [b_ref[:, g * LANES:(g + 1) * LANES].astype(BF16) for g in range(n_groups)]
    cmats = [c_ref[:, g * LANES:(g + 1) * LANES].astype(BF16) for g in range(n_groups)]
    cbs = [_dot_nt(cmats[g], bmats[g]) for g in range(n_groups)]

    def head(hd):
        grp = hd // heads_per_group
        idx = SSD_H * d + hd
        hs = pl.ds(hd * SSD_P, SSD_P)
        dt_row = ds_ref[0, pl.ds(idx, 1), :]
        dt_col = _pick_col(dt_cols, idx)
        acum_col = _pick_col(acum_cols, idx)
        acum_row = ds_ref[2, pl.ds(idx, 1), :]
        tot = jnp.sum(ds_ref[1, pl.ds(idx, 1), :], axis=1, keepdims=True)
        decay = jnp.exp(jnp.where(causal, acum_col - acum_row, -jnp.inf))
        xh = x_ref[:, hs]
        h_in = h_ref[hd]
        xw = xh * (jnp.exp(tot - acum_col) * dt_col)
        y_off = _dot_nt(cmats[grp], h_in.astype(BF16))
        st = _dot_tn(xw.astype(BF16), bmats[grp])
        y_diag = _dot((cbs[grp] * decay * dt_row).astype(BF16), xh.astype(BF16))
        yield
        o_ref[:, hs] = y_diag + y_off * jnp.exp(acum_col)
        h_ref[hd] = jnp.exp(tot) * h_in + st

    _run_interleaved([head(hd) for hd in range(SSD_H)])


def ssd(xbc, dt, dt_t, a_log, dt_bias, rows):
    nct = (rows.L + rows.Lc) // CHUNK
    nd = SSD_H * SSD_P
    gw = 2 * LANES

    def rb(b, d, j):
        return rows.chunk_block(b, rows.scan_chunk(d, j))

    nh = 2 * SSD_H
    par = jnp.stack([a_log.reshape(nh), dt_bias.reshape(nh)])
    par_rows = jnp.pad(par, ((0, 0), (0, LANES - nh)))
    return pl.pallas_call(
        _ssd_kernel,
        grid=(rows.B, 2, nct),
        in_specs=[pl.BlockSpec((CHUNK, nd), lambda b, d, j: (rb(b, d, j), 0)),
                  pl.BlockSpec((CHUNK, gw), lambda b, d, j: (rb(b, d, j), nd // gw)),
                  pl.BlockSpec((CHUNK, gw), lambda b, d, j: (rb(b, d, j), nd // gw + 1)),
                  pl.BlockSpec((CHUNK, LANES), lambda b, d, j: (rb(b, d, j), 0)),
                  pl.BlockSpec((nh, CHUNK), lambda b, d, j: (0, rb(b, d, j))),
                  pl.BlockSpec((2, LANES), lambda b, d, j: (0, 0)),
                  pl.BlockSpec((nh, 2), lambda b, d, j: (0, 0))],
        out_specs=pl.BlockSpec((None, CHUNK, nd), lambda b, d, j: (d, rb(b, d, j), 0)),
        out_shape=jax.ShapeDtypeStruct((2, rows.n, nd), F32),
        scratch_shapes=[pltpu.VMEM((SSD_H, SSD_P, LANES), F32), pltpu.VMEM((3, nh, CHUNK), F32)],
        compiler_params=_cparams(3),
        name="ssd",
    )(xbc, xbc, xbc, dt, dt_t, par_rows, par.T)


S5_TC = 64
S5_JB = 2
S5_NS = S5_G * S5_P


def _s5_kernel(*refs, bsz):
    uf_refs, ub_refs = refs[:bsz], refs[bsz:2 * bsz]
    w_ref, c_ref, lre_ref, lim_ref, sf_ref, sb_ref = refs[2 * bsz:2 * bsz + 6]
    z_ref, s_ref, wb_ref, cb_ref, u_ref, y_ref = refs[2 * bsz + 6:]

    @pl.when(pl.program_id(0) == 0)
    def _():
        s_ref[...] = jnp.zeros_like(s_ref)
        wb_ref[...] = w_ref[...].astype(BF16)
        cb_ref[...] = c_ref[...].astype(BF16)

    r_i = lax.broadcasted_iota(jnp.int32, (S5_TC, S5_TC), 0)
    c_i = lax.broadcasted_iota(jnp.int32, (S5_TC, S5_TC), 1)
    flip = jnp.where(r_i + c_i == S5_TC - 1, 1.0, 0.0)
    group = 2 * bsz
    n_blk = S5_W // LANES
    sw = S5_NS // n_blk
    for b in range(bsz):
        uf = uf_refs[b][...]
        ub = _dot(flip.astype(BF16), ub_refs[b][...].astype(BF16))
        for j in range(n_blk):
            u_ref[j, pl.ds(b, S5_TC, stride=group), :] = uf[:, j * LANES:(j + 1) * LANES]
            u_ref[j, pl.ds(bsz + b, S5_TC, stride=group), :] = ub[:, j * LANES:(j + 1) * LANES]

    for j in range(n_blk):
        z_ref[:, 2 * j * sw:2 * (j + 1) * sw] = _dot(u_ref[j].astype(BF16), wb_ref[j])

    for j0 in range(0, n_blk, S5_JB):
        blocks = range(j0, j0 + S5_JB)
        re_sl = [slice(2 * j * sw, (2 * j + 1) * sw) for j in blocks]
        im_sl = [slice((2 * j + 1) * sw, (2 * j + 2) * sw) for j in blocks]
        ar = [lre_ref[:, j * sw:(j + 1) * sw] for j in blocks]
        ai = [lim_ref[:, j * sw:(j + 1) * sw] for j in blocks]

        def step(t, carry):
            r0 = pl.multiple_of(t * 8, 8)
            new = []
            for k in range(S5_JB):
                zr, zi = carry[2 * k], carry[2 * k + 1]
                nr = ar[k] * zr - ai[k] * zi + z_ref[pl.ds(r0, 8), re_sl[k]]
                ni = ar[k] * zi + ai[k] * zr + z_ref[pl.ds(r0, 8), im_sl[k]]
                z_ref[pl.ds(r0, 8), re_sl[k]] = nr
                z_ref[pl.ds(r0, 8), im_sl[k]] = ni
                new += [nr, ni]
            return tuple(new)

        init = []
        for k in range(S5_JB):
            init += [s_ref[:, re_sl[k]], s_ref[:, im_sl[k]]]
        fin = lax.fori_loop(0, S5_TC, step, tuple(init))
        for k in range(S5_JB):
            s_ref[:, re_sl[k]] = fin[2 * k]
            s_ref[:, im_sl[k]] = fin[2 * k + 1]

    row = lax.broadcasted_iota(jnp.int32, (S5_TC * 8, LANES), 0)
    is_bwd = (row & 4) != 0
    for j in range(n_blk):
        yy = _dot(z_ref[:, 2 * j * sw:2 * (j + 1) * sw].astype(BF16), cb_ref[j])
        y_ref[j] = jnp.where(is_bwd, yy[:, LANES:], yy[:, :LANES])
    for b in range(bsz):
        for j in range(n_blk):
            ls = slice(j * LANES, (j + 1) * LANES)
            sf_ref[b, :, ls] = y_ref[j, pl.ds(b, S5_TC, stride=group), :]
            sb_ref[b, :, ls] = _dot_hi(flip, y_ref[j, pl.ds(bsz + b, S5_TC, stride=group), :])


def s5_params(lam_re, lam_im, log_dt, b_re, b_im, c_re, c_im):
    dt = jnp.exp(log_dt)[..., None]
    mag = jnp.exp(lam_re * dt)
    lb_re, lb_im = mag * jnp.cos(lam_im * dt), mag * jnp.sin(lam_im * dt)
    den = lam_re * lam_re + lam_im * lam_im
    f_re = ((lb_re - 1) * lam_re + lb_im * lam_im) / den
    f_im = (lb_im * lam_re - (lb_re - 1) * lam_im) / den
    cf_re = c_re[None] * f_re[:, :, None, :] - c_im[None] * f_im[:, :, None, :]
    cf_im = c_re[None] * f_im[:, :, None, :] + c_im[None] * f_re[:, :, None, :]
    gpb = LANES // S5_GS
    n_blk = S5_G // gpb
    eye = jnp.eye(gpb, dtype=F32)

    def w_blocks(b):
        bb = b.reshape(n_blk, gpb, S5_P, S5_GS)
        return jnp.einsum('jgpi,gh->jgihp', bb, eye).reshape(n_blk, gpb * S5_GS, gpb * S5_P)

    def c_blocks(c):
        cc = c.reshape(2, n_blk, gpb, S5_GS, S5_P)
        return jnp.einsum('djgip,gh->jgpdhi', cc, eye).reshape(n_blk, gpb * S5_P, 2 * gpb * S5_GS)

    def lam_rows(l, bsz):
        return jnp.repeat(l.reshape(2, S5_NS), bsz, axis=0)

    w_cat = jnp.concatenate([w_blocks(b_re), w_blocks(b_im)], axis=2)
    c_cat = jnp.concatenate([c_blocks(cf_re), -c_blocks(cf_im)], axis=1)
    return w_cat, c_cat, lb_re, lb_im, lam_rows


def s5(p, w_cat, c_cat, lre, lim, rows):
    bsz, L, Lc = rows.B, rows.L, rows.Lc
    assert bsz == 4
    n_blk = S5_W // LANES
    sw = S5_NS // n_blk
    rt = S5_TC * 2 * bsz
    ncc, ncl = Lc // S5_TC, L // S5_TC
    lat0 = rows.n_lat // S5_TC

    def fwd_block(b):
        return lambda i: (jnp.where(i < ncc, lat0 + b * ncc + i, b * ncl + i - ncc), 0)

    def bwd_block(b):
        return lambda i: (jnp.where(i < ncc, lat0 + b * ncc + ncc - 1 - i, b * ncl + ncl - 1 + ncc - i), 0)

    full = lambda shape: pl.BlockSpec(shape, lambda i: (0,) * len(shape))
    u_specs = ([pl.BlockSpec((S5_TC, S5_W), fwd_block(b)) for b in range(bsz)]
               + [pl.BlockSpec((S5_TC, S5_W), bwd_block(b)) for b in range(bsz)])
    out_specs = [pl.BlockSpec((bsz, S5_TC, S5_W), lambda i: (0, jnp.maximum(i - ncc, 0), 0)),
                 pl.BlockSpec((bsz, S5_TC, S5_W), lambda i: (0, jnp.minimum(ncl - 1 + ncc - i, ncl - 1), 0))]
    return pl.pallas_call(
        functools.partial(_s5_kernel, bsz=bsz),
        grid=((L + Lc) // S5_TC,),
        in_specs=u_specs + [full((n_blk, LANES, 2 * sw)), full((n_blk, 2 * sw, 2 * LANES)),
                            full((2 * bsz, S5_NS)), full((2 * bsz, S5_NS))],
        out_specs=out_specs,
        out_shape=[jax.ShapeDtypeStruct((bsz, L, S5_W), F32)] * 2,
        scratch_shapes=[pltpu.VMEM((rt, 2 * S5_NS), F32), pltpu.VMEM((2 * bsz, 2 * S5_NS), F32),
                        pltpu.VMEM((n_blk, LANES, 2 * sw), BF16), pltpu.VMEM((n_blk, 2 * sw, 2 * LANES), BF16),
                        pltpu.VMEM((n_blk, rt, LANES), F32), pltpu.VMEM((n_blk, rt, LANES), F32)],
        compiler_params=_cparams(1),
        name="s5",
    )(*([p] * (2 * bsz)), w_cat, c_cat, lre, lim)


def _finish_cd_kernel(y_ref, xs_ref, z_ref, sf_ref, sb_ref, u_ref, dssd_ref, nw_ref, d5_ref, glu_ref, out_ref):
    y = (y_ref[0] + y_ref[1] + dssd_ref[...] * xs_ref[...]) * _silu(z_ref[...])
    y = y * lax.rsqrt(jnp.mean(y * y, axis=-1, keepdims=True) + EPS) * nw_ref[...]
    nd = y.shape[1]
    out_ref[:, 0:nd] = y.astype(out_ref.dtype)
    s = sf_ref[...] + sb_ref[...] + d5_ref[...] * u_ref[...]
    s = 0.5 * s * (1.0 + lax.erf(s * (2.0 ** -0.5)))
    gl = _dot(s.astype(BF16), glu_ref[...].astype(BF16))
    out_ref[:, nd:nd + S5_W] = (gl[:, 0:S5_W] * _sigmoid(gl[:, S5_W:])).astype(out_ref.dtype)


def finish_cd(ydir, xbc, p1, sf, sb, p2, ssd_d_lanes, norm_w, s5_d, glu_w, widx, n_rows, bt=256):
    nd = SSD_H * SSD_P
    vec = lambda n: pl.BlockSpec((None, 1, n), lambda i: (widx, 0, 0))
    return pl.pallas_call(
        _finish_cd_kernel,
        grid=(n_rows // bt,),
        in_specs=[pl.BlockSpec((2, bt, nd), lambda i: (0, i, 0)),
                  pl.BlockSpec((bt, nd), lambda i: (i, 0)),
                  pl.BlockSpec((bt, nd), lambda i: (i, 0)),
                  pl.BlockSpec((bt, S5_W), lambda i: (i, 0)),
                  pl.BlockSpec((bt, S5_W), lambda i: (i, 0)),
                  pl.BlockSpec((bt, S5_W), lambda i: (i, 0)),
                  vec(nd), vec(nd), vec(S5_W),
                  pl.BlockSpec((None, S5_W, 2 * S5_W), lambda i: (widx, 0, 0))],
        out_specs=pl.BlockSpec((bt, nd + S5_W), lambda i: (i, 0)),
        out_shape=jax.ShapeDtypeStruct((n_rows, nd + S5_W), BF16),
        compiler_params=_cparams(1),
        name="finish_cd",
    )(ydir, xbc, p1, sf, sb, p2, ssd_d_lanes, norm_w.reshape(norm_w.shape[0], 1, nd),
      s5_d.reshape(s5_d.shape[0], 1, S5_W), glu_w)


def _router_kernel(x_ref, sh_ref, sc_ref, w_ref, h_ref, o_ref):
    x = x_ref[...]
    h = x * lax.rsqrt(jnp.mean(x * x, axis=-1, keepdims=True) + EPS) * (1.0 + sc_ref[...]) + sh_ref[...]
    h_ref[...] = h
    logits = _dot_hi(h, w_ref[...])
    lane = lax.broadcasted_iota(jnp.int32, logits.shape, 1)
    logits = jnp.where(lane < N_EXPERTS, logits, -jnp.inf)
    m1 = jnp.max(logits, axis=1, keepdims=True)
    i1 = jnp.min(jnp.where(logits == m1, lane, LANES), axis=1, keepdims=True)
    rest = jnp.where(lane == i1, -jnp.inf, logits)
    m2 = jnp.max(rest, axis=1, keepdims=True)
    i2 = jnp.min(jnp.where(rest == m2, lane, LANES), axis=1, keepdims=True)
    e2 = jnp.exp(m2 - m1)
    p1 = 1.0 / (1.0 + e2)
    p2 = e2 / (1.0 + e2)
    o_ref[...] = jnp.where(lane == 0, i1.astype(F32),
                           jnp.where(lane == 1, i2.astype(F32),
                                     jnp.where(lane == 2, p1, jnp.where(lane == 3, p2, 0.0))))


def modulate_route(x, shift, scale, w_router_padded, rows, bt=512):
    T, D = x.shape
    mod_spec = pl.BlockSpec((None, 1, D), lambda i: (rows.mod_row(i, bt), 0, 0))
    return pl.pallas_call(
        _router_kernel,
        grid=(T // bt,),
        in_specs=[pl.BlockSpec((bt, D), lambda i: (i, 0)), mod_spec, mod_spec,
                  pl.BlockSpec((D, LANES), lambda i: (0, 0))],
        out_specs=[pl.BlockSpec((bt, D), lambda i: (i, 0)), pl.BlockSpec((bt, LANES), lambda i: (i, 0))],
        out_shape=[jax.ShapeDtypeStruct((T, D), F32), jax.ShapeDtypeStruct((T, LANES), F32)],
        compiler_params=_cparams(1),
        name="modulate_route",
    )(x, shift, scale, w_router_padded)


def _gather_kernel(idx_ref, nt_ref, src_ref, o_ref, buf_ref, sem, *, bm, s):
    i = pl.program_id(0)
    n_used = nt_ref[0]
    slot = i % 2

    def row_copy(tile, slot_, r):
        return pltpu.make_async_copy(src_ref.at[idx_ref[tile * bm + r]],
                                     buf_ref.at[slot_, pl.ds(pl.multiple_of(r * s, s), s), :], sem.at[slot_])

    def issue(tile, slot_):
        def body(r, c):
            row_copy(tile, slot_, r).start()
            return c
        lax.fori_loop(0, bm, body, 0, unroll=8)

    @pl.when(i == 0)
    def _():
        issue(0, 0)

    @pl.when(i + 1 < n_used)
    def _():
        issue(i + 1, 1 - slot)

    @pl.when(i < n_used)
    def _():
        def body(r, c):
            row_copy(i, slot, r).wait()
            return c
        lax.fori_loop(0, bm, body, 0, unroll=8)
        for j in range(s):
            o_ref[:, j * LANES:(j + 1) * LANES] = buf_ref[slot, pl.ds(j, bm, stride=s), :].astype(o_ref.dtype)

    @pl.when(i >= n_used)
    def _():
        o_ref[...] = jnp.zeros_like(o_ref)


def gather_rows(src, idx, n_used_tiles, bm=256):
    N = idx.shape[0]
    T, D = src.shape
    s = D // LANES
    return pl.pallas_call(
        functools.partial(_gather_kernel, bm=bm, s=s),
        grid_spec=pltpu.PrefetchScalarGridSpec(
            num_scalar_prefetch=2,
            grid=(N // bm,),
            in_specs=[pl.BlockSpec(memory_space=pl.ANY)],
            out_specs=pl.BlockSpec((bm, D), lambda i, idx, nt: (i, 0)),
            scratch_shapes=[pltpu.VMEM((2, bm * s, LANES), F32), pltpu.SemaphoreType.DMA((2,))]),
        out_shape=jax.ShapeDtypeStruct((N, D), BF16),
        compiler_params=_cparams(1),
        name="gather_rows",
    )(idx, n_used_tiles, src.reshape(T, s, LANES))


def _gmm_kernel(hdr_ref, grp_ref, last_ref, rows_ref, x_ref, w_ref, o_ref, wf_ref, wb_ref, sem,
                *, widx, bn, nf, parts, bm):
    n = pl.program_id(0)
    m = pl.program_id(1)
    n_used, n_grp = hdr_ref[0], hdr_ref[1]
    total = nf * n_grp
    grp = grp_ref[m]
    order = n * n_grp + grp
    half = bm // 2

    def group_after(nq, gq):
        wrap = gq + 1 >= n_grp
        return jnp.where(wrap, nq + 1, nq), jnp.where(wrap, 0, gq + 1)

    def fetch(nq, gq):
        expert = hdr_ref[2 + gq]
        return [pltpu.make_async_copy(
            w_ref.at[widx, expert, :, pl.ds(pl.multiple_of((nq + p * nf) * bn, bn), bn)],
            wf_ref.at[p], sem.at[p]) for p in range(parts)]

    def rows_times_weights(r0, nr):
        x = x_ref[r0:r0 + nr, :]
        if parts == 2:
            o_ref[r0:r0 + nr, :] = (_silu(_dot(x, wb_ref[0])) * _dot(x, wb_ref[1])).astype(o_ref.dtype)
        else:
            o_ref[r0:r0 + nr, :] = _dot(x, wb_ref[0]).astype(o_ref.dtype)

    is_first = (m == 0) | (grp_ref[jnp.maximum(m - 1, 0)] != grp)
    is_last = last_ref[m] == 1

    @pl.when(is_first)
    def _():
        @pl.when((n == 0) & (m == 0))
        def _():
            for cp in fetch(n, grp):
                cp.start()

        for cp in fetch(n, grp):
            cp.wait()
        wb_ref[...] = wf_ref[...].astype(BF16)

        @pl.when(order + 1 < total)
        def _():
            for cp in fetch(*group_after(n, grp)):
                cp.start()

    @pl.when((m < n_used) & jnp.logical_not(is_last))
    def _():
        rows_times_weights(0, bm)

    @pl.when(is_last)
    def _():
        rows_times_weights(0, half)

        @pl.when(rows_ref[m] > half)
        def _():
            rows_times_weights(half, half)

        @pl.when(rows_ref[m] <= half)
        def _():
            o_ref[half:, :] = jnp.zeros((half, bn), o_ref.dtype)

    @pl.when(m >= n_used)
    def _():
        o_ref[...] = jnp.zeros_like(o_ref)


def gmm(x, w, widx, plan, *, bm, bn, swiglu):
    M, K = x.shape
    parts = 2 if swiglu else 1
    N = w.shape[3] // parts
    nf = N // bn
    return pl.pallas_call(
        functools.partial(_gmm_kernel, widx=widx, bn=bn, nf=nf, parts=parts, bm=bm),
        grid_spec=pltpu.PrefetchScalarGridSpec(
            num_scalar_prefetch=4,
            grid=(nf, M // bm),
            in_specs=[pl.BlockSpec((bm, K), lambda n, m, *_: (m, 0)),
                      pl.BlockSpec(memory_space=pl.ANY)],
            out_specs=pl.BlockSpec((bm, bn), lambda n, m, *_: (m, n)),
            scratch_shapes=[pltpu.VMEM((parts, K, bn), F32), pltpu.VMEM((parts, K, bn), BF16),
                            pltpu.SemaphoreType.DMA((parts,))]),
        out_shape=jax.ShapeDtypeStruct((M, N), BF16 if swiglu else F32),
        compiler_params=_cparams(2),
        name="gmm_swiglu" if swiglu else "gmm",
    )(plan.header, plan.tile_group, plan.tile_last, plan.tile_rows, x, w)


def _combine_kernel(pos_ref, y_ref, x_ref, pr_ref, g_ref, nw_ref, o_ref, buf_ref, sem, *, bm, n_tok):
    i = pl.program_id(0)
    slot = i % 2

    def copies(tile, slot_, r):
        return [pltpu.make_async_copy(y_ref.at[pl.ds(pos_ref[c * n_tok + tile * bm + r], 1), :],
                                      buf_ref.at[slot_, c, pl.ds(r, 1), :], sem.at[slot_, c]) for c in range(2)]

    def issue(tile, slot_):
        def body(r, carry):
            for cp in copies(tile, slot_, r):
                cp.start()
            return carry
        lax.fori_loop(0, bm, body, 0, unroll=4)

    @pl.when(i == 0)
    def _():
        issue(0, 0)

    @pl.when(i + 1 < pl.num_programs(0))
    def _():
        issue(i + 1, 1 - slot)

    def drain(r, carry):
        for cp in copies(i, slot, r):
            cp.wait()
        return carry

    lax.fori_loop(0, bm, drain, 0, unroll=4)
    pr = pr_ref[...]
    mix = pr[:, 2:3] * buf_ref[slot, 0] + pr[:, 3:4] * buf_ref[slot, 1]
    x = x_ref[...] + g_ref[...] * mix
    o_ref[...] = x * lax.rsqrt(jnp.mean(x * x, axis=-1, keepdims=True) + EPS) * nw_ref[...]


def moe_combine(ys, pos, x, route, gate, norm_w, rows, bm=256):
    T, D = x.shape
    return pl.pallas_call(
        functools.partial(_combine_kernel, bm=bm, n_tok=T),
        grid_spec=pltpu.PrefetchScalarGridSpec(
            num_scalar_prefetch=1,
            grid=(T // bm,),
            in_specs=[pl.BlockSpec(memory_space=pl.ANY),
                      pl.BlockSpec((bm, D), lambda i, pos: (i, 0)),
                      pl.BlockSpec((bm, LANES), lambda i, pos: (i, 0)),
                      pl.BlockSpec((None, 1, D), lambda i, pos: (rows.mod_row(i, bm), 0, 0)),
                      pl.BlockSpec((1, D), lambda i, pos: (0, 0))],
            out_specs=pl.BlockSpec((bm, D), lambda i, pos: (i, 0)),
            scratch_shapes=[pltpu.VMEM((2, 2, bm, D), F32), pltpu.SemaphoreType.DMA((2, 2))]),
        out_shape=jax.ShapeDtypeStruct((T, D), F32),
        compiler_params=_cparams(1),
        name="moe_combine",
    )(pos, ys, x, route, gate, norm_w[None])


class MoePlan(NamedTuple):
    pos: jax.Array
    src: jax.Array
    n_active: jax.Array
    header: jax.Array
    tile_group: jax.Array
    tile_last: jax.Array
    tile_rows: jax.Array


def moe_plan(route, n_tok, bm):
    e = jnp.concatenate([route[:, 0], route[:, 1]]).astype(jnp.int32)
    onehot = (e[:, None] == jnp.arange(N_EXPERTS, dtype=jnp.int32)[None, :]).astype(jnp.int32)
    rank = jnp.take_along_axis(jnp.cumsum(onehot, axis=0), e[:, None], axis=1)[:, 0] - 1
    counts = jnp.sum(onehot, axis=0)
    tiles = (counts + bm - 1) // bm
    tile_end = jnp.cumsum(tiles)
    start = (tile_end - tiles) * bm
    pos = start[e] + rank
    n_tiles = (2 * n_tok) // bm + N_EXPERTS
    tok = jnp.concatenate([jnp.arange(n_tok, dtype=jnp.int32)] * 2)
    src = jnp.zeros((n_tiles * bm,), jnp.int32).at[pos].set(tok)
    n_active = tile_end[-1]
    all_tiles = jnp.arange(n_tiles, dtype=jnp.int32)
    used = all_tiles < n_active
    tile_ids = jnp.minimum(all_tiles, n_active - 1)
    tile_expert = jnp.sum((tile_ids[:, None] >= tile_end[None, :]).astype(jnp.int32), axis=1)
    ids = jnp.arange(N_EXPERTS, dtype=jnp.int32)
    in_use = tiles > 0
    group_of = jnp.cumsum(in_use.astype(jnp.int32)) - 1
    n_groups = jnp.sum(in_use.astype(jnp.int32))
    experts_in_use = jnp.sum(jnp.where(in_use[None, :] & (group_of[None, :] == ids[:, None]), ids[None, :], 0), axis=1)
    tile_last = used & (all_tiles + 1 == tile_end[tile_expert])
    tile_rows = jnp.clip(counts[tile_expert] - (all_tiles - (tile_end - tiles)[tile_expert]) * bm, 0, bm)
    header = jnp.concatenate([n_active.reshape(1), n_groups.reshape(1), experts_in_use])
    return MoePlan(pos.astype(jnp.int32), src, n_active.reshape(1).astype(jnp.int32), header.astype(jnp.int32),
                   group_of[tile_expert].astype(jnp.int32), tile_last.astype(jnp.int32),
                   jnp.where(used, tile_rows, 0).astype(jnp.int32))


def kernel(x, c, ctx, c_ctx, ada_w, ada_b, ab_w_in, hy_conv_w, hy_conv_b, hy_filt_w1, hy_filt_b1, hy_filt_w2, hy_filt_b2, hy_filt_w3, hy_filt_freq, hy_bias, ml_conv_w, ml_conv_b, ml_gate_b, ml_norm_w, ab_w_out, ffn_w_gu, ffn_w_down, cd_w_in, ssd_conv_w, ssd_conv_b, ssd_A_log, ssd_dt_bias, ssd_D, ssd_norm_w, s5_lam_re, s5_lam_im, s5_log_dt, s5_B_re, s5_B_im, s5_C_re, s5_C_im, s5_D, s5_glu_w, cd_w_out, moe_router, moe_w_gu, moe_w_down, final_norm_w):
    bsz, L, D = x.shape
    Lc = ctx.shape[1]
    rows = Rows(bsz, L, Lc)
    n_lat = rows.n_lat
    assert bsz == 4 and L % 1024 == 0 and Lc == 256 and D % 256 == 0
    bn_d = min(1024, D)
    bn_f = min(512, ffn_w_down.shape[1])

    x_lat, x_ctx = x.reshape(n_lat, D), ctx.reshape(bsz * Lc, D)
    cond = jnp.concatenate([c, c_ctx[None], jnp.zeros((8 - bsz - 1, D), F32)], axis=0)

    def mods(layer):
        m = adaln(cond, ada_w, ada_b, layer)
        return [m[:, k * D:(k + 1) * D].reshape(8, 1, D) for k in range(6)]

    md = mods(0)
    h = modulate(x_lat, md[0], md[1], rows, rows.n, BF16, x_tail=x_ctx)
    n_main = 3 * HY_D + 4 * ML_H * ML_DH
    p = mm_t(h, jnp.swapaxes(ab_w_in, 1, 2), 0, n_cols=n_main, bm=1024, bn=1024)
    w_gate = jnp.pad(ab_w_in[:, :, n_main:], ((0, 0), (0, 0), (0, LANES - 4 * ML_H)))
    gates = mm(h, w_gate, 0, n_cols=LANES, bm=1024, bn=LANES)
    gates_t = gates[:, :4 * ML_H].T

    filt = (hy_filt_w1, hy_filt_b1, hy_filt_w2, hy_filt_b2, hy_filt_w3, hy_filt_freq)
    yh_lat = hyena(p, L, bsz, 0, GRID_W, 0, hy_conv_w, hy_conv_b, filt, hy_bias)
    yh_ctx = hyena(p, Lc, bsz, n_lat, Lc, 0, hy_conv_w, hy_conv_b, filt, hy_bias)

    qk = conv_silu(p, 3 * HY_D, 2 * ML_H * ML_DH, ml_conv_w, ml_conv_b, 0, rows)
    hdir = mlstm(qk, p, 3 * HY_D + 2 * ML_H * ML_DH, gates, gates_t, ml_gate_b[0], rows)
    cat = finish_ab(yh_lat, yh_ctx, hdir, p, 3 * HY_D + 3 * ML_H * ML_DH, ml_norm_w, 0, rows)
    xs = mm_residual(cat, ab_w_out, 0, x_lat, md[2], rows, bm=1024, bn=bn_d, res_tail=x_ctx)

    h = modulate(xs, md[3], md[4], rows, rows.n, BF16)
    act = mm_swiglu(h, ffn_w_gu, 0, bm=1024, bn=bn_f)
    xs = mm_residual(act, ffn_w_down, 0, xs, md[5], rows, bm=512, bn=min(512, D))

    md = mods(1)
    h = modulate(xs, md[0], md[1], rows, rows.n, BF16)
    nd = SSD_H * SSD_P
    n_xbc = nd + 4 * LANES
    p1 = mm_t(h, jnp.swapaxes(cd_w_in, 1, 2), 0, n_cols=nd + n_xbc, bm=1024, bn=512)
    w_tail = cd_w_in[:, :, nd + n_xbc:]
    w_tail = jnp.concatenate([w_tail[:, :, 2 * SSD_H:], w_tail[:, :, :2 * SSD_H],
                              jnp.zeros((1, D, LANES - 2 * SSD_H), F32)], axis=2)
    p2 = mm(h, w_tail, 0, n_cols=S5_W + LANES, bm=1024, bn=S5_W + LANES)

    xbc = conv_silu(p1, nd, n_xbc, ssd_conv_w, ssd_conv_b, 0, rows)
    dt = p2[:, S5_W:]
    dt_t = dt[:, :2 * SSD_H].T
    ydir = ssd(xbc, dt, dt_t, ssd_A_log[0], ssd_dt_bias[0], rows)

    w_cat, c_cat, lb_re, lb_im, lam_rows = s5_params(
        s5_lam_re[0], s5_lam_im[0], s5_log_dt[0], s5_B_re[0], s5_B_im[0], s5_C_re[0], s5_C_im[0])
    sf, sb = s5(p2, w_cat, c_cat, lam_rows(lb_re, bsz), lam_rows(lb_im, bsz), rows)
    sf, sb = sf.reshape(n_lat, S5_W), sb.reshape(n_lat, S5_W)

    ssd_d_lanes = jnp.repeat(ssd_D, SSD_P, axis=1).reshape(ssd_D.shape[0], 1, nd)
    cat = finish_cd(ydir, xbc, p1, sf, sb, p2, ssd_d_lanes, ssd_norm_w, s5_D, s5_glu_w, 0, n_lat)
    xl = mm_residual(cat, cd_w_out, 0, xs, md[2], rows, bm=1024, bn=bn_d)

    h2, route = modulate_route(xl, md[3], md[4], jnp.pad(moe_router[0], ((0, 0), (0, LANES - N_EXPERTS))), rows)
    bm_e = 512
    plan = moe_plan(route, n_lat, bm_e)
    xg = gather_rows(h2, plan.src, plan.n_active * (bm_e // 256), bm=256)
    act = gmm(xg, moe_w_gu, 0, plan, bm=bm_e, bn=min(512, moe_w_down.shape[2]), swiglu=True)
    ys = gmm(act, moe_w_down, 0, plan, bm=bm_e, bn=min(512, D), swiglu=False)
    out = moe_combine(ys, plan.pos, xl, route, md[5], final_norm_w, rows, bm=256)
    return out.reshape(bsz, L, D)
```

```python
import functools
import math
from typing import NamedTuple

import jax
import jax.numpy as jnp
import numpy as np
from jax import lax
from jax.experimental import pallas as pl
from jax.experimental.pallas import tpu as pltpu

F32 = jnp.float32
BF16 = jnp.bfloat16

EPS = 1e-6
GRID_W = 64
CHUNK = 128
LANES = 128
HY_D = 1024
HY_EMB = 33
HY_FAST_DECAY = 0.3
HY_SLOW_DECAY = 1.5
HY_DECAY_TARGET = 1e-2
ML_H = 8
ML_DH = 128
SSD_H = 16
SSD_P = 64
S5_W = 512
S5_G = 32
S5_GS = 16
S5_P = 64
N_EXPERTS = 8
VMEM_LIMIT_BYTES = 56 * 1024 * 1024


def _cparams(n_axes):
    return pltpu.CompilerParams(dimension_semantics=("arbitrary",) * n_axes,
                                vmem_limit_bytes=VMEM_LIMIT_BYTES)


def _sigmoid(x):
    return 1.0 / (1.0 + jnp.exp(-x))


def _silu(x):
    return x * _sigmoid(x)


def _log_sigmoid(x):
    return jnp.minimum(x, 0.0) - jnp.log(1.0 + jnp.exp(-jnp.abs(x)))


def _softplus(x):
    return jnp.maximum(x, 0.0) + jnp.log(1.0 + jnp.exp(-jnp.abs(x)))


def _dot(a, b):
    return jnp.dot(a, b, preferred_element_type=F32)


def _dot_nt(a, b):
    return lax.dot_general(a, b, (((1,), (1,)), ((), ())), preferred_element_type=F32)


def _dot_tn(a, b):
    return lax.dot_general(a, b, (((0,), (0,)), ((), ())), preferred_element_type=F32)


def _dot_hi(a, b):
    return jnp.dot(a, b, preferred_element_type=F32, precision=lax.Precision.HIGHEST)


class Rows:
    def __init__(self, bsz, L, Lc):
        self.B, self.L, self.Lc = bsz, L, Lc
        self.n_lat = bsz * L
        self.n = bsz * (L + Lc)

    def mod_row(self, i, bm):
        n_lat_blocks = self.n_lat // bm
        return jnp.where(i < n_lat_blocks, (i * bm) // self.L, self.B)

    def chunk_block(self, b, c):
        ncc = self.Lc // CHUNK
        return jnp.where(c < ncc, self.n_lat // CHUNK + b * ncc + c, b * (self.L // CHUNK) + c - ncc)

    def scan_chunk(self, d, j):
        ncc = self.Lc // CHUNK
        nct = (self.L + self.Lc) // CHUNK
        back = jnp.where(j < ncc, ncc - 1 - j, nct - 1 + ncc - j)
        return jnp.where(d == 0, j, back)


def _adaln_kernel(c_ref, w_ref, b_ref, o_ref):
    cond = _silu(c_ref[...]).astype(BF16)
    o_ref[...] = _dot(cond, w_ref[...].astype(BF16)) + b_ref[...]


def adaln(cond, ada_w, ada_b, layer, bn=1024):
    bn = min(bn, ada_w.shape[1])
    _, D, N = ada_w.shape
    return pl.pallas_call(
        _adaln_kernel,
        grid=(N // bn,),
        in_specs=[pl.BlockSpec((8, D), lambda n: (0, 0)),
                  pl.BlockSpec((None, D, bn), lambda n: (layer, 0, n)),
                  pl.BlockSpec((None, 1, bn), lambda n: (layer, 0, n))],
        out_specs=pl.BlockSpec((8, bn), lambda n: (0, n)),
        out_shape=jax.ShapeDtypeStruct((8, N), F32),
        compiler_params=_cparams(1),
        name="adaln",
    )(cond, ada_w, ada_b.reshape(ada_b.shape[0], 1, N))


def _head_tail_specs(block, n_head_blocks, row_col):
    def head(*ids):
        r, c = row_col(*ids)
        return jnp.minimum(r, n_head_blocks - 1), c

    def tail(*ids):
        r, c = row_col(*ids)
        return jnp.maximum(r - n_head_blocks, 0), c

    return pl.BlockSpec(block, head), pl.BlockSpec(block, tail)


def _modulate_kernel(*refs, n_head_blocks):
    x_ref, sh_ref, sc_ref, o_ref = refs[0], refs[-3], refs[-2], refs[-1]
    x = x_ref[...]
    if len(refs) == 5:
        x = jnp.where(pl.program_id(0) < n_head_blocks, x, refs[1][...])
    y = x * lax.rsqrt(jnp.mean(x * x, axis=-1, keepdims=True) + EPS)
    o_ref[...] = (y * (1.0 + sc_ref[...]) + sh_ref[...]).astype(o_ref.dtype)


def modulate(x, shift, scale, rows, n_rows, out_dtype, bm=512, x_tail=None):
    D = x.shape[1]
    mod_spec = pl.BlockSpec((None, 1, D), lambda i: (rows.mod_row(i, bm), 0, 0))
    if x_tail is None:
        xs, x_specs = [x], [pl.BlockSpec((bm, D), lambda i: (i, 0))]
    else:
        xs, x_specs = [x, x_tail], list(_head_tail_specs((bm, D), x.shape[0] // bm, lambda i: (i, 0)))
    return pl.pallas_call(
        functools.partial(_modulate_kernel, n_head_blocks=x.shape[0] // bm),
        grid=(n_rows // bm,),
        in_specs=x_specs + [mod_spec, mod_spec],
        out_specs=pl.BlockSpec((bm, D), lambda i: (i, 0)),
        out_shape=jax.ShapeDtypeStruct((n_rows, D), out_dtype),
        compiler_params=_cparams(1),
        name="modulate",
    )(*xs, shift, scale)


def _mm_kernel(x_ref, w_ref, o_ref, wb_ref):
    @pl.when(pl.program_id(1) == 0)
    def _():
        wb_ref[...] = w_ref[...].astype(BF16)
    o_ref[...] = _dot(x_ref[...], wb_ref[...]).astype(o_ref.dtype)


def mm(x, w, widx, *, n_cols, col_off=0, bm, bn, out_dtype=F32):
    M, K = x.shape
    off = col_off // bn
    return pl.pallas_call(
        _mm_kernel,
        grid=(n_cols // bn, M // bm),
        in_specs=[pl.BlockSpec((bm, K), lambda n, m: (m, 0)),
                  pl.BlockSpec((None, K, bn), lambda n, m: (widx, 0, n + off))],
        out_specs=pl.BlockSpec((bm, bn), lambda n, m: (m, n)),
        out_shape=jax.ShapeDtypeStruct((M, n_cols), out_dtype),
        scratch_shapes=[pltpu.VMEM((K, bn), BF16)],
        compiler_params=_cparams(2),
        name="mm",
    )(x, w)


def _mm_t_kernel(x_ref, wt_ref, o_ref, wb_ref):
    @pl.when(pl.program_id(1) == 0)
    def _():
        wb_ref[...] = wt_ref[...].astype(BF16)
    o_ref[...] = _dot_nt(x_ref[...], wb_ref[...]).astype(o_ref.dtype)


def mm_t(x, wt, widx, *, n_cols, bm, bn, out_dtype=F32):
    M, K = x.shape
    return pl.pallas_call(
        _mm_t_kernel,
        grid=(n_cols // bn, M // bm),
        in_specs=[pl.BlockSpec((bm, K), lambda n, m: (m, 0)),
                  pl.BlockSpec((None, bn, K), lambda n, m: (widx, n, 0))],
        out_specs=pl.BlockSpec((bm, bn), lambda n, m: (m, n)),
        out_shape=jax.ShapeDtypeStruct((M, n_cols), out_dtype),
        scratch_shapes=[pltpu.VMEM((bn, K), BF16)],
        compiler_params=_cparams(2),
        name="mm_t",
    )(x, wt)


def _mm_res_kernel(*refs, n_head_blocks):
    x_ref, w_ref, r_ref = refs[:3]
    g_ref, o_ref, wb_ref = refs[-3:]

    @pl.when(pl.program_id(1) == 0)
    def _():
        wb_ref[...] = w_ref[...].astype(BF16)
    res = r_ref[...]
    if len(refs) == 7:
        res = jnp.where(pl.program_id(1) < n_head_blocks, res, refs[3][...])
    o_ref[...] = res + g_ref[...] * _dot(x_ref[...], wb_ref[...])


def mm_residual(x, w, widx, res, gate, rows, *, bm, bn, res_tail=None):
    M, K = x.shape
    N = w.shape[2]
    if res_tail is None:
        rs, r_specs = [res], [pl.BlockSpec((bm, bn), lambda n, m: (m, n))]
    else:
        rs, r_specs = [res, res_tail], list(_head_tail_specs((bm, bn), res.shape[0] // bm, lambda n, m: (m, n)))
    return pl.pallas_call(
        functools.partial(_mm_res_kernel, n_head_blocks=res.shape[0] // bm),
        grid=(N // bn, M // bm),
        in_specs=[pl.BlockSpec((bm, K), lambda n, m: (m, 0)),
                  pl.BlockSpec((None, K, bn), lambda n, m: (widx, 0, n))] + r_specs
                 + [pl.BlockSpec((None, 1, bn), lambda n, m: (rows.mod_row(m, bm), 0, n))],
        out_specs=pl.BlockSpec((bm, bn), lambda n, m: (m, n)),
        out_shape=jax.ShapeDtypeStruct((M, N), F32),
        scratch_shapes=[pltpu.VMEM((K, bn), BF16)],
        compiler_params=_cparams(2),
        name="mm_residual",
    )(x, w, *rs, gate)


def _mm_swiglu_kernel(x_ref, wg_ref, wu_ref, o_ref, wgb_ref, wub_ref):
    @pl.when(pl.program_id(1) == 0)
    def _():
        wgb_ref[...] = wg_ref[...].astype(BF16)
        wub_ref[...] = wu_ref[...].astype(BF16)
    x = x_ref[...]
    g = _dot(x, wgb_ref[...])
    u = _dot(x, wub_ref[...])
    o_ref[...] = (_silu(g) * u).astype(o_ref.dtype)


def mm_swiglu(x, w_gu, widx, *, bm, bn):
    M, K = x.shape
    F = w_gu.shape[2] // 2
    nf = F // bn
    return pl.pallas_call(
        _mm_swiglu_kernel,
        grid=(nf, M // bm),
        in_specs=[pl.BlockSpec((bm, K), lambda n, m: (m, 0)),
                  pl.BlockSpec((None, K, bn), lambda n, m: (widx, 0, n)),
                  pl.BlockSpec((None, K, bn), lambda n, m: (widx, 0, n + nf))],
        out_specs=pl.BlockSpec((bm, bn), lambda n, m: (m, n)),
        out_shape=jax.ShapeDtypeStruct((M, F), BF16),
        scratch_shapes=[pltpu.VMEM((K, bn), BF16), pltpu.VMEM((K, bn), BF16)],
        compiler_params=_cparams(2),
        name="mm_swiglu",
    )(x, w_gu, w_gu)


def _conv3(x, w_ref, b_ref, period):
    n = x.shape[0]
    pos = lax.broadcasted_iota(jnp.int32, x.shape, 0) & (period - 1)
    prev = jnp.where(pos == 0, 0.0, pltpu.roll(x, 1, 0))
    nxt = jnp.where(pos == period - 1, 0.0, pltpu.roll(x, n - 1, 0))
    w = w_ref[...]
    return b_ref[...] + prev * w[0:1] + x * w[1:2] + nxt * w[2:3]


def _conv_silu_kernel(x_ref, w_ref, b_ref, o_ref, *, n_lat_blocks, lat_period, ctx_period):
    period = jnp.where(pl.program_id(0) < n_lat_blocks, lat_period, ctx_period)
    o_ref[...] = _silu(_conv3(x_ref[...], w_ref, b_ref, period)).astype(o_ref.dtype)


def conv_silu(p, col_off, n_cols, w, b, widx, rows, *, bt=1024, bc=512):
    assert rows.n_lat % bt == 0 and bt % rows.Lc == 0 and bt % GRID_W == 0
    off = col_off // bc
    kern = functools.partial(_conv_silu_kernel, n_lat_blocks=rows.n_lat // bt,
                             lat_period=GRID_W, ctx_period=rows.Lc)
    return pl.pallas_call(
        kern,
        grid=(rows.n // bt, n_cols // bc),
        in_specs=[pl.BlockSpec((bt, bc), lambda i, j: (i, j + off)),
                  pl.BlockSpec((None, 3, bc), lambda i, j: (widx, 0, j)),
                  pl.BlockSpec((None, 1, bc), lambda i, j: (widx, 0, j))],
        out_specs=pl.BlockSpec((bt, bc), lambda i, j: (i, j)),
        out_shape=jax.ShapeDtypeStruct((rows.n, n_cols), F32),
        compiler_params=_cparams(2),
        name="conv_silu",
    )(p, w, b.reshape(b.shape[0], 1, b.shape[1]))


def _hyena_conv_kernel(p0_ref, p1_ref, p2_ref, w0_ref, w1_ref, w2_ref, b0_ref, b1_ref, b2_ref,
                       x0_ref, u_ref, *, period):
    x0_ref[...] = _conv3(p0_ref[...], w0_ref, b0_ref, period)
    x1 = _conv3(p1_ref[...], w1_ref, b1_ref, period)
    v = _conv3(p2_ref[...], w2_ref, b2_ref, period)
    u_ref[...] = (x1 * v).astype(u_ref.dtype)


def hyena_conv(p, w, b, widx, *, row0, bsz, seq, period, bt=256, bc=512):
    nt = seq // bt
    ncb = HY_D // bc
    rb0 = row0 // bt

    def pspec(k):
        return pl.BlockSpec((bt, bc), lambda bb, i, j: (rb0 + bb * nt + i, j + k * ncb))

    def wspec(k):
        return pl.BlockSpec((None, 3, bc), lambda bb, i, j: (widx, 0, j + k * ncb))

    def bspec(k):
        return pl.BlockSpec((None, 1, bc), lambda bb, i, j: (widx, 0, j + k * ncb))

    ospec = pl.BlockSpec((bt, bc), lambda bb, i, j: (i, bb * ncb + j))
    b3 = b.reshape(b.shape[0], 1, b.shape[1])
    return pl.pallas_call(
        functools.partial(_hyena_conv_kernel, period=period),
        grid=(bsz, nt, ncb),
        in_specs=[pspec(0), pspec(1), pspec(2), wspec(0), wspec(1), wspec(2), bspec(0), bspec(1), bspec(2)],
        out_specs=[ospec, ospec],
        out_shape=[jax.ShapeDtypeStruct((seq, bsz * HY_D), F32),
                   jax.ShapeDtypeStruct((seq, bsz * HY_D), BF16)],
        compiler_params=_cparams(3),
        name="hyena_conv",
    )(p, p, p, w, w, w, b3, b3, b3)


def _hyena_feats(L):
    pos = np.arange(L, dtype=np.float64)
    t = pos / max(L - 1, 1)
    n_bands = (HY_EMB - 1) // 2
    bands = np.linspace(1e-4, n_bands - 1, n_bands)
    ang = (2 * math.pi / L) * pos[:, None] * bands[None, :]
    feats = np.concatenate([t[:, None], np.cos(ang), -np.sin(ang)], axis=-1)
    feats = np.pad(feats, ((0, 0), (0, LANES - HY_EMB)))
    deltas = np.abs(np.linspace(math.log(HY_DECAY_TARGET) / HY_SLOW_DECAY,
                                math.log(HY_DECAY_TARGET) / HY_FAST_DECAY, HY_D))
    return feats.astype(np.float32), t.astype(np.float32)[:, None], deltas.astype(np.float32)[None, :]


def _hyena_filter_kernel(feats_ref, t_ref, dl_ref, w1_ref, b1_ref, w2_ref, b2_ref, w3f_ref, w3b_ref,
                         fq_ref, hf_ref, hb_ref, hid_ref):
    @pl.when(pl.program_id(0) == 0)
    def _():
        fq = fq_ref[...]
        h1 = jnp.sin(fq[0:1] * (_dot_hi(feats_ref[...], w1_ref[...]) + b1_ref[...]))
        hid_ref[...] = jnp.sin(fq[1:2] * (_dot_hi(h1, w2_ref[...]) + b2_ref[...]))

    h = hid_ref[...]
    win = jnp.exp(-t_ref[...] * dl_ref[...])
    h_f = _dot_hi(h, w3f_ref[...]) * win
    h_b = _dot_hi(h, w3b_ref[...]) * win
    row = lax.broadcasted_iota(jnp.int32, h_b.shape, 0)
    h_b = jnp.where(row == 0, 0.0, h_b)
    l1 = jnp.sum(jnp.abs(h_f), axis=0, keepdims=True) + jnp.sum(jnp.abs(h_b), axis=0, keepdims=True)
    hf_ref[...] = h_f / l1
    hb_ref[...] = h_b / l1


def hyena_filter(L, w1, b1, w2, b2, w3, freq, widx, bc=256):
    feats, t, deltas = _hyena_feats(L)
    hid = w2.shape[1]
    w1p = jnp.pad(w1[widx], ((0, LANES - HY_EMB), (0, 0)))
    ncb = HY_D // bc
    full = lambda shape: pl.BlockSpec(shape, lambda j: (0,) * len(shape))
    return pl.pallas_call(
        _hyena_filter_kernel,
        grid=(ncb,),
        in_specs=[full((L, LANES)), full((L, 1)), pl.BlockSpec((1, bc), lambda j: (0, j)),
                  full((LANES, hid)), full((1, hid)), full((hid, hid)), full((1, hid)),
                  pl.BlockSpec((hid, bc), lambda j: (0, j)),
                  pl.BlockSpec((hid, bc), lambda j: (0, j + ncb)),
                  full((2, hid))],
        out_specs=[pl.BlockSpec((L, bc), lambda j: (0, j)), pl.BlockSpec((L, bc), lambda j: (0, j))],
        out_shape=[jax.ShapeDtypeStruct((L, HY_D), F32), jax.ShapeDtypeStruct((L, HY_D), F32)],
        scratch_shapes=[pltpu.VMEM((L, hid), F32)],
        compiler_params=_cparams(1),
        name="hyena_filter",
    )(jnp.asarray(feats), jnp.asarray(t), jnp.asarray(deltas), w1p, b1[widx][None], w2[widx],
      b2[widx][None], w3[widx], w3[widx], freq[widx])


def _dft_matrix(L):
    f = np.arange(L, dtype=np.int64)[:, None]
    s = np.arange(L, dtype=np.int64)[None, :]
    ang = (math.pi / L) * ((f * s) % (2 * L)).astype(np.float64)
    a_cos = np.cos(ang)
    a_sin = -np.sin(ang)
    a_sin[0, :] = np.where(np.arange(L) % 2 == 0, 1.0, -1.0)
    return np.concatenate([a_cos, a_sin], axis=0).astype(np.float32)


def _spectrum_kernel(ac_ref, as_ref, hf_ref, hb_ref, kre_ref, kim_ref, kny_ref, *, L, bf):
    hf = hf_ref[...].astype(BF16)
    hb = hb_ref[...].astype(BF16)
    ac = ac_ref[...].astype(BF16)
    a_s = as_ref[...].astype(BF16)
    cf, cb = _dot(ac, hf), _dot(ac, hb)
    sf, sb = _dot(a_s, hf), _dot(a_s, hb)
    f = lax.broadcasted_iota(jnp.int32, cf.shape, 0) + pl.program_id(1) * bf
    wgt = jnp.where(f == 0, 0.5 / L, 1.0 / L)
    kre_ref[...] = (cf + cb) * wgt
    kim_ref[...] = jnp.where(f == 0, 0.0, (sf - sb) * wgt)
    kny_ref[...] = jnp.where(f == 0, (sf + sb) * wgt, (cf + cb) * wgt)


def hyena_spectrum(a_mat, hf, hb, L, bf=256, bc=512):
    bf = min(bf, L)
    nf = L // bf
    C = hf.shape[1]
    ospec = pl.BlockSpec((bf, bc), lambda c, i: (i, c))
    hspec = pl.BlockSpec((L, bc), lambda c, i: (0, c))
    return pl.pallas_call(
        functools.partial(_spectrum_kernel, L=L, bf=bf),
        grid=(C // bc, nf),
        in_specs=[pl.BlockSpec((bf, L), lambda c, i: (i, 0)), pl.BlockSpec((bf, L), lambda c, i: (i + nf, 0)),
                  hspec, hspec],
        out_specs=[ospec, ospec, ospec],
        out_shape=[jax.ShapeDtypeStruct((L, C), F32)] * 3,
        compiler_params=_cparams(2),
        name="hyena_spectrum",
    )(a_mat, a_mat, hf, hb)


def _lc_fwd_kernel(ac_ref, as_ref, u_ref, kre_ref, kim_ref, kny_ref, yre_ref, yim_ref, acb_ref, asb_ref):
    @pl.when(pl.program_id(1) == 0)
    def _():
        acb_ref[...] = ac_ref[...].astype(BF16)
        asb_ref[...] = as_ref[...].astype(BF16)
    u = u_ref[...]
    ure = _dot(acb_ref[...], u)
    uim = _dot(asb_ref[...], u)
    kim = kim_ref[...]
    yre_ref[...] = (kre_ref[...] * ure - kim * uim).astype(yre_ref.dtype)
    yim_ref[...] = (kny_ref[...] * uim + kim * ure).astype(yim_ref.dtype)


def long_conv_fwd(a_mat, u, kre, kim, kny, L, bsz, bf=512):
    bf = min(bf, L)
    nf = L // bf
    C = kre.shape[1]
    kspec = pl.BlockSpec((bf, C), lambda i, b: (i, 0))
    ospec = pl.BlockSpec((bf, C), lambda i, b: (i, b))
    return pl.pallas_call(
        _lc_fwd_kernel,
        grid=(nf, bsz),
        in_specs=[pl.BlockSpec((bf, L), lambda i, b: (i, 0)), pl.BlockSpec((bf, L), lambda i, b: (i + nf, 0)),
                  pl.BlockSpec((L, C), lambda i, b: (0, b)), kspec, kspec, kspec],
        out_specs=[ospec, ospec],
        out_shape=[jax.ShapeDtypeStruct((L, bsz * C), BF16)] * 2,
        scratch_shapes=[pltpu.VMEM((bf, L), BF16), pltpu.VMEM((bf, L), BF16)],
        compiler_params=_cparams(2),
        name="long_conv_fwd",
    )(a_mat, a_mat, u, kre, kim, kny)


def _lc_inv_kernel(atc_ref, ats_ref, yre_ref, yim_ref, x0_ref, u_ref, bias_ref, o_ref, atcb_ref, atsb_ref):
    @pl.when(pl.program_id(1) == 0)
    def _():
        atcb_ref[...] = atc_ref[...].astype(BF16)
        atsb_ref[...] = ats_ref[...].astype(BF16)
    y = _dot(atcb_ref[...], yre_ref[...]) + _dot(atsb_ref[...], yim_ref[...])
    u = u_ref[...].astype(F32)
    o_ref[...] = (x0_ref[...] * (y + bias_ref[...] * u)).astype(o_ref.dtype)


def long_conv_inv(at_mat, yre, yim, x0, u, bias, widx, L, bsz, bt=512):
    bt = min(bt, L)
    nt = L // bt
    C = bias.shape[1]
    tspec = pl.BlockSpec((bt, C), lambda i, b: (i, b))
    yspec = pl.BlockSpec((L, C), lambda i, b: (0, b))
    return pl.pallas_call(
        _lc_inv_kernel,
        grid=(nt, bsz),
        in_specs=[pl.BlockSpec((bt, L), lambda i, b: (i, 0)), pl.BlockSpec((bt, L), lambda i, b: (i, 1)),
                  yspec, yspec, tspec, tspec, pl.BlockSpec((None, 1, C), lambda i, b: (widx, 0, 0))],
        out_specs=tspec,
        out_shape=jax.ShapeDtypeStruct((L, bsz * C), BF16),
        scratch_shapes=[pltpu.VMEM((bt, L), BF16), pltpu.VMEM((bt, L), BF16)],
        compiler_params=_cparams(2),
        name="long_conv_inv",
    )(at_mat, at_mat, yre, yim, x0, u, bias.reshape(bias.shape[0], 1, C))


def hyena(p, L, bsz, row0, period, widx, conv_w, conv_b, filt, bias):
    a_np = _dft_matrix(L)
    a_mat = jnp.asarray(a_np)
    at_mat = jnp.asarray(np.ascontiguousarray(a_np.T))
    x0, u = hyena_conv(p, conv_w, conv_b, widx, row0=row0, bsz=bsz, seq=L, period=period, bt=min(1024, L))
    hf, hb = hyena_filter(L, *filt, widx)
    kre, kim, kny = hyena_spectrum(a_mat, hf, hb, L)
    yre, yim = long_conv_fwd(a_mat, u, kre, kim, kny, L, bsz)
    return long_conv_inv(at_mat, yre, yim, x0, u, bias, widx, L, bsz)


def _pick_col(g, idx):
    lane = lax.broadcasted_iota(jnp.int32, g.shape, 1)
    return jnp.sum(jnp.where(lane == idx, g, 0.0), axis=1, keepdims=True)


def _chunk_masks(d, n):
    t_i = lax.broadcasted_iota(jnp.int32, (n, n), 0)
    s_i = lax.broadcasted_iota(jnp.int32, (n, n), 1)
    lag = (t_i - s_i) * jnp.where(d == 0, 1, -1)
    return lag >= 0, lag <= 0


def _running_sums(x_cols, x_rows, causal, causal_t):
    tri = jnp.where(causal, 1.0, 0.0)
    tri_t = jnp.where(causal_t, 1.0, 0.0)
    return _dot_hi(tri, x_cols), _dot_hi(x_rows, tri_t)


def _mlstm_kernel(q_ref, k_ref, v_ref, gc_ref, gr_ref, bc_ref, br_ref, o_ref, c_ref, n_ref, m_ref, gs_ref):
    d = pl.program_id(1)

    @pl.when(pl.program_id(2) == 0)
    def _():
        c_ref[...] = jnp.zeros_like(c_ref)
        n_ref[...] = jnp.zeros_like(n_ref)
        m_ref[...] = jnp.zeros_like(m_ref)

    gcol = gc_ref[...] + bc_ref[...]
    lane = lax.broadcasted_iota(jnp.int32, gcol.shape, 1)
    gcol = jnp.where((lane & ML_H) != 0, _log_sigmoid(gcol), gcol)
    grow = gr_ref[...] + br_ref[...]
    sub = lax.broadcasted_iota(jnp.int32, grow.shape, 0)
    grow = jnp.where((sub & ML_H) != 0, _log_sigmoid(grow), grow)

    causal, causal_t = _chunk_masks(d, CHUNK)
    ccol, crow = _running_sums(gcol, grow, causal, causal_t)
    gs_ref[0] = grow
    gs_ref[1] = crow

    def head(h):
        sl = pl.ds(h * ML_DH, ML_DH)
        return _mlstm_head(h, d, q_ref.at[:, sl], k_ref.at[:, sl], v_ref.at[:, sl], o_ref.at[:, sl], gcol, ccol,
                           gs_ref, causal, c_ref.at[h], n_ref.at[h], m_ref.at[h])

    _run_interleaved([head(h) for h in range(ML_H)])


def _run_interleaved(stages):
    live = list(stages)
    while live:
        still = []
        for gen in live:
            try:
                next(gen)
                still.append(gen)
            except StopIteration:
                pass
        live = still


def _mlstm_head(h, d, q_ref, k_ref, v_ref, o_ref, gcol, ccol, gs_ref, causal, c_ref, n_ref, m_ref):
    i_idx = 2 * ML_H * d + h
    f_idx = i_idx + ML_H
    ig_row = gs_ref[0, pl.ds(i_idx, 1), :]
    ig_col = _pick_col(gcol, i_idx)
    bcum_row = gs_ref[1, pl.ds(f_idx, 1), :]
    bcum_col = _pick_col(ccol, f_idx)
    g = jnp.sum(gs_ref[0, pl.ds(f_idx, 1), :], axis=1, keepdims=True)

    q = q_ref[...] * (ML_DH ** -0.5)
    k = k_ref[...]
    qb, kb, vb = q.astype(BF16), k.astype(BF16), v_ref[...].astype(BF16)
    c_in, n_in, m_in = c_ref[...], n_ref[...], m_ref[0:1, 0:1]

    dmat = jnp.where(causal, bcum_col - bcum_row + ig_row, -jnp.inf)
    m_inter = bcum_col + m_in
    m_t = jnp.maximum(jnp.max(dmat, axis=1, keepdims=True), m_inter)
    decay = jnp.exp(dmat - m_t)
    w_inter = jnp.exp(m_inter - m_t)
    a_row = g - bcum_row + ig_row
    a_col = g - bcum_col + ig_col
    m_loc = jnp.max(a_row, axis=1, keepdims=True)
    kw = k * jnp.exp(a_col - m_loc)
    qk = _dot_nt(qb, kb)
    qc = _dot(qb, c_in.astype(BF16))
    c_loc = _dot_tn(kw.astype(BF16), vb)
    yield

    s = qk * decay
    sv = _dot(s.astype(BF16), vb)
    den = jnp.sum(s, axis=1, keepdims=True) + w_inter * jnp.sum(q * n_in, axis=1, keepdims=True)
    yield

    num = sv + w_inter * qc
    o_ref[...] = num / jnp.maximum(jnp.abs(den), jnp.exp(-m_t))
    n_loc = jnp.sum(kw, axis=0, keepdims=True)
    m_new = jnp.maximum(g + m_in, m_loc)
    s_prev = jnp.exp(g + m_in - m_new)
    s_loc = jnp.exp(m_loc - m_new)
    c_ref[...] = s_prev * c_in + s_loc * c_loc
    n_ref[...] = s_prev * n_in + s_loc * n_loc
    m_ref[...] = jnp.broadcast_to(m_new, m_ref.shape)


def mlstm(qk, p, v_col_off, gates, gates_t, gate_b, rows):
    nct = (rows.L + rows.Lc) // CHUNK
    W = ML_H * ML_DH
    voff = v_col_off // W

    def rb(b, d, j):
        return rows.chunk_block(b, rows.scan_chunk(d, j))

    ng = 4 * ML_H
    bias = gate_b.reshape(ng)
    bias_row = jnp.pad(bias, (0, LANES - ng))[None]
    return pl.pallas_call(
        _mlstm_kernel,
        grid=(rows.B, 2, nct),
        in_specs=[pl.BlockSpec((CHUNK, W), lambda b, d, j: (rb(b, d, j), 0)),
                  pl.BlockSpec((CHUNK, W), lambda b, d, j: (rb(b, d, j), 1)),
                  pl.BlockSpec((CHUNK, W), lambda b, d, j: (rb(b, d, j), voff)),
                  pl.BlockSpec((CHUNK, LANES), lambda b, d, j: (rb(b, d, j), 0)),
                  pl.BlockSpec((ng, CHUNK), lambda b, d, j: (0, rb(b, d, j))),
                  pl.BlockSpec((1, LANES), lambda b, d, j: (0, 0)),
                  pl.BlockSpec((ng, 1), lambda b, d, j: (0, 0))],
        out_specs=pl.BlockSpec((None, CHUNK, W), lambda b, d, j: (d, rb(b, d, j), 0)),
        out_shape=jax.ShapeDtypeStruct((2, rows.n, W), F32),
        scratch_shapes=[pltpu.VMEM((ML_H, ML_DH, ML_DH), F32), pltpu.VMEM((ML_H, 1, ML_DH), F32),
                        pltpu.VMEM((ML_H, 8, LANES), F32), pltpu.VMEM((2, ng, CHUNK), F32)],
        compiler_params=_cparams(3),
        name="mlstm",
    )(qk, qk, p, gates, gates_t, bias_row, bias[:, None])


def _finish_ab_kernel(yl_ref, yc_ref, h_ref, o_ref, nw_ref, out_ref, *, n_lat_blocks):
    @pl.when(pl.program_id(0) < n_lat_blocks)
    def _():
        out_ref[:, 0:HY_D] = yl_ref[...]

    @pl.when(pl.program_id(0) >= n_lat_blocks)
    def _():
        out_ref[:, 0:HY_D] = yc_ref[...]

    hs = h_ref[0] + h_ref[1]
    og = _sigmoid(o_ref[...])
    nw = nw_ref[...]
    for i in range(ML_H):
        sl = slice(i * ML_DH, (i + 1) * ML_DH)
        x = hs[:, sl]
        mu = jnp.mean(x, axis=-1, keepdims=True)
        xc = x - mu
        var = jnp.mean(xc * xc, axis=-1, keepdims=True)
        y = xc * lax.rsqrt(var + EPS) * nw[:, sl] * og[:, sl]
        out_ref[:, HY_D + i * ML_DH:HY_D + (i + 1) * ML_DH] = y.astype(out_ref.dtype)


def finish_ab(yh_lat, yh_ctx, hdir, p, o_col_off, norm_w, widx, rows, bt=256):
    n_lat_blocks = rows.n_lat // bt
    per_b = rows.L // bt
    ooff = o_col_off // (ML_H * ML_DH)
    W = ML_H * ML_DH

    assert rows.Lc == bt
    n_blocks = rows.n // bt

    def lat_map(i):
        j = jnp.minimum(i, n_lat_blocks - 1)
        return (j % per_b, j // per_b)

    return pl.pallas_call(
        functools.partial(_finish_ab_kernel, n_lat_blocks=n_lat_blocks),
        grid=(n_blocks,),
        in_specs=[pl.BlockSpec((bt, HY_D), lat_map),
                  pl.BlockSpec((bt, HY_D), lambda i: (0, jnp.maximum(i - n_lat_blocks, 0))),
                  pl.BlockSpec((2, bt, W), lambda i: (0, i, 0)),
                  pl.BlockSpec((bt, W), lambda i: (i, ooff)),
                  pl.BlockSpec((None, 1, W), lambda i: (widx, 0, 0))],
        out_specs=pl.BlockSpec((bt, HY_D + W), lambda i: (i, 0)),
        out_shape=jax.ShapeDtypeStruct((rows.n, HY_D + W), BF16),
        compiler_params=_cparams(1),
        name="finish_ab",
    )(yh_lat, yh_ctx, hdir, p, norm_w.reshape(norm_w.shape[0], 1, W))


def _ssd_kernel(x_ref, b_ref, c_ref, dc_ref, dr_ref, pr_ref, pc_ref, o_ref, h_ref, ds_ref):
    d = pl.program_id(1)

    @pl.when(pl.program_id(2) == 0)
    def _():
        h_ref[...] = jnp.zeros_like(h_ref)

    dt_cols = _softplus(dc_ref[...] + pr_ref[1:2, :])
    la_cols = dt_cols * -jnp.exp(pr_ref[0:1, :])
    dt_rows = _softplus(dr_ref[...] + pc_ref[:, 1:2])
    la_rows = dt_rows * -jnp.exp(pc_ref[:, 0:1])

    causal, causal_t = _chunk_masks(d, CHUNK)
    acum_cols, acum_rows = _running_sums(la_cols, la_rows, causal, causal_t)
    ds_ref[0] = dt_rows
    ds_ref[1] = la_rows
    ds_ref[2] = acum_rows
    n_groups = b_ref.shape[1] // LANES
    heads_per_group = SSD_H // n_groups
    bmats = [b_ref[:, g * LANES:(g + 1) * LANES].astype(BF16) for g in range(n_groups)]
    cmats = [c_ref[:, g * LANES:(g + 1) * LANES].astype(BF16) for g in range(n_groups)]
    cbs = [_dot_nt(cmats[g], bmats[g]) for g in range(n_groups)]

    def head(hd):
        grp = hd // heads_per_group
        idx = SSD_H * d + hd
        hs = pl.ds(hd * SSD_P, SSD_P)
        dt_row = ds_ref[0, pl.ds(idx, 1), :]
        dt_col = _pick_col(dt_cols, idx)
        acum_col = _pick_col(acum_cols, idx)
        acum_row = ds_ref[2, pl.ds(idx, 1), :]
        tot = jnp.sum(ds_ref[1, pl.ds(idx, 1), :], axis=1, keepdims=True)
        decay = jnp.exp(jnp.where(causal, acum_col - acum_row, -jnp.inf))
        xh = x_ref[:, hs]
        h_in = h_ref[hd]
        xw = xh * (jnp.exp(tot - acum_col) * dt_col)
        y_off = _dot_nt(cmats[grp], h_in.astype(BF16))
        st = _dot_tn(xw.astype(BF16), bmats[grp])
        y_diag = _dot((cbs[grp] * decay * dt_row).astype(BF16), xh.astype(BF16))
        yield
        o_ref[:, hs] = y_diag + y_off * jnp.exp(acum_col)
        h_ref[hd] = jnp.exp(tot) * h_in + st

    _run_interleaved([head(hd) for hd in range(SSD_H)])


def ssd(xbc, dt, dt_t, a_log, dt_bias, rows):
    nct = (rows.L + rows.Lc) // CHUNK
    nd = SSD_H * SSD_P
    gw = 2 * LANES

    def rb(b, d, j):
        return rows.chunk_block(b, rows.scan_chunk(d, j))

    nh = 2 * SSD_H
    par = jnp.stack([a_log.reshape(nh), dt_bias.reshape(nh)])
    par_rows = jnp.pad(par, ((0, 0), (0, LANES - nh)))
    return pl.pallas_call(
        _ssd_kernel,
        grid=(rows.B, 2, nct),
        in_specs=[pl.BlockSpec((CHUNK, nd), lambda b, d, j: (rb(b, d, j), 0)),
                  pl.BlockSpec((CHUNK, gw), lambda b, d, j: (rb(b, d, j), nd // gw)),
                  pl.BlockSpec((CHUNK, gw), lambda b, d, j: (rb(b, d, j), nd // gw + 1)),
                  pl.BlockSpec((CHUNK, LANES), lambda b, d, j: (rb(b, d, j), 0)),
                  pl.BlockSpec((nh, CHUNK), lambda b, d, j: (0, rb(b, d, j))),
                  pl.BlockSpec((2, LANES), lambda b, d, j: (0, 0)),
                  pl.BlockSpec((nh, 2), lambda b, d, j: (0, 0))],
        out_specs=pl.BlockSpec((None, CHUNK, nd), lambda b, d, j: (d, rb(b, d, j), 0)),
        out_shape=jax.ShapeDtypeStruct((2, rows.n, nd), F32),
        scratch_shapes=[pltpu.VMEM((SSD_H, SSD_P, LANES), F32), pltpu.VMEM((3, nh, CHUNK), F32)],
        compiler_params=_cparams(3),
        name="ssd",
    )(xbc, xbc, xbc, dt, dt_t, par_rows, par.T)


S5_TC = 64
S5_JB = 2
S5_NS = S5_G * S5_P


def _s5_kernel(*refs, bsz):
    uf_refs, ub_refs = refs[:bsz], refs[bsz:2 * bsz]
    w_ref, c_ref, lre_ref, lim_ref, sf_ref, sb_ref = refs[2 * bsz:2 * bsz + 6]
    z_ref, s_ref, wb_ref, cb_ref, u_ref, y_ref = refs[2 * bsz + 6:]

    @pl.when(pl.program_id(0) == 0)
    def _():
        s_ref[...] = jnp.zeros_like(s_ref)
        wb_ref[...] = w_ref[...].astype(BF16)
        cb_ref[...] = c_ref[...].astype(BF16)

    r_i = lax.broadcasted_iota(jnp.int32, (S5_TC, S5_TC), 0)
    c_i = lax.broadcasted_iota(jnp.int32, (S5_TC, S5_TC), 1)
    flip = jnp.where(r_i + c_i == S5_TC - 1, 1.0, 0.0)
    group = 2 * bsz
    n_blk = S5_W // LANES
    sw = S5_NS // n_blk
    for b in range(bsz):
        uf = uf_refs[b][...]
        ub = _dot(flip.astype(BF16), ub_refs[b][...].astype(BF16))
        for j in range(n_blk):
            u_ref[j, pl.ds(b, S5_TC, stride=group), :] = uf[:, j * LANES:(j + 1) * LANES]
            u_ref[j, pl.ds(bsz + b, S5_TC, stride=group), :] = ub[:, j * LANES:(j + 1) * LANES]

    for j in range(n_blk):
        z_ref[:, 2 * j * sw:2 * (j + 1) * sw] = _dot(u_ref[j].astype(BF16), wb_ref[j])

    for j0 in range(0, n_blk, S5_JB):
        blocks = range(j0, j0 + S5_JB)
        re_sl = [slice(2 * j * sw, (2 * j + 1) * sw) for j in blocks]
        im_sl = [slice((2 * j + 1) * sw, (2 * j + 2) * sw) for j in blocks]
        ar = [lre_ref[:, j * sw:(j + 1) * sw] for j in blocks]
        ai = [lim_ref[:, j * sw:(j + 1) * sw] for j in blocks]

        def step(t, carry):
            r0 = pl.multiple_of(t * 8, 8)
            new = []
            for k in range(S5_JB):
                zr, zi = carry[2 * k], carry[2 * k + 1]
                nr = ar[k] * zr - ai[k] * zi + z_ref[pl.ds(r0, 8), re_sl[k]]
                ni = ar[k] * zi + ai[k] * zr + z_ref[pl.ds(r0, 8), im_sl[k]]
                z_ref[pl.ds(r0, 8), re_sl[k]] = nr
                z_ref[pl.ds(r0, 8), im_sl[k]] = ni
                new += [nr, ni]
            return tuple(new)

        init = []
        for k in range(S5_JB):
            init += [s_ref[:, re_sl[k]], s_ref[:, im_sl[k]]]
        fin = lax.fori_loop(0, S5_TC, step, tuple(init))
        for k in range(S5_JB):
            s_ref[:, re_sl[k]] = fin[2 * k]
            s_ref[:, im_sl[k]] = fin[2 * k + 1]

    row = lax.broadcasted_iota(jnp.int32, (S5_TC * 8, LANES), 0)
    is_bwd = (row & 4) != 0
    for j in range(n_blk):
        yy = _dot(z_ref[:, 2 * j * sw:2 * (j + 1) * sw].astype(BF16), cb_ref[j])
        y_ref[j] = jnp.where(is_bwd, yy[:, LANES:], yy[:, :LANES])
    for b in range(bsz):
        for j in range(n_blk):
            ls = slice(j * LANES, (j + 1) * LANES)
            sf_ref[b, :, ls] = y_ref[j, pl.ds(b, S5_TC, stride=group), :]
            sb_ref[b, :, ls] = _dot_hi(flip, y_ref[j, pl.ds(bsz + b, S5_TC, stride=group), :])


def s5_params(lam_re, lam_im, log_dt, b_re, b_im, c_re, c_im):
    dt = jnp.exp(log_dt)[..., None]
    mag = jnp.exp(lam_re * dt)
    lb_re, lb_im = mag * jnp.cos(lam_im * dt), mag * jnp.sin(lam_im * dt)
    den = lam_re * lam_re + lam_im * lam_im
    f_re = ((lb_re - 1) * lam_re + lb_im * lam_im) / den
    f_im = (lb_im * lam_re - (lb_re - 1) * lam_im) / den
    cf_re = c_re[None] * f_re[:, :, None, :] - c_im[None] * f_im[:, :, None, :]
    cf_im = c_re[None] * f_im[:, :, None, :] + c_im[None] * f_re[:, :, None, :]
    gpb = LANES // S5_GS
    n_blk = S5_G // gpb
    eye = jnp.eye(gpb, dtype=F32)

    def w_blocks(b):
        bb = b.reshape(n_blk, gpb, S5_P, S5_GS)
        return jnp.einsum('jgpi,gh->jgihp', bb, eye).reshape(n_blk, gpb * S5_GS, gpb * S5_P)

    def c_blocks(c):
        cc = c.reshape(2, n_blk, gpb, S5_GS, S5_P)
        return jnp.einsum('djgip,gh->jgpdhi', cc, eye).reshape(n_blk, gpb * S5_P, 2 * gpb * S5_GS)

    def lam_rows(l, bsz):
        return jnp.repeat(l.reshape(2, S5_NS), bsz, axis=0)

    w_cat = jnp.concatenate([w_blocks(b_re), w_blocks(b_im)], axis=2)
    c_cat = jnp.concatenate([c_blocks(cf_re), -c_blocks(cf_im)], axis=1)
    return w_cat, c_cat, lb_re, lb_im, lam_rows


def s5(p, w_cat, c_cat, lre, lim, rows):
    bsz, L, Lc = rows.B, rows.L, rows.Lc
    assert bsz == 4
    n_blk = S5_W // LANES
    sw = S5_NS // n_blk
    rt = S5_TC * 2 * bsz
    ncc, ncl = Lc // S5_TC, L // S5_TC
    lat0 = rows.n_lat // S5_TC

    def fwd_block(b):
        return lambda i: (jnp.where(i < ncc, lat0 + b * ncc + i, b * ncl + i - ncc), 0)

    def bwd_block(b):
        return lambda i: (jnp.where(i < ncc, lat0 + b * ncc + ncc - 1 - i, b * ncl + ncl - 1 + ncc - i), 0)

    full = lambda shape: pl.BlockSpec(shape, lambda i: (0,) * len(shape))
    u_specs = ([pl.BlockSpec((S5_TC, S5_W), fwd_block(b)) for b in range(bsz)]
               + [pl.BlockSpec((S5_TC, S5_W), bwd_block(b)) for b in range(bsz)])
    out_specs = [pl.BlockSpec((bsz, S5_TC, S5_W), lambda i: (0, jnp.maximum(i - ncc, 0), 0)),
                 pl.BlockSpec((bsz, S5_TC, S5_W), lambda i: (0, jnp.minimum(ncl - 1 + ncc - i, ncl - 1), 0))]
    return pl.pallas_call(
        functools.partial(_s5_kernel, bsz=bsz),
        grid=((L + Lc) // S5_TC,),
        in_specs=u_specs + [full((n_blk, LANES, 2 * sw)), full((n_blk, 2 * sw, 2 * LANES)),
                            full((2 * bsz, S5_NS)), full((2 * bsz, S5_NS))],
        out_specs=out_specs,
        out_shape=[jax.ShapeDtypeStruct((bsz, L, S5_W), F32)] * 2,
        scratch_shapes=[pltpu.VMEM((rt, 2 * S5_NS), F32), pltpu.VMEM((2 * bsz, 2 * S5_NS), F32),
                        pltpu.VMEM((n_blk, LANES, 2 * sw), BF16), pltpu.VMEM((n_blk, 2 * sw, 2 * LANES), BF16),
                        pltpu.VMEM((n_blk, rt, LANES), F32), pltpu.VMEM((n_blk, rt, LANES), F32)],
        compiler_params=_cparams(1),
        name="s5",
    )(*([p] * (2 * bsz)), w_cat, c_cat, lre, lim)


def _finish_cd_kernel(y_ref, xs_ref, z_ref, sf_ref, sb_ref, u_ref, dssd_ref, nw_ref, d5_ref, glu_ref, out_ref):
    y = (y_ref[0] + y_ref[1] + dssd_ref[...] * xs_ref[...]) * _silu(z_ref[...])
    y = y * lax.rsqrt(jnp.mean(y * y, axis=-1, keepdims=True) + EPS) * nw_ref[...]
    nd = y.shape[1]
    out_ref[:, 0:nd] = y.astype(out_ref.dtype)
    s = sf_ref[...] + sb_ref[...] + d5_ref[...] * u_ref[...]
    s = 0.5 * s * (1.0 + lax.erf(s * (2.0 ** -0.5)))
    gl = _dot(s.astype(BF16), glu_ref[...].astype(BF16))
    out_ref[:, nd:nd + S5_W] = (gl[:, 0:S5_W] * _sigmoid(gl[:, S5_W:])).astype(out_ref.dtype)


def finish_cd(ydir, xbc, p1, sf, sb, p2, ssd_d_lanes, norm_w, s5_d, glu_w, widx, n_rows, bt=512):
    nd = SSD_H * SSD_P
    vec = lambda n: pl.BlockSpec((None, 1, n), lambda i: (widx, 0, 0))
    return pl.pallas_call(
        _finish_cd_kernel,
        grid=(n_rows // bt,),
        in_specs=[pl.BlockSpec((2, bt, nd), lambda i: (0, i, 0)),
                  pl.BlockSpec((bt, nd), lambda i: (i, 0)),
                  pl.BlockSpec((bt, nd), lambda i: (i, 0)),
                  pl.BlockSpec((bt, S5_W), lambda i: (i, 0)),
                  pl.BlockSpec((bt, S5_W), lambda i: (i, 0)),
                  pl.BlockSpec((bt, S5_W), lambda i: (i, 0)),
                  vec(nd), vec(nd), vec(S5_W),
                  pl.BlockSpec((None, S5_W, 2 * S5_W), lambda i: (widx, 0, 0))],
        out_specs=pl.BlockSpec((bt, nd + S5_W), lambda i: (i, 0)),
        out_shape=jax.ShapeDtypeStruct((n_rows, nd + S5_W), BF16),
        compiler_params=_cparams(1),
        name="finish_cd",
    )(ydir, xbc, p1, sf, sb, p2, ssd_d_lanes, norm_w.reshape(norm_w.shape[0], 1, nd),
      s5_d.reshape(s5_d.shape[0], 1, S5_W), glu_w)


def _router_kernel(x_ref, sh_ref, sc_ref, w_ref, h_ref, o_ref):
    x = x_ref[...]
    h = x * lax.rsqrt(jnp.mean(x * x, axis=-1, keepdims=True) + EPS) * (1.0 + sc_ref[...]) + sh_ref[...]
    h_ref[...] = h
    logits = _dot_hi(h, w_ref[...])
    lane = lax.broadcasted_iota(jnp.int32, logits.shape, 1)
    logits = jnp.where(lane < N_EXPERTS, logits, -jnp.inf)
    m1 = jnp.max(logits, axis=1, keepdims=True)
    i1 = jnp.min(jnp.where(logits == m1, lane, LANES), axis=1, keepdims=True)
    rest = jnp.where(lane == i1, -jnp.inf, logits)
    m2 = jnp.max(rest, axis=1, keepdims=True)
    i2 = jnp.min(jnp.where(rest == m2, lane, LANES), axis=1, keepdims=True)
    e2 = jnp.exp(m2 - m1)
    p1 = 1.0 / (1.0 + e2)
    p2 = e2 / (1.0 + e2)
    o_ref[...] = jnp.where(lane == 0, i1.astype(F32),
                           jnp.where(lane == 1, i2.astype(F32),
                                     jnp.where(lane == 2, p1, jnp.where(lane == 3, p2, 0.0))))


def modulate_route(x, shift, scale, w_router_padded, rows, bt=512):
    T, D = x.shape
    mod_spec = pl.BlockSpec((None, 1, D), lambda i: (rows.mod_row(i, bt), 0, 0))
    return pl.pallas_call(
        _router_kernel,
        grid=(T // bt,),
        in_specs=[pl.BlockSpec((bt, D), lambda i: (i, 0)), mod_spec, mod_spec,
                  pl.BlockSpec((D, LANES), lambda i: (0, 0))],
        out_specs=[pl.BlockSpec((bt, D), lambda i: (i, 0)), pl.BlockSpec((bt, LANES), lambda i: (i, 0))],
        out_shape=[jax.ShapeDtypeStruct((T, D), F32), jax.ShapeDtypeStruct((T, LANES), F32)],
        compiler_params=_cparams(1),
        name="modulate_route",
    )(x, shift, scale, w_router_padded)


def _gather_kernel(idx_ref, nt_ref, src_ref, o_ref, buf_ref, sem, *, bm, s):
    i = pl.program_id(0)
    n_used = nt_ref[0]
    slot = i % 2

    def row_copy(tile, slot_, r):
        return pltpu.make_async_copy(src_ref.at[idx_ref[tile * bm + r]],
                                     buf_ref.at[slot_, pl.ds(pl.multiple_of(r * s, s), s), :], sem.at[slot_])

    def issue(tile, slot_):
        def body(r2, c):
            row_copy(tile, slot_, 2 * r2).start(priority=0)
            row_copy(tile, slot_, 2 * r2 + 1).start(priority=1)
            return c
        lax.fori_loop(0, bm // 2, body, 0, unroll=4)

    @pl.when(i == 0)
    def _():
        issue(0, 0)

    @pl.when(i + 1 < n_used)
    def _():
        issue(i + 1, 1 - slot)

    @pl.when(i < n_used)
    def _():
        def body(r, c):
            row_copy(i, slot, r).wait()
            return c
        lax.fori_loop(0, bm, body, 0, unroll=8)
        for j in range(s):
            o_ref[:, j * LANES:(j + 1) * LANES] = buf_ref[slot, pl.ds(j, bm, stride=s), :].astype(o_ref.dtype)

    @pl.when(i >= n_used)
    def _():
        o_ref[...] = jnp.zeros_like(o_ref)


def gather_rows(src, idx, n_used_tiles, bm=256):
    N = idx.shape[0]
    T, D = src.shape
    s = D // LANES
    return pl.pallas_call(
        functools.partial(_gather_kernel, bm=bm, s=s),
        grid_spec=pltpu.PrefetchScalarGridSpec(
            num_scalar_prefetch=2,
            grid=(N // bm,),
            in_specs=[pl.BlockSpec(memory_space=pl.ANY)],
            out_specs=pl.BlockSpec((bm, D), lambda i, idx, nt: (i, 0)),
            scratch_shapes=[pltpu.VMEM((2, bm * s, LANES), F32), pltpu.SemaphoreType.DMA((2,))]),
        out_shape=jax.ShapeDtypeStruct((N, D), BF16),
        compiler_params=_cparams(1),
        name="gather_rows",
    )(idx, n_used_tiles, src.reshape(T, s, LANES))


def _gmm_kernel(hdr_ref, grp_ref, last_ref, rows_ref, x_ref, w_ref, o_ref, wf_ref, wb_ref, sem,
                *, widx, bn, nf, parts, bm):
    n = pl.program_id(0)
    m = pl.program_id(1)
    n_used, n_grp = hdr_ref[0], hdr_ref[1]
    total = nf * n_grp
    grp = grp_ref[m]
    order = n * n_grp + grp
    half = bm // 2

    def group_after(nq, gq):
        wrap = gq + 1 >= n_grp
        return jnp.where(wrap, nq + 1, nq), jnp.where(wrap, 0, gq + 1)

    def fetch(nq, gq):
        expert = hdr_ref[2 + gq]
        return [pltpu.make_async_copy(
            w_ref.at[widx, expert, :, pl.ds(pl.multiple_of((nq + p * nf) * bn, bn), bn)],
            wf_ref.at[p], sem.at[p]) for p in range(parts)]

    def rows_times_weights(r0, nr):
        x = x_ref[r0:r0 + nr, :]
        if parts == 2:
            o_ref[r0:r0 + nr, :] = (_silu(_dot(x, wb_ref[0])) * _dot(x, wb_ref[1])).astype(o_ref.dtype)
        else:
            o_ref[r0:r0 + nr, :] = _dot(x, wb_ref[0]).astype(o_ref.dtype)

    is_first = (m == 0) | (grp_ref[jnp.maximum(m - 1, 0)] != grp)
    is_last = last_ref[m] == 1

    @pl.when(is_first)
    def _():
        @pl.when((n == 0) & (m == 0))
        def _():
            for cp in fetch(n, grp):
                cp.start()

        for cp in fetch(n, grp):
            cp.wait()
        wb_ref[...] = wf_ref[...].astype(BF16)

        @pl.when(order + 1 < total)
        def _():
            for cp in fetch(*group_after(n, grp)):
                cp.start()

    @pl.when((m < n_used) & jnp.logical_not(is_last))
    def _():
        rows_times_weights(0, bm)

    @pl.when(is_last)
    def _():
        rows_times_weights(0, half)

        @pl.when(rows_ref[m] > half)
        def _():
            rows_times_weights(half, half)

        @pl.when(rows_ref[m] <= half)
        def _():
            o_ref[half:, :] = jnp.zeros((half, bn), o_ref.dtype)

    @pl.when(m >= n_used)
    def _():
        o_ref[...] = jnp.zeros_like(o_ref)


def gmm(x, w, widx, plan, *, bm, bn, swiglu):
    M, K = x.shape
    parts = 2 if swiglu else 1
    N = w.shape[3] // parts
    nf = N // bn
    return pl.pallas_call(
        functools.partial(_gmm_kernel, widx=widx, bn=bn, nf=nf, parts=parts, bm=bm),
        grid_spec=pltpu.PrefetchScalarGridSpec(
            num_scalar_prefetch=4,
            grid=(nf, M // bm),
            in_specs=[pl.BlockSpec((bm, K), lambda n, m, *_: (m, 0)),
                      pl.BlockSpec(memory_space=pl.ANY)],
            out_specs=pl.BlockSpec((bm, bn), lambda n, m, *_: (m, n)),
            scratch_shapes=[pltpu.VMEM((parts, K, bn), F32), pltpu.VMEM((parts, K, bn), BF16),
                            pltpu.SemaphoreType.DMA((parts,))]),
        out_shape=jax.ShapeDtypeStruct((M, N), BF16 if swiglu else F32),
        compiler_params=_cparams(2),
        name="gmm_swiglu" if swiglu else "gmm",
    )(plan.header, plan.tile_group, plan.tile_last, plan.tile_rows, x, w)


def _combine_kernel(pos_ref, y_ref, x_ref, pr_ref, g_ref, nw_ref, o_ref, buf_ref, sem, *, bm, n_tok):
    i = pl.program_id(0)
    slot = i % 2

    def copies(tile, slot_, r):
        return [pltpu.make_async_copy(y_ref.at[pl.ds(pos_ref[c * n_tok + tile * bm + r], 1), :],
                                      buf_ref.at[slot_, c, pl.ds(r, 1), :], sem.at[slot_, c]) for c in range(2)]

    def issue(tile, slot_):
        def body(r, carry):
            for c, cp in enumerate(copies(tile, slot_, r)):
                cp.start(priority=c)
            return carry
        lax.fori_loop(0, bm, body, 0, unroll=4)

    @pl.when(i == 0)
    def _():
        issue(0, 0)

    @pl.when(i + 1 < pl.num_programs(0))
    def _():
        issue(i + 1, 1 - slot)

    def drain(r, carry):
        for cp in copies(i, slot, r):
            cp.wait()
        return carry

    lax.fori_loop(0, bm, drain, 0, unroll=4)
    pr = pr_ref[...]
    mix = pr[:, 2:3] * buf_ref[slot, 0] + pr[:, 3:4] * buf_ref[slot, 1]
    x = x_ref[...] + g_ref[...] * mix
    o_ref[...] = x * lax.rsqrt(jnp.mean(x * x, axis=-1, keepdims=True) + EPS) * nw_ref[...]


def moe_combine(ys, pos, x, route, gate, norm_w, rows, bm=256):
    T, D = x.shape
    return pl.pallas_call(
        functools.partial(_combine_kernel, bm=bm, n_tok=T),
        grid_spec=pltpu.PrefetchScalarGridSpec(
            num_scalar_prefetch=1,
            grid=(T // bm,),
            in_specs=[pl.BlockSpec(memory_space=pl.ANY),
                      pl.BlockSpec((bm, D), lambda i, pos: (i, 0)),
                      pl.BlockSpec((bm, LANES), lambda i, pos: (i, 0)),
                      pl.BlockSpec((None, 1, D), lambda i, pos: (rows.mod_row(i, bm), 0, 0)),
                      pl.BlockSpec((1, D), lambda i, pos: (0, 0))],
            out_specs=pl.BlockSpec((bm, D), lambda i, pos: (i, 0)),
            scratch_shapes=[pltpu.VMEM((2, 2, bm, D), F32), pltpu.SemaphoreType.DMA((2, 2))]),
        out_shape=jax.ShapeDtypeStruct((T, D), F32),
        compiler_params=_cparams(1),
        name="moe_combine",
    )(pos, ys, x, route, gate, norm_w[None])


class MoePlan(NamedTuple):
    pos: jax.Array
    src: jax.Array
    n_active: jax.Array
    header: jax.Array
    tile_group: jax.Array
    tile_last: jax.Array
    tile_rows: jax.Array


def moe_plan(route, n_tok, bm):
    e = jnp.concatenate([route[:, 0], route[:, 1]]).astype(jnp.int32)
    onehot = (e[:, None] == jnp.arange(N_EXPERTS, dtype=jnp.int32)[None, :]).astype(jnp.int32)
    rank = jnp.take_along_axis(jnp.cumsum(onehot, axis=0), e[:, None], axis=1)[:, 0] - 1
    counts = jnp.sum(onehot, axis=0)
    tiles = (counts + bm - 1) // bm
    tile_end = jnp.cumsum(tiles)
    start = (tile_end - tiles) * bm
    pos = start[e] + rank
    n_tiles = (2 * n_tok) // bm + N_EXPERTS
    tok = jnp.concatenate([jnp.arange(n_tok, dtype=jnp.int32)] * 2)
    src = jnp.zeros((n_tiles * bm,), jnp.int32).at[pos].set(tok)
    n_active = tile_end[-1]
    all_tiles = jnp.arange(n_tiles, dtype=jnp.int32)
    used = all_tiles < n_active
    tile_ids = jnp.minimum(all_tiles, n_active - 1)
    tile_expert = jnp.sum((tile_ids[:, None] >= tile_end[None, :]).astype(jnp.int32), axis=1)
    ids = jnp.arange(N_EXPERTS, dtype=jnp.int32)
    in_use = tiles > 0
    group_of = jnp.cumsum(in_use.astype(jnp.int32)) - 1
    n_groups = jnp.sum(in_use.astype(jnp.int32))
    experts_in_use = jnp.sum(jnp.where(in_use[None, :] & (group_of[None, :] == ids[:, None]), ids[None, :], 0), axis=1)
    tile_last = used & (all_tiles + 1 == tile_end[tile_expert])
    tile_rows = jnp.clip(counts[tile_expert] - (all_tiles - (tile_end - tiles)[tile_expert]) * bm, 0, bm)
    header = jnp.concatenate([n_active.reshape(1), n_groups.reshape(1), experts_in_use])
    return MoePlan(pos.astype(jnp.int32), src, n_active.reshape(1).astype(jnp.int32), header.astype(jnp.int32),
                   group_of[tile_expert].astype(jnp.int32), tile_last.astype(jnp.int32),
                   jnp.where(used, tile_rows, 0).astype(jnp.int32))


def kernel(x, c, ctx, c_ctx, ada_w, ada_b, ab_w_in, hy_conv_w, hy_conv_b, hy_filt_w1, hy_filt_b1, hy_filt_w2, hy_filt_b2, hy_filt_w3, hy_filt_freq, hy_bias, ml_conv_w, ml_conv_b, ml_gate_b, ml_norm_w, ab_w_out, ffn_w_gu, ffn_w_down, cd_w_in, ssd_conv_w, ssd_conv_b, ssd_A_log, ssd_dt_bias, ssd_D, ssd_norm_w, s5_lam_re, s5_lam_im, s5_log_dt, s5_B_re, s5_B_im, s5_C_re, s5_C_im, s5_D, s5_glu_w, cd_w_out, moe_router, moe_w_gu, moe_w_down, final_norm_w):
    bsz, L, D = x.shape
    Lc = ctx.shape[1]
    rows = Rows(bsz, L, Lc)
    n_lat = rows.n_lat
    assert bsz == 4 and L % 1024 == 0 and Lc == 256 and D % 256 == 0
    bn_d = min(1024, D)
    bn_f = min(512, ffn_w_down.shape[1])

    x_lat, x_ctx = x.reshape(n_lat, D), ctx.reshape(bsz * Lc, D)
    cond = jnp.concatenate([c, c_ctx[None], jnp.zeros((8 - bsz - 1, D), F32)], axis=0)

    def mods(layer):
        m = adaln(cond, ada_w, ada_b, layer)
        return [m[:, k * D:(k + 1) * D].reshape(8, 1, D) for k in range(6)]

    md = mods(0)
    h = modulate(x_lat, md[0], md[1], rows, rows.n, BF16, x_tail=x_ctx)
    n_main = 3 * HY_D + 4 * ML_H * ML_DH
    p = mm_t(h, jnp.swapaxes(ab_w_in, 1, 2), 0, n_cols=n_main, bm=1024, bn=1024)
    w_gate = jnp.pad(ab_w_in[:, :, n_main:], ((0, 0), (0, 0), (0, LANES - 4 * ML_H)))
    gates = mm(h, w_gate, 0, n_cols=LANES, bm=1024, bn=LANES)
    gates_t = gates[:, :4 * ML_H].T

    filt = (hy_filt_w1, hy_filt_b1, hy_filt_w2, hy_filt_b2, hy_filt_w3, hy_filt_freq)
    yh_lat = hyena(p, L, bsz, 0, GRID_W, 0, hy_conv_w, hy_conv_b, filt, hy_bias)
    yh_ctx = hyena(p, Lc, bsz, n_lat, Lc, 0, hy_conv_w, hy_conv_b, filt, hy_bias)

    qk = conv_silu(p, 3 * HY_D, 2 * ML_H * ML_DH, ml_conv_w, ml_conv_b, 0, rows)
    hdir = mlstm(qk, p, 3 * HY_D + 2 * ML_H * ML_DH, gates, gates_t, ml_gate_b[0], rows)
    cat = finish_ab(yh_lat, yh_ctx, hdir, p, 3 * HY_D + 3 * ML_H * ML_DH, ml_norm_w, 0, rows)
    xs = mm_residual(cat, ab_w_out, 0, x_lat, md[2], rows, bm=1024, bn=bn_d, res_tail=x_ctx)

    h = modulate(xs, md[3], md[4], rows, rows.n, BF16)
    act = mm_swiglu(h, ffn_w_gu, 0, bm=1024, bn=bn_f)
    xs = mm_residual(act, ffn_w_down, 0, xs, md[5], rows, bm=512, bn=min(512, D))

    md = mods(1)
    h = modulate(xs, md[0], md[1], rows, rows.n, BF16)
    nd = SSD_H * SSD_P
    n_xbc = nd + 4 * LANES
    p1 = mm_t(h, jnp.swapaxes(cd_w_in, 1, 2), 0, n_cols=nd + n_xbc, bm=1024, bn=512)
    w_tail = cd_w_in[:, :, nd + n_xbc:]
    w_tail = jnp.concatenate([w_tail[:, :, 2 * SSD_H:], w_tail[:, :, :2 * SSD_H],
                              jnp.zeros((1, D, LANES - 2 * SSD_H), F32)], axis=2)
    p2 = mm(h, w_tail, 0, n_cols=S5_W + LANES, bm=1024, bn=S5_W + LANES)

    xbc = conv_silu(p1, nd, n_xbc, ssd_conv_w, ssd_conv_b, 0, rows)
    dt = p2[:, S5_W:]
    dt_t = dt[:, :2 * SSD_H].T
    ydir = ssd(xbc, dt, dt_t, ssd_A_log[0], ssd_dt_bias[0], rows)

    w_cat, c_cat, lb_re, lb_im, lam_rows = s5_params(
        s5_lam_re[0], s5_lam_im[0], s5_log_dt[0], s5_B_re[0], s5_B_im[0], s5_C_re[0], s5_C_im[0])
    sf, sb = s5(p2, w_cat, c_cat, lam_rows(lb_re, bsz), lam_rows(lb_im, bsz), rows)
    sf, sb = sf.reshape(n_lat, S5_W), sb.reshape(n_lat, S5_W)

    ssd_d_lanes = jnp.repeat(ssd_D, SSD_P, axis=1).reshape(ssd_D.shape[0], 1, nd)
    cat = finish_cd(ydir, xbc, p1, sf, sb, p2, ssd_d_lanes, ssd_norm_w, s5_D, s5_glu_w, 0, n_lat)
    xl = mm_residual(cat, cd_w_out, 0, xs, md[2], rows, bm=1024, bn=bn_d)

    h2, route = modulate_route(xl, md[3], md[4], jnp.pad(moe_router[0], ((0, 0), (0, LANES - N_EXPERTS))), rows)
    bm_e = 512
    plan = moe_plan(route, n_lat, bm_e)
    xg = gather_rows(h2, plan.src, plan.n_active * (bm_e // 256), bm=256)
    act = gmm(xg, moe_w_gu, 0, plan, bm=bm_e, bn=min(512, moe_w_down.shape[2]), swiglu=True)
    ys = gmm(act, moe_w_down, 0, plan, bm=bm_e, bn=min(512, D), swiglu=False)
    out = moe_combine(ys, plan.pos, xl, route, md[5], final_norm_w, rows, bm=256)
    return out.reshape(bsz, L, D)
```

```python
import functools
import math
from typing import NamedTuple

import jax
import jax.numpy as jnp
import numpy as np
from jax import lax
from jax.experimental import pallas as pl
from jax.experimental.pallas import tpu as pltpu

F32 = jnp.float32
BF16 = jnp.bfloat16

EPS = 1e-6
GRID_W = 64
CHUNK = 128
LANES = 128
HY_D = 1024
HY_EMB = 33
HY_FAST_DECAY = 0.3
HY_SLOW_DECAY = 1.5
HY_DECAY_TARGET = 1e-2
ML_H = 8
ML_DH = 128
SSD_H = 16
SSD_P = 64
S5_W = 512
S5_G = 32
S5_GS = 16
S5_P = 64
N_EXPERTS = 8
VMEM_LIMIT_BYTES = 56 * 1024 * 1024


def _cparams(n_axes):
    return pltpu.CompilerParams(dimension_semantics=("arbitrary",) * n_axes,
                                vmem_limit_bytes=VMEM_LIMIT_BYTES)


def _sigmoid(x):
    return 1.0 / (1.0 + jnp.exp(-x))


def _silu(x):
    return x * _sigmoid(x)


def _log_sigmoid(x):
    return jnp.minimum(x, 0.0) - jnp.log(1.0 + jnp.exp(-jnp.abs(x)))


def _softplus(x):
    return jnp.maximum(x, 0.0) + jnp.log(1.0 + jnp.exp(-jnp.abs(x)))


def _dot(a, b):
    return jnp.dot(a, b, preferred_element_type=F32)


def _dot_nt(a, b):
    return lax.dot_general(a, b, (((1,), (1,)), ((), ())), preferred_element_type=F32)


def _dot_tn(a, b):
    return lax.dot_general(a, b, (((0,), (0,)), ((), ())), preferred_element_type=F32)


def _dot_hi(a, b):
    return jnp.dot(a, b, preferred_element_type=F32, precision=lax.Precision.HIGHEST)


class Rows:
    def __init__(self, bsz, L, Lc):
        self.B, self.L, self.Lc = bsz, L, Lc
        self.n_lat = bsz * L
        self.n = bsz * (L + Lc)

    def mod_row(self, i, bm):
        n_lat_blocks = self.n_lat // bm
        return jnp.where(i < n_lat_blocks, (i * bm) // self.L, self.B)

    def chunk_block(self, b, c):
        ncc = self.Lc // CHUNK
        return jnp.where(c < ncc, self.n_lat // CHUNK + b * ncc + c, b * (self.L // CHUNK) + c - ncc)

    def scan_chunk(self, d, j):
        ncc = self.Lc // CHUNK
        nct = (self.L + self.Lc) // CHUNK
        back = jnp.where(j < ncc, ncc - 1 - j, nct - 1 + ncc - j)
        return jnp.where(d == 0, j, back)


def _adaln_kernel(c_ref, w_ref, b_ref, o_ref):
    cond = _silu(c_ref[...]).astype(BF16)
    o_ref[...] = _dot(cond, w_ref[...].astype(BF16)) + b_ref[...]


def adaln(cond, ada_w, ada_b, layer, bn=1024):
    bn = min(bn, ada_w.shape[1])
    _, D, N = ada_w.shape
    return pl.pallas_call(
        _adaln_kernel,
        grid=(N // bn,),
        in_specs=[pl.BlockSpec((8, D), lambda n: (0, 0)),
                  pl.BlockSpec((None, D, bn), lambda n: (layer, 0, n)),
                  pl.BlockSpec((None, 1, bn), lambda n: (layer, 0, n))],
        out_specs=pl.BlockSpec((8, bn), lambda n: (0, n)),
        out_shape=jax.ShapeDtypeStruct((8, N), F32),
        compiler_params=_cparams(1),
        name="adaln",
    )(cond, ada_w, ada_b.reshape(ada_b.shape[0], 1, N))


def _head_tail_specs(block, n_head_blocks, row_col):
    def head(*ids):
        r, c = row_col(*ids)
        return jnp.minimum(r, n_head_blocks - 1), c

    def tail(*ids):
        r, c = row_col(*ids)
        return jnp.maximum(r - n_head_blocks, 0), c

    return pl.BlockSpec(block, head), pl.BlockSpec(block, tail)


def _modulate_kernel(*refs, n_head_blocks):
    x_ref, sh_ref, sc_ref, o_ref = refs[0], refs[-3], refs[-2], refs[-1]
    x = x_ref[...]
    if len(refs) == 5:
        x = jnp.where(pl.program_id(0) < n_head_blocks, x, refs[1][...])
    y = x * lax.rsqrt(jnp.mean(x * x, axis=-1, keepdims=True) + EPS)
    o_ref[...] = (y * (1.0 + sc_ref[...]) + sh_ref[...]).astype(o_ref.dtype)


def modulate(x, shift, scale, rows, n_rows, out_dtype, bm=1024, x_tail=None):
    D = x.shape[1]
    mod_spec = pl.BlockSpec((None, 1, D), lambda i: (rows.mod_row(i, bm), 0, 0))
    if x_tail is None:
        xs, x_specs = [x], [pl.BlockSpec((bm, D), lambda i: (i, 0))]
    else:
        xs, x_specs = [x, x_tail], list(_head_tail_specs((bm, D), x.shape[0] // bm, lambda i: (i, 0)))
    return pl.pallas_call(
        functools.partial(_modulate_kernel, n_head_blocks=x.shape[0] // bm),
        grid=(n_rows // bm,),
        in_specs=x_specs + [mod_spec, mod_spec],
        out_specs=pl.BlockSpec((bm, D), lambda i: (i, 0)),
        out_shape=jax.ShapeDtypeStruct((n_rows, D), out_dtype),
        compiler_params=_cparams(1),
        name="modulate",
    )(*xs, shift, scale)


def _mm_kernel(x_ref, w_ref, o_ref, wb_ref):
    @pl.when(pl.program_id(1) == 0)
    def _():
        wb_ref[...] = w_ref[...].astype(BF16)
    o_ref[...] = _dot(x_ref[...], wb_ref[...]).astype(o_ref.dtype)


def mm(x, w, widx, *, n_cols, col_off=0, bm, bn, out_dtype=F32):
    M, K = x.shape
    off = col_off // bn
    return pl.pallas_call(
        _mm_kernel,
        grid=(n_cols // bn, M // bm),
        in_specs=[pl.BlockSpec((bm, K), lambda n, m: (m, 0)),
                  pl.BlockSpec((None, K, bn), lambda n, m: (widx, 0, n + off))],
        out_specs=pl.BlockSpec((bm, bn), lambda n, m: (m, n)),
        out_shape=jax.ShapeDtypeStruct((M, n_cols), out_dtype),
        scratch_shapes=[pltpu.VMEM((K, bn), BF16)],
        compiler_params=_cparams(2),
        name="mm",
    )(x, w)


def _mm_t_kernel(x_ref, wt_ref, o_ref, wb_ref):
    @pl.when(pl.program_id(1) == 0)
    def _():
        wb_ref[...] = wt_ref[...].astype(BF16)
    o_ref[...] = _dot_nt(x_ref[...], wb_ref[...]).astype(o_ref.dtype)


def mm_t(x, wt, widx, *, n_cols, bm, bn, out_dtype=F32):
    M, K = x.shape
    return pl.pallas_call(
        _mm_t_kernel,
        grid=(n_cols // bn, M // bm),
        in_specs=[pl.BlockSpec((bm, K), lambda n, m: (m, 0)),
                  pl.BlockSpec((None, bn, K), lambda n, m: (widx, n, 0))],
        out_specs=pl.BlockSpec((bm, bn), lambda n, m: (m, n)),
        out_shape=jax.ShapeDtypeStruct((M, n_cols), out_dtype),
        scratch_shapes=[pltpu.VMEM((bn, K), BF16)],
        compiler_params=_cparams(2),
        name="mm_t",
    )(x, wt)


def _mm_res_kernel(*refs, n_head_blocks):
    x_ref, w_ref, r_ref = refs[:3]
    g_ref, o_ref, wb_ref = refs[-3:]

    @pl.when(pl.program_id(1) == 0)
    def _():
        wb_ref[...] = w_ref[...].astype(BF16)
    res = r_ref[...]
    if len(refs) == 7:
        res = jnp.where(pl.program_id(1) < n_head_blocks, res, refs[3][...])
    o_ref[...] = res + g_ref[...] * _dot(x_ref[...], wb_ref[...])


def mm_residual(x, w, widx, res, gate, rows, *, bm, bn, res_tail=None):
    M, K = x.shape
    N = w.shape[2]
    if res_tail is None:
        rs, r_specs = [res], [pl.BlockSpec((bm, bn), lambda n, m: (m, n))]
    else:
        rs, r_specs = [res, res_tail], list(_head_tail_specs((bm, bn), res.shape[0] // bm, lambda n, m: (m, n)))
    return pl.pallas_call(
        functools.partial(_mm_res_kernel, n_head_blocks=res.shape[0] // bm),
        grid=(N // bn, M // bm),
        in_specs=[pl.BlockSpec((bm, K), lambda n, m: (m, 0)),
                  pl.BlockSpec((None, K, bn), lambda n, m: (widx, 0, n))] + r_specs
                 + [pl.BlockSpec((None, 1, bn), lambda n, m: (rows.mod_row(m, bm), 0, n))],
        out_specs=pl.BlockSpec((bm, bn), lambda n, m: (m, n)),
        out_shape=jax.ShapeDtypeStruct((M, N), F32),
        scratch_shapes=[pltpu.VMEM((K, bn), BF16)],
        compiler_params=_cparams(2),
        name="mm_residual",
    )(x, w, *rs, gate)


def _mm_swiglu_kernel(x_ref, wg_ref, wu_ref, o_ref, wgb_ref, wub_ref):
    @pl.when(pl.program_id(1) == 0)
    def _():
        wgb_ref[...] = wg_ref[...].astype(BF16)
        wub_ref[...] = wu_ref[...].astype(BF16)
    x = x_ref[...]
    g = _dot(x, wgb_ref[...])
    u = _dot(x, wub_ref[...])
    o_ref[...] = (_silu(g) * u).astype(o_ref.dtype)


def mm_swiglu(x, w_gu, widx, *, bm, bn):
    M, K = x.shape
    F = w_gu.shape[2] // 2
    nf = F // bn
    return pl.pallas_call(
        _mm_swiglu_kernel,
        grid=(nf, M // bm),
        in_specs=[pl.BlockSpec((bm, K), lambda n, m: (m, 0)),
                  pl.BlockSpec((None, K, bn), lambda n, m: (widx, 0, n)),
                  pl.BlockSpec((None, K, bn), lambda n, m: (widx, 0, n + nf))],
        out_specs=pl.BlockSpec((bm, bn), lambda n, m: (m, n)),
        out_shape=jax.ShapeDtypeStruct((M, F), BF16),
        scratch_shapes=[pltpu.VMEM((K, bn), BF16), pltpu.VMEM((K, bn), BF16)],
        compiler_params=_cparams(2),
        name="mm_swiglu",
    )(x, w_gu, w_gu)


def _conv3(x, w_ref, b_ref, period):
    n = x.shape[0]
    pos = lax.broadcasted_iota(jnp.int32, x.shape, 0) & (period - 1)
    prev = jnp.where(pos == 0, 0.0, pltpu.roll(x, 1, 0))
    nxt = jnp.where(pos == period - 1, 0.0, pltpu.roll(x, n - 1, 0))
    w = w_ref[...]
    return b_ref[...] + prev * w[0:1] + x * w[1:2] + nxt * w[2:3]


def _conv_silu_kernel(x_ref, w_ref, b_ref, o_ref, *, n_lat_blocks, lat_period, ctx_period):
    period = jnp.where(pl.program_id(0) < n_lat_blocks, lat_period, ctx_period)
    o_ref[...] = _silu(_conv3(x_ref[...], w_ref, b_ref, period)).astype(o_ref.dtype)


def conv_silu(p, col_off, n_cols, w, b, widx, rows, *, bt=1024, bc=512):
    assert rows.n_lat % bt == 0 and bt % rows.Lc == 0 and bt % GRID_W == 0
    off = col_off // bc
    kern = functools.partial(_conv_silu_kernel, n_lat_blocks=rows.n_lat // bt,
                             lat_period=GRID_W, ctx_period=rows.Lc)
    return pl.pallas_call(
        kern,
        grid=(rows.n // bt, n_cols // bc),
        in_specs=[pl.BlockSpec((bt, bc), lambda i, j: (i, j + off)),
                  pl.BlockSpec((None, 3, bc), lambda i, j: (widx, 0, j)),
                  pl.BlockSpec((None, 1, bc), lambda i, j: (widx, 0, j))],
        out_specs=pl.BlockSpec((bt, bc), lambda i, j: (i, j)),
        out_shape=jax.ShapeDtypeStruct((rows.n, n_cols), F32),
        compiler_params=_cparams(2),
        name="conv_silu",
    )(p, w, b.reshape(b.shape[0], 1, b.shape[1]))


def _hyena_conv_kernel(p0_ref, p1_ref, p2_ref, w0_ref, w1_ref, w2_ref, b0_ref, b1_ref, b2_ref,
                       x0_ref, u_ref, *, period):
    x0_ref[...] = _conv3(p0_ref[...], w0_ref, b0_ref, period)
    x1 = _conv3(p1_ref[...], w1_ref, b1_ref, period)
    v = _conv3(p2_ref[...], w2_ref, b2_ref, period)
    u_ref[...] = (x1 * v).astype(u_ref.dtype)


def hyena_conv(p, w, b, widx, *, row0, bsz, seq, period, bt=256, bc=512):
    nt = seq // bt
    ncb = HY_D // bc
    rb0 = row0 // bt

    def pspec(k):
        return pl.BlockSpec((bt, bc), lambda bb, i, j: (rb0 + bb * nt + i, j + k * ncb))

    def wspec(k):
        return pl.BlockSpec((None, 3, bc), lambda bb, i, j: (widx, 0, j + k * ncb))

    def bspec(k):
        return pl.BlockSpec((None, 1, bc), lambda bb, i, j: (widx, 0, j + k * ncb))

    ospec = pl.BlockSpec((bt, bc), lambda bb, i, j: (i, bb * ncb + j))
    b3 = b.reshape(b.shape[0], 1, b.shape[1])
    return pl.pallas_call(
        functools.partial(_hyena_conv_kernel, period=period),
        grid=(bsz, nt, ncb),
        in_specs=[pspec(0), pspec(1), pspec(2), wspec(0), wspec(1), wspec(2), bspec(0), bspec(1), bspec(2)],
        out_specs=[ospec, ospec],
        out_shape=[jax.ShapeDtypeStruct((seq, bsz * HY_D), F32),
                   jax.ShapeDtypeStruct((seq, bsz * HY_D), BF16)],
        compiler_params=_cparams(3),
        name="hyena_conv",
    )(p, p, p, w, w, w, b3, b3, b3)


def _hyena_feats(L):
    pos = np.arange(L, dtype=np.float64)
    t = pos / max(L - 1, 1)
    n_bands = (HY_EMB - 1) // 2
    bands = np.linspace(1e-4, n_bands - 1, n_bands)
    ang = (2 * math.pi / L) * pos[:, None] * bands[None, :]
    feats = np.concatenate([t[:, None], np.cos(ang), -np.sin(ang)], axis=-1)
    feats = np.pad(feats, ((0, 0), (0, LANES - HY_EMB)))
    deltas = np.abs(np.linspace(math.log(HY_DECAY_TARGET) / HY_SLOW_DECAY,
                                math.log(HY_DECAY_TARGET) / HY_FAST_DECAY, HY_D))
    return feats.astype(np.float32), t.astype(np.float32)[:, None], deltas.astype(np.float32)[None, :]


def _hyena_filter_kernel(feats_ref, t_ref, dl_ref, w1_ref, b1_ref, w2_ref, b2_ref, w3f_ref, w3b_ref,
                         fq_ref, hf_ref, hb_ref, hid_ref):
    @pl.when(pl.program_id(0) == 0)
    def _():
        fq = fq_ref[...]
        h1 = jnp.sin(fq[0:1] * (_dot_hi(feats_ref[...], w1_ref[...]) + b1_ref[...]))
        hid_ref[...] = jnp.sin(fq[1:2] * (_dot_hi(h1, w2_ref[...]) + b2_ref[...]))

    h = hid_ref[...]
    win = jnp.exp(-t_ref[...] * dl_ref[...])
    h_f = _dot_hi(h, w3f_ref[...]) * win
    h_b = _dot_hi(h, w3b_ref[...]) * win
    row = lax.broadcasted_iota(jnp.int32, h_b.shape, 0)
    h_b = jnp.where(row == 0, 0.0, h_b)
    l1 = jnp.sum(jnp.abs(h_f), axis=0, keepdims=True) + jnp.sum(jnp.abs(h_b), axis=0, keepdims=True)
    hf_ref[...] = h_f / l1
    hb_ref[...] = h_b / l1


def hyena_filter(L, w1, b1, w2, b2, w3, freq, widx, bc=256):
    feats, t, deltas = _hyena_feats(L)
    hid = w2.shape[1]
    w1p = jnp.pad(w1[widx], ((0, LANES - HY_EMB), (0, 0)))
    ncb = HY_D // bc
    full = lambda shape: pl.BlockSpec(shape, lambda j: (0,) * len(shape))
    return pl.pallas_call(
        _hyena_filter_kernel,
        grid=(ncb,),
        in_specs=[full((L, LANES)), full((L, 1)), pl.BlockSpec((1, bc), lambda j: (0, j)),
                  full((LANES, hid)), full((1, hid)), full((hid, hid)), full((1, hid)),
                  pl.BlockSpec((hid, bc), lambda j: (0, j)),
                  pl.BlockSpec((hid, bc), lambda j: (0, j + ncb)),
                  full((2, hid))],
        out_specs=[pl.BlockSpec((L, bc), lambda j: (0, j)), pl.BlockSpec((L, bc), lambda j: (0, j))],
        out_shape=[jax.ShapeDtypeStruct((L, HY_D), F32), jax.ShapeDtypeStruct((L, HY_D), F32)],
        scratch_shapes=[pltpu.VMEM((L, hid), F32)],
        compiler_params=_cparams(1),
        name="hyena_filter",
    )(jnp.asarray(feats), jnp.asarray(t), jnp.asarray(deltas), w1p, b1[widx][None], w2[widx],
      b2[widx][None], w3[widx], w3[widx], freq[widx])


def _dft_matrix(L):
    f = np.arange(L, dtype=np.int64)[:, None]
    s = np.arange(L, dtype=np.int64)[None, :]
    ang = (math.pi / L) * ((f * s) % (2 * L)).astype(np.float64)
    a_cos = np.cos(ang)
    a_sin = -np.sin(ang)
    a_sin[0, :] = np.where(np.arange(L) % 2 == 0, 1.0, -1.0)
    return np.concatenate([a_cos, a_sin], axis=0).astype(np.float32)


def _spectrum_kernel(ac_ref, as_ref, hf_ref, hb_ref, kre_ref, kim_ref, kny_ref, *, L, bf):
    hf = hf_ref[...].astype(BF16)
    hb = hb_ref[...].astype(BF16)
    ac = ac_ref[...].astype(BF16)
    a_s = as_ref[...].astype(BF16)
    cf, cb = _dot(ac, hf), _dot(ac, hb)
    sf, sb = _dot(a_s, hf), _dot(a_s, hb)
    f = lax.broadcasted_iota(jnp.int32, cf.shape, 0) + pl.program_id(1) * bf
    wgt = jnp.where(f == 0, 0.5 / L, 1.0 / L)
    kre_ref[...] = (cf + cb) * wgt
    kim_ref[...] = jnp.where(f == 0, 0.0, (sf - sb) * wgt)
    kny_ref[...] = jnp.where(f == 0, (sf + sb) * wgt, (cf + cb) * wgt)


def hyena_spectrum(a_mat, hf, hb, L, bf=256, bc=512):
    bf = min(bf, L)
    nf = L // bf
    C = hf.shape[1]
    ospec = pl.BlockSpec((bf, bc), lambda c, i: (i, c))
    hspec = pl.BlockSpec((L, bc), lambda c, i: (0, c))
    return pl.pallas_call(
        functools.partial(_spectrum_kernel, L=L, bf=bf),
        grid=(C // bc, nf),
        in_specs=[pl.BlockSpec((bf, L), lambda c, i: (i, 0)), pl.BlockSpec((bf, L), lambda c, i: (i + nf, 0)),
                  hspec, hspec],
        out_specs=[ospec, ospec, ospec],
        out_shape=[jax.ShapeDtypeStruct((L, C), F32)] * 3,
        compiler_params=_cparams(2),
        name="hyena_spectrum",
    )(a_mat, a_mat, hf, hb)


def _lc_fwd_kernel(ac_ref, as_ref, u_ref, kre_ref, kim_ref, kny_ref, yre_ref, yim_ref, acb_ref, asb_ref):
    @pl.when(pl.program_id(1) == 0)
    def _():
        acb_ref[...] = ac_ref[...].astype(BF16)
        asb_ref[...] = as_ref[...].astype(BF16)
    u = u_ref[...]
    ure = _dot(acb_ref[...], u)
    uim = _dot(asb_ref[...], u)
    kim = kim_ref[...]
    yre_ref[...] = (kre_ref[...] * ure - kim * uim).astype(yre_ref.dtype)
    yim_ref[...] = (kny_ref[...] * uim + kim * ure).astype(yim_ref.dtype)


def long_conv_fwd(a_mat, u, kre, kim, kny, L, bsz, bf=512):
    bf = min(bf, L)
    nf = L // bf
    C = kre.shape[1]
    kspec = pl.BlockSpec((bf, C), lambda i, b: (i, 0))
    ospec = pl.BlockSpec((bf, C), lambda i, b: (i, b))
    return pl.pallas_call(
        _lc_fwd_kernel,
        grid=(nf, bsz),
        in_specs=[pl.BlockSpec((bf, L), lambda i, b: (i, 0)), pl.BlockSpec((bf, L), lambda i, b: (i + nf, 0)),
                  pl.BlockSpec((L, C), lambda i, b: (0, b)), kspec, kspec, kspec],
        out_specs=[ospec, ospec],
        out_shape=[jax.ShapeDtypeStruct((L, bsz * C), BF16)] * 2,
        scratch_shapes=[pltpu.VMEM((bf, L), BF16), pltpu.VMEM((bf, L), BF16)],
        compiler_params=_cparams(2),
        name="long_conv_fwd",
    )(a_mat, a_mat, u, kre, kim, kny)


def _lc_inv_kernel(atc_ref, ats_ref, yre_ref, yim_ref, x0_ref, u_ref, bias_ref, o_ref, atcb_ref, atsb_ref):
    @pl.when(pl.program_id(1) == 0)
    def _():
        atcb_ref[...] = atc_ref[...].astype(BF16)
        atsb_ref[...] = ats_ref[...].astype(BF16)
    y = _dot(atcb_ref[...], yre_ref[...]) + _dot(atsb_ref[...], yim_ref[...])
    u = u_ref[...].astype(F32)
    o_ref[...] = (x0_ref[...] * (y + bias_ref[...] * u)).astype(o_ref.dtype)


def long_conv_inv(at_mat, yre, yim, x0, u, bias, widx, L, bsz, bt=512):
    bt = min(bt, L)
    nt = L // bt
    C = bias.shape[1]
    tspec = pl.BlockSpec((bt, C), lambda i, b: (i, b))
    yspec = pl.BlockSpec((L, C), lambda i, b: (0, b))
    return pl.pallas_call(
        _lc_inv_kernel,
        grid=(nt, bsz),
        in_specs=[pl.BlockSpec((bt, L), lambda i, b: (i, 0)), pl.BlockSpec((bt, L), lambda i, b: (i, 1)),
                  yspec, yspec, tspec, tspec, pl.BlockSpec((None, 1, C), lambda i, b: (widx, 0, 0))],
        out_specs=tspec,
        out_shape=jax.ShapeDtypeStruct((L, bsz * C), BF16),
        scratch_shapes=[pltpu.VMEM((bt, L), BF16), pltpu.VMEM((bt, L), BF16)],
        compiler_params=_cparams(2),
        name="long_conv_inv",
    )(at_mat, at_mat, yre, yim, x0, u, bias.reshape(bias.shape[0], 1, C))


def hyena(p, L, bsz, row0, period, widx, conv_w, conv_b, filt, bias):
    a_np = _dft_matrix(L)
    a_mat = jnp.asarray(a_np)
    at_mat = jnp.asarray(np.ascontiguousarray(a_np.T))
    x0, u = hyena_conv(p, conv_w, conv_b, widx, row0=row0, bsz=bsz, seq=L, period=period, bt=min(1024, L))
    hf, hb = hyena_filter(L, *filt, widx)
    kre, kim, kny = hyena_spectrum(a_mat, hf, hb, L)
    yre, yim = long_conv_fwd(a_mat, u, kre, kim, kny, L, bsz)
    return long_conv_inv(at_mat, yre, yim, x0, u, bias, widx, L, bsz)


def _pick_col(g, idx):
    lane = lax.broadcasted_iota(jnp.int32, g.shape, 1)
    return jnp.sum(jnp.where(lane == idx, g, 0.0), axis=1, keepdims=True)


def _chunk_masks(d, n):
    t_i = lax.broadcasted_iota(jnp.int32, (n, n), 0)
    s_i = lax.broadcasted_iota(jnp.int32, (n, n), 1)
    lag = (t_i - s_i) * jnp.where(d == 0, 1, -1)
    return lag >= 0, lag <= 0


def _running_sums(x_cols, x_rows, causal, causal_t):
    tri = jnp.where(causal, 1.0, 0.0)
    tri_t = jnp.where(causal_t, 1.0, 0.0)
    return _dot_hi(tri, x_cols), _dot_hi(x_rows, tri_t)


def _mlstm_kernel(q_ref, k_ref, v_ref, gc_ref, gr_ref, bc_ref, br_ref, o_ref, c_ref, n_ref, m_ref, gs_ref):
    d = pl.program_id(1)

    @pl.when(pl.program_id(2) == 0)
    def _():
        c_ref[...] = jnp.zeros_like(c_ref)
        n_ref[...] = jnp.zeros_like(n_ref)
        m_ref[...] = jnp.zeros_like(m_ref)

    gcol = gc_ref[...] + bc_ref[...]
    lane = lax.broadcasted_iota(jnp.int32, gcol.shape, 1)
    gcol = jnp.where((lane & ML_H) != 0, _log_sigmoid(gcol), gcol)
    grow = gr_ref[...] + br_ref[...]
    sub = lax.broadcasted_iota(jnp.int32, grow.shape, 0)
    grow = jnp.where((sub & ML_H) != 0, _log_sigmoid(grow), grow)

    causal, causal_t = _chunk_masks(d, CHUNK)
    ccol, crow = _running_sums(gcol, grow, causal, causal_t)
    gs_ref[0] = grow
    gs_ref[1] = crow

    def head(h):
        sl = pl.ds(h * ML_DH, ML_DH)
        return _mlstm_head(h, d, q_ref.at[:, sl], k_ref.at[:, sl], v_ref.at[:, sl], o_ref.at[:, sl], gcol, ccol,
                           gs_ref, causal, c_ref.at[h], n_ref.at[h], m_ref.at[h])

    _run_interleaved([head(h) for h in range(ML_H)])


def _run_interleaved(stages):
    live = list(stages)
    while live:
        still = []
        for gen in live:
            try:
                next(gen)
                still.append(gen)
            except StopIteration:
                pass
        live = still


def _mlstm_head(h, d, q_ref, k_ref, v_ref, o_ref, gcol, ccol, gs_ref, causal, c_ref, n_ref, m_ref):
    i_idx = 2 * ML_H * d + h
    f_idx = i_idx + ML_H
    ig_row = gs_ref[0, pl.ds(i_idx, 1), :]
    ig_col = _pick_col(gcol, i_idx)
    bcum_row = gs_ref[1, pl.ds(f_idx, 1), :]
    bcum_col = _pick_col(ccol, f_idx)
    g = jnp.sum(gs_ref[0, pl.ds(f_idx, 1), :], axis=1, keepdims=True)

    q = q_ref[...] * (ML_DH ** -0.5)
    k = k_ref[...]
    qb, kb, vb = q.astype(BF16), k.astype(BF16), v_ref[...].astype(BF16)
    c_in, n_in, m_in = c_ref[...], n_ref[...], m_ref[0:1, 0:1]

    dmat = jnp.where(causal, bcum_col - bcum_row + ig_row, -jnp.inf)
    m_inter = bcum_col + m_in
    m_t = jnp.maximum(jnp.max(dmat, axis=1, keepdims=True), m_inter)
    decay = jnp.exp(dmat - m_t)
    w_inter = jnp.exp(m_inter - m_t)
    a_row = g - bcum_row + ig_row
    a_col = g - bcum_col + ig_col
    m_loc = jnp.max(a_row, axis=1, keepdims=True)
    kw = k * jnp.exp(a_col - m_loc)
    qk = _dot_nt(qb, kb)
    qc = _dot(qb, c_in.astype(BF16))
    c_loc = _dot_tn(kw.astype(BF16), vb)
    yield

    s = qk * decay
    sv = _dot(s.astype(BF16), vb)
    den = jnp.sum(s, axis=1, keepdims=True) + w_inter * jnp.sum(q * n_in, axis=1, keepdims=True)
    yield

    num = sv + w_inter * qc
    o_ref[...] = num / jnp.maximum(jnp.abs(den), jnp.exp(-m_t))
    n_loc = jnp.sum(kw, axis=0, keepdims=True)
    m_new = jnp.maximum(g + m_in, m_loc)
    s_prev = jnp.exp(g + m_in - m_new)
    s_loc = jnp.exp(m_loc - m_new)
    c_ref[...] = s_prev * c_in + s_loc * c_loc
    n_ref[...] = s_prev * n_in + s_loc * n_loc
    m_ref[...] = jnp.broadcast_to(m_new, m_ref.shape)


def mlstm(qk, p, v_col_off, gates, gates_t, gate_b, rows):
    nct = (rows.L + rows.Lc) // CHUNK
    W = ML_H * ML_DH
    voff = v_col_off // W

    def rb(b, d, j):
        return rows.chunk_block(b, rows.scan_chunk(d, j))

    ng = 4 * ML_H
    bias = gate_b.reshape(ng)
    bias_row = jnp.pad(bias, (0, LANES - ng))[None]
    return pl.pallas_call(
        _mlstm_kernel,
        grid=(rows.B, 2, nct),
        in_specs=[pl.BlockSpec((CHUNK, W), lambda b, d, j: (rb(b, d, j), 0)),
                  pl.BlockSpec((CHUNK, W), lambda b, d, j: (rb(b, d, j), 1)),
                  pl.BlockSpec((CHUNK, W), lambda b, d, j: (rb(b, d, j), voff)),
                  pl.BlockSpec((CHUNK, LANES), lambda b, d, j: (rb(b, d, j), 0)),
                  pl.BlockSpec((ng, CHUNK), lambda b, d, j: (0, rb(b, d, j))),
                  pl.BlockSpec((1, LANES), lambda b, d, j: (0, 0)),
                  pl.BlockSpec((ng, 1), lambda b, d, j: (0, 0))],
        out_specs=pl.BlockSpec((None, CHUNK, W), lambda b, d, j: (d, rb(b, d, j), 0)),
        out_shape=jax.ShapeDtypeStruct((2, rows.n, W), F32),
        scratch_shapes=[pltpu.VMEM((ML_H, ML_DH, ML_DH), F32), pltpu.VMEM((ML_H, 1, ML_DH), F32),
                        pltpu.VMEM((ML_H, 8, LANES), F32), pltpu.VMEM((2, ng, CHUNK), F32)],
        compiler_params=_cparams(3),
        name="mlstm",
    )(qk, qk, p, gates, gates_t, bias_row, bias[:, None])


def _finish_ab_kernel(yl_ref, yc_ref, h_ref, o_ref, nw_ref, out_ref, *, n_lat_blocks):
    @pl.when(pl.program_id(0) < n_lat_blocks)
    def _():
        out_ref[:, 0:HY_D] = yl_ref[...]

    @pl.when(pl.program_id(0) >= n_lat_blocks)
    def _():
        out_ref[:, 0:HY_D] = yc_ref[...]

    hs = h_ref[0] + h_ref[1]
    og = _sigmoid(o_ref[...])
    nw = nw_ref[...]
    for i in range(ML_H):
        sl = slice(i * ML_DH, (i + 1) * ML_DH)
        x = hs[:, sl]
        mu = jnp.mean(x, axis=-1, keepdims=True)
        xc = x - mu
        var = jnp.mean(xc * xc, axis=-1, keepdims=True)
        y = xc * lax.rsqrt(var + EPS) * nw[:, sl] * og[:, sl]
        out_ref[:, HY_D + i * ML_DH:HY_D + (i + 1) * ML_DH] = y.astype(out_ref.dtype)


def finish_ab(yh_lat, yh_ctx, hdir, p, o_col_off, norm_w, widx, rows, bt=256):
    n_lat_blocks = rows.n_lat // bt
    per_b = rows.L // bt
    ooff = o_col_off // (ML_H * ML_DH)
    W = ML_H * ML_DH

    assert rows.Lc == bt
    n_blocks = rows.n // bt

    def lat_map(i):
        j = jnp.minimum(i, n_lat_blocks - 1)
        return (j % per_b, j // per_b)

    return pl.pallas_call(
        functools.partial(_finish_ab_kernel, n_lat_blocks=n_lat_blocks),
        grid=(n_blocks,),
        in_specs=[pl.BlockSpec((bt, HY_D), lat_map),
                  pl.BlockSpec((bt, HY_D), lambda i: (0, jnp.maximum(i - n_lat_blocks, 0))),
                  pl.BlockSpec((2, bt, W), lambda i: (0, i, 0)),
                  pl.BlockSpec((bt, W), lambda i: (i, ooff)),
                  pl.BlockSpec((None, 1, W), lambda i: (widx, 0, 0))],
        out_specs=pl.BlockSpec((bt, HY_D + W), lambda i: (i, 0)),
        out_shape=jax.ShapeDtypeStruct((rows.n, HY_D + W), BF16),
        compiler_params=_cparams(1),
        name="finish_ab",
    )(yh_lat, yh_ctx, hdir, p, norm_w.reshape(norm_w.shape[0], 1, W))


def _ssd_kernel(x_ref, b_ref, c_ref, dc_ref, dr_ref, pr_ref, pc_ref, o_ref, h_ref, ds_ref):
    d = pl.program_id(1)

    @pl.when(pl.program_id(2) == 0)
    def _():
        h_ref[...] = jnp.zeros_like(h_ref)

    dt_cols = _softplus(dc_ref[...] + pr_ref[1:2, :])
    la_cols = dt_cols * -jnp.exp(pr_ref[0:1, :])
    dt_rows = _softplus(dr_ref[...] + pc_ref[:, 1:2])
    la_rows = dt_rows * -jnp.exp(pc_ref[:, 0:1])

    causal, causal_t = _chunk_masks(d, CHUNK)
    acum_cols, acum_rows = _running_sums(la_cols, la_rows, causal, causal_t)
    ds_ref[0] = dt_rows
    ds_ref[1] = la_rows
    ds_ref[2] = acum_rows
    n_groups = b_ref.shape[1] // LANES
    heads_per_group = SSD_H // n_groups
    bmats = [b_ref[:, g * LANES:(g + 1) * LANES].astype(BF16) for g in range(n_groups)]
    cmats = [c_ref[:, g * LANES:(g + 1) * LANES].astype(BF16) for g in range(n_groups)]
    cbs = [_dot_nt(cmats[g], bmats[g]) for g in range(n_groups)]

    def head(hd):
        grp = hd // heads_per_group
        idx = SSD_H * d + hd
        hs = pl.ds(hd * SSD_P, SSD_P)
        dt_row = ds_ref[0, pl.ds(idx, 1), :]
        dt_col = _pick_col(dt_cols, idx)
        acum_col = _pick_col(acum_cols, idx)
        acum_row = ds_ref[2, pl.ds(idx, 1), :]
        tot = jnp.sum(ds_ref[1, pl.ds(idx, 1), :], axis=1, keepdims=True)
        decay = jnp.exp(jnp.where(causal, acum_col - acum_row, -jnp.inf))
        xh = x_ref[:, hs]
        h_in = h_ref[hd]
        xw = xh * (jnp.exp(tot - acum_col) * dt_col)
        y_off = _dot_nt(cmats[grp], h_in.astype(BF16))
        st = _dot_tn(xw.astype(BF16), bmats[grp])
        y_diag = _dot((cbs[grp] * decay * dt_row).astype(BF16), xh.astype(BF16))
        yield
        o_ref[:, hs] = y_diag + y_off * jnp.exp(acum_col)
        h_ref[hd] = jnp.exp(tot) * h_in + st

    _run_interleaved([head(hd) for hd in range(SSD_H)])


def ssd(xbc, dt, dt_t, a_log, dt_bias, rows):
    nct = (rows.L + rows.Lc) // CHUNK
    nd = SSD_H * SSD_P
    gw = 2 * LANES

    def rb(b, d, j):
        return rows.chunk_block(b, rows.scan_chunk(d, j))

    nh = 2 * SSD_H
    par = jnp.stack([a_log.reshape(nh), dt_bias.reshape(nh)])
    par_rows = jnp.pad(par, ((0, 0), (0, LANES - nh)))
    return pl.pallas_call(
        _ssd_kernel,
        grid=(rows.B, 2, nct),
        in_specs=[pl.BlockSpec((CHUNK, nd), lambda b, d, j: (rb(b, d, j), 0)),
                  pl.BlockSpec((CHUNK, gw), lambda b, d, j: (rb(b, d, j), nd // gw)),
                  pl.BlockSpec((CHUNK, gw), lambda b, d, j: (rb(b, d, j), nd // gw + 1)),
                  pl.BlockSpec((CHUNK, LANES), lambda b, d, j: (rb(b, d, j), 0)),
                  pl.BlockSpec((nh, CHUNK), lambda b, d, j: (0, rb(b, d, j))),
                  pl.BlockSpec((2, LANES), lambda b, d, j: (0, 0)),
                  pl.BlockSpec((nh, 2), lambda b, d, j: (0, 0))],
        out_specs=pl.BlockSpec((None, CHUNK, nd), lambda b, d, j: (d, rb(b, d, j), 0)),
        out_shape=jax.ShapeDtypeStruct((2, rows.n, nd), F32),
        scratch_shapes=[pltpu.VMEM((SSD_H, SSD_P, LANES), F32), pltpu.VMEM((3, nh, CHUNK), F32)],
        compiler_params=_cparams(3),
        name="ssd",
    )(xbc, xbc, xbc, dt, dt_t, par_rows, par.T)


S5_TC = 64
S5_JB = 2
S5_NS = S5_G * S5_P


def _s5_kernel(*refs, bsz):
    uf_refs, ub_refs = refs[:bsz], refs[bsz:2 * bsz]
    w_ref, c_ref, lre_ref, lim_ref, sf_ref, sb_ref = refs[2 * bsz:2 * bsz + 6]
    z_ref, s_ref, wb_ref, cb_ref, u_ref, y_ref = refs[2 * bsz + 6:]

    @pl.when(pl.program_id(0) == 0)
    def _():
        s_ref[...] = jnp.zeros_like(s_ref)
        wb_ref[...] = w_ref[...].astype(BF16)
        cb_ref[...] = c_ref[...].astype(BF16)

    r_i = lax.broadcasted_iota(jnp.int32, (S5_TC, S5_TC), 0)
    c_i = lax.broadcasted_iota(jnp.int32, (S5_TC, S5_TC), 1)
    flip = jnp.where(r_i + c_i == S5_TC - 1, 1.0, 0.0)
    group = 2 * bsz
    n_blk = S5_W // LANES
    sw = S5_NS // n_blk
    for b in range(bsz):
        uf = uf_refs[b][...]
        ub = _dot(flip.astype(BF16), ub_refs[b][...].astype(BF16))
        for j in range(n_blk):
            u_ref[j, pl.ds(b, S5_TC, stride=group), :] = uf[:, j * LANES:(j + 1) * LANES]
            u_ref[j, pl.ds(bsz + b, S5_TC, stride=group), :] = ub[:, j * LANES:(j + 1) * LANES]

    for j in range(n_blk):
        z_ref[:, 2 * j * sw:2 * (j + 1) * sw] = _dot(u_ref[j].astype(BF16), wb_ref[j])

    for j0 in range(0, n_blk, S5_JB):
        blocks = range(j0, j0 + S5_JB)
        re_sl = [slice(2 * j * sw, (2 * j + 1) * sw) for j in blocks]
        im_sl = [slice((2 * j + 1) * sw, (2 * j + 2) * sw) for j in blocks]
        ar = [lre_ref[:, j * sw:(j + 1) * sw] for j in blocks]
        ai = [lim_ref[:, j * sw:(j + 1) * sw] for j in blocks]

        def step(t, carry):
            r0 = pl.multiple_of(t * 8, 8)
            new = []
            for k in range(S5_JB):
                zr, zi = carry[2 * k], carry[2 * k + 1]
                nr = ar[k] * zr - ai[k] * zi + z_ref[pl.ds(r0, 8), re_sl[k]]
                ni = ar[k] * zi + ai[k] * zr + z_ref[pl.ds(r0, 8), im_sl[k]]
                z_ref[pl.ds(r0, 8), re_sl[k]] = nr
                z_ref[pl.ds(r0, 8), im_sl[k]] = ni
                new += [nr, ni]
            return tuple(new)

        init = []
        for k in range(S5_JB):
            init += [s_ref[:, re_sl[k]], s_ref[:, im_sl[k]]]
        fin = lax.fori_loop(0, S5_TC, step, tuple(init))
        for k in range(S5_JB):
            s_ref[:, re_sl[k]] = fin[2 * k]
            s_ref[:, im_sl[k]] = fin[2 * k + 1]

    row = lax.broadcasted_iota(jnp.int32, (S5_TC * 8, LANES), 0)
    is_bwd = (row & 4) != 0
    for j in range(n_blk):
        yy = _dot(z_ref[:, 2 * j * sw:2 * (j + 1) * sw].astype(BF16), cb_ref[j])
        y_ref[j] = jnp.where(is_bwd, yy[:, LANES:], yy[:, :LANES])
    for b in range(bsz):
        for j in range(n_blk):
            ls = slice(j * LANES, (j + 1) * LANES)
            sf_ref[b, :, ls] = y_ref[j, pl.ds(b, S5_TC, stride=group), :]
            sb_ref[b, :, ls] = _dot_hi(flip, y_ref[j, pl.ds(bsz + b, S5_TC, stride=group), :])


def s5_params(lam_re, lam_im, log_dt, b_re, b_im, c_re, c_im):
    dt = jnp.exp(log_dt)[..., None]
    mag = jnp.exp(lam_re * dt)
    lb_re, lb_im = mag * jnp.cos(lam_im * dt), mag * jnp.sin(lam_im * dt)
    den = lam_re * lam_re + lam_im * lam_im
    f_re = ((lb_re - 1) * lam_re + lb_im * lam_im) / den
    f_im = (lb_im * lam_re - (lb_re - 1) * lam_im) / den
    cf_re = c_re[None] * f_re[:, :, None, :] - c_im[None] * f_im[:, :, None, :]
    cf_im = c_re[None] * f_im[:, :, None, :] + c_im[None] * f_re[:, :, None, :]
    gpb = LANES // S5_GS
    n_blk = S5_G // gpb
    eye = jnp.eye(gpb, dtype=F32)

    def w_blocks(b):
        bb = b.reshape(n_blk, gpb, S5_P, S5_GS)
        return jnp.einsum('jgpi,gh->jgihp', bb, eye).reshape(n_blk, gpb * S5_GS, gpb * S5_P)

    def c_blocks(c):
        cc = c.reshape(2, n_blk, gpb, S5_GS, S5_P)
        return jnp.einsum('djgip,gh->jgpdhi', cc, eye).reshape(n_blk, gpb * S5_P, 2 * gpb * S5_GS)

    def lam_rows(l, bsz):
        return jnp.repeat(l.reshape(2, S5_NS), bsz, axis=0)

    w_cat = jnp.concatenate([w_blocks(b_re), w_blocks(b_im)], axis=2)
    c_cat = jnp.concatenate([c_blocks(cf_re), -c_blocks(cf_im)], axis=1)
    return w_cat, c_cat, lb_re, lb_im, lam_rows


def s5(p, w_cat, c_cat, lre, lim, rows):
    bsz, L, Lc = rows.B, rows.L, rows.Lc
    assert bsz == 4
    n_blk = S5_W // LANES
    sw = S5_NS // n_blk
    rt = S5_TC * 2 * bsz
    ncc, ncl = Lc // S5_TC, L // S5_TC
    lat0 = rows.n_lat // S5_TC

    def fwd_block(b):
        return lambda i: (jnp.where(i < ncc, lat0 + b * ncc + i, b * ncl + i - ncc), 0)

    def bwd_block(b):
        return lambda i: (jnp.where(i < ncc, lat0 + b * ncc + ncc - 1 - i, b * ncl + ncl - 1 + ncc - i), 0)

    full = lambda shape: pl.BlockSpec(shape, lambda i: (0,) * len(shape))
    u_specs = ([pl.BlockSpec((S5_TC, S5_W), fwd_block(b)) for b in range(bsz)]
               + [pl.BlockSpec((S5_TC, S5_W), bwd_block(b)) for b in range(bsz)])
    out_specs = [pl.BlockSpec((bsz, S5_TC, S5_W), lambda i: (0, jnp.maximum(i - ncc, 0), 0)),
                 pl.BlockSpec((bsz, S5_TC, S5_W), lambda i: (0, jnp.minimum(ncl - 1 + ncc - i, ncl - 1), 0))]
    return pl.pallas_call(
        functools.partial(_s5_kernel, bsz=bsz),
        grid=((L + Lc) // S5_TC,),
        in_specs=u_specs + [full((n_blk, LANES, 2 * sw)), full((n_blk, 2 * sw, 2 * LANES)),
                            full((2 * bsz, S5_NS)), full((2 * bsz, S5_NS))],
        out_specs=out_specs,
        out_shape=[jax.ShapeDtypeStruct((bsz, L, S5_W), F32)] * 2,
        scratch_shapes=[pltpu.VMEM((rt, 2 * S5_NS), F32), pltpu.VMEM((2 * bsz, 2 * S5_NS), F32),
                        pltpu.VMEM((n_blk, LANES, 2 * sw), BF16), pltpu.VMEM((n_blk, 2 * sw, 2 * LANES), BF16),
                        pltpu.VMEM((n_blk, rt, LANES), F32), pltpu.VMEM((n_blk, rt, LANES), F32)],
        compiler_params=_cparams(1),
        name="s5",
    )(*([p] * (2 * bsz)), w_cat, c_cat, lre, lim)


def _finish_cd_kernel(y_ref, xs_ref, z_ref, sf_ref, sb_ref, u_ref, dssd_ref, nw_ref, d5_ref, glu_ref, out_ref):
    y = (y_ref[0] + y_ref[1] + dssd_ref[...] * xs_ref[...]) * _silu(z_ref[...])
    y = y * lax.rsqrt(jnp.mean(y * y, axis=-1, keepdims=True) + EPS) * nw_ref[...]
    nd = y.shape[1]
    out_ref[:, 0:nd] = y.astype(out_ref.dtype)
    s = sf_ref[...] + sb_ref[...] + d5_ref[...] * u_ref[...]
    s = 0.5 * s * (1.0 + lax.erf(s * (2.0 ** -0.5)))
    gl = _dot(s.astype(BF16), glu_ref[...].astype(BF16))
    out_ref[:, nd:nd + S5_W] = (gl[:, 0:S5_W] * _sigmoid(gl[:, S5_W:])).astype(out_ref.dtype)


def finish_cd(ydir, xbc, p1, sf, sb, p2, ssd_d_lanes, norm_w, s5_d, glu_w, widx, n_rows, bt=512):
    nd = SSD_H * SSD_P
    vec = lambda n: pl.BlockSpec((None, 1, n), lambda i: (widx, 0, 0))
    return pl.pallas_call(
        _finish_cd_kernel,
        grid=(n_rows // bt,),
        in_specs=[pl.BlockSpec((2, bt, nd), lambda i: (0, i, 0)),
                  pl.BlockSpec((bt, nd), lambda i: (i, 0)),
                  pl.BlockSpec((bt, nd), lambda i: (i, 0)),
                  pl.BlockSpec((bt, S5_W), lambda i: (i, 0)),
                  pl.BlockSpec((bt, S5_W), lambda i: (i, 0)),
                  pl.BlockSpec((bt, S5_W), lambda i: (i, 0)),
                  vec(nd), vec(nd), vec(S5_W),
                  pl.BlockSpec((None, S5_W, 2 * S5_W), lambda i: (widx, 0, 0))],
        out_specs=pl.BlockSpec((bt, nd + S5_W), lambda i: (i, 0)),
        out_shape=jax.ShapeDtypeStruct((n_rows, nd + S5_W), BF16),
        compiler_params=_cparams(1),
        name="finish_cd",
    )(ydir, xbc, p1, sf, sb, p2, ssd_d_lanes, norm_w.reshape(norm_w.shape[0], 1, nd),
      s5_d.reshape(s5_d.shape[0], 1, S5_W), glu_w)


def _router_kernel(x_ref, sh_ref, sc_ref, w_ref, h_ref, o_ref):
    x = x_ref[...]
    h = x * lax.rsqrt(jnp.mean(x * x, axis=-1, keepdims=True) + EPS) * (1.0 + sc_ref[...]) + sh_ref[...]
    h_ref[...] = h
    logits = _dot_hi(h, w_ref[...])
    lane = lax.broadcasted_iota(jnp.int32, logits.shape, 1)
    logits = jnp.where(lane < N_EXPERTS, logits, -jnp.inf)
    m1 = jnp.max(logits, axis=1, keepdims=True)
    i1 = jnp.min(jnp.where(logits == m1, lane, LANES), axis=1, keepdims=True)
    rest = jnp.where(lane == i1, -jnp.inf, logits)
    m2 = jnp.max(rest, axis=1, keepdims=True)
    i2 = jnp.min(jnp.where(rest == m2, lane, LANES), axis=1, keepdims=True)
    e2 = jnp.exp(m2 - m1)
    p1 = 1.0 / (1.0 + e2)
    p2 = e2 / (1.0 + e2)
    o_ref[...] = jnp.where(lane == 0, i1.astype(F32),
                           jnp.where(lane == 1, i2.astype(F32),
                                     jnp.where(lane == 2, p1, jnp.where(lane == 3, p2, 0.0))))


def modulate_route(x, shift, scale, w_router_padded, rows, bt=512):
    T, D = x.shape
    mod_spec = pl.BlockSpec((None, 1, D), lambda i: (rows.mod_row(i, bt), 0, 0))
    return pl.pallas_call(
        _router_kernel,
        grid=(T // bt,),
        in_specs=[pl.BlockSpec((bt, D), lambda i: (i, 0)), mod_spec, mod_spec,
                  pl.BlockSpec((D, LANES), lambda i: (0, 0))],
        out_specs=[pl.BlockSpec((bt, D), lambda i: (i, 0)), pl.BlockSpec((bt, LANES), lambda i: (i, 0))],
        out_shape=[jax.ShapeDtypeStruct((T, D), F32), jax.ShapeDtypeStruct((T, LANES), F32)],
        compiler_params=_cparams(1),
        name="modulate_route",
    )(x, shift, scale, w_router_padded)


def _gather_kernel(idx_ref, nt_ref, src_ref, o_ref, buf_ref, sem, *, bm, s):
    i = pl.program_id(0)
    n_used = nt_ref[0]
    slot = i % 2

    def row_copy(tile, slot_, r):
        return pltpu.make_async_copy(src_ref.at[idx_ref[tile * bm + r]],
                                     buf_ref.at[slot_, pl.ds(pl.multiple_of(r * s, s), s), :], sem.at[slot_])

    def issue(tile, slot_):
        def body(r2, c):
            row_copy(tile, slot_, 2 * r2).start(priority=0)
            row_copy(tile, slot_, 2 * r2 + 1).start(priority=1)
            return c
        lax.fori_loop(0, bm // 2, body, 0, unroll=4)

    @pl.when(i == 0)
    def _():
        issue(0, 0)

    @pl.when(i + 1 < n_used)
    def _():
        issue(i + 1, 1 - slot)

    @pl.when(i < n_used)
    def _():
        def body(r, c):
            row_copy(i, slot, r).wait()
            return c
        lax.fori_loop(0, bm, body, 0, unroll=8)
        for j in range(s):
            o_ref[:, j * LANES:(j + 1) * LANES] = buf_ref[slot, pl.ds(j, bm, stride=s), :].astype(o_ref.dtype)

    @pl.when(i >= n_used)
    def _():
        o_ref[...] = jnp.zeros_like(o_ref)


def gather_rows(src, idx, n_used_tiles, bm=256):
    N = idx.shape[0]
    T, D = src.shape
    s = D // LANES
    return pl.pallas_call(
        functools.partial(_gather_kernel, bm=bm, s=s),
        grid_spec=pltpu.PrefetchScalarGridSpec(
            num_scalar_prefetch=2,
            grid=(N // bm,),
            in_specs=[pl.BlockSpec(memory_space=pl.ANY)],
            out_specs=pl.BlockSpec((bm, D), lambda i, idx, nt: (i, 0)),
            scratch_shapes=[pltpu.VMEM((2, bm * s, LANES), F32), pltpu.SemaphoreType.DMA((2,))]),
        out_shape=jax.ShapeDtypeStruct((N, D), BF16),
        compiler_params=_cparams(1),
        name="gather_rows",
    )(idx, n_used_tiles, src.reshape(T, s, LANES))


def _gmm_kernel(hdr_ref, grp_ref, last_ref, rows_ref, x_ref, w_ref, o_ref, wf_ref, wb_ref, sem,
                *, widx, bn, nf, parts, bm):
    n = pl.program_id(0)
    m = pl.program_id(1)
    n_used, n_grp = hdr_ref[0], hdr_ref[1]
    total = nf * n_grp
    grp = grp_ref[m]
    order = n * n_grp + grp
    half = bm // 2

    def group_after(nq, gq):
        wrap = gq + 1 >= n_grp
        return jnp.where(wrap, nq + 1, nq), jnp.where(wrap, 0, gq + 1)

    def fetch(nq, gq):
        expert = hdr_ref[2 + gq]
        return [pltpu.make_async_copy(
            w_ref.at[widx, expert, :, pl.ds(pl.multiple_of((nq + p * nf) * bn, bn), bn)],
            wf_ref.at[p], sem.at[p]) for p in range(parts)]

    def rows_times_weights(r0, nr):
        x = x_ref[r0:r0 + nr, :]
        if parts == 2:
            o_ref[r0:r0 + nr, :] = (_silu(_dot(x, wb_ref[0])) * _dot(x, wb_ref[1])).astype(o_ref.dtype)
        else:
            o_ref[r0:r0 + nr, :] = _dot(x, wb_ref[0]).astype(o_ref.dtype)

    is_first = (m == 0) | (grp_ref[jnp.maximum(m - 1, 0)] != grp)
    is_last = last_ref[m] == 1

    @pl.when(is_first)
    def _():
        @pl.when((n == 0) & (m == 0))
        def _():
            for cp in fetch(n, grp):
                cp.start()

        for cp in fetch(n, grp):
            cp.wait()
        wb_ref[...] = wf_ref[...].astype(BF16)

        @pl.when(order + 1 < total)
        def _():
            for cp in fetch(*group_after(n, grp)):
                cp.start()

    @pl.when((m < n_used) & jnp.logical_not(is_last))
    def _():
        rows_times_weights(0, bm)

    @pl.when(is_last)
    def _():
        rows_times_weights(0, half)

        @pl.when(rows_ref[m] > half)
        def _():
            rows_times_weights(half, half)

        @pl.when(rows_ref[m] <= half)
        def _():
            o_ref[half:, :] = jnp.zeros((half, bn), o_ref.dtype)

    @pl.when(m >= n_used)
    def _():
        o_ref[...] = jnp.zeros_like(o_ref)


def gmm(x, w, widx, plan, *, bm, bn, swiglu):
    M, K = x.shape
    parts = 2 if swiglu else 1
    N = w.shape[3] // parts
    nf = N // bn
    return pl.pallas_call(
        functools.partial(_gmm_kernel, widx=widx, bn=bn, nf=nf, parts=parts, bm=bm),
        grid_spec=pltpu.PrefetchScalarGridSpec(
            num_scalar_prefetch=4,
            grid=(nf, M // bm),
            in_specs=[pl.BlockSpec((bm, K), lambda n, m, *_: (m, 0)),
                      pl.BlockSpec(memory_space=pl.ANY)],
            out_specs=pl.BlockSpec((bm, bn), lambda n, m, *_: (m, n)),
            scratch_shapes=[pltpu.VMEM((parts, K, bn), F32), pltpu.VMEM((parts, K, bn), BF16),
                            pltpu.SemaphoreType.DMA((parts,))]),
        out_shape=jax.ShapeDtypeStruct((M, N), BF16 if swiglu else F32),
        compiler_params=_cparams(2),
        name="gmm_swiglu" if swiglu else "gmm",
    )(plan.header, plan.tile_group, plan.tile_last, plan.tile_rows, x, w)


def _combine_kernel(pos_ref, y_ref, x_ref, pr_ref, g_ref, nw_ref, o_ref, buf_ref, sem, *, bm, n_tok):
    i = pl.program_id(0)
    slot = i % 2

    def copies(tile, slot_, r):
        return [pltpu.make_async_copy(y_ref.at[pl.ds(pos_ref[c * n_tok + tile * bm + r], 1), :],
                                      buf_ref.at[slot_, c, pl.ds(r, 1), :], sem.at[slot_, c]) for c in range(2)]

    def issue(tile, slot_):
        def body(r, carry):
            for c, cp in enumerate(copies(tile, slot_, r)):
                cp.start(priority=c)
            return carry
        lax.fori_loop(0, bm, body, 0, unroll=4)

    @pl.when(i == 0)
    def _():
        issue(0, 0)

    @pl.when(i + 1 < pl.num_programs(0))
    def _():
        issue(i + 1, 1 - slot)

    def drain(r, carry):
        for cp in copies(i, slot, r):
            cp.wait()
        return carry

    lax.fori_loop(0, bm, drain, 0, unroll=4)
    pr = pr_ref[...]
    mix = pr[:, 2:3] * buf_ref[slot, 0] + pr[:, 3:4] * buf_ref[slot, 1]
    x = x_ref[...] + g_ref[...] * mix
    o_ref[...] = x * lax.rsqrt(jnp.mean(x * x, axis=-1, keepdims=True) + EPS) * nw_ref[...]


def moe_combine(ys, pos, x, route, gate, norm_w, rows, bm=256):
    T, D = x.shape
    return pl.pallas_call(
        functools.partial(_combine_kernel, bm=bm, n_tok=T),
        grid_spec=pltpu.PrefetchScalarGridSpec(
            num_scalar_prefetch=1,
            grid=(T // bm,),
            in_specs=[pl.BlockSpec(memory_space=pl.ANY),
                      pl.BlockSpec((bm, D), lambda i, pos: (i, 0)),
                      pl.BlockSpec((bm, LANES), lambda i, pos: (i, 0)),
                      pl.BlockSpec((None, 1, D), lambda i, pos: (rows.mod_row(i, bm), 0, 0)),
                      pl.BlockSpec((1, D), lambda i, pos: (0, 0))],
            out_specs=pl.BlockSpec((bm, D), lambda i, pos: (i, 0)),
            scratch_shapes=[pltpu.VMEM((2, 2, bm, D), F32), pltpu.SemaphoreType.DMA((2, 2))]),
        out_shape=jax.ShapeDtypeStruct((T, D), F32),
        compiler_params=_cparams(1),
        name="moe_combine",
    )(pos, ys, x, route, gate, norm_w[None])


class MoePlan(NamedTuple):
    pos: jax.Array
    src: jax.Array
    n_active: jax.Array
    header: jax.Array
    tile_group: jax.Array
    tile_last: jax.Array
    tile_rows: jax.Array


def moe_plan(route, n_tok, bm):
    e = jnp.concatenate([route[:, 0], route[:, 1]]).astype(jnp.int32)
    onehot = (e[:, None] == jnp.arange(N_EXPERTS, dtype=jnp.int32)[None, :]).astype(jnp.int32)
    rank = jnp.take_along_axis(jnp.cumsum(onehot, axis=0), e[:, None], axis=1)[:, 0] - 1
    counts = jnp.sum(onehot, axis=0)
    tiles = (counts + bm - 1) // bm
    tile_end = jnp.cumsum(tiles)
    start = (tile_end - tiles) * bm
    pos = start[e] + rank
    n_tiles = (2 * n_tok) // bm + N_EXPERTS
    tok = jnp.concatenate([jnp.arange(n_tok, dtype=jnp.int32)] * 2)
    src = jnp.zeros((n_tiles * bm,), jnp.int32).at[pos].set(tok)
    n_active = tile_end[-1]
    all_tiles = jnp.arange(n_tiles, dtype=jnp.int32)
    used = all_tiles < n_active
    tile_ids = jnp.minimum(all_tiles, n_active - 1)
    tile_expert = jnp.sum((tile_ids[:, None] >= tile_end[None, :]).astype(jnp.int32), axis=1)
    ids = jnp.arange(N_EXPERTS, dtype=jnp.int32)
    in_use = tiles > 0
    group_of = jnp.cumsum(in_use.astype(jnp.int32)) - 1
    n_groups = jnp.sum(in_use.astype(jnp.int32))
    experts_in_use = jnp.sum(jnp.where(in_use[None, :] & (group_of[None, :] == ids[:, None]), ids[None, :], 0), axis=1)
    tile_last = used & (all_tiles + 1 == tile_end[tile_expert])
    tile_rows = jnp.clip(counts[tile_expert] - (all_tiles - (tile_end - tiles)[tile_expert]) * bm, 0, bm)
    header = jnp.concatenate([n_active.reshape(1), n_groups.reshape(1), experts_in_use])
    return MoePlan(pos.astype(jnp.int32), src, n_active.reshape(1).astype(jnp.int32), header.astype(jnp.int32),
                   group_of[tile_expert].astype(jnp.int32), tile_last.astype(jnp.int32),
                   jnp.where(used, tile_rows, 0).astype(jnp.int32))


def kernel(x, c, ctx, c_ctx, ada_w, ada_b, ab_w_in, hy_conv_w, hy_conv_b, hy_filt_w1, hy_filt_b1, hy_filt_w2, hy_filt_b2, hy_filt_w3, hy_filt_freq, hy_bias, ml_conv_w, ml_conv_b, ml_gate_b, ml_norm_w, ab_w_out, ffn_w_gu, ffn_w_down, cd_w_in, ssd_conv_w, ssd_conv_b, ssd_A_log, ssd_dt_bias, ssd_D, ssd_norm_w, s5_lam_re, s5_lam_im, s5_log_dt, s5_B_re, s5_B_im, s5_C_re, s5_C_im, s5_D, s5_glu_w, cd_w_out, moe_router, moe_w_gu, moe_w_down, final_norm_w):
    bsz, L, D = x.shape
    Lc = ctx.shape[1]
    rows = Rows(bsz, L, Lc)
    n_lat = rows.n_lat
    assert bsz == 4 and L % 1024 == 0 and Lc == 256 and D % 256 == 0
    bn_d = min(1024, D)
    bn_f = min(512, ffn_w_down.shape[1])

    x_lat, x_ctx = x.reshape(n_lat, D), ctx.reshape(bsz * Lc, D)
    cond = jnp.concatenate([c, c_ctx[None], jnp.zeros((8 - bsz - 1, D), F32)], axis=0)

    def mods(layer):
        m = adaln(cond, ada_w, ada_b, layer)
        return [m[:, k * D:(k + 1) * D].reshape(8, 1, D) for k in range(6)]

    md = mods(0)
    h = modulate(x_lat, md[0], md[1], rows, rows.n, BF16, x_tail=x_ctx)
    n_main = 3 * HY_D + 4 * ML_H * ML_DH
    p = mm_t(h, jnp.swapaxes(ab_w_in, 1, 2), 0, n_cols=n_main, bm=1024, bn=1024)
    w_gate = jnp.pad(ab_w_in[:, :, n_main:], ((0, 0), (0, 0), (0, LANES - 4 * ML_H)))
    gates = mm(h, w_gate, 0, n_cols=LANES, bm=1024, bn=LANES)
    gates_t = gates[:, :4 * ML_H].T

    filt = (hy_filt_w1, hy_filt_b1, hy_filt_w2, hy_filt_b2, hy_filt_w3, hy_filt_freq)
    yh_lat = hyena(p, L, bsz, 0, GRID_W, 0, hy_conv_w, hy_conv_b, filt, hy_bias)
    yh_ctx = hyena(p, Lc, bsz, n_lat, Lc, 0, hy_conv_w, hy_conv_b, filt, hy_bias)

    qk = conv_silu(p, 3 * HY_D, 2 * ML_H * ML_DH, ml_conv_w, ml_conv_b, 0, rows)
    hdir = mlstm(qk, p, 3 * HY_D + 2 * ML_H * ML_DH, gates, gates_t, ml_gate_b[0], rows)
    cat = finish_ab(yh_lat, yh_ctx, hdir, p, 3 * HY_D + 3 * ML_H * ML_DH, ml_norm_w, 0, rows)
    xs = mm_residual(cat, ab_w_out, 0, x_lat, md[2], rows, bm=1024, bn=bn_d, res_tail=x_ctx)

    h = modulate(xs, md[3], md[4], rows, rows.n, BF16)
    act = mm_swiglu(h, ffn_w_gu, 0, bm=1024, bn=bn_f)
    xs = mm_residual(act, ffn_w_down, 0, xs, md[5], rows, bm=512, bn=min(512, D))

    md = mods(1)
    h = modulate(xs, md[0], md[1], rows, rows.n, BF16)
    nd = SSD_H * SSD_P
    n_xbc = nd + 4 * LANES
    p1 = mm_t(h, jnp.swapaxes(cd_w_in, 1, 2), 0, n_cols=nd + n_xbc, bm=1024, bn=512)
    w_tail = cd_w_in[:, :, nd + n_xbc:]
    w_tail = jnp.concatenate([w_tail[:, :, 2 * SSD_H:], w_tail[:, :, :2 * SSD_H],
                              jnp.zeros((1, D, LANES - 2 * SSD_H), F32)], axis=2)
    p2 = mm(h, w_tail, 0, n_cols=S5_W + LANES, bm=1024, bn=S5_W + LANES)

    xbc = conv_silu(p1, nd, n_xbc, ssd_conv_w, ssd_conv_b, 0, rows)
    dt = p2[:, S5_W:]
    dt_t = dt[:, :2 * SSD_H].T
    ydir = ssd(xbc, dt, dt_t, ssd_A_log[0], ssd_dt_bias[0], rows)

    w_cat, c_cat, lb_re, lb_im, lam_rows = s5_params(
        s5_lam_re[0], s5_lam_im[0], s5_log_dt[0], s5_B_re[0], s5_B_im[0], s5_C_re[0], s5_C_im[0])
    sf, sb = s5(p2, w_cat, c_cat, lam_rows(lb_re, bsz), lam_rows(lb_im, bsz), rows)
    sf, sb = sf.reshape(n_lat, S5_W), sb.reshape(n_lat, S5_W)

    ssd_d_lanes = jnp.repeat(ssd_D, SSD_P, axis=1).reshape(ssd_D.shape[0], 1, nd)
    cat = finish_cd(ydir, xbc, p1, sf, sb, p2, ssd_d_lanes, ssd_norm_w, s5_D, s5_glu_w, 0, n_lat)
    xl = mm_residual(cat, cd_w_out, 0, xs, md[2], rows, bm=1024, bn=bn_d)

    h2, route = modulate_route(xl, md[3], md[4], jnp.pad(moe_router[0], ((0, 0), (0, LANES - N_EXPERTS))), rows)
    bm_e = 512
    plan = moe_plan(route, n_lat, bm_e)
    xg = gather_rows(h2, plan.src, plan.n_active, bm=bm_e)
    act = gmm(xg, moe_w_gu, 0, plan, bm=bm_e, bn=min(512, moe_w_down.shape[2]), swiglu=True)
    ys = gmm(act, moe_w_down, 0, plan, bm=bm_e, bn=min(512, D), swiglu=False)
    out = moe_combine(ys, plan.pos, xl, route, md[5], final_norm_w, rows, bm=512)
    return out.reshape(bsz, L, D)
```
